```python
import math
import jax, jax.numpy as jnp
from jax import lax
import numpy as np

D_MODEL = 1024
BATCH = 32
SEQ = 2048
DEPTH = 1

HEAD_DIM = 64
DSA_HEADS = 8
KV_LATENT = 256
IDX_HEADS = 4
IDX_DIM = 64
TOPK_MAX = 256
Q_BLOCK = 128
DIL_GROUPS = ((128, 1), (512, 4), (2048, 16))
N_DIL_GROUPS = 3
DIL_HEADS_PER_GROUP = 4
DIL_HEADS = N_DIL_GROUPS * DIL_HEADS_PER_GROUP
N_BRANCHES = 2
NUM_BUCKETS = 32
MAX_DISTANCE = 2048
N_REL_HEADS = DSA_HEADS + DIL_HEADS
N_GROUPS = 4
EXPERTS_PER_GROUP = 8
N_EXPERTS = N_GROUPS * EXPERTS_PER_GROUP
EXPERT_FF = 512
TOP_K_INNER = 2
RMS_EPS = 1e-6
NEG = -1e30

COLS_DSA_Q = DSA_HEADS * HEAD_DIM
COLS_KV = KV_LATENT
COLS_IDX_Q = IDX_HEADS * IDX_DIM
COLS_IDX_K = IDX_DIM
COLS_IDX_W = IDX_HEADS
COLS_DIL = 3 * DIL_HEADS * HEAD_DIM
COLS_GATE = N_BRANCHES * D_MODEL
IN_COLS = COLS_DSA_Q + COLS_KV + COLS_IDX_Q + COLS_IDX_K + COLS_IDX_W + COLS_DIL + COLS_GATE
SPLIT_POINTS = [COLS_DSA_Q,
                COLS_DSA_Q + COLS_KV,
                COLS_DSA_Q + COLS_KV + COLS_IDX_Q,
                COLS_DSA_Q + COLS_KV + COLS_IDX_Q + COLS_IDX_K,
                COLS_DSA_Q + COLS_KV + COLS_IDX_Q + COLS_IDX_K + COLS_IDX_W,
                COLS_DSA_Q + COLS_KV + COLS_IDX_Q + COLS_IDX_K + COLS_IDX_W + COLS_DIL]
DIL_OUT = DIL_HEADS_PER_GROUP * HEAD_DIM

kernel_name = "hybrid_dsa_dilated_hmoe_block"


def rms_norm(x, g):
    xf = x.astype(jnp.float32)
    y = xf * lax.rsqrt(jnp.mean(xf * xf, axis=-1, keepdims=True) + RMS_EPS)
    return (y * g.astype(jnp.float32)).astype(x.dtype)


def rel_bucket(dist):
    max_exact = NUM_BUCKETS // 2
    n = jnp.maximum(dist, 0)
    nf = jnp.maximum(n, 1).astype(jnp.float32)
    large = max_exact + (jnp.log(nf / max_exact) / math.log(MAX_DISTANCE / max_exact)
                         * (NUM_BUCKETS - max_exact)).astype(jnp.int32)
    large = jnp.minimum(large, NUM_BUCKETS - 1)
    return jnp.where(n < max_exact, n, large)


def dsa_branch(q, c, q_idx, k_idx, w_idx, w_uk, w_uv, bias_tab):
    B, L = c.shape[:2]
    topk = min(TOPK_MAX, L // 4)
    nb = L // Q_BLOCK
    q_lat = jnp.einsum('blhd,hcd->blhc', q, w_uk) * HEAD_DIM ** -0.5
    k_idx_f = k_idx.astype(jnp.float32)
    bias_tab_f = bias_tab.astype(jnp.float32)
    key_pos = jnp.arange(L)

    def to_blocks(a):
        return jnp.moveaxis(a.reshape((B, nb, Q_BLOCK) + a.shape[2:]), 1, 0)

    def block_fn(args):
        n, qlb, qib, wib = args
        qpos = n * Q_BLOCK + jnp.arange(Q_BLOCK)
        s = jnp.einsum('bqhd,bsd->bqhs', qib.astype(jnp.float32), k_idx_f) * IDX_DIM ** -0.5
        score = jnp.einsum('bqh,bqhs->bqs', wib.astype(jnp.float32) * IDX_HEADS ** -0.5,
                           jax.nn.relu(s))
        causal = key_pos[None, :] <= qpos[:, None]
        score = jnp.where(causal[None], score, NEG)
        _, sel = lax.top_k(score, topk)
        valid = sel <= qpos[None, :, None]
        c_sel = jax.vmap(lambda cb, ib: cb[ib])(c, sel)
        logits = jnp.einsum('bqhc,bqkc->bqhk', qlb, c_sel).astype(jnp.float32)
        bias = bias_tab_f[rel_bucket(qpos[None, :, None] - sel)]
        logits = logits + jnp.moveaxis(bias, -1, 2)
        logits = jnp.where(valid[:, :, None, :], logits, NEG)
        p = jax.nn.softmax(logits, axis=-1).astype(c.dtype)
        o_lat = jnp.einsum('bqhk,bqkc->bqhc', p, c_sel)
        return jnp.einsum('bqhc,hcd->bqhd', o_lat, w_uv)

    out = lax.map(block_fn, (jnp.arange(nb), to_blocks(q_lat), to_blocks(q_idx), to_blocks(w_idx)))
    return jnp.moveaxis(out, 0, 1).reshape(B, L, DSA_HEADS * HEAD_DIM)


def dilated_group(q, k, v, window, dilation, bias_tab):
    B, L, H, dh = q.shape
    w = window // dilation
    ls = L // dilation
    nb = -(-ls // w)
    pad = nb * w - ls

    def to_sub(a):
        a = a.reshape(B, ls, dilation, H, dh).transpose(0, 2, 1, 3, 4)
        a = jnp.pad(a, ((0, 0), (0, 0), (0, pad), (0, 0), (0, 0)))
        return a.reshape(B, dilation, nb, w, H, dh)

    def with_prev(a):
        prev = jnp.pad(a, ((0, 0), (0, 0), (1, 0), (0, 0), (0, 0), (0, 0)))[:, :, :-1]
        return jnp.concatenate([prev, a], axis=3)

    qs = to_sub(q)
    kb = with_prev(to_sub(k))
    vb = with_prev(to_sub(v))
    qf = jnp.arange(w)[:, None] + w
    kf = jnp.arange(2 * w)[None, :]
    step = qf - kf
    key_sub = jnp.arange(nb)[:, None, None] * w + kf[None] - w
    mask = (step >= 0)[None] & (step <= w)[None] & (key_sub >= 0)
    bias = bias_tab.astype(jnp.float32)[rel_bucket(step * dilation)]
    logits = jnp.einsum('brnqhd,brnkhd->brnhqk', qs, kb).astype(jnp.float32) * dh ** -0.5
    logits = logits + jnp.moveaxis(bias, -1, 0)[None, None, None]
    logits = jnp.where(mask[None, None, :, None], logits, NEG)
    m = jnp.max(logits, axis=-1, keepdims=True)
    e = jnp.exp(logits - m)
    s = jnp.sum(e, axis=-1, keepdims=True)
    o = jnp.einsum('brnhqk,brnkhd->brnqhd', (e / s).astype(v.dtype), vb)
    lse = (m + jnp.log(s))[..., 0]
    o = o.reshape(B, dilation, nb * w, H, dh)[:, :, :ls].transpose(0, 2, 1, 3, 4).reshape(B, L, H, dh)
    lse = lse.transpose(0, 1, 2, 4, 3).reshape(B, dilation, nb * w, H)[:, :, :ls]
    lse = lse.transpose(0, 2, 1, 3).reshape(B, L, H)
    return o, lse


def dilated_branch(dil, bias_tab):
    B, L = dil.shape[:2]
    qkv = dil.reshape(B, L, 3, N_DIL_GROUPS, DIL_HEADS_PER_GROUP, HEAD_DIM)
    outs, lses = [], []
    for g, (window, dilation) in enumerate(DIL_GROUPS):
        tab = bias_tab[:, DSA_HEADS + g * DIL_HEADS_PER_GROUP: DSA_HEADS + (g + 1) * DIL_HEADS_PER_GROUP]
        o, lse = dilated_group(qkv[:, :, 0, g], qkv[:, :, 1, g], qkv[:, :, 2, g], window, dilation, tab)
        outs.append(o)
        lses.append(lse)
    alpha = jax.nn.softmax(jnp.stack(lses, axis=0), axis=0)
    o = jnp.sum(alpha[..., None].astype(dil.dtype) * jnp.stack(outs, axis=0), axis=0)
    return o.reshape(B, L, DIL_OUT)


def hier_moe(h, w_rg, b_rg, w_re, b_re, w_gate, w_up, w_down):
    B, L, D = h.shape
    t = h.reshape(-1, D)
    n_tok = t.shape[0]
    g_logits = (t @ w_rg + b_rg).astype(jnp.float32)
    g_prob = jax.nn.softmax(g_logits, axis=-1)
    g_sel = jnp.argmax(g_logits, axis=-1)
    p_g = jnp.take_along_axis(g_prob, g_sel[:, None], axis=1)[:, 0]
    e_logits = (jnp.einsum('nd,gde->nge', t, w_re) + b_re).astype(jnp.float32)
    e_logits = jnp.take_along_axis(e_logits, g_sel[:, None, None], axis=1)[:, 0]
    top_vals, top_idx = lax.top_k(e_logits, TOP_K_INNER)
    weights = p_g[:, None] * jax.nn.softmax(top_vals, axis=-1)
    gid = g_sel[:, None] * EXPERTS_PER_GROUP + top_idx
    combine = jnp.zeros((n_tok, N_EXPERTS), jnp.float32).at[jnp.arange(n_tok)[:, None], gid].add(weights)
    combine = combine.astype(t.dtype)
    out = jnp.zeros_like(t)
    for e in range(N_EXPERTS):
        hid = jax.nn.silu(t @ w_gate[e]) * (t @ w_up[e])
        out = out + combine[:, e:e + 1] * (hid @ w_down[e])
    return out.reshape(B, L, D)


def setup_inputs(seed: int = 0) -> dict:
    key = jax.random.key(seed)
    ks = jax.random.split(key, 20)
    f32 = jnp.float32

    def nrm(k, shape, scale):
        return jax.random.normal(k, shape, f32) * scale

    def gain(k, shape):
        return jnp.ones(shape, f32) + 0.01 * jax.random.normal(k, shape, f32)

    return {
        "x": jax.random.normal(ks[0], (BATCH, SEQ, D_MODEL), f32),
        "attn_norm": gain(ks[1], (DEPTH, D_MODEL)),
        "w_in": nrm(ks[2], (DEPTH, D_MODEL, IN_COLS), D_MODEL ** -0.5),
        "kv_norm": gain(ks[3], (DEPTH, KV_LATENT)),
        "w_uk": nrm(ks[4], (DEPTH, DSA_HEADS, KV_LATENT, HEAD_DIM), KV_LATENT ** -0.5),
        "w_uv": nrm(ks[5], (DEPTH, DSA_HEADS, KV_LATENT, HEAD_DIM), KV_LATENT ** -0.5),
        "rel_bias": nrm(ks[6], (NUM_BUCKETS, N_REL_HEADS), 0.1),
        "w_branch_a": nrm(ks[7], (DEPTH, DSA_HEADS * HEAD_DIM, D_MODEL), (DSA_HEADS * HEAD_DIM) ** -0.5),
        "w_branch_b": nrm(ks[8], (DEPTH, DIL_OUT, D_MODEL), DIL_OUT ** -0.5),
        "w_out": nrm(ks[9], (DEPTH, D_MODEL, D_MODEL), D_MODEL ** -0.5),
        "ffn_norm": gain(ks[10], (DEPTH, D_MODEL)),
        "w_router_group": nrm(ks[11], (DEPTH, D_MODEL, N_GROUPS), D_MODEL ** -0.5),
        "b_router_group": nrm(ks[12], (DEPTH, N_GROUPS), 0.01),
        "w_router_expert": nrm(ks[13], (DEPTH, N_GROUPS, D_MODEL, EXPERTS_PER_GROUP), D_MODEL ** -0.5),
        "b_router_expert": nrm(ks[14], (DEPTH, N_GROUPS, EXPERTS_PER_GROUP), 0.01),
        "w_gate": nrm(ks[15], (DEPTH, N_EXPERTS, D_MODEL, EXPERT_FF), D_MODEL ** -0.5),
        "w_up": nrm(ks[16], (DEPTH, N_EXPERTS, D_MODEL, EXPERT_FF), D_MODEL ** -0.5),
        "w_down": nrm(ks[17], (DEPTH, N_EXPERTS, EXPERT_FF, D_MODEL), EXPERT_FF ** -0.5),
        "final_norm": gain(ks[18], (D_MODEL,)),
    }


def reference(x, attn_norm, w_in, kv_norm, w_uk, w_uv, rel_bias, w_branch_a, w_branch_b, w_out,
              ffn_norm, w_router_group, b_router_group, w_router_expert, b_router_expert,
              w_gate, w_up, w_down, final_norm):
    B, L, D = x.shape
    h = x
    for layer in range(DEPTH):
        u = rms_norm(h, attn_norm[layer])
        proj = u @ w_in[layer]
        q_a, c_kv, q_idx, k_idx, w_idx, dil, gates = jnp.split(proj, SPLIT_POINTS, axis=-1)
        c_kv = rms_norm(c_kv, kv_norm[layer])
        y_a = dsa_branch(q_a.reshape(B, L, DSA_HEADS, HEAD_DIM), c_kv,
                         q_idx.reshape(B, L, IDX_HEADS, IDX_DIM), k_idx, w_idx,
                         w_uk[layer], w_uv[layer], rel_bias[:, :DSA_HEADS])
        y_b = dilated_branch(dil, rel_bias)
        g = jax.nn.sigmoid(gates.reshape(B, L, N_BRANCHES, D).astype(jnp.float32)).astype(x.dtype)
        mixed = g[:, :, 0] * (y_a @ w_branch_a[layer]) + g[:, :, 1] * (y_b @ w_branch_b[layer])
        h = h + mixed @ w_out[layer]
        h = h + hier_moe(rms_norm(h, ffn_norm[layer]), w_router_group[layer], b_router_group[layer],
                         w_router_expert[layer], b_router_expert[layer],
                         w_gate[layer], w_up[layer], w_down[layer])
    return rms_norm(h, final_norm)
```

```python
import functools
import math

import numpy as np
import jax
import jax.numpy as jnp
from jax import lax
from jax.experimental import pallas as pl
from jax.experimental.pallas import tpu as pltpu

F32 = jnp.float32
BF16 = jnp.bfloat16
I32 = jnp.int32

LANES = 128
VMEM_LIMIT_BYTES = 56 * 1024 * 1024

HEAD_DIM = 64
DSA_HEADS = 8
KV_LATENT = 256
IDX_HEADS = 4
IDX_DIM = 64
TOPK_MAX = 256
DIL_GROUPS = ((128, 1), (512, 4), (2048, 16))
DIL_HPG = 4
DIL_OUT = DIL_HPG * HEAD_DIM
NUM_BUCKETS = 32
MAX_DISTANCE = 2048
N_GROUPS = 4
EXPERTS_PER_GROUP = 8
N_EXPERTS = N_GROUPS * EXPERTS_PER_GROUP
RMS_EPS = 1e-6
NEG = -1e30
INT_MIN = -2 ** 31

BLK = 128
PROJ_TM = 512
MIX_TM = 256
MOE_TM = 256
FIN_TM = 256
ROUTE_COLS = 128


def _cparams(sem):
    return pltpu.CompilerParams(dimension_semantics=sem, vmem_limit_bytes=VMEM_LIMIT_BYTES)


C_QA = 0
C_KV = 512
C_QI = 768
C_KW = 1024
C_DIL = 1152
C_GATE = C_DIL + 9 * 256
C_END = C_GATE + 2048


def _proj_kernel(x_ref, g_ref, w_ref, kvg_ref, qa_ref, ckv_ref, qi_ref, kw_ref, *rest):
    dil_refs = rest[:9]
    gate_ref = rest[9]
    x = x_ref[...]
    u = x * lax.rsqrt(jnp.mean(x * x, axis=-1, keepdims=True) + RMS_EPS) * g_ref[...]
    u = u.astype(BF16)

    def mm(a, b):
        return jnp.dot(u, w_ref[:, a:b], preferred_element_type=F32)

    qa_ref[...] = mm(C_QA, C_KV).astype(BF16)
    c = mm(C_KV, C_QI)
    c = c * lax.rsqrt(jnp.mean(c * c, axis=-1, keepdims=True) + RMS_EPS) * kvg_ref[...]
    ckv_ref[...] = c.astype(BF16)
    qi_ref[...] = mm(C_QI, C_KW).astype(BF16)
    kw_ref[...] = mm(C_KW, C_DIL)
    for j in range(9):
        dil_refs[j][...] = mm(C_DIL + 256 * j, C_DIL + 256 * (j + 1)).astype(BF16)
    for j in range(4):
        gate_ref[:, 512 * j:512 * (j + 1)] = mm(C_GATE + 512 * j, C_GATE + 512 * (j + 1)).astype(BF16)


def _proj(x2, attn_norm, w_packed, kv_norm):
    n, d = x2.shape
    tm = PROJ_TM
    row = lambda i: (i, 0)
    const = lambda i: (0, 0)
    outs = [jax.ShapeDtypeStruct((n, 512), BF16), jax.ShapeDtypeStruct((n, 256), BF16),
            jax.ShapeDtypeStruct((n, 256), BF16), jax.ShapeDtypeStruct((n, 128), F32)]
    outs += [jax.ShapeDtypeStruct((n, 256), BF16)] * 9
    outs += [jax.ShapeDtypeStruct((n, 2048), BF16)]
    out_specs = [pl.BlockSpec((tm, s.shape[1]), row) for s in outs]
    return pl.pallas_call(
        _proj_kernel,
        out_shape=outs,
        grid=(n // tm,),
        in_specs=[pl.BlockSpec((tm, d), row), pl.BlockSpec((1, d), const),
                  pl.BlockSpec((d, C_END), const), pl.BlockSpec((1, KV_LATENT), const)],
        out_specs=out_specs,
        compiler_params=_cparams(("parallel",)),
        name="proj",
    )(x2, attn_norm.reshape(1, d), w_packed, kv_norm.reshape(1, KV_LATENT))


def _bucket_thresholds():
    max_exact = NUM_BUCKETS // 2
    d = np.arange(0, MAX_DISTANCE + 1)
    nf = np.maximum(d, 1).astype(np.float32)
    large = max_exact + (np.log(nf / np.float32(max_exact)) / np.float32(math.log(MAX_DISTANCE / max_exact))
                         * np.float32(NUM_BUCKETS - max_exact)).astype(np.int32)
    large = np.minimum(large, NUM_BUCKETS - 1)
    bucket = np.where(d < max_exact, d, large)
    assert np.all(np.diff(bucket) >= 0)
    return [int(np.argmax(bucket >= b)) for b in range(1, NUM_BUCKETS)]


_BUCKET_THR = _bucket_thresholds()


def _bias_from_distance(dist, tab_ref, heads):
    masks = [dist >= t for t in _BUCKET_THR]
    out = []
    for h in heads:
        v = jnp.full(dist.shape, tab_ref[0, h], F32)
        for b in range(1, NUM_BUCKETS):
            v = jnp.where(masks[b - 1], tab_ref[b, h], v)
        out.append(v)
    return out


def _dsa_bias_kernel(tab_ref, o_ref):
    delta = pl.program_id(0)
    j = lax.broadcasted_iota(I32, (BLK, BLK), 0)
    i = lax.broadcasted_iota(I32, (BLK, BLK), 1)
    dist = jnp.maximum(delta * BLK + i - j, 0)
    tiles = _bias_from_distance(dist, tab_ref, range(DSA_HEADS))
    for h in range(DSA_HEADS):
        o_ref[0, h] = tiles[h]


def _dil_bias_kernel(tab_ref, o_ref, *, dilations):
    g = pl.program_id(0)
    i = lax.broadcasted_iota(I32, (BLK, 2 * BLK), 0)
    j = lax.broadcasted_iota(I32, (BLK, 2 * BLK), 1)
    step = i + BLK - j
    valid = (step >= 0) & (step <= BLK)
    for gi, r in enumerate(dilations):
        @pl.when(g == gi)
        def _():
            dist = jnp.maximum(step, 0) * r
            heads = [DSA_HEADS + gi * DIL_HPG + hh for hh in range(DIL_HPG)]
            tiles = _bias_from_distance(dist, tab_ref, heads)
            for hh in range(DIL_HPG):
                o_ref[0, hh] = jnp.where(valid, tiles[hh], NEG)


def _bias_tiles(rel_bias, nkc):
    smem = pl.BlockSpec(memory_space=pltpu.SMEM)
    dsa = pl.pallas_call(
        _dsa_bias_kernel,
        out_shape=jax.ShapeDtypeStruct((nkc, DSA_HEADS, BLK, BLK), F32),
        grid=(nkc,),
        in_specs=[smem],
        out_specs=pl.BlockSpec((1, DSA_HEADS, BLK, BLK), lambda d: (d, 0, 0, 0)),
        compiler_params=_cparams(("parallel",)),
        name="dsa_bias",
    )(rel_bias)
    dil = pl.pallas_call(
        functools.partial(_dil_bias_kernel, dilations=tuple(r for _, r in DIL_GROUPS)),
        out_shape=jax.ShapeDtypeStruct((len(DIL_GROUPS), DIL_HPG, BLK, 2 * BLK), F32),
        grid=(len(DIL_GROUPS),),
        in_specs=[smem],
        out_specs=pl.BlockSpec((1, DIL_HPG, BLK, 2 * BLK), lambda g: (g, 0, 0, 0)),
        compiler_params=_cparams(("parallel",)),
        name="dil_bias",
    )(rel_bias)
    return dsa, dil


def _dsa_kernel(qiT_ref, wiT_ref, qaT_ref, kidx_ref, ckv_ref, ckvT_ref, wuk_ref, wuv_ref, bias_ref,
                y_ref, keys_ref, qlT_ref, pT_ref, acc_ref, m_ref, l_ref, a_ref, *, topk, idx_bits):
    qb = pl.program_id(1)
    nch = qb + 1
    row = lax.broadcasted_iota(I32, (BLK, BLK), 0)
    col = lax.broadcasted_iota(I32, (BLK, BLK), 1)

    for h in range(DSA_HEADS):
        ql = jnp.dot(wuk_ref[h], qaT_ref[0, h * HEAD_DIM:(h + 1) * HEAD_DIM, :],
                     preferred_element_type=F32) * (HEAD_DIM ** -0.5)
        qlT_ref[:, h * BLK:(h + 1) * BLK] = ql.astype(BF16)

    wq = wiT_ref[0]

    def score_body(kc, carry):
        kx = kidx_ref[0, pl.ds(pl.multiple_of(kc * BLK, BLK), BLK), :]
        sc = jnp.zeros((BLK, BLK), F32)
        for h in range(IDX_HEADS):
            s = jnp.dot(kx, qiT_ref[0, h * IDX_DIM:(h + 1) * IDX_DIM, :],
                        preferred_element_type=F32) * (IDX_DIM ** -0.5)
            sc = sc + (wq[h:h + 1, :] * (IDX_HEADS ** -0.5)) * jnp.maximum(s, 0.0)
        sc = jnp.where(row <= col + (qb - kc) * BLK, sc, NEG)
        bits = pltpu.bitcast(sc, I32)
        bits = jnp.where(bits == INT_MIN, 0, bits)
        keys_ref[kc] = bits ^ ((bits >> 31) & 0x7FFFFFFF)
        return carry

    lax.fori_loop(0, nch, score_body, 0)

    def count(pred):
        def body(kc, cnt):
            return cnt + jnp.where(pred(kc, keys_ref[kc]), 1, 0)
        cnt = lax.fori_loop(0, nch, body, jnp.zeros((BLK, BLK), I32))
        return jnp.sum(cnt, axis=0, keepdims=True)

    def thr_body(it, lo):
        cand = lo + jnp.left_shift(jnp.int32(1), 31 - it)
        c = count(lambda kc, k: k >= cand)
        return jnp.where(c >= topk, cand, lo)

    thr = lax.fori_loop(0, 32, thr_body, jnp.full((1, BLK), INT_MIN, I32))

    need = topk - count(lambda kc, k: k > thr)

    def cut_body(it, lo):
        cand = lo + jnp.left_shift(jnp.int32(1), idx_bits - 1 - it)
        c = count(lambda kc, k: (k == thr) & (row < cand - kc * BLK))
        return jnp.where(c < need, cand, lo)

    cut = lax.fori_loop(0, idx_bits, cut_body, jnp.zeros((1, BLK), I32))

    m_ref[...] = jnp.full(m_ref.shape, NEG, F32)
    l_ref[...] = jnp.zeros(l_ref.shape, F32)
    acc_ref[...] = jnp.zeros(acc_ref.shape, F32)

    def att_body(kc, carry):
        off = pl.multiple_of(kc * BLK, BLK)
        k = keys_ref[kc]
        sel = (k > thr) | ((k == thr) & (row <= cut - kc * BLK))
        sel = sel & (row <= col + (qb - kc) * BLK)
        lg = jnp.dot(ckv_ref[0, pl.ds(off, BLK), :], qlT_ref[...], preferred_element_type=F32)
        delta = qb - kc
        for h in range(DSA_HEADS):
            hs = slice(h * BLK, (h + 1) * BLK)
            x = jnp.where(sel, lg[:, hs] + bias_ref[delta, h], NEG)
            m_old = m_ref[:, hs]
            m_new = jnp.maximum(m_old, jnp.max(x, axis=0, keepdims=True))
            p = jnp.exp(x - m_new)
            alpha = jnp.exp(m_old - m_new)
            l_ref[:, hs] = alpha * l_ref[:, hs] + jnp.sum(p, axis=0, keepdims=True)
            m_ref[:, hs] = m_new
            a_ref[:, hs] = alpha
            pT_ref[:, hs] = p.astype(BF16)
        acc_ref[...] = acc_ref[...] * a_ref[...] + jnp.dot(ckvT_ref[0, kc], pT_ref[...],
                                                           preferred_element_type=F32)
        return carry

    lax.fori_loop(0, nch, att_body, 0)

    o = acc_ref[...] * (1.0 / l_ref[...])
    for h in range(DSA_HEADS):
        oh = o[:, h * BLK:(h + 1) * BLK].T.astype(BF16)
        yh = jnp.dot(oh, wuv_ref[h], preferred_element_type=F32)
        y_ref[0, :, h * HEAD_DIM:(h + 1) * HEAD_DIM] = yh.astype(BF16)


def _dsa(qiT, wiT, qaT, kidx, ckv, ckvT, wuk, wuv, bias_tiles):
    b, seq, _ = ckv.shape
    nkc = seq // BLK
    topk = min(TOPK_MAX, seq // 4)
    idx_bits = int(math.log2(seq))
    assert 2 ** idx_bits == seq
    qblk = lambda rows: pl.BlockSpec((1, rows, BLK), lambda bi, qi: (bi, 0, qi))
    full3 = lambda s: pl.BlockSpec(s, lambda bi, qi: (bi, 0, 0))
    const = lambda s: pl.BlockSpec(s, lambda bi, qi: (0,) * len(s))
    hl = DSA_HEADS * BLK
    return pl.pallas_call(
        functools.partial(_dsa_kernel, topk=topk, idx_bits=idx_bits),
        out_shape=jax.ShapeDtypeStruct((b, seq, DSA_HEADS * HEAD_DIM), BF16),
        grid=(b, nkc),
        in_specs=[qblk(IDX_HEADS * IDX_DIM), qblk(8), qblk(DSA_HEADS * HEAD_DIM),
                  full3((1, seq, IDX_DIM)), full3((1, seq, KV_LATENT)),
                  pl.BlockSpec((1, nkc, KV_LATENT, BLK), lambda bi, qi: (bi, 0, 0, 0)),
                  const((DSA_HEADS, KV_LATENT, HEAD_DIM)), const((DSA_HEADS, KV_LATENT, HEAD_DIM)),
                  const((nkc, DSA_HEADS, BLK, BLK))],
        out_specs=pl.BlockSpec((1, BLK, DSA_HEADS * HEAD_DIM), lambda bi, qi: (bi, qi, 0)),
        scratch_shapes=[pltpu.VMEM((nkc, BLK, BLK), I32),
                        pltpu.VMEM((KV_LATENT, hl), BF16),
                        pltpu.VMEM((BLK, hl), BF16),
                        pltpu.VMEM((KV_LATENT, hl), F32),
                        pltpu.VMEM((1, hl), F32), pltpu.VMEM((1, hl), F32), pltpu.VMEM((1, hl), F32)],
        compiler_params=_cparams(("parallel", "arbitrary")),
        name="dsa",
    )(qiT, wiT, qaT, kidx, ckv, ckvT, wuk, wuv, bias_tiles)


def _dil_kernel(q_ref, k_ref, v_ref, bm_ref, o_ref, lse_ref, *, nblk):
    def block(qo, ko, nkeys, bcol):
        for hh in range(DIL_HPG):
            hs = slice(hh * HEAD_DIM, (hh + 1) * HEAD_DIM)
            qh = q_ref[0, pl.ds(qo, BLK), hs]
            kh = k_ref[0, pl.ds(ko, nkeys), hs]
            vh = v_ref[0, pl.ds(ko, nkeys), hs]
            s = lax.dot_general(qh, kh, (((1,), (1,)), ((), ())), preferred_element_type=F32)
            s = s * (HEAD_DIM ** -0.5) + bm_ref[0, hh, :, bcol:bcol + nkeys]
            m = jnp.max(s, axis=-1, keepdims=True)
            e = jnp.exp(s - m)
            l = jnp.sum(e, axis=-1, keepdims=True)
            o = jnp.dot(e.astype(BF16), vh, preferred_element_type=F32) * (1.0 / l)
            o_ref[0, pl.ds(qo, BLK), hs] = o.astype(BF16)
            lse_ref[0, pl.ds(qo, BLK), hs] = jnp.broadcast_to(m + jnp.log(l), (BLK, HEAD_DIM))

    block(0, 0, BLK, BLK)

    def body(n, carry):
        block(pl.multiple_of(n * BLK, BLK), pl.multiple_of((n - 1) * BLK, BLK), 2 * BLK, 0)
        return carry

    lax.fori_loop(1, nblk, body, 0)


def _dilated_group(q, k, v, bm, g, dilation):
    b, seq, c = q.shape
    ls = seq // dilation
    view = lambda a: a.reshape(b, ls, dilation * c)
    blk = pl.BlockSpec((1, ls, c), lambda bi, ri: (bi, 0, ri))
    o, lse = pl.pallas_call(
        functools.partial(_dil_kernel, nblk=ls // BLK),
        out_shape=[jax.ShapeDtypeStruct((b, ls, dilation * c), BF16),
                   jax.ShapeDtypeStruct((b, ls, dilation * c), F32)],
        grid=(b, dilation),
        in_specs=[blk, blk, blk, pl.BlockSpec((1, DIL_HPG, BLK, 2 * BLK), lambda bi, ri: (g, 0, 0, 0))],
        out_specs=[blk, blk],
        compiler_params=_cparams(("parallel", "parallel")),
        name=f"dilated_g{g}",
    )(view(q), view(k), view(v), bm)
    return o.reshape(b, seq, c), lse.reshape(b, seq, c)


def _mix_kernel(x_ref, ya_ref, o1_ref, o2_ref, o3_ref, l1_ref, l2_ref, l3_ref, gate_ref,
                wa_ref, wb_ref, wo_ref, fg_ref, wr_ref, br_ref,
                h_ref, t_ref, rw_ref, ri_ref):
    l1, l2, l3 = l1_ref[...], l2_ref[...], l3_ref[...]
    mx = jnp.maximum(jnp.maximum(l1, l2), l3)
    e1, e2, e3 = jnp.exp(l1 - mx), jnp.exp(l2 - mx), jnp.exp(l3 - mx)
    inv = 1.0 / (e1 + e2 + e3)
    yb = ((e1 * inv) * o1_ref[...].astype(F32) + (e2 * inv) * o2_ref[...].astype(F32)
          + (e3 * inv) * o3_ref[...].astype(F32))
    a = jnp.dot(ya_ref[...], wa_ref[...], preferred_element_type=F32)
    bmix = jnp.dot(yb.astype(BF16), wb_ref[...], preferred_element_type=F32)
    d = a.shape[1]
    g0 = jax.nn.sigmoid(gate_ref[:, :d].astype(F32))
    g1 = jax.nn.sigmoid(gate_ref[:, d:].astype(F32))
    mixed = g0 * a + g1 * bmix
    h = x_ref[...] + jnp.dot(mixed.astype(BF16), wo_ref[...], preferred_element_type=F32)
    h_ref[...] = h
    t = h * lax.rsqrt(jnp.mean(h * h, axis=-1, keepdims=True) + RMS_EPS) * fg_ref[...]
    t_ref[...] = t

    logits = jnp.dot(t, wr_ref[...], preferred_element_type=F32,
                     precision=lax.Precision.HIGHEST) + br_ref[...]
    lane = lax.broadcasted_iota(I32, logits.shape, 1)
    ninf = -jnp.inf
    big = jnp.int32(10 ** 6)

    def first_argmax(v, vmax):
        return jnp.min(jnp.where(v == vmax, lane, big), axis=-1, keepdims=True)

    gl = jnp.where(lane < N_GROUPS, logits, ninf)
    gmax = jnp.max(gl, axis=-1, keepdims=True)
    gsel = first_argmax(gl, gmax)
    p_g = 1.0 / jnp.sum(jnp.exp(gl - gmax), axis=-1, keepdims=True)
    lo = N_GROUPS + gsel * EXPERTS_PER_GROUP
    el = jnp.where((lane >= lo) & (lane < lo + EXPERTS_PER_GROUP), logits, ninf)
    v1 = jnp.max(el, axis=-1, keepdims=True)
    i1 = first_argmax(el, v1)
    el2 = jnp.where(lane == i1, ninf, el)
    v2 = jnp.max(el2, axis=-1, keepdims=True)
    i2 = first_argmax(el2, v2)
    e2 = jnp.exp(v2 - v1)
    w1 = p_g / (1.0 + e2)
    w2 = p_g * e2 / (1.0 + e2)
    rw_ref[...] = jnp.where(lane == 0, w1, jnp.where(lane == 1, w2, 0.0))
    ri_ref[...] = jnp.where(lane == 0, i1 - N_GROUPS, jnp.where(lane == 1, i2 - N_GROUPS, 0))


def _mix(x2, ya, os_, lses, gates, wa, wb, wo, ffn_norm, wr, br):
    n, d = x2.shape
    tm = MIX_TM
    row = lambda c: pl.BlockSpec((tm, c), lambda i: (i, 0))
    const = lambda s: pl.BlockSpec(s, lambda i: (0, 0))
    return pl.pallas_call(
        _mix_kernel,
        out_shape=[jax.ShapeDtypeStruct((n, d), F32), jax.ShapeDtypeStruct((n, d), F32),
                   jax.ShapeDtypeStruct((n, ROUTE_COLS), F32), jax.ShapeDtypeStruct((n, ROUTE_COLS), I32)],
        grid=(n // tm,),
        in_specs=[row(d), row(512), row(DIL_OUT), row(DIL_OUT), row(DIL_OUT),
                  row(DIL_OUT), row(DIL_OUT), row(DIL_OUT), row(2 * d),
                  const(wa.shape), const(wb.shape), const(wo.shape), const((1, d)),
                  const(wr.shape), const((1, ROUTE_COLS))],
        out_specs=[row(d), row(d), row(ROUTE_COLS), row(ROUTE_COLS)],
        compiler_params=_cparams(("parallel",)),
        name="mix",
    )(x2, ya, *os_, *lses, gates, wa, wb, wo, ffn_norm.reshape(1, d), wr, br)


def _row_gather(src_hbm, idx_ref, nrows, buf, sem, slot):
    def copy(r):
        return pltpu.make_async_copy(src_hbm.at[pl.ds(idx_ref[0, 0, r], 1)],
                                     buf.at[slot, pl.ds(r, 1)], sem.at[slot])

    def start():
        def body(r, c):
            copy(r).start()
            return c
        lax.fori_loop(0, nrows, body, 0)

    def wait():
        def body(r, c):
            copy(r).wait()
            return c
        lax.fori_loop(0, nrows, body, 0)

    return start, wait


def _gather_pipeline(i, nsteps, src_hbm, cur_ref, nxt_ref, nrows, buf, sem):
    slot = i % 2

    @pl.when(i == 0)
    def _():
        _row_gather(src_hbm, cur_ref, nrows, buf, sem, 0)[0]()

    @pl.when(i + 1 < nsteps)
    def _():
        _row_gather(src_hbm, nxt_ref, nrows, buf, sem, 1 - slot)[0]()

    _row_gather(src_hbm, cur_ref, nrows, buf, sem, slot)[1]()
    return slot


def _expert_kernel(te_ref, cur_ref, nxt_ref, t_hbm, wg_ref, wu_ref, wd_ref, y_ref, buf, sem):
    i = pl.program_id(0)
    slot = _gather_pipeline(i, pl.num_programs(0), t_hbm, cur_ref, nxt_ref, MOE_TM, buf, sem)
    xt = buf[slot].astype(BF16)
    hg = jnp.dot(xt, wg_ref[0], preferred_element_type=F32)
    hu = jnp.dot(xt, wu_ref[0], preferred_element_type=F32)
    hid = (hg * jax.nn.sigmoid(hg)) * hu
    y_ref[...] = jnp.dot(hid.astype(BF16), wd_ref[0], preferred_element_type=F32)


def _experts(tile_expert, row_token, t, wg, wu, wd):
    n, d = t.shape
    ntiles = tile_expert.shape[0]
    ff = wg.shape[2]
    tok3 = row_token.reshape(ntiles, 1, MOE_TM)
    smem_cur = pl.BlockSpec((1, 1, MOE_TM), lambda i, te: (i, 0, 0), memory_space=pltpu.SMEM)
    smem_nxt = pl.BlockSpec((1, 1, MOE_TM), lambda i, te: (jnp.minimum(i + 1, ntiles - 1), 0, 0),
                            memory_space=pltpu.SMEM)
    wspec = lambda s: pl.BlockSpec((1,) + s, lambda i, te: (te[i], 0, 0))
    return pl.pallas_call(
        _expert_kernel,
        out_shape=jax.ShapeDtypeStruct((ntiles * MOE_TM, d), F32),
        grid_spec=pltpu.PrefetchScalarGridSpec(
            num_scalar_prefetch=1,
            grid=(ntiles,),
            in_specs=[smem_cur, smem_nxt, pl.BlockSpec(memory_space=pl.ANY),
                      wspec((d, ff)), wspec((d, ff)), wspec((ff, d))],
            out_specs=pl.BlockSpec((MOE_TM, d), lambda i, te: (i, 0)),
            scratch_shapes=[pltpu.VMEM((2, MOE_TM, d), F32), pltpu.SemaphoreType.DMA((2,))],
        ),
        compiler_params=_cparams(("arbitrary",)),
        name="experts",
    )(tile_expert, tok3, tok3, t, wg, wu, wd)


def _final_kernel(cur_ref, nxt_ref, y_hbm, h_ref, rw_ref, fn_ref, o_ref, buf, sem):
    i = pl.program_id(0)
    slot = _gather_pipeline(i, pl.num_programs(0), y_hbm, cur_ref, nxt_ref, 2 * FIN_TM, buf, sem)
    rw = rw_ref[...]
    h = h_ref[...] + rw[:, 0:1] * buf[slot, :FIN_TM] + rw[:, 1:2] * buf[slot, FIN_TM:]
    o_ref[...] = h * lax.rsqrt(jnp.mean(h * h, axis=-1, keepdims=True) + RMS_EPS) * fn_ref[...]


def _final(pos_tiles, y_sorted, h, rw, final_norm):
    n, d = h.shape
    tm = FIN_TM
    nt = n // tm
    smem_cur = pl.BlockSpec((1, 1, 2 * tm), lambda i: (i, 0, 0), memory_space=pltpu.SMEM)
    smem_nxt = pl.BlockSpec((1, 1, 2 * tm), lambda i: (jnp.minimum(i + 1, nt - 1), 0, 0),
                            memory_space=pltpu.SMEM)
    return pl.pallas_call(
        _final_kernel,
        out_shape=jax.ShapeDtypeStruct((n, d), F32),
        grid=(nt,),
        in_specs=[smem_cur, smem_nxt, pl.BlockSpec(memory_space=pl.ANY),
                  pl.BlockSpec((tm, d), lambda i: (i, 0)), pl.BlockSpec((tm, ROUTE_COLS), lambda i: (i, 0)),
                  pl.BlockSpec((1, d), lambda i: (0, 0))],
        out_specs=pl.BlockSpec((tm, d), lambda i: (i, 0)),
        scratch_shapes=[pltpu.VMEM((2, 2 * tm, d), F32), pltpu.SemaphoreType.DMA((2,))],
        compiler_params=_cparams(("arbitrary",)),
        name="final",
    )(pos_tiles, pos_tiles, y_sorted, h, rw, final_norm.reshape(1, d))


def _route_plan(gid):
    n = gid.shape[0]
    e = gid.reshape(-1)
    onehot = (e[:, None] == jnp.arange(N_EXPERTS, dtype=I32)[None, :]).astype(I32)
    csum = jnp.cumsum(onehot, axis=0)
    rank = jnp.take_along_axis(csum, e[:, None], axis=1)[:, 0] - 1
    counts = csum[-1]
    padded = ((counts + MOE_TM - 1) // MOE_TM) * MOE_TM
    seg_end = jnp.cumsum(padded)
    pos = (seg_end - padded)[e] + rank
    nrows = 2 * n + N_EXPERTS * MOE_TM
    row_token = jnp.zeros((nrows,), I32).at[pos].set(jnp.arange(2 * n, dtype=I32) // 2)
    tile_start = jnp.arange(nrows // MOE_TM, dtype=I32) * MOE_TM
    tile_expert = jnp.minimum(jnp.searchsorted(seg_end, tile_start, side="right"), N_EXPERTS - 1).astype(I32)
    return row_token, tile_expert, pos.reshape(n, 2)


def _pack_w_in(w):
    d = w.shape[0]
    o_kv, o_qi, o_ki, o_wi = 512, 768, 1024, 1088
    o_dil = o_wi + IDX_HEADS
    o_gate = o_dil + 9 * 256
    pad = jnp.zeros((d, LANES - IDX_DIM - IDX_HEADS), w.dtype)
    packed = jnp.concatenate([w[:, :o_ki], w[:, o_ki:o_wi], w[:, o_wi:o_dil], pad, w[:, o_dil:o_gate],
                              w[:, o_gate:]], axis=1)
    assert packed.shape[1] == C_END
    return packed.astype(BF16)


def kernel(x, attn_norm, w_in, kv_norm, w_uk, w_uv, rel_bias, w_branch_a, w_branch_b, w_out, ffn_norm,
           w_router_group, b_router_group, w_router_expert, b_router_expert, w_gate, w_up, w_down,
           final_norm):
    b, seq, d = x.shape
    n = b * seq
    nkc = seq // BLK
    assert w_in.shape[0] == 1, "one layer"
    x2 = x.reshape(n, d)

    outs = _proj(x2, attn_norm[0], _pack_w_in(w_in[0]), kv_norm[0])
    qa, ckv, qi, kw = outs[:4]
    dil = outs[4:13]
    gates = outs[13]

    dsa_bias, dil_bias = _bias_tiles(rel_bias, nkc)

    t3 = lambda a: jnp.swapaxes(a.reshape(b, seq, a.shape[-1]), 1, 2)
    kw3 = kw.reshape(b, seq, LANES)
    wiT = jnp.swapaxes(kw3[:, :, IDX_DIM:IDX_DIM + 8], 1, 2)
    kidx = kw3[:, :, :IDX_DIM].astype(BF16)
    ckv3 = ckv.reshape(b, seq, KV_LATENT)
    ckvT = jnp.swapaxes(ckv3.reshape(b, nkc, BLK, KV_LATENT), 2, 3)
    ya = _dsa(t3(qi), wiT, t3(qa), kidx, ckv3, ckvT, w_uk[0].astype(BF16), w_uv[0].astype(BF16), dsa_bias)

    os_, lses = [], []
    for g, (_, dilation) in enumerate(DIL_GROUPS):
        r3 = lambda a: a.reshape(b, seq, DIL_OUT)
        o, lse = _dilated_group(r3(dil[g]), r3(dil[3 + g]), r3(dil[6 + g]), dil_bias, g, dilation)
        os_.append(o.reshape(n, DIL_OUT))
        lses.append(lse.reshape(n, DIL_OUT))

    wr = jnp.concatenate([w_router_group[0],
                          jnp.swapaxes(w_router_expert[0], 0, 1).reshape(d, N_EXPERTS),
                          jnp.zeros((d, ROUTE_COLS - N_GROUPS - N_EXPERTS), F32)], axis=1)
    br = jnp.concatenate([b_router_group[0], b_router_expert[0].reshape(-1),
                          jnp.zeros((ROUTE_COLS - N_GROUPS - N_EXPERTS,), F32)]).reshape(1, ROUTE_COLS)
    h, t, rw, ri = _mix(x2, ya.reshape(n, -1), os_, lses, gates,
                        w_branch_a[0].astype(BF16), w_branch_b[0].astype(BF16), w_out[0].astype(BF16),
                        ffn_norm[0], wr, br)

    row_token, tile_expert, pos = _route_plan(ri[:, :2])
    y_sorted = _experts(tile_expert, row_token, t, w_gate[0].astype(BF16), w_up[0].astype(BF16),
                        w_down[0].astype(BF16))

    pos_tiles = jnp.swapaxes(pos.reshape(n // FIN_TM, FIN_TM, 2), 1, 2).reshape(n // FIN_TM, 1, 2 * FIN_TM)
    out = _final(pos_tiles, y_sorted, h, rw, final_norm)
    return out.reshape(b, seq, d)
```

```python
import functools
import math

import numpy as np
import jax
import jax.numpy as jnp
from jax import lax
from jax.experimental import pallas as pl
from jax.experimental.pallas import tpu as pltpu

F32 = jnp.float32
BF16 = jnp.bfloat16
I32 = jnp.int32

LANES = 128
VMEM_LIMIT_BYTES = 56 * 1024 * 1024

HEAD_DIM = 64
DSA_HEADS = 8
KV_LATENT = 256
IDX_HEADS = 4
IDX_DIM = 64
TOPK_MAX = 256
DIL_GROUPS = ((128, 1), (512, 4), (2048, 16))
DIL_HPG = 4
DIL_OUT = DIL_HPG * HEAD_DIM
NUM_BUCKETS = 32
MAX_DISTANCE = 2048
N_GROUPS = 4
EXPERTS_PER_GROUP = 8
N_EXPERTS = N_GROUPS * EXPERTS_PER_GROUP
RMS_EPS = 1e-6
NEG = -1e30
INT_MIN = -2 ** 31

BLK = 128
PROJ_TM = 512
MIX_TM = 256
MOE_TM = 256
FIN_TM = 256
ROUTE_COLS = 128


def _cparams(sem):
    return pltpu.CompilerParams(dimension_semantics=sem, vmem_limit_bytes=VMEM_LIMIT_BYTES)


C_QA = 0
C_KV = 512
C_QI = 768
C_KW = 1024
C_DIL = 1152
C_GATE = C_DIL + 9 * 256
C_END = C_GATE + 2048


def _proj_kernel(x_ref, g_ref, w_ref, kvg_ref, qa_ref, ckv_ref, qi_ref, kw_ref, *rest):
    dil_refs = rest[:9]
    gate_ref = rest[9]
    x = x_ref[...]
    u = x * lax.rsqrt(jnp.mean(x * x, axis=-1, keepdims=True) + RMS_EPS) * g_ref[...]
    u = u.astype(BF16)

    def mm(a, b):
        return jnp.dot(u, w_ref[:, a:b], preferred_element_type=F32)

    qa_ref[...] = mm(C_QA, C_KV).astype(BF16)
    c = mm(C_KV, C_QI)
    c = c * lax.rsqrt(jnp.mean(c * c, axis=-1, keepdims=True) + RMS_EPS) * kvg_ref[...]
    ckv_ref[...] = c.astype(BF16)
    qi_ref[...] = mm(C_QI, C_KW).astype(BF16)
    kw_ref[...] = mm(C_KW, C_DIL)
    for j in range(9):
        dil_refs[j][...] = mm(C_DIL + 256 * j, C_DIL + 256 * (j + 1)).astype(BF16)
    for j in range(4):
        gate_ref[:, 512 * j:512 * (j + 1)] = mm(C_GATE + 512 * j, C_GATE + 512 * (j + 1)).astype(BF16)


def _proj(x2, attn_norm, w_packed, kv_norm):
    n, d = x2.shape
    tm = PROJ_TM
    row = lambda i: (i, 0)
    const = lambda i: (0, 0)
    outs = [jax.ShapeDtypeStruct((n, 512), BF16), jax.ShapeDtypeStruct((n, 256), BF16),
            jax.ShapeDtypeStruct((n, 256), BF16), jax.ShapeDtypeStruct((n, 128), F32)]
    outs += [jax.ShapeDtypeStruct((n, 256), BF16)] * 9
    outs += [jax.ShapeDtypeStruct((n, 2048), BF16)]
    out_specs = [pl.BlockSpec((tm, s.shape[1]), row) for s in outs]
    return pl.pallas_call(
        _proj_kernel,
        out_shape=outs,
        grid=(n // tm,),
        in_specs=[pl.BlockSpec((tm, d), row), pl.BlockSpec((1, d), const),
                  pl.BlockSpec((d, C_END), const), pl.BlockSpec((1, KV_LATENT), const)],
        out_specs=out_specs,
        compiler_params=_cparams(("parallel",)),
        name="proj",
    )(x2, attn_norm.reshape(1, d), w_packed, kv_norm.reshape(1, KV_LATENT))


def _bucket_thresholds():
    max_exact = NUM_BUCKETS // 2
    d = np.arange(0, MAX_DISTANCE + 1)
    nf = np.maximum(d, 1).astype(np.float32)
    large = max_exact + (np.log(nf / np.float32(max_exact)) / np.float32(math.log(MAX_DISTANCE / max_exact))
                         * np.float32(NUM_BUCKETS - max_exact)).astype(np.int32)
    large = np.minimum(large, NUM_BUCKETS - 1)
    bucket = np.where(d < max_exact, d, large)
    assert np.all(np.diff(bucket) >= 0)
    return [int(np.argmax(bucket >= b)) for b in range(1, NUM_BUCKETS)]


_BUCKET_THR = _bucket_thresholds()


def _bias_from_distance(dist, tab_ref, heads):
    masks = [dist >= t for t in _BUCKET_THR]
    out = []
    for h in heads:
        v = jnp.full(dist.shape, tab_ref[0, h], F32)
        for b in range(1, NUM_BUCKETS):
            v = jnp.where(masks[b - 1], tab_ref[b, h], v)
        out.append(v)
    return out


def _dsa_bias_kernel(tab_ref, o_ref):
    delta = pl.program_id(0)
    j = lax.broadcasted_iota(I32, (BLK, BLK), 0)
    i = lax.broadcasted_iota(I32, (BLK, BLK), 1)
    dist = jnp.maximum(delta * BLK + i - j, 0)
    tiles = _bias_from_distance(dist, tab_ref, range(DSA_HEADS))
    for h in range(DSA_HEADS):
        o_ref[0, h] = tiles[h]


def _dil_bias_kernel(tab_ref, o_ref, *, dilations):
    g = pl.program_id(0)
    i = lax.broadcasted_iota(I32, (BLK, 2 * BLK), 0)
    j = lax.broadcasted_iota(I32, (BLK, 2 * BLK), 1)
    step = i + BLK - j
    valid = (step >= 0) & (step <= BLK)
    for gi, r in enumerate(dilations):
        @pl.when(g == gi)
        def _():
            dist = jnp.maximum(step, 0) * r
            heads = [DSA_HEADS + gi * DIL_HPG + hh for hh in range(DIL_HPG)]
            tiles = _bias_from_distance(dist, tab_ref, heads)
            for hh in range(DIL_HPG):
                o_ref[0, hh] = jnp.where(valid, tiles[hh], NEG)


def _bias_tiles(rel_bias, nkc):
    smem = pl.BlockSpec(memory_space=pltpu.SMEM)
    dsa = pl.pallas_call(
        _dsa_bias_kernel,
        out_shape=jax.ShapeDtypeStruct((nkc, DSA_HEADS, BLK, BLK), F32),
        grid=(nkc,),
        in_specs=[smem],
        out_specs=pl.BlockSpec((1, DSA_HEADS, BLK, BLK), lambda d: (d, 0, 0, 0)),
        compiler_params=_cparams(("parallel",)),
        name="dsa_bias",
    )(rel_bias)
    dil = pl.pallas_call(
        functools.partial(_dil_bias_kernel, dilations=tuple(r for _, r in DIL_GROUPS)),
        out_shape=jax.ShapeDtypeStruct((len(DIL_GROUPS), DIL_HPG, BLK, 2 * BLK), F32),
        grid=(len(DIL_GROUPS),),
        in_specs=[smem],
        out_specs=pl.BlockSpec((1, DIL_HPG, BLK, 2 * BLK), lambda g: (g, 0, 0, 0)),
        compiler_params=_cparams(("parallel",)),
        name="dil_bias",
    )(rel_bias)
    return dsa, dil


def _dsa_kernel(qiT_ref, wiT_ref, qaT_ref, kidx_ref, ckv_ref, ckvT_ref, wuk_ref, wuv_ref, bias_ref,
                y_ref, keys_ref, qlT_ref, pT_ref, acc_ref, m_ref, l_ref, a_ref, *, topk, idx_bits):
    qb = pl.program_id(1)
    nch = qb + 1
    row = lax.broadcasted_iota(I32, (BLK, BLK), 0)
    col = lax.broadcasted_iota(I32, (BLK, BLK), 1)

    for h in range(DSA_HEADS):
        ql = jnp.dot(wuk_ref[h], qaT_ref[0, h * HEAD_DIM:(h + 1) * HEAD_DIM, :],
                     preferred_element_type=F32) * (HEAD_DIM ** -0.5)
        qlT_ref[:, h * BLK:(h + 1) * BLK] = ql.astype(BF16)

    wq = wiT_ref[0]

    def score_body(kc, carry):
        kx = kidx_ref[0, pl.ds(pl.multiple_of(kc * BLK, BLK), BLK), :]
        sc = jnp.zeros((BLK, BLK), F32)
        for h in range(IDX_HEADS):
            s = jnp.dot(kx, qiT_ref[0, h * IDX_DIM:(h + 1) * IDX_DIM, :],
                        preferred_element_type=F32) * (IDX_DIM ** -0.5)
            sc = sc + (wq[h:h + 1, :] * (IDX_HEADS ** -0.5)) * jnp.maximum(s, 0.0)
        sc = jnp.where(row <= col + (qb - kc) * BLK, sc, NEG)
        bits = pltpu.bitcast(sc, I32)
        bits = jnp.where(bits == INT_MIN, 0, bits)
        keys_ref[kc] = bits ^ ((bits >> 31) & 0x7FFFFFFF)
        return carry

    lax.fori_loop(0, nch, score_body, 0)

    def count(pred):
        def body(kc, cnt):
            return cnt + jnp.where(pred(kc, keys_ref[kc]), 1, 0)
        cnt = lax.fori_loop(0, nch, body, jnp.zeros((BLK, BLK), I32))
        return jnp.sum(cnt, axis=0, keepdims=True)

    def thr_body(it, lo):
        cand = lo + jnp.left_shift(jnp.int32(1), 31 - it)
        c = count(lambda kc, k: k >= cand)
        return jnp.where(c >= topk, cand, lo)

    thr = lax.fori_loop(0, 32, thr_body, jnp.full((1, BLK), INT_MIN, I32))

    need = topk - count(lambda kc, k: k > thr)

    def cut_body(it, lo):
        cand = lo + jnp.left_shift(jnp.int32(1), idx_bits - 1 - it)
        c = count(lambda kc, k: (k == thr) & (row < cand - kc * BLK))
        return jnp.where(c < need, cand, lo)

    cut = lax.fori_loop(0, idx_bits, cut_body, jnp.zeros((1, BLK), I32))

    m_ref[...] = jnp.full(m_ref.shape, NEG, F32)
    l_ref[...] = jnp.zeros(l_ref.shape, F32)
    acc_ref[...] = jnp.zeros(acc_ref.shape, F32)

    def att_body(kc, carry):
        off = pl.multiple_of(kc * BLK, BLK)
        k = keys_ref[kc]
        sel = (k > thr) | ((k == thr) & (row <= cut - kc * BLK))
        sel = sel & (row <= col + (qb - kc) * BLK)
        lg = jnp.dot(ckv_ref[0, pl.ds(off, BLK), :], qlT_ref[...], preferred_element_type=F32)
        delta = qb - kc
        for h in range(DSA_HEADS):
            hs = slice(h * BLK, (h + 1) * BLK)
            x = jnp.where(sel, lg[:, hs] + bias_ref[delta, h], NEG)
            m_old = m_ref[:, hs]
            m_new = jnp.maximum(m_old, jnp.max(x, axis=0, keepdims=True))
            p = jnp.exp(x - m_new)
            alpha = jnp.exp(m_old - m_new)
            l_ref[:, hs] = alpha * l_ref[:, hs] + jnp.sum(p, axis=0, keepdims=True)
            m_ref[:, hs] = m_new
            a_ref[:, hs] = alpha
            pT_ref[:, hs] = p.astype(BF16)
        acc_ref[...] = acc_ref[...] * a_ref[...] + jnp.dot(ckvT_ref[0, kc], pT_ref[...],
                                                           preferred_element_type=F32)
        return carry

    lax.fori_loop(0, nch, att_body, 0)

    o = acc_ref[...] * (1.0 / l_ref[...])
    for h in range(DSA_HEADS):
        oh = o[:, h * BLK:(h + 1) * BLK].T.astype(BF16)
        yh = jnp.dot(oh, wuv_ref[h], preferred_element_type=F32)
        y_ref[0, :, h * HEAD_DIM:(h + 1) * HEAD_DIM] = yh.astype(BF16)


def _dsa(qiT, wiT, qaT, kidx, ckv, ckvT, wuk, wuv, bias_tiles):
    b, seq, _ = ckv.shape
    nkc = seq // BLK
    topk = min(TOPK_MAX, seq // 4)
    idx_bits = int(math.log2(seq))
    assert 2 ** idx_bits == seq
    qblk = lambda rows: pl.BlockSpec((1, rows, BLK), lambda bi, qi: (bi, 0, qi))
    full3 = lambda s: pl.BlockSpec(s, lambda bi, qi: (bi, 0, 0))
    const = lambda s: pl.BlockSpec(s, lambda bi, qi: (0,) * len(s))
    hl = DSA_HEADS * BLK
    return pl.pallas_call(
        functools.partial(_dsa_kernel, topk=topk, idx_bits=idx_bits),
        out_shape=jax.ShapeDtypeStruct((b, seq, DSA_HEADS * HEAD_DIM), BF16),
        grid=(b, nkc),
        in_specs=[qblk(IDX_HEADS * IDX_DIM), qblk(8), qblk(DSA_HEADS * HEAD_DIM),
                  full3((1, seq, IDX_DIM)), full3((1, seq, KV_LATENT)),
                  pl.BlockSpec((1, nkc, KV_LATENT, BLK), lambda bi, qi: (bi, 0, 0, 0)),
                  const((DSA_HEADS, KV_LATENT, HEAD_DIM)), const((DSA_HEADS, KV_LATENT, HEAD_DIM)),
                  const((nkc, DSA_HEADS, BLK, BLK))],
        out_specs=pl.BlockSpec((1, BLK, DSA_HEADS * HEAD_DIM), lambda bi, qi: (bi, qi, 0)),
        scratch_shapes=[pltpu.VMEM((nkc, BLK, BLK), I32),
                        pltpu.VMEM((KV_LATENT, hl), BF16),
                        pltpu.VMEM((BLK, hl), BF16),
                        pltpu.VMEM((KV_LATENT, hl), F32),
                        pltpu.VMEM((1, hl), F32), pltpu.VMEM((1, hl), F32), pltpu.VMEM((1, hl), F32)],
        compiler_params=_cparams(("parallel", "arbitrary")),
        name="dsa",
    )(qiT, wiT, qaT, kidx, ckv, ckvT, wuk, wuv, bias_tiles)


def _dil_kernel(q_ref, k_ref, v_ref, bm_ref, o_ref, lse_ref, *, nblk):
    def block(qo, ko, nkeys, bcol):
        for hh in range(DIL_HPG):
            hs = slice(hh * HEAD_DIM, (hh + 1) * HEAD_DIM)
            qh = q_ref[0, pl.ds(qo, BLK), hs]
            kh = k_ref[0, pl.ds(ko, nkeys), hs]
            vh = v_ref[0, pl.ds(ko, nkeys), hs]
            s = lax.dot_general(qh, kh, (((1,), (1,)), ((), ())), preferred_element_type=F32)
            s = s * (HEAD_DIM ** -0.5) + bm_ref[0, hh, :, bcol:bcol + nkeys]
            m = jnp.max(s, axis=-1, keepdims=True)
            e = jnp.exp(s - m)
            l = jnp.sum(e, axis=-1, keepdims=True)
            o = jnp.dot(e.astype(BF16), vh, preferred_element_type=F32) * (1.0 / l)
            o_ref[0, pl.ds(qo, BLK), hs] = o.astype(BF16)
            lse_ref[0, pl.ds(qo, BLK), hs] = jnp.broadcast_to(m + jnp.log(l), (BLK, HEAD_DIM))

    block(0, 0, BLK, BLK)

    def body(n, carry):
        block(pl.multiple_of(n * BLK, BLK), pl.multiple_of((n - 1) * BLK, BLK), 2 * BLK, 0)
        return carry

    lax.fori_loop(1, nblk, body, 0)


def _dilated_group(q, k, v, bm, g, dilation):
    b, seq, c = q.shape
    ls = seq // dilation
    view = lambda a: a.reshape(b, ls, dilation * c)
    blk = pl.BlockSpec((1, ls, c), lambda bi, ri: (bi, 0, ri))
    o, lse = pl.pallas_call(
        functools.partial(_dil_kernel, nblk=ls // BLK),
        out_shape=[jax.ShapeDtypeStruct((b, ls, dilation * c), BF16),
                   jax.ShapeDtypeStruct((b, ls, dilation * c), F32)],
        grid=(b, dilation),
        in_specs=[blk, blk, blk, pl.BlockSpec((1, DIL_HPG, BLK, 2 * BLK), lambda bi, ri: (g, 0, 0, 0))],
        out_specs=[blk, blk],
        compiler_params=_cparams(("parallel", "parallel")),
        name=f"dilated_g{g}",
    )(view(q), view(k), view(v), bm)
    return o.reshape(b, seq, c), lse.reshape(b, seq, c)


def _store_row_tiles(ref, base, val):
    rows, d = val.shape
    dt = d // LANES
    for s in range(dt):
        ref[pl.ds(base * dt + s, rows, stride=dt), :] = val[:, s * LANES:(s + 1) * LANES]


def _load_row_tiles(ref, base, rows, dt):
    return jnp.concatenate([ref[pl.ds(base * dt + s, rows, stride=dt), :] for s in range(dt)], axis=1)


def _mix_kernel(x_ref, ya_ref, o1_ref, o2_ref, o3_ref, l1_ref, l2_ref, l3_ref, gate_ref,
                wa_ref, wb_ref, wo_ref, fg_ref, wr_ref, br_ref,
                h_ref, t_ref, rw_ref, ri_ref):
    l1, l2, l3 = l1_ref[...], l2_ref[...], l3_ref[...]
    mx = jnp.maximum(jnp.maximum(l1, l2), l3)
    e1, e2, e3 = jnp.exp(l1 - mx), jnp.exp(l2 - mx), jnp.exp(l3 - mx)
    inv = 1.0 / (e1 + e2 + e3)
    yb = ((e1 * inv) * o1_ref[...].astype(F32) + (e2 * inv) * o2_ref[...].astype(F32)
          + (e3 * inv) * o3_ref[...].astype(F32))
    a = jnp.dot(ya_ref[...], wa_ref[...], preferred_element_type=F32)
    bmix = jnp.dot(yb.astype(BF16), wb_ref[...], preferred_element_type=F32)
    d = a.shape[1]
    g0 = jax.nn.sigmoid(gate_ref[:, :d].astype(F32))
    g1 = jax.nn.sigmoid(gate_ref[:, d:].astype(F32))
    mixed = g0 * a + g1 * bmix
    h = x_ref[...] + jnp.dot(mixed.astype(BF16), wo_ref[...], preferred_element_type=F32)
    h_ref[...] = h
    t = h * lax.rsqrt(jnp.mean(h * h, axis=-1, keepdims=True) + RMS_EPS) * fg_ref[...]
    _store_row_tiles(t_ref, 0, t)

    logits = jnp.dot(t, wr_ref[...], preferred_element_type=F32,
                     precision=lax.Precision.HIGHEST) + br_ref[...]
    lane = lax.broadcasted_iota(I32, logits.shape, 1)
    ninf = -jnp.inf
    big = jnp.int32(10 ** 6)

    def first_argmax(v, vmax):
        return jnp.min(jnp.where(v == vmax, lane, big), axis=-1, keepdims=True)

    gl = jnp.where(lane < N_GROUPS, logits, ninf)
    gmax = jnp.max(gl, axis=-1, keepdims=True)
    gsel = first_argmax(gl, gmax)
    p_g = 1.0 / jnp.sum(jnp.exp(gl - gmax), axis=-1, keepdims=True)
    lo = N_GROUPS + gsel * EXPERTS_PER_GROUP
    el = jnp.where((lane >= lo) & (lane < lo + EXPERTS_PER_GROUP), logits, ninf)
    v1 = jnp.max(el, axis=-1, keepdims=True)
    i1 = first_argmax(el, v1)
    el2 = jnp.where(lane == i1, ninf, el)
    v2 = jnp.max(el2, axis=-1, keepdims=True)
    i2 = first_argmax(el2, v2)
    e2 = jnp.exp(v2 - v1)
    w1 = p_g / (1.0 + e2)
    w2 = p_g * e2 / (1.0 + e2)
    rw_ref[...] = jnp.where(lane == 0, w1, jnp.where(lane == 1, w2, 0.0))
    ri_ref[...] = jnp.where(lane == 0, i1 - N_GROUPS, jnp.where(lane == 1, i2 - N_GROUPS, 0))


def _mix(x2, ya, os_, lses, gates, wa, wb, wo, ffn_norm, wr, br):
    n, d = x2.shape
    tm = MIX_TM
    row = lambda c: pl.BlockSpec((tm, c), lambda i: (i, 0))
    const = lambda s: pl.BlockSpec(s, lambda i: (0, 0))
    return pl.pallas_call(
        _mix_kernel,
        out_shape=[jax.ShapeDtypeStruct((n, d), F32), jax.ShapeDtypeStruct((n * (d // LANES), LANES), F32),
                   jax.ShapeDtypeStruct((n, ROUTE_COLS), F32), jax.ShapeDtypeStruct((n, ROUTE_COLS), I32)],
        grid=(n // tm,),
        in_specs=[row(d), row(512), row(DIL_OUT), row(DIL_OUT), row(DIL_OUT),
                  row(DIL_OUT), row(DIL_OUT), row(DIL_OUT), row(2 * d),
                  const(wa.shape), const(wb.shape), const(wo.shape), const((1, d)),
                  const(wr.shape), const((1, ROUTE_COLS))],
        out_specs=[row(d), pl.BlockSpec((tm * (d // LANES), LANES), lambda i: (i, 0)),
                   row(ROUTE_COLS), row(ROUTE_COLS)],
        compiler_params=_cparams(("parallel",)),
        name="mix",
    )(x2, ya, *os_, *lses, gates, wa, wb, wo, ffn_norm.reshape(1, d), wr, br)


GATHER_UNROLL = 8


def _start_row_gather(src_hbm, idx_ref, nrows, dt, buf, sem, slot):
    def body(g, c):
        for u in range(GATHER_UNROLL):
            r = g * GATHER_UNROLL + u
            src = pl.multiple_of(idx_ref[0, 0, r] * dt, dt)
            dst = pl.multiple_of((slot * nrows + r) * dt, dt)
            pltpu.make_async_copy(src_hbm.at[pl.ds(src, dt)], buf.at[pl.ds(dst, dt)], sem.at[slot]).start()
        return c
    lax.fori_loop(0, nrows // GATHER_UNROLL, body, 0)


def _wait_row_gather(src_hbm, nrows, dt, buf, sem, slot):
    dst = pl.multiple_of(slot * nrows * dt, dt)
    pltpu.make_async_copy(src_hbm.at[pl.ds(0, nrows * dt)], buf.at[pl.ds(dst, nrows * dt)], sem.at[slot]).wait()


def _gather_pipeline(i, nsteps, src_hbm, cur_ref, nxt_ref, nrows, dt, buf, sem):
    slot = i % 2

    @pl.when(i == 0)
    def _():
        _start_row_gather(src_hbm, cur_ref, nrows, dt, buf, sem, 0)

    @pl.when(i + 1 < nsteps)
    def _():
        _start_row_gather(src_hbm, nxt_ref, nrows, dt, buf, sem, 1 - slot)

    _wait_row_gather(src_hbm, nrows, dt, buf, sem, slot)
    return slot


def _expert_kernel(te_ref, cur_ref, nxt_ref, t_hbm, wg_ref, wu_ref, wd_ref, y_ref, buf, sem):
    i = pl.program_id(0)
    dt = wg_ref.shape[1] // LANES
    slot = _gather_pipeline(i, pl.num_programs(0), t_hbm, cur_ref, nxt_ref, MOE_TM, dt, buf, sem)
    xt = _load_row_tiles(buf, slot * MOE_TM, MOE_TM, dt).astype(BF16)
    hg = jnp.dot(xt, wg_ref[0], preferred_element_type=F32)
    hu = jnp.dot(xt, wu_ref[0], preferred_element_type=F32)
    hid = (hg * jax.nn.sigmoid(hg)) * hu
    _store_row_tiles(y_ref, 0, jnp.dot(hid.astype(BF16), wd_ref[0], preferred_element_type=F32))


def _experts(tile_expert, row_token, t, wg, wu, wd, d):
    dt = d // LANES
    ntiles = tile_expert.shape[0]
    ff = wg.shape[2]
    tok3 = row_token.reshape(ntiles, 1, MOE_TM)
    smem_cur = pl.BlockSpec((1, 1, MOE_TM), lambda i, te: (i, 0, 0), memory_space=pltpu.SMEM)
    smem_nxt = pl.BlockSpec((1, 1, MOE_TM), lambda i, te: (jnp.minimum(i + 1, ntiles - 1), 0, 0),
                            memory_space=pltpu.SMEM)
    wspec = lambda s: pl.BlockSpec((1,) + s, lambda i, te: (te[i], 0, 0))
    return pl.pallas_call(
        _expert_kernel,
        out_shape=jax.ShapeDtypeStruct((ntiles * MOE_TM * dt, LANES), F32),
        grid_spec=pltpu.PrefetchScalarGridSpec(
            num_scalar_prefetch=1,
            grid=(ntiles,),
            in_specs=[smem_cur, smem_nxt, pl.BlockSpec(memory_space=pl.ANY),
                      wspec((d, ff)), wspec((d, ff)), wspec((ff, d))],
            out_specs=pl.BlockSpec((MOE_TM * dt, LANES), lambda i, te: (i, 0)),
            scratch_shapes=[pltpu.VMEM((2 * MOE_TM * dt, LANES), F32), pltpu.SemaphoreType.DMA((2,))],
        ),
        compiler_params=_cparams(("arbitrary",)),
        name="experts",
    )(tile_expert, tok3, tok3, t, wg, wu, wd)


def _final_kernel(cur_ref, nxt_ref, y_hbm, h_ref, rw_ref, fn_ref, o_ref, buf, sem):
    i = pl.program_id(0)
    dt = h_ref.shape[1] // LANES
    slot = _gather_pipeline(i, pl.num_programs(0), y_hbm, cur_ref, nxt_ref, 2 * FIN_TM, dt, buf, sem)
    rw = rw_ref[...]
    y0 = _load_row_tiles(buf, slot * 2 * FIN_TM, FIN_TM, dt)
    y1 = _load_row_tiles(buf, slot * 2 * FIN_TM + FIN_TM, FIN_TM, dt)
    h = h_ref[...] + rw[:, 0:1] * y0 + rw[:, 1:2] * y1
    o_ref[...] = h * lax.rsqrt(jnp.mean(h * h, axis=-1, keepdims=True) + RMS_EPS) * fn_ref[...]


def _final(pos_tiles, y_sorted, h, rw, final_norm):
    n, d = h.shape
    tm = FIN_TM
    nt = n // tm
    smem_cur = pl.BlockSpec((1, 1, 2 * tm), lambda i: (i, 0, 0), memory_space=pltpu.SMEM)
    smem_nxt = pl.BlockSpec((1, 1, 2 * tm), lambda i: (jnp.minimum(i + 1, nt - 1), 0, 0),
                            memory_space=pltpu.SMEM)
    return pl.pallas_call(
        _final_kernel,
        out_shape=jax.ShapeDtypeStruct((n, d), F32),
        grid=(nt,),
        in_specs=[smem_cur, smem_nxt, pl.BlockSpec(memory_space=pl.ANY),
                  pl.BlockSpec((tm, d), lambda i: (i, 0)), pl.BlockSpec((tm, ROUTE_COLS), lambda i: (i, 0)),
                  pl.BlockSpec((1, d), lambda i: (0, 0))],
        out_specs=pl.BlockSpec((tm, d), lambda i: (i, 0)),
        scratch_shapes=[pltpu.VMEM((2 * 2 * tm * (d // LANES), LANES), F32), pltpu.SemaphoreType.DMA((2,))],
        compiler_params=_cparams(("arbitrary",)),
        name="final",
    )(pos_tiles, pos_tiles, y_sorted, h, rw, final_norm.reshape(1, d))


def _route_plan(gid):
    n = gid.shape[0]
    e = gid.reshape(-1)
    onehot = (e[:, None] == jnp.arange(N_EXPERTS, dtype=I32)[None, :]).astype(I32)
    csum = jnp.cumsum(onehot, axis=0)
    rank = jnp.take_along_axis(csum, e[:, None], axis=1)[:, 0] - 1
    counts = csum[-1]
    padded = ((counts + MOE_TM - 1) // MOE_TM) * MOE_TM
    seg_end = jnp.cumsum(padded)
    pos = (seg_end - padded)[e] + rank
    nrows = 2 * n + N_EXPERTS * MOE_TM
    row_token = jnp.zeros((nrows,), I32).at[pos].set(jnp.arange(2 * n, dtype=I32) // 2)
    tile_start = jnp.arange(nrows // MOE_TM, dtype=I32) * MOE_TM
    tile_expert = jnp.minimum(jnp.searchsorted(seg_end, tile_start, side="right"), N_EXPERTS - 1).astype(I32)
    return row_token, tile_expert, pos.reshape(n, 2)


def _pack_w_in(w):
    d = w.shape[0]
    o_kv, o_qi, o_ki, o_wi = 512, 768, 1024, 1088
    o_dil = o_wi + IDX_HEADS
    o_gate = o_dil + 9 * 256
    pad = jnp.zeros((d, LANES - IDX_DIM - IDX_HEADS), w.dtype)
    packed = jnp.concatenate([w[:, :o_ki], w[:, o_ki:o_wi], w[:, o_wi:o_dil], pad, w[:, o_dil:o_gate],
                              w[:, o_gate:]], axis=1)
    assert packed.shape[1] == C_END
    return packed.astype(BF16)


def kernel(x, attn_norm, w_in, kv_norm, w_uk, w_uv, rel_bias, w_branch_a, w_branch_b, w_out, ffn_norm,
           w_router_group, b_router_group, w_router_expert, b_router_expert, w_gate, w_up, w_down,
           final_norm):
    b, seq, d = x.shape
    n = b * seq
    nkc = seq // BLK
    assert w_in.shape[0] == 1, "one layer"
    x2 = x.reshape(n, d)

    outs = _proj(x2, attn_norm[0], _pack_w_in(w_in[0]), kv_norm[0])
    qa, ckv, qi, kw = outs[:4]
    dil = outs[4:13]
    gates = outs[13]

    dsa_bias, dil_bias = _bias_tiles(rel_bias, nkc)

    t3 = lambda a: jnp.swapaxes(a.reshape(b, seq, a.shape[-1]), 1, 2)
    kw3 = kw.reshape(b, seq, LANES)
    wiT = jnp.swapaxes(kw3[:, :, IDX_DIM:IDX_DIM + 8], 1, 2)
    kidx = kw3[:, :, :IDX_DIM].astype(BF16)
    ckv3 = ckv.reshape(b, seq, KV_LATENT)
    ckvT = jnp.swapaxes(ckv3.reshape(b, nkc, BLK, KV_LATENT), 2, 3)
    ya = _dsa(t3(qi), wiT, t3(qa), kidx, ckv3, ckvT, w_uk[0].astype(BF16), w_uv[0].astype(BF16), dsa_bias)

    os_, lses = [], []
    for g, (_, dilation) in enumerate(DIL_GROUPS):
        r3 = lambda a: a.reshape(b, seq, DIL_OUT)
        o, lse = _dilated_group(r3(dil[g]), r3(dil[3 + g]), r3(dil[6 + g]), dil_bias, g, dilation)
        os_.append(o.reshape(n, DIL_OUT))
        lses.append(lse.reshape(n, DIL_OUT))

    wr = jnp.concatenate([w_router_group[0],
                          jnp.swapaxes(w_router_expert[0], 0, 1).reshape(d, N_EXPERTS),
                          jnp.zeros((d, ROUTE_COLS - N_GROUPS - N_EXPERTS), F32)], axis=1)
    br = jnp.concatenate([b_router_group[0], b_router_expert[0].reshape(-1),
                          jnp.zeros((ROUTE_COLS - N_GROUPS - N_EXPERTS,), F32)]).reshape(1, ROUTE_COLS)
    h, t, rw, ri = _mix(x2, ya.reshape(n, -1), os_, lses, gates,
                        w_branch_a[0].astype(BF16), w_branch_b[0].astype(BF16), w_out[0].astype(BF16),
                        ffn_norm[0], wr, br)

    row_token, tile_expert, pos = _route_plan(ri[:, :2])
    y_sorted = _experts(tile_expert, row_token, t, w_gate[0].astype(BF16), w_up[0].astype(BF16),
                        w_down[0].astype(BF16), d)

    pos_tiles = jnp.swapaxes(pos.reshape(n // FIN_TM, FIN_TM, 2), 1, 2).reshape(n // FIN_TM, 1, 2 * FIN_TM)
    out = _final(pos_tiles, y_sorted, h, rw, final_norm)
    return out.reshape(b, seq, d)
```

```python
import functools
import math

import numpy as np
import jax
import jax.numpy as jnp
from jax import lax
from jax.experimental import pallas as pl
from jax.experimental.pallas import tpu as pltpu

F32 = jnp.float32
BF16 = jnp.bfloat16
I32 = jnp.int32
I16 = jnp.int16

LANES = 128
VMEM_LIMIT_BYTES = 56 * 1024 * 1024

HEAD_DIM = 64
DSA_HEADS = 8
KV_LATENT = 256
IDX_HEADS = 4
IDX_DIM = 64
TOPK_MAX = 256
DIL_GROUPS = ((128, 1), (512, 4), (2048, 16))
DIL_HPG = 4
DIL_OUT = DIL_HPG * HEAD_DIM
NUM_BUCKETS = 32
MAX_DISTANCE = 2048
N_GROUPS = 4
EXPERTS_PER_GROUP = 8
N_EXPERTS = N_GROUPS * EXPERTS_PER_GROUP
RMS_EPS = 1e-6
NEG = -1e30
INT_MIN = -2 ** 31
I16_MIN = -2 ** 15

BLK = 128
PROJ_TM = 512
MIX_TM = 256
MOE_TM = 256
FIN_TM = 256
ROUTE_COLS = 128


def _cparams(sem):
    return pltpu.CompilerParams(dimension_semantics=sem, vmem_limit_bytes=VMEM_LIMIT_BYTES)


C_QA = 0
C_KV = 512
C_QI = 768
C_KW = 1024
C_DIL = 1152
C_GATE = C_DIL + 9 * 256
C_END = C_GATE + 2048


def _proj_kernel(x_ref, g_ref, w_ref, kvg_ref, qa_ref, ckv_ref, qi_ref, kw_ref, *rest):
    dil_refs = rest[:9]
    gate_ref = rest[9]
    x = x_ref[...]
    u = x * lax.rsqrt(jnp.mean(x * x, axis=-1, keepdims=True) + RMS_EPS) * g_ref[...]
    u = u.astype(BF16)

    def mm(a, b):
        return jnp.dot(u, w_ref[:, a:b], preferred_element_type=F32)

    qa_ref[...] = mm(C_QA, C_KV).astype(BF16)
    c = mm(C_KV, C_QI)
    c = c * lax.rsqrt(jnp.mean(c * c, axis=-1, keepdims=True) + RMS_EPS) * kvg_ref[...]
    ckv_ref[...] = c.astype(BF16)
    qi_ref[...] = mm(C_QI, C_KW).astype(BF16)
    kw_ref[...] = mm(C_KW, C_DIL)
    for j in range(9):
        dil_refs[j][...] = mm(C_DIL + 256 * j, C_DIL + 256 * (j + 1)).astype(BF16)
    for j in range(4):
        gate_ref[:, 512 * j:512 * (j + 1)] = mm(C_GATE + 512 * j, C_GATE + 512 * (j + 1)).astype(BF16)


def _proj(x2, attn_norm, w_packed, kv_norm):
    n, d = x2.shape
    tm = PROJ_TM
    row = lambda i: (i, 0)
    const = lambda i: (0, 0)
    outs = [jax.ShapeDtypeStruct((n, 512), BF16), jax.ShapeDtypeStruct((n, 256), BF16),
            jax.ShapeDtypeStruct((n, 256), BF16), jax.ShapeDtypeStruct((n, 128), F32)]
    outs += [jax.ShapeDtypeStruct((n, 256), BF16)] * 9
    outs += [jax.ShapeDtypeStruct((n, 2048), BF16)]
    out_specs = [pl.BlockSpec((tm, s.shape[1]), row) for s in outs]
    return pl.pallas_call(
        _proj_kernel,
        out_shape=outs,
        grid=(n // tm,),
        in_specs=[pl.BlockSpec((tm, d), row), pl.BlockSpec((1, d), const),
                  pl.BlockSpec((d, C_END), const), pl.BlockSpec((1, KV_LATENT), const)],
        out_specs=out_specs,
        compiler_params=_cparams(("parallel",)),
        name="proj",
    )(x2, attn_norm.reshape(1, d), w_packed, kv_norm.reshape(1, KV_LATENT))


def _bucket_thresholds():
    max_exact = NUM_BUCKETS // 2
    d = np.arange(0, MAX_DISTANCE + 1)
    nf = np.maximum(d, 1).astype(np.float32)
    large = max_exact + (np.log(nf / np.float32(max_exact)) / np.float32(math.log(MAX_DISTANCE / max_exact))
                         * np.float32(NUM_BUCKETS - max_exact)).astype(np.int32)
    large = np.minimum(large, NUM_BUCKETS - 1)
    bucket = np.where(d < max_exact, d, large)
    assert np.all(np.diff(bucket) >= 0)
    return [int(np.argmax(bucket >= b)) for b in range(1, NUM_BUCKETS)]


_BUCKET_THR = _bucket_thresholds()


def _bias_from_distance(dist, tab_ref, heads):
    masks = [dist >= t for t in _BUCKET_THR]
    out = []
    for h in heads:
        v = jnp.full(dist.shape, tab_ref[0, h], F32)
        for b in range(1, NUM_BUCKETS):
            v = jnp.where(masks[b - 1], tab_ref[b, h], v)
        out.append(v)
    return out


def _dsa_bias_kernel(tab_ref, o_ref):
    delta = pl.program_id(0)
    j = lax.broadcasted_iota(I32, (BLK, BLK), 0)
    i = lax.broadcasted_iota(I32, (BLK, BLK), 1)
    dist = jnp.maximum(delta * BLK + i - j, 0)
    tiles = _bias_from_distance(dist, tab_ref, range(DSA_HEADS))
    for h in range(DSA_HEADS):
        o_ref[0, h] = tiles[h]


def _dil_bias_kernel(tab_ref, o_ref, *, dilations):
    g = pl.program_id(0)
    i = lax.broadcasted_iota(I32, (BLK, 2 * BLK), 0)
    j = lax.broadcasted_iota(I32, (BLK, 2 * BLK), 1)
    step = i + BLK - j
    valid = (step >= 0) & (step <= BLK)
    for gi, r in enumerate(dilations):
        @pl.when(g == gi)
        def _():
            dist = jnp.maximum(step, 0) * r
            heads = [DSA_HEADS + gi * DIL_HPG + hh for hh in range(DIL_HPG)]
            tiles = _bias_from_distance(dist, tab_ref, heads)
            for hh in range(DIL_HPG):
                o_ref[0, hh] = jnp.where(valid, tiles[hh], NEG)


def _bias_tiles(rel_bias, nkc):
    smem = pl.BlockSpec(memory_space=pltpu.SMEM)
    dsa = pl.pallas_call(
        _dsa_bias_kernel,
        out_shape=jax.ShapeDtypeStruct((nkc, DSA_HEADS, BLK, BLK), F32),
        grid=(nkc,),
        in_specs=[smem],
        out_specs=pl.BlockSpec((1, DSA_HEADS, BLK, BLK), lambda d: (d, 0, 0, 0)),
        compiler_params=_cparams(("parallel",)),
        name="dsa_bias",
    )(rel_bias)
    dil = pl.pallas_call(
        functools.partial(_dil_bias_kernel, dilations=tuple(r for _, r in DIL_GROUPS)),
        out_shape=jax.ShapeDtypeStruct((len(DIL_GROUPS), DIL_HPG, BLK, 2 * BLK), F32),
        grid=(len(DIL_GROUPS),),
        in_specs=[smem],
        out_specs=pl.BlockSpec((1, DIL_HPG, BLK, 2 * BLK), lambda g: (g, 0, 0, 0)),
        compiler_params=_cparams(("parallel",)),
        name="dil_bias",
    )(rel_bias)
    return dsa, dil


SUP = 4
SROWS = SUP * BLK


def _fold_rows(x, rows):
    parts = [x[i:i + rows] for i in range(0, x.shape[0], rows)]
    while len(parts) > 1:
        parts = [parts[i] + parts[i + 1] for i in range(0, len(parts), 2)]
    return parts[0]


def _dsa_kernel(qiT_ref, wiT_ref, qaT_ref, kidx_ref, ckv_ref, ckvT_ref, wuk_ref, wuv_ref, bias_ref,
                y_ref, keys_ref, hi_ref, lo_ref, qlT_ref, x_ref, am_ref, pT_ref, acc_ref, *, topk, idx_bits):
    qb = pl.program_id(1)
    nsc = lax.shift_right_logical(qb, SUP.bit_length() - 1) + 1
    seq = keys_ref.shape[0]
    row = lax.broadcasted_iota(I32, (SROWS, BLK), 0)
    col = lax.broadcasted_iota(I32, (SROWS, BLK), 1)

    def causal(sc):
        return row <= col + (qb * BLK - sc * SROWS)

    for h in range(DSA_HEADS):
        ql = jnp.dot(wuk_ref[h], qaT_ref[0, h * HEAD_DIM:(h + 1) * HEAD_DIM, :],
                     preferred_element_type=F32) * (HEAD_DIM ** -0.5)
        qlT_ref[:, h * BLK:(h + 1) * BLK] = ql.astype(BF16)

    wq = wiT_ref[0] * (IDX_HEADS ** -0.5)

    def score_body(sc, carry):
        off = pl.multiple_of(sc * SROWS, SROWS)
        kx = kidx_ref[0, pl.ds(off, SROWS), :]
        acc = jnp.zeros((SROWS, BLK), F32)
        for h in range(IDX_HEADS):
            s = jnp.dot(kx, qiT_ref[0, h * IDX_DIM:(h + 1) * IDX_DIM, :],
                        preferred_element_type=F32) * (IDX_DIM ** -0.5)
            acc = acc + wq[h:h + 1, :] * jnp.maximum(s, 0.0)
        acc = jnp.where(causal(sc), acc, NEG)
        bits = pltpu.bitcast(acc, I32)
        bits = jnp.where(bits == INT_MIN, 0, bits)
        key = bits ^ ((bits >> 31) & 0x7FFFFFFF)
        keys_ref[pl.ds(off, SROWS), :] = key
        hi_ref[pl.ds(off, SROWS), :] = (key >> 16).astype(I16)
        return carry

    lax.fori_loop(0, nsc, score_body, 0)

    def count(pred):
        def body(sc, cnt):
            k = keys_ref[pl.ds(pl.multiple_of(sc * SROWS, SROWS), SROWS), :]
            ones = jnp.where(pred(sc, k), 1, 0)
            return cnt + jnp.sum(ones.reshape(SROWS // 8, 8, BLK), axis=0)
        cnt = lax.fori_loop(0, nsc, body, jnp.zeros((8, BLK), I32))
        return jnp.sum(cnt, axis=0, keepdims=True)

    def count16(ref, pred):
        def body(sc, cnt):
            v = ref[pl.ds(pl.multiple_of(sc * SROWS, SROWS), SROWS), :]
            ones = jnp.where(pred(v), jnp.int16(1), jnp.int16(0))
            return cnt + _fold_rows(ones, 16)
        cnt = lax.fori_loop(0, nsc, body, jnp.zeros((16, BLK), I16))
        return jnp.sum(cnt.astype(I32), axis=0, keepdims=True)

    def kth_largest16(ref, need, n_all):
        def body(it, carry):
            lo, n_lo = carry
            cand = lo + jnp.left_shift(jnp.int32(1), 15 - it)
            cand16 = cand.astype(I16)
            c = count16(ref, lambda v: v >= cand16)
            take = c >= need
            return jnp.where(take, cand, lo), jnp.where(take, c, n_lo)
        return lax.fori_loop(0, 16, body, (jnp.full((1, BLK), I16_MIN, I32), n_all))

    n_keys = jnp.zeros((1, BLK), I32) + nsc * SROWS
    t_hi, n_ge_hi = kth_largest16(hi_ref, topk, n_keys)
    t_hi16 = t_hi.astype(I16)
    n_gt_hi = count16(hi_ref, lambda v: v > t_hi16)

    def low_body(sc, carry):
        off = pl.multiple_of(sc * SROWS, SROWS)
        k = keys_ref[pl.ds(off, SROWS), :]
        low = (k & 0xFFFF) + I16_MIN
        lo_ref[pl.ds(off, SROWS), :] = jnp.where((k >> 16) == t_hi, low, I16_MIN).astype(I16)
        return carry

    lax.fori_loop(0, nsc, low_body, 0)
    t_lo, n_ge_lo = kth_largest16(lo_ref, topk - n_gt_hi, n_ge_hi - n_gt_hi)
    thr = jnp.left_shift(t_hi, 16) | (t_lo - I16_MIN)
    n_ge = n_gt_hi + n_ge_lo

    def find_cut():
        need = topk - count(lambda sc, k: k > thr)

        def cut_body(it, lo):
            cand = lo + jnp.left_shift(jnp.int32(1), idx_bits - 1 - it)
            c = count(lambda sc, k: (k == thr) & (row < cand - sc * SROWS))
            return jnp.where(c < need, cand, lo)

        return lax.fori_loop(0, idx_bits, cut_body, jnp.zeros((1, BLK), I32))

    cut = lax.cond(jnp.max(n_ge) > topk, find_cut, lambda: jnp.full((1, BLK), seq, I32))

    def logit_body(sc, m):
        off = pl.multiple_of(sc * SROWS, SROWS)
        k = keys_ref[pl.ds(off, SROWS), :]
        sel = ((k > thr) | ((k == thr) & (row <= cut - sc * SROWS))) & causal(sc)
        am_ref[...] = jnp.where(sel, 0.0, NEG)
        ck = ckv_ref[0, pl.ds(off, SROWS), :]
        new_m = []
        for hp in range(DSA_HEADS // 2):
            lg2 = jnp.dot(ck, qlT_ref[:, 2 * hp * BLK:(2 * hp + 2) * BLK], preferred_element_type=F32)
            for h in (2 * hp, 2 * hp + 1):
                hs = slice(h * BLK, (h + 1) * BLK)
                lg = lg2[:, (h % 2) * BLK:(h % 2 + 1) * BLK]
                mh = m[h]
                for j in range(SUP):
                    rs = slice(j * BLK, (j + 1) * BLK)
                    delta = jnp.maximum(qb - (sc * SUP + j), 0)
                    x = lg[rs] + bias_ref[delta, h] + am_ref[rs, :]
                    x_ref[pl.ds(off + j * BLK, BLK), hs] = x
                    mh = jnp.maximum(mh, jnp.max(x.reshape(BLK // 8, 8, BLK), axis=0))
                new_m.append(mh)
        return tuple(new_m)

    m8 = lax.fori_loop(0, nsc, logit_body, tuple(jnp.full((8, BLK), NEG, F32) for _ in range(DSA_HEADS)))
    m = [jnp.max(v, axis=0, keepdims=True) for v in m8]

    acc_ref[...] = jnp.zeros(acc_ref.shape, F32)

    def prob_body(sc, l):
        off = pl.multiple_of(sc * SROWS, SROWS)
        new_l = []
        for h in range(DSA_HEADS):
            hs = slice(h * BLK, (h + 1) * BLK)
            p = jnp.exp(x_ref[pl.ds(off, SROWS), hs] - m[h])
            pT_ref[:, hs] = p.astype(BF16)
            new_l.append(l[h] + jnp.sum(p.reshape(SROWS // 8, 8, BLK), axis=0))
        acc_ref[...] += jnp.dot(ckvT_ref[0, sc], pT_ref[...], preferred_element_type=F32)
        return tuple(new_l)

    l8 = lax.fori_loop(0, nsc, prob_body, tuple(jnp.zeros((8, BLK), F32) for _ in range(DSA_HEADS)))

    for h in range(DSA_HEADS):
        inv = 1.0 / jnp.sum(l8[h], axis=0, keepdims=True)
        oh = (acc_ref[:, h * BLK:(h + 1) * BLK] * inv).T.astype(BF16)
        yh = jnp.dot(oh, wuv_ref[h], preferred_element_type=F32)
        y_ref[0, :, h * HEAD_DIM:(h + 1) * HEAD_DIM] = yh.astype(BF16)


def _dsa(qiT, wiT, qaT, kidx, ckv, ckvT, wuk, wuv, bias_tiles):
    b, seq, _ = ckv.shape
    nkc = seq // BLK
    assert nkc % SUP == 0
    topk = min(TOPK_MAX, seq // 4)
    idx_bits = int(math.log2(seq))
    assert 2 ** idx_bits == seq
    qblk = lambda rows: pl.BlockSpec((1, rows, BLK), lambda bi, qi: (bi, 0, qi))
    full3 = lambda s: pl.BlockSpec(s, lambda bi, qi: (bi, 0, 0))
    const = lambda s: pl.BlockSpec(s, lambda bi, qi: (0,) * len(s))
    hl = DSA_HEADS * BLK
    return pl.pallas_call(
        functools.partial(_dsa_kernel, topk=topk, idx_bits=idx_bits),
        out_shape=jax.ShapeDtypeStruct((b, seq, DSA_HEADS * HEAD_DIM), BF16),
        grid=(b, nkc),
        in_specs=[qblk(IDX_HEADS * IDX_DIM), qblk(8), qblk(DSA_HEADS * HEAD_DIM),
                  full3((1, seq, IDX_DIM)), full3((1, seq, KV_LATENT)),
                  pl.BlockSpec((1, nkc // SUP, KV_LATENT, SROWS), lambda bi, qi: (bi, 0, 0, 0)),
                  const((DSA_HEADS, KV_LATENT, HEAD_DIM)), const((DSA_HEADS, KV_LATENT, HEAD_DIM)),
                  const((nkc, DSA_HEADS, BLK, BLK))],
        out_specs=pl.BlockSpec((1, BLK, DSA_HEADS * HEAD_DIM), lambda bi, qi: (bi, qi, 0)),
        scratch_shapes=[pltpu.VMEM((seq, BLK), I32),
                        pltpu.VMEM((seq, BLK), I16),
                        pltpu.VMEM((seq, BLK), I16),
                        pltpu.VMEM((KV_LATENT, hl), BF16),
                        pltpu.VMEM((seq, hl), F32),
                        pltpu.VMEM((SROWS, BLK), F32),
                        pltpu.VMEM((SROWS, hl), BF16),
                        pltpu.VMEM((KV_LATENT, hl), F32)],
        compiler_params=_cparams(("parallel", "arbitrary")),
        name="dsa",
    )(qiT, wiT, qaT, kidx, ckv, ckvT, wuk, wuv, bias_tiles)


def _dil_kernel(q_ref, k_ref, v_ref, bm_ref, o_ref, lse_ref, *, nblk):
    def block(qo, ko, nkeys, bcol):
        for hh in range(DIL_HPG):
            hs = slice(hh * HEAD_DIM, (hh + 1) * HEAD_DIM)
            qh = q_ref[0, pl.ds(qo, BLK), hs]
            kh = k_ref[0, pl.ds(ko, nkeys), hs]
            vh = v_ref[0, pl.ds(ko, nkeys), hs]
            s = lax.dot_general(qh, kh, (((1,), (1,)), ((), ())), preferred_element_type=F32)
            s = s * (HEAD_DIM ** -0.5) + bm_ref[0, hh, :, bcol:bcol + nkeys]
            m = jnp.max(s, axis=-1, keepdims=True)
            e = jnp.exp(s - m)
            l = jnp.sum(e, axis=-1, keepdims=True)
            o = jnp.dot(e.astype(BF16), vh, preferred_element_type=F32) * (1.0 / l)
            o_ref[0, pl.ds(qo, BLK), hs] = o.astype(BF16)
            lse_ref[0, pl.ds(qo, BLK), hs] = jnp.broadcast_to(m + jnp.log(l), (BLK, HEAD_DIM))

    block(0, 0, BLK, BLK)

    def body(n, carry):
        block(pl.multiple_of(n * BLK, BLK), pl.multiple_of((n - 1) * BLK, BLK), 2 * BLK, 0)
        return carry

    lax.fori_loop(1, nblk, body, 0)


def _dilated_group(q, k, v, bm, g, dilation):
    b, seq, c = q.shape
    ls = seq // dilation
    view = lambda a: a.reshape(b, ls, dilation * c)
    blk = pl.BlockSpec((1, ls, c), lambda bi, ri: (bi, 0, ri))
    o, lse = pl.pallas_call(
        functools.partial(_dil_kernel, nblk=ls // BLK),
        out_shape=[jax.ShapeDtypeStruct((b, ls, dilation * c), BF16),
                   jax.ShapeDtypeStruct((b, ls, dilation * c), F32)],
        grid=(b, dilation),
        in_specs=[blk, blk, blk, pl.BlockSpec((1, DIL_HPG, BLK, 2 * BLK), lambda bi, ri: (g, 0, 0, 0))],
        out_specs=[blk, blk],
        compiler_params=_cparams(("parallel", "parallel")),
        name=f"dilated_g{g}",
    )(view(q), view(k), view(v), bm)
    return o.reshape(b, seq, c), lse.reshape(b, seq, c)


def _store_row_tiles(ref, base, val):
    rows, d = val.shape
    dt = d // LANES
    for s in range(dt):
        ref[pl.ds(base * dt + s, rows, stride=dt), :] = val[:, s * LANES:(s + 1) * LANES]


def _load_row_tiles(ref, base, rows, dt):
    return jnp.concatenate([ref[pl.ds(base * dt + s, rows, stride=dt), :] for s in range(dt)], axis=1)


def _mix_kernel(x_ref, ya_ref, o1_ref, o2_ref, o3_ref, l1_ref, l2_ref, l3_ref, gate_ref,
                wa_ref, wb_ref, wo_ref, fg_ref, wr_ref, br_ref,
                h_ref, t_ref, rw_ref, ri_ref):
    l1, l2, l3 = l1_ref[...], l2_ref[...], l3_ref[...]
    mx = jnp.maximum(jnp.maximum(l1, l2), l3)
    e1, e2, e3 = jnp.exp(l1 - mx), jnp.exp(l2 - mx), jnp.exp(l3 - mx)
    inv = 1.0 / (e1 + e2 + e3)
    yb = ((e1 * inv) * o1_ref[...].astype(F32) + (e2 * inv) * o2_ref[...].astype(F32)
          + (e3 * inv) * o3_ref[...].astype(F32))
    a = jnp.dot(ya_ref[...], wa_ref[...], preferred_element_type=F32)
    bmix = jnp.dot(yb.astype(BF16), wb_ref[...], preferred_element_type=F32)
    d = a.shape[1]
    g0 = jax.nn.sigmoid(gate_ref[:, :d].astype(F32))
    g1 = jax.nn.sigmoid(gate_ref[:, d:].astype(F32))
    mixed = g0 * a + g1 * bmix
    h = x_ref[...] + jnp.dot(mixed.astype(BF16), wo_ref[...], preferred_element_type=F32)
    h_ref[...] = h
    t = h * lax.rsqrt(jnp.mean(h * h, axis=-1, keepdims=True) + RMS_EPS) * fg_ref[...]
    _store_row_tiles(t_ref, 0, t)

    logits = jnp.dot(t, wr_ref[...], preferred_element_type=F32,
                     precision=lax.Precision.HIGHEST) + br_ref[...]
    lane = lax.broadcasted_iota(I32, logits.shape, 1)
    ninf = -jnp.inf
    big = jnp.int32(10 ** 6)

    def first_argmax(v, vmax):
        return jnp.min(jnp.where(v == vmax, lane, big), axis=-1, keepdims=True)

    gl = jnp.where(lane < N_GROUPS, logits, ninf)
    gmax = jnp.max(gl, axis=-1, keepdims=True)
    gsel = first_argmax(gl, gmax)
    p_g = 1.0 / jnp.sum(jnp.exp(gl - gmax), axis=-1, keepdims=True)
    lo = N_GROUPS + gsel * EXPERTS_PER_GROUP
    el = jnp.where((lane >= lo) & (lane < lo + EXPERTS_PER_GROUP), logits, ninf)
    v1 = jnp.max(el, axis=-1, keepdims=True)
    i1 = first_argmax(el, v1)
    el2 = jnp.where(lane == i1, ninf, el)
    v2 = jnp.max(el2, axis=-1, keepdims=True)
    i2 = first_argmax(el2, v2)
    e2 = jnp.exp(v2 - v1)
    w1 = p_g / (1.0 + e2)
    w2 = p_g * e2 / (1.0 + e2)
    rw_ref[...] = jnp.where(lane == 0, w1, jnp.where(lane == 1, w2, 0.0))
    ri_ref[...] = jnp.where(lane == 0, i1 - N_GROUPS, jnp.where(lane == 1, i2 - N_GROUPS, 0))


def _mix(x2, ya, os_, lses, gates, wa, wb, wo, ffn_norm, wr, br):
    n, d = x2.shape
    tm = MIX_TM
    row = lambda c: pl.BlockSpec((tm, c), lambda i: (i, 0))
    const = lambda s: pl.BlockSpec(s, lambda i: (0, 0))
    return pl.pallas_call(
        _mix_kernel,
        out_shape=[jax.ShapeDtypeStruct((n, d), F32), jax.ShapeDtypeStruct((n * (d // LANES), LANES), F32),
                   jax.ShapeDtypeStruct((n, ROUTE_COLS), F32), jax.ShapeDtypeStruct((n, ROUTE_COLS), I32)],
        grid=(n // tm,),
        in_specs=[row(d), row(512), row(DIL_OUT), row(DIL_OUT), row(DIL_OUT),
                  row(DIL_OUT), row(DIL_OUT), row(DIL_OUT), row(2 * d),
                  const(wa.shape), const(wb.shape), const(wo.shape), const((1, d)),
                  const(wr.shape), const((1, ROUTE_COLS))],
        out_specs=[row(d), pl.BlockSpec((tm * (d // LANES), LANES), lambda i: (i, 0)),
                   row(ROUTE_COLS), row(ROUTE_COLS)],
        compiler_params=_cparams(("parallel",)),
        name="mix",
    )(x2, ya, *os_, *lses, gates, wa, wb, wo, ffn_norm.reshape(1, d), wr, br)


GATHER_UNROLL = 8


def _start_row_gather(src_hbm, idx_ref, nrows, dt, buf, sem, slot):
    def body(g, c):
        for u in range(GATHER_UNROLL):
            r = g * GATHER_UNROLL + u
            src = pl.multiple_of(idx_ref[0, 0, r] * dt, dt)
            dst = pl.multiple_of((slot * nrows + r) * dt, dt)
            pltpu.make_async_copy(src_hbm.at[pl.ds(src, dt)], buf.at[pl.ds(dst, dt)], sem.at[slot]).start()
        return c
    lax.fori_loop(0, nrows // GATHER_UNROLL, body, 0)


def _wait_row_gather(src_hbm, nrows, dt, buf, sem, slot):
    dst = pl.multiple_of(slot * nrows * dt, dt)
    pltpu.make_async_copy(src_hbm.at[pl.ds(0, nrows * dt)], buf.at[pl.ds(dst, nrows * dt)], sem.at[slot]).wait()


def _gather_pipeline(i, nsteps, src_hbm, cur_ref, nxt_ref, nrows, dt, buf, sem):
    slot = i % 2

    @pl.when(i == 0)
    def _():
        _start_row_gather(src_hbm, cur_ref, nrows, dt, buf, sem, 0)

    @pl.when(i + 1 < nsteps)
    def _():
        _start_row_gather(src_hbm, nxt_ref, nrows, dt, buf, sem, 1 - slot)

    _wait_row_gather(src_hbm, nrows, dt, buf, sem, slot)
    return slot


def _expert_kernel(te_ref, cur_ref, nxt_ref, t_hbm, wg_ref, wu_ref, wd_ref, y_ref, buf, sem):
    i = pl.program_id(0)
    dt = wg_ref.shape[1] // LANES
    slot = _gather_pipeline(i, pl.num_programs(0), t_hbm, cur_ref, nxt_ref, MOE_TM, dt, buf, sem)
    xt = _load_row_tiles(buf, slot * MOE_TM, MOE_TM, dt).astype(BF16)
    hg = jnp.dot(xt, wg_ref[0], preferred_element_type=F32)
    hu = jnp.dot(xt, wu_ref[0], preferred_element_type=F32)
    hid = (hg * jax.nn.sigmoid(hg)) * hu
    _store_row_tiles(y_ref, 0, jnp.dot(hid.astype(BF16), wd_ref[0], preferred_element_type=F32))


def _experts(tile_expert, row_token, t, wg, wu, wd, d):
    dt = d // LANES
    ntiles = tile_expert.shape[0]
    ff = wg.shape[2]
    tok3 = row_token.reshape(ntiles, 1, MOE_TM)
    smem_cur = pl.BlockSpec((1, 1, MOE_TM), lambda i, te: (i, 0, 0), memory_space=pltpu.SMEM)
    smem_nxt = pl.BlockSpec((1, 1, MOE_TM), lambda i, te: (jnp.minimum(i + 1, ntiles - 1), 0, 0),
                            memory_space=pltpu.SMEM)
    wspec = lambda s: pl.BlockSpec((1,) + s, lambda i, te: (te[i], 0, 0))
    return pl.pallas_call(
        _expert_kernel,
        out_shape=jax.ShapeDtypeStruct((ntiles * MOE_TM * dt, LANES), F32),
        grid_spec=pltpu.PrefetchScalarGridSpec(
            num_scalar_prefetch=1,
            grid=(ntiles,),
            in_specs=[smem_cur, smem_nxt, pl.BlockSpec(memory_space=pl.ANY),
                      wspec((d, ff)), wspec((d, ff)), wspec((ff, d))],
            out_specs=pl.BlockSpec((MOE_TM * dt, LANES), lambda i, te: (i, 0)),
            scratch_shapes=[pltpu.VMEM((2 * MOE_TM * dt, LANES), F32), pltpu.SemaphoreType.DMA((2,))],
        ),
        compiler_params=_cparams(("arbitrary",)),
        name="experts",
    )(tile_expert, tok3, tok3, t, wg, wu, wd)


def _final_kernel(cur_ref, nxt_ref, y_hbm, h_ref, rw_ref, fn_ref, o_ref, buf, sem):
    i = pl.program_id(0)
    dt = h_ref.shape[1] // LANES
    slot = _gather_pipeline(i, pl.num_programs(0), y_hbm, cur_ref, nxt_ref, 2 * FIN_TM, dt, buf, sem)
    rw = rw_ref[...]
    y0 = _load_row_tiles(buf, slot * 2 * FIN_TM, FIN_TM, dt)
    y1 = _load_row_tiles(buf, slot * 2 * FIN_TM + FIN_TM, FIN_TM, dt)
    h = h_ref[...] + rw[:, 0:1] * y0 + rw[:, 1:2] * y1
    o_ref[...] = h * lax.rsqrt(jnp.mean(h * h, axis=-1, keepdims=True) + RMS_EPS) * fn_ref[...]


def _final(pos_tiles, y_sorted, h, rw, final_norm):
    n, d = h.shape
    tm = FIN_TM
    nt = n // tm
    smem_cur = pl.BlockSpec((1, 1, 2 * tm), lambda i: (i, 0, 0), memory_space=pltpu.SMEM)
    smem_nxt = pl.BlockSpec((1, 1, 2 * tm), lambda i: (jnp.minimum(i + 1, nt - 1), 0, 0),
                            memory_space=pltpu.SMEM)
    return pl.pallas_call(
        _final_kernel,
        out_shape=jax.ShapeDtypeStruct((n, d), F32),
        grid=(nt,),
        in_specs=[smem_cur, smem_nxt, pl.BlockSpec(memory_space=pl.ANY),
                  pl.BlockSpec((tm, d), lambda i: (i, 0)), pl.BlockSpec((tm, ROUTE_COLS), lambda i: (i, 0)),
                  pl.BlockSpec((1, d), lambda i: (0, 0))],
        out_specs=pl.BlockSpec((tm, d), lambda i: (i, 0)),
        scratch_shapes=[pltpu.VMEM((2 * 2 * tm * (d // LANES), LANES), F32), pltpu.SemaphoreType.DMA((2,))],
        compiler_params=_cparams(("arbitrary",)),
        name="final",
    )(pos_tiles, pos_tiles, y_sorted, h, rw, final_norm.reshape(1, d))


def _route_plan(gid):
    n = gid.shape[0]
    e = gid.reshape(-1)
    onehot = (e[:, None] == jnp.arange(N_EXPERTS, dtype=I32)[None, :]).astype(I32)
    csum = jnp.cumsum(onehot, axis=0)
    rank = jnp.take_along_axis(csum, e[:, None], axis=1)[:, 0] - 1
    counts = csum[-1]
    padded = ((counts + MOE_TM - 1) // MOE_TM) * MOE_TM
    seg_end = jnp.cumsum(padded)
    pos = (seg_end - padded)[e] + rank
    nrows = 2 * n + N_EXPERTS * MOE_TM
    row_token = jnp.zeros((nrows,), I32).at[pos].set(jnp.arange(2 * n, dtype=I32) // 2)
    tile_start = jnp.arange(nrows // MOE_TM, dtype=I32) * MOE_TM
    tile_expert = jnp.minimum(jnp.searchsorted(seg_end, tile_start, side="right"), N_EXPERTS - 1).astype(I32)
    return row_token, tile_expert, pos.reshape(n, 2)


def _pack_w_in(w):
    d = w.shape[0]
    o_kv, o_qi, o_ki, o_wi = 512, 768, 1024, 1088
    o_dil = o_wi + IDX_HEADS
    o_gate = o_dil + 9 * 256
    pad = jnp.zeros((d, LANES - IDX_DIM - IDX_HEADS), w.dtype)
    packed = jnp.concatenate([w[:, :o_ki], w[:, o_ki:o_wi], w[:, o_wi:o_dil], pad, w[:, o_dil:o_gate],
                              w[:, o_gate:]], axis=1)
    assert packed.shape[1] == C_END
    return packed.astype(BF16)


def kernel(x, attn_norm, w_in, kv_norm, w_uk, w_uv, rel_bias, w_branch_a, w_branch_b, w_out, ffn_norm,
           w_router_group, b_router_group, w_router_expert, b_router_expert, w_gate, w_up, w_down,
           final_norm):
    b, seq, d = x.shape
    n = b * seq
    nkc = seq // BLK
    assert w_in.shape[0] == 1, "one layer"
    x2 = x.reshape(n, d)

    outs = _proj(x2, attn_norm[0], _pack_w_in(w_in[0]), kv_norm[0])
    qa, ckv, qi, kw = outs[:4]
    dil = outs[4:13]
    gates = outs[13]

    dsa_bias, dil_bias = _bias_tiles(rel_bias, nkc)

    t3 = lambda a: jnp.swapaxes(a.reshape(b, seq, a.shape[-1]), 1, 2)
    kw3 = kw.reshape(b, seq, LANES)
    wiT = jnp.swapaxes(kw3[:, :, IDX_DIM:IDX_DIM + 8], 1, 2)
    kidx = kw3[:, :, :IDX_DIM].astype(BF16)
    ckv3 = ckv.reshape(b, seq, KV_LATENT)
    ckvT = jnp.swapaxes(ckv3.reshape(b, nkc // SUP, SROWS, KV_LATENT), 2, 3)
    ya = _dsa(t3(qi), wiT, t3(qa), kidx, ckv3, ckvT, w_uk[0].astype(BF16), w_uv[0].astype(BF16), dsa_bias)

    os_, lses = [], []
    for g, (_, dilation) in enumerate(DIL_GROUPS):
        r3 = lambda a: a.reshape(b, seq, DIL_OUT)
        o, lse = _dilated_group(r3(dil[g]), r3(dil[3 + g]), r3(dil[6 + g]), dil_bias, g, dilation)
        os_.append(o.reshape(n, DIL_OUT))
        lses.append(lse.reshape(n, DIL_OUT))

    wr = jnp.concatenate([w_router_group[0],
                          jnp.swapaxes(w_router_expert[0], 0, 1).reshape(d, N_EXPERTS),
                          jnp.zeros((d, ROUTE_COLS - N_GROUPS - N_EXPERTS), F32)], axis=1)
    br = jnp.concatenate([b_router_group[0], b_router_expert[0].reshape(-1),
                          jnp.zeros((ROUTE_COLS - N_GROUPS - N_EXPERTS,), F32)]).reshape(1, ROUTE_COLS)
    h, t, rw, ri = _mix(x2, ya.reshape(n, -1), os_, lses, gates,
                        w_branch_a[0].astype(BF16), w_branch_b[0].astype(BF16), w_out[0].astype(BF16),
                        ffn_norm[0], wr, br)

    row_token, tile_expert, pos = _route_plan(ri[:, :2])
    y_sorted = _experts(tile_expert, row_token, t, w_gate[0].astype(BF16), w_up[0].astype(BF16),
                        w_down[0].astype(BF16), d)

    pos_tiles = jnp.swapaxes(pos.reshape(n // FIN_TM, FIN_TM, 2), 1, 2).reshape(n // FIN_TM, 1, 2 * FIN_TM)
    out = _final(pos_tiles, y_sorted, h, rw, final_norm)
    return out.reshape(b, seq, d)
```

```python
import functools
import math

import numpy as np
import jax
import jax.numpy as jnp
from jax import lax
from jax.experimental import pallas as pl
from jax.experimental.pallas import tpu as pltpu

F32 = jnp.float32
BF16 = jnp.bfloat16
I32 = jnp.int32
I16 = jnp.int16

LANES = 128
VMEM_LIMIT_BYTES = 56 * 1024 * 1024

HEAD_DIM = 64
DSA_HEADS = 8
KV_LATENT = 256
IDX_HEADS = 4
IDX_DIM = 64
TOPK_MAX = 256
DIL_GROUPS = ((128, 1), (512, 4), (2048, 16))
DIL_HPG = 4
DIL_OUT = DIL_HPG * HEAD_DIM
NUM_BUCKETS = 32
MAX_DISTANCE = 2048
N_GROUPS = 4
EXPERTS_PER_GROUP = 8
N_EXPERTS = N_GROUPS * EXPERTS_PER_GROUP
RMS_EPS = 1e-6
NEG = -1e30
INT_MIN = -2 ** 31
I16_MIN = -2 ** 15

BLK = 128
PROJ_TM = 512
MIX_TM = 256
MOE_TM = 256
FIN_TM = 256
ROUTE_COLS = 128


def _cparams(sem):
    return pltpu.CompilerParams(dimension_semantics=sem, vmem_limit_bytes=VMEM_LIMIT_BYTES)


C_QA = 0
C_KV = 512
C_QI = 768
C_KW = 1024
C_DIL = 1152
C_GATE = C_DIL + 9 * 256
C_END = C_GATE + 2048


def _proj_kernel(x_ref, g_ref, w_ref, kvg_ref, qa_ref, ckv_ref, qi_ref, kw_ref, *rest):
    dil_refs = rest[:9]
    gate_ref = rest[9]
    x = x_ref[...]
    u = x * lax.rsqrt(jnp.mean(x * x, axis=-1, keepdims=True) + RMS_EPS) * g_ref[...]
    u = u.astype(BF16)

    def mm(a, b):
        return jnp.dot(u, w_ref[:, a:b], preferred_element_type=F32)

    qa_ref[...] = mm(C_QA, C_KV).astype(BF16)
    c = mm(C_KV, C_QI)
    c = c * lax.rsqrt(jnp.mean(c * c, axis=-1, keepdims=True) + RMS_EPS) * kvg_ref[...]
    ckv_ref[...] = c.astype(BF16)
    qi_ref[...] = mm(C_QI, C_KW).astype(BF16)
    kw_ref[...] = mm(C_KW, C_DIL)
    for j in range(9):
        dil_refs[j][...] = mm(C_DIL + 256 * j, C_DIL + 256 * (j + 1)).astype(BF16)
    for j in range(4):
        gate_ref[:, 512 * j:512 * (j + 1)] = mm(C_GATE + 512 * j, C_GATE + 512 * (j + 1)).astype(BF16)


def _proj(x2, attn_norm, w_packed, kv_norm):
    n, d = x2.shape
    tm = PROJ_TM
    row = lambda i: (i, 0)
    const = lambda i: (0, 0)
    outs = [jax.ShapeDtypeStruct((n, 512), BF16), jax.ShapeDtypeStruct((n, 256), BF16),
            jax.ShapeDtypeStruct((n, 256), BF16), jax.ShapeDtypeStruct((n, 128), F32)]
    outs += [jax.ShapeDtypeStruct((n, 256), BF16)] * 9
    outs += [jax.ShapeDtypeStruct((n, 2048), BF16)]
    out_specs = [pl.BlockSpec((tm, s.shape[1]), row) for s in outs]
    return pl.pallas_call(
        _proj_kernel,
        out_shape=outs,
        grid=(n // tm,),
        in_specs=[pl.BlockSpec((tm, d), row), pl.BlockSpec((1, d), const),
                  pl.BlockSpec((d, C_END), const), pl.BlockSpec((1, KV_LATENT), const)],
        out_specs=out_specs,
        compiler_params=_cparams(("parallel",)),
        name="proj",
    )(x2, attn_norm.reshape(1, d), w_packed, kv_norm.reshape(1, KV_LATENT))


def _bucket_thresholds():
    max_exact = NUM_BUCKETS // 2
    d = np.arange(0, MAX_DISTANCE + 1)
    nf = np.maximum(d, 1).astype(np.float32)
    large = max_exact + (np.log(nf / np.float32(max_exact)) / np.float32(math.log(MAX_DISTANCE / max_exact))
                         * np.float32(NUM_BUCKETS - max_exact)).astype(np.int32)
    large = np.minimum(large, NUM_BUCKETS - 1)
    bucket = np.where(d < max_exact, d, large)
    assert np.all(np.diff(bucket) >= 0)
    return [int(np.argmax(bucket >= b)) for b in range(1, NUM_BUCKETS)]


_BUCKET_THR = _bucket_thresholds()


def _bias_from_distance(dist, tab_ref, heads):
    masks = [dist >= t for t in _BUCKET_THR]
    out = []
    for h in heads:
        v = jnp.full(dist.shape, tab_ref[0, h], F32)
        for b in range(1, NUM_BUCKETS):
            v = jnp.where(masks[b - 1], tab_ref[b, h], v)
        out.append(v)
    return out


def _dsa_bias_kernel(tab_ref, o_ref):
    delta = pl.program_id(0)
    j = lax.broadcasted_iota(I32, (BLK, BLK), 0)
    i = lax.broadcasted_iota(I32, (BLK, BLK), 1)
    dist = jnp.maximum(delta * BLK + i - j, 0)
    tiles = _bias_from_distance(dist, tab_ref, range(DSA_HEADS))
    for h in range(DSA_HEADS):
        o_ref[0, h] = tiles[h]


def _dil_bias_kernel(tab_ref, o_ref, *, dilations):
    g = pl.program_id(0)
    i = lax.broadcasted_iota(I32, (BLK, 2 * BLK), 0)
    j = lax.broadcasted_iota(I32, (BLK, 2 * BLK), 1)
    step = i + BLK - j
    valid = (step >= 0) & (step <= BLK)
    for gi, r in enumerate(dilations):
        @pl.when(g == gi)
        def _():
            dist = jnp.maximum(step, 0) * r
            heads = [DSA_HEADS + gi * DIL_HPG + hh for hh in range(DIL_HPG)]
            tiles = _bias_from_distance(dist, tab_ref, heads)
            for hh in range(DIL_HPG):
                o_ref[0, hh] = jnp.where(valid, tiles[hh], NEG)


def _bias_tiles(rel_bias, nkc):
    smem = pl.BlockSpec(memory_space=pltpu.SMEM)
    dsa = pl.pallas_call(
        _dsa_bias_kernel,
        out_shape=jax.ShapeDtypeStruct((nkc, DSA_HEADS, BLK, BLK), F32),
        grid=(nkc,),
        in_specs=[smem],
        out_specs=pl.BlockSpec((1, DSA_HEADS, BLK, BLK), lambda d: (d, 0, 0, 0)),
        compiler_params=_cparams(("parallel",)),
        name="dsa_bias",
    )(rel_bias)
    dil = pl.pallas_call(
        functools.partial(_dil_bias_kernel, dilations=tuple(r for _, r in DIL_GROUPS)),
        out_shape=jax.ShapeDtypeStruct((len(DIL_GROUPS), DIL_HPG, BLK, 2 * BLK), F32),
        grid=(len(DIL_GROUPS),),
        in_specs=[smem],
        out_specs=pl.BlockSpec((1, DIL_HPG, BLK, 2 * BLK), lambda g: (g, 0, 0, 0)),
        compiler_params=_cparams(("parallel",)),
        name="dil_bias",
    )(rel_bias)
    return dsa, dil


SUP = 4
SROWS = SUP * BLK


def _fold_rows(x, rows):
    parts = [x[i:i + rows] for i in range(0, x.shape[0], rows)]
    while len(parts) > 1:
        parts = [parts[i] + parts[i + 1] for i in range(0, len(parts), 2)]
    return parts[0]


def _dsa_kernel(qiT_ref, wiT_ref, qaT_ref, kidx_ref, ckv_ref, ckvT_ref, wuk_ref, wuv_ref, bias_ref, tri_ref,
                y_ref, keys_ref, hi_ref, lo_ref, qlT_ref, x_ref, pT_ref, *, topk, nsc, qb0):
    qb = qb0 + pl.program_id(1)
    row = lax.broadcasted_iota(I32, (SROWS, BLK), 0)
    col = lax.broadcasted_iota(I32, (SROWS, BLK), 1)
    trips = [slice(sc * SROWS, (sc + 1) * SROWS) for sc in range(nsc)]

    def causal(sc):
        return row <= col + (qb * BLK - sc * SROWS)

    for h in range(DSA_HEADS):
        ql = jnp.dot(wuk_ref[h], qaT_ref[0, h * HEAD_DIM:(h + 1) * HEAD_DIM, :],
                     preferred_element_type=F32) * (HEAD_DIM ** -0.5)
        qlT_ref[:, h * BLK:(h + 1) * BLK] = ql.astype(BF16)

    wq = wiT_ref[0] * (IDX_HEADS ** -0.5)
    for sc, ts in enumerate(trips):
        kx = kidx_ref[0, ts, :]
        acc = jnp.zeros((SROWS, BLK), F32)
        for h in range(IDX_HEADS):
            s = jnp.dot(kx, qiT_ref[0, h * IDX_DIM:(h + 1) * IDX_DIM, :],
                        preferred_element_type=F32) * (IDX_DIM ** -0.5)
            acc = acc + wq[h:h + 1, :] * jnp.maximum(s, 0.0)
        acc = jnp.where(causal(sc), acc, NEG)
        bits = pltpu.bitcast(acc, I32)
        bits = jnp.where(bits == INT_MIN, 0, bits)
        key = bits ^ ((bits >> 31) & 0x7FFFFFFF)
        keys_ref[ts, :] = key
        hi_ref[ts, :] = (key >> 16).astype(I16)

    def count16(ref, pred):
        cnt = jnp.zeros((16, BLK), I16)
        for ts in trips:
            cnt = cnt + _fold_rows(jnp.where(pred(ref[ts, :]), jnp.int16(1), jnp.int16(0)), 16)
        return jnp.sum(cnt.astype(I32), axis=0, keepdims=True)

    def kth_largest16(ref, need):
        def body(it, lo):
            cand = lo + jnp.left_shift(jnp.int32(1), 15 - it)
            cand16 = cand.astype(I16)
            return jnp.where(count16(ref, lambda v: v >= cand16) >= need, cand, lo)
        return lax.fori_loop(0, 16, body, jnp.full((1, BLK), I16_MIN, I32))

    t_hi = kth_largest16(hi_ref, topk)
    t_hi16 = t_hi.astype(I16)
    n_gt_hi = count16(hi_ref, lambda v: v > t_hi16)
    for ts in trips:
        k = keys_ref[ts, :]
        low = (k & 0xFFFF) + I16_MIN
        lo_ref[ts, :] = jnp.where((k >> 16) == t_hi, low, I16_MIN).astype(I16)
    t_lo = kth_largest16(lo_ref, topk - n_gt_hi)
    t_lo16 = t_lo.astype(I16)
    thr = jnp.left_shift(t_hi, 16) | (t_lo - I16_MIN)
    ties_wanted = (topk - n_gt_hi - count16(lo_ref, lambda v: v > t_lo16)).astype(F32)

    m = [jnp.full((8, BLK), NEG, F32) for _ in range(DSA_HEADS)]
    ties_before = jnp.zeros((1, BLK), F32)
    for sc, ts in enumerate(trips):
        k = keys_ref[ts, :]
        tie = k == thr
        tie_rank = jnp.dot(tri_ref[...], jnp.where(tie, 1.0, 0.0).astype(BF16),
                           preferred_element_type=F32) + ties_before
        ties_before = tie_rank[SROWS - 1:SROWS, :]
        sel = ((k > thr) | (tie & (tie_rank <= ties_wanted))) & causal(sc)
        am = jnp.where(sel, 0.0, NEG)
        ck = ckv_ref[0, ts, :]
        for hp in range(DSA_HEADS // 2):
            lg2 = jnp.dot(ck, qlT_ref[:, 2 * hp * BLK:(2 * hp + 2) * BLK], preferred_element_type=F32)
            for h in (2 * hp, 2 * hp + 1):
                lg = lg2[:, (h % 2) * BLK:(h % 2 + 1) * BLK]
                for j in range(SUP):
                    rs = slice(j * BLK, (j + 1) * BLK)
                    delta = jnp.maximum(qb - (sc * SUP + j), 0)
                    x = lg[rs] + bias_ref[delta, h] + am[rs]
                    x_ref[sc * SROWS + j * BLK:sc * SROWS + (j + 1) * BLK, h * BLK:(h + 1) * BLK] = x
                    m[h] = jnp.maximum(m[h], jnp.max(x.reshape(BLK // 8, 8, BLK), axis=0))
    m = [jnp.max(v, axis=0, keepdims=True) for v in m]

    l = [jnp.zeros((8, BLK), F32) for _ in range(DSA_HEADS)]
    for sc, ts in enumerate(trips):
        for h in range(DSA_HEADS):
            hs = slice(h * BLK, (h + 1) * BLK)
            p = jnp.exp(x_ref[ts, hs] - m[h])
            pT_ref[ts, hs] = p.astype(BF16)
            l[h] = l[h] + jnp.sum(p.reshape(SROWS // 8, 8, BLK), axis=0)

    for hp in range(DSA_HEADS // 2):
        o2 = jnp.dot(ckvT_ref[0], pT_ref[:, 2 * hp * BLK:(2 * hp + 2) * BLK],
                     preferred_element_type=F32)
        for h in (2 * hp, 2 * hp + 1):
            inv = 1.0 / jnp.sum(l[h], axis=0, keepdims=True)
            oh = (o2[:, (h % 2) * BLK:(h % 2 + 1) * BLK] * inv).T.astype(BF16)
            yh = jnp.dot(oh, wuv_ref[h], preferred_element_type=F32)
            y_ref[0, :, h * HEAD_DIM:(h + 1) * HEAD_DIM] = yh.astype(BF16)


def _dsa_group(g, qiT, wiT, qaT, kidx, ckv, ckvT, wuk, wuv, bias_tiles, tri):
    b, seq, _ = ckv.shape
    nkc = seq // BLK
    nsc = g + 1
    nk = nsc * SROWS
    topk = min(TOPK_MAX, seq // 4)
    qblk = lambda rows: pl.BlockSpec((1, rows, BLK), lambda bi, qi: (bi, 0, g * SUP + qi))
    head3 = lambda cols: pl.BlockSpec((1, nk, cols), lambda bi, qi: (bi, 0, 0))
    const = lambda s: pl.BlockSpec(s, lambda bi, qi: (0,) * len(s))
    hl = DSA_HEADS * BLK
    return pl.pallas_call(
        functools.partial(_dsa_kernel, topk=topk, nsc=nsc, qb0=g * SUP),
        out_shape=jax.ShapeDtypeStruct((b, SROWS, DSA_HEADS * HEAD_DIM), BF16),
        grid=(b, SUP),
        in_specs=[qblk(IDX_HEADS * IDX_DIM), qblk(8), qblk(DSA_HEADS * HEAD_DIM),
                  head3(IDX_DIM), head3(KV_LATENT),
                  pl.BlockSpec((1, KV_LATENT, nk), lambda bi, qi: (bi, 0, 0)),
                  const((DSA_HEADS, KV_LATENT, HEAD_DIM)), const((DSA_HEADS, KV_LATENT, HEAD_DIM)),
                  const((nkc, DSA_HEADS, BLK, BLK)), const((SROWS, SROWS))],
        out_specs=pl.BlockSpec((1, BLK, DSA_HEADS * HEAD_DIM), lambda bi, qi: (bi, qi, 0)),
        scratch_shapes=[pltpu.VMEM((nk, BLK), I32),
                        pltpu.VMEM((nk, BLK), I16),
                        pltpu.VMEM((nk, BLK), I16),
                        pltpu.VMEM((KV_LATENT, hl), BF16),
                        pltpu.VMEM((nk, hl), F32),
                        pltpu.VMEM((nk, hl), BF16)],
        compiler_params=_cparams(("parallel", "arbitrary")),
        name=f"dsa_g{g}",
    )(qiT, wiT, qaT, kidx, ckv, ckvT, wuk, wuv, bias_tiles, tri)


def _dsa(qiT, wiT, qaT, kidx, ckv, ckvT, wuk, wuv, bias_tiles):
    seq = ckv.shape[1]
    assert seq % SROWS == 0 and seq >= 4 * TOPK_MAX
    tri = jnp.tril(jnp.ones((SROWS, SROWS), BF16))
    groups = [_dsa_group(g, qiT, wiT, qaT, kidx, ckv, ckvT, wuk, wuv, bias_tiles, tri)
              for g in range(seq // SROWS)]
    return jnp.concatenate(groups, axis=1)


def _dil_kernel(q_ref, k_ref, v_ref, bm_ref, o_ref, lse_ref, *, nblk):
    def block(qo, ko, nkeys, bcol):
        for hh in range(DIL_HPG):
            hs = slice(hh * HEAD_DIM, (hh + 1) * HEAD_DIM)
            qh = q_ref[0, pl.ds(qo, BLK), hs]
            kh = k_ref[0, pl.ds(ko, nkeys), hs]
            vh = v_ref[0, pl.ds(ko, nkeys), hs]
            s = lax.dot_general(qh, kh, (((1,), (1,)), ((), ())), preferred_element_type=F32)
            s = s * (HEAD_DIM ** -0.5) + bm_ref[0, hh, :, bcol:bcol + nkeys]
            m = jnp.max(s, axis=-1, keepdims=True)
            e = jnp.exp(s - m)
            l = jnp.sum(e, axis=-1, keepdims=True)
            o = jnp.dot(e.astype(BF16), vh, preferred_element_type=F32) * (1.0 / l)
            o_ref[0, pl.ds(qo, BLK), hs] = o.astype(BF16)
            lse_ref[0, pl.ds(qo, BLK), hs] = jnp.broadcast_to(m + jnp.log(l), (BLK, HEAD_DIM))

    block(0, 0, BLK, BLK)

    def body(n, carry):
        block(pl.multiple_of(n * BLK, BLK), pl.multiple_of((n - 1) * BLK, BLK), 2 * BLK, 0)
        return carry

    lax.fori_loop(1, nblk, body, 0)


def _dilated_group(q, k, v, bm, g, dilation):
    b, seq, c = q.shape
    ls = seq // dilation
    view = lambda a: a.reshape(b, ls, dilation * c)
    blk = pl.BlockSpec((1, ls, c), lambda bi, ri: (bi, 0, ri))
    o, lse = pl.pallas_call(
        functools.partial(_dil_kernel, nblk=ls // BLK),
        out_shape=[jax.ShapeDtypeStruct((b, ls, dilation * c), BF16),
                   jax.ShapeDtypeStruct((b, ls, dilation * c), F32)],
        grid=(b, dilation),
        in_specs=[blk, blk, blk, pl.BlockSpec((1, DIL_HPG, BLK, 2 * BLK), lambda bi, ri: (g, 0, 0, 0))],
        out_specs=[blk, blk],
        compiler_params=_cparams(("parallel", "parallel")),
        name=f"dilated_g{g}",
    )(view(q), view(k), view(v), bm)
    return o.reshape(b, seq, c), lse.reshape(b, seq, c)


def _store_row_tiles(ref, base, val):
    rows, d = val.shape
    dt = d // LANES
    for s in range(dt):
        ref[pl.ds(base * dt + s, rows, stride=dt), :] = val[:, s * LANES:(s + 1) * LANES]


def _load_row_tiles(ref, base, rows, dt):
    return jnp.concatenate([ref[pl.ds(base * dt + s, rows, stride=dt), :] for s in range(dt)], axis=1)


def _mix_kernel(x_ref, ya_ref, o1_ref, o2_ref, o3_ref, l1_ref, l2_ref, l3_ref, gate_ref,
                wa_ref, wb_ref, wo_ref, fg_ref, wr_ref, br_ref,
                h_ref, t_ref, rw_ref, ri_ref):
    l1, l2, l3 = l1_ref[...], l2_ref[...], l3_ref[...]
    mx = jnp.maximum(jnp.maximum(l1, l2), l3)
    e1, e2, e3 = jnp.exp(l1 - mx), jnp.exp(l2 - mx), jnp.exp(l3 - mx)
    inv = 1.0 / (e1 + e2 + e3)
    yb = ((e1 * inv) * o1_ref[...].astype(F32) + (e2 * inv) * o2_ref[...].astype(F32)
          + (e3 * inv) * o3_ref[...].astype(F32))
    a = jnp.dot(ya_ref[...], wa_ref[...], preferred_element_type=F32)
    bmix = jnp.dot(yb.astype(BF16), wb_ref[...], preferred_element_type=F32)
    d = a.shape[1]
    g0 = jax.nn.sigmoid(gate_ref[:, :d].astype(F32))
    g1 = jax.nn.sigmoid(gate_ref[:, d:].astype(F32))
    mixed = g0 * a + g1 * bmix
    h = x_ref[...] + jnp.dot(mixed.astype(BF16), wo_ref[...], preferred_element_type=F32)
    h_ref[...] = h
    t = h * lax.rsqrt(jnp.mean(h * h, axis=-1, keepdims=True) + RMS_EPS) * fg_ref[...]
    _store_row_tiles(t_ref, 0, t)

    logits = jnp.dot(t, wr_ref[...], preferred_element_type=F32,
                     precision=lax.Precision.HIGHEST) + br_ref[...]
    lane = lax.broadcasted_iota(I32, logits.shape, 1)
    ninf = -jnp.inf
    big = jnp.int32(10 ** 6)

    def first_argmax(v, vmax):
        return jnp.min(jnp.where(v == vmax, lane, big), axis=-1, keepdims=True)

    gl = jnp.where(lane < N_GROUPS, logits, ninf)
    gmax = jnp.max(gl, axis=-1, keepdims=True)
    gsel = first_argmax(gl, gmax)
    p_g = 1.0 / jnp.sum(jnp.exp(gl - gmax), axis=-1, keepdims=True)
    lo = N_GROUPS + gsel * EXPERTS_PER_GROUP
    el = jnp.where((lane >= lo) & (lane < lo + EXPERTS_PER_GROUP), logits, ninf)
    v1 = jnp.max(el, axis=-1, keepdims=True)
    i1 = first_argmax(el, v1)
    el2 = jnp.where(lane == i1, ninf, el)
    v2 = jnp.max(el2, axis=-1, keepdims=True)
    i2 = first_argmax(el2, v2)
    e2 = jnp.exp(v2 - v1)
    w1 = p_g / (1.0 + e2)
    w2 = p_g * e2 / (1.0 + e2)
    rw_ref[...] = jnp.where(lane == 0, w1, jnp.where(lane == 1, w2, 0.0))
    ri_ref[...] = jnp.where(lane == 0, i1 - N_GROUPS, jnp.where(lane == 1, i2 - N_GROUPS, 0))


def _mix(x2, ya, os_, lses, gates, wa, wb, wo, ffn_norm, wr, br):
    n, d = x2.shape
    tm = MIX_TM
    row = lambda c: pl.BlockSpec((tm, c), lambda i: (i, 0))
    const = lambda s: pl.BlockSpec(s, lambda i: (0, 0))
    return pl.pallas_call(
        _mix_kernel,
        out_shape=[jax.ShapeDtypeStruct((n, d), F32), jax.ShapeDtypeStruct((n * (d // LANES), LANES), F32),
                   jax.ShapeDtypeStruct((n, ROUTE_COLS), F32), jax.ShapeDtypeStruct((n, ROUTE_COLS), I32)],
        grid=(n // tm,),
        in_specs=[row(d), row(512), row(DIL_OUT), row(DIL_OUT), row(DIL_OUT),
                  row(DIL_OUT), row(DIL_OUT), row(DIL_OUT), row(2 * d),
                  const(wa.shape), const(wb.shape), const(wo.shape), const((1, d)),
                  const(wr.shape), const((1, ROUTE_COLS))],
        out_specs=[row(d), pl.BlockSpec((tm * (d // LANES), LANES), lambda i: (i, 0)),
                   row(ROUTE_COLS), row(ROUTE_COLS)],
        compiler_params=_cparams(("parallel",)),
        name="mix",
    )(x2, ya, *os_, *lses, gates, wa, wb, wo, ffn_norm.reshape(1, d), wr, br)


GATHER_UNROLL = 8


def _start_row_gather(src_hbm, idx_ref, nrows, dt, buf, sem, slot):
    def body(g, c):
        for u in range(GATHER_UNROLL):
            r = g * GATHER_UNROLL + u
            src = pl.multiple_of(idx_ref[0, 0, r] * dt, dt)
            dst = pl.multiple_of((slot * nrows + r) * dt, dt)
            pltpu.make_async_copy(src_hbm.at[pl.ds(src, dt)], buf.at[pl.ds(dst, dt)], sem.at[slot]).start()
        return c
    lax.fori_loop(0, nrows // GATHER_UNROLL, body, 0)


def _wait_row_gather(src_hbm, nrows, dt, buf, sem, slot):
    dst = pl.multiple_of(slot * nrows * dt, dt)
    pltpu.make_async_copy(src_hbm.at[pl.ds(0, nrows * dt)], buf.at[pl.ds(dst, nrows * dt)], sem.at[slot]).wait()


def _gather_pipeline(i, nsteps, src_hbm, cur_ref, nxt_ref, nrows, dt, buf, sem):
    slot = i % 2

    @pl.when(i == 0)
    def _():
        _start_row_gather(src_hbm, cur_ref, nrows, dt, buf, sem, 0)

    @pl.when(i + 1 < nsteps)
    def _():
        _start_row_gather(src_hbm, nxt_ref, nrows, dt, buf, sem, 1 - slot)

    _wait_row_gather(src_hbm, nrows, dt, buf, sem, slot)
    return slot


def _expert_kernel(te_ref, cur_ref, nxt_ref, t_hbm, wg_ref, wu_ref, wd_ref, y_ref, buf, sem):
    i = pl.program_id(0)
    dt = wg_ref.shape[1] // LANES
    slot = _gather_pipeline(i, pl.num_programs(0), t_hbm, cur_ref, nxt_ref, MOE_TM, dt, buf, sem)
    xt = _load_row_tiles(buf, slot * MOE_TM, MOE_TM, dt).astype(BF16)
    hg = jnp.dot(xt, wg_ref[0], preferred_element_type=F32)
    hu = jnp.dot(xt, wu_ref[0], preferred_element_type=F32)
    hid = (hg * jax.nn.sigmoid(hg)) * hu
    _store_row_tiles(y_ref, 0, jnp.dot(hid.astype(BF16), wd_ref[0], preferred_element_type=F32))


def _experts(tile_expert, row_token, t, wg, wu, wd, d):
    dt = d // LANES
    ntiles = tile_expert.shape[0]
    ff = wg.shape[2]
    tok3 = row_token.reshape(ntiles, 1, MOE_TM)
    smem_cur = pl.BlockSpec((1, 1, MOE_TM), lambda i, te: (i, 0, 0), memory_space=pltpu.SMEM)
    smem_nxt = pl.BlockSpec((1, 1, MOE_TM), lambda i, te: (jnp.minimum(i + 1, ntiles - 1), 0, 0),
                            memory_space=pltpu.SMEM)
    wspec = lambda s: pl.BlockSpec((1,) + s, lambda i, te: (te[i], 0, 0))
    return pl.pallas_call(
        _expert_kernel,
        out_shape=jax.ShapeDtypeStruct((ntiles * MOE_TM * dt, LANES), F32),
        grid_spec=pltpu.PrefetchScalarGridSpec(
            num_scalar_prefetch=1,
            grid=(ntiles,),
            in_specs=[smem_cur, smem_nxt, pl.BlockSpec(memory_space=pl.ANY),
                      wspec((d, ff)), wspec((d, ff)), wspec((ff, d))],
            out_specs=pl.BlockSpec((MOE_TM * dt, LANES), lambda i, te: (i, 0)),
            scratch_shapes=[pltpu.VMEM((2 * MOE_TM * dt, LANES), F32), pltpu.SemaphoreType.DMA((2,))],
        ),
        compiler_params=_cparams(("arbitrary",)),
        name="experts",
    )(tile_expert, tok3, tok3, t, wg, wu, wd)


def _final_kernel(cur_ref, nxt_ref, y_hbm, h_ref, rw_ref, fn_ref, o_ref, buf, sem):
    i = pl.program_id(0)
    dt = h_ref.shape[1] // LANES
    slot = _gather_pipeline(i, pl.num_programs(0), y_hbm, cur_ref, nxt_ref, 2 * FIN_TM, dt, buf, sem)
    rw = rw_ref[...]
    y0 = _load_row_tiles(buf, slot * 2 * FIN_TM, FIN_TM, dt)
    y1 = _load_row_tiles(buf, slot * 2 * FIN_TM + FIN_TM, FIN_TM, dt)
    h = h_ref[...] + rw[:, 0:1] * y0 + rw[:, 1:2] * y1
    o_ref[...] = h * lax.rsqrt(jnp.mean(h * h, axis=-1, keepdims=True) + RMS_EPS) * fn_ref[...]


def _final(pos_tiles, y_sorted, h, rw, final_norm):
    n, d = h.shape
    tm = FIN_TM
    nt = n // tm
    smem_cur = pl.BlockSpec((1, 1, 2 * tm), lambda i: (i, 0, 0), memory_space=pltpu.SMEM)
    smem_nxt = pl.BlockSpec((1, 1, 2 * tm), lambda i: (jnp.minimum(i + 1, nt - 1), 0, 0),
                            memory_space=pltpu.SMEM)
    return pl.pallas_call(
        _final_kernel,
        out_shape=jax.ShapeDtypeStruct((n, d), F32),
        grid=(nt,),
        in_specs=[smem_cur, smem_nxt, pl.BlockSpec(memory_space=pl.ANY),
                  pl.BlockSpec((tm, d), lambda i: (i, 0)), pl.BlockSpec((tm, ROUTE_COLS), lambda i: (i, 0)),
                  pl.BlockSpec((1, d), lambda i: (0, 0))],
        out_specs=pl.BlockSpec((tm, d), lambda i: (i, 0)),
        scratch_shapes=[pltpu.VMEM((2 * 2 * tm * (d // LANES), LANES), F32), pltpu.SemaphoreType.DMA((2,))],
        compiler_params=_cparams(("arbitrary",)),
        name="final",
    )(pos_tiles, pos_tiles, y_sorted, h, rw, final_norm.reshape(1, d))


def _route_plan(gid):
    n = gid.shape[0]
    e = gid.reshape(-1)
    onehot = (e[:, None] == jnp.arange(N_EXPERTS, dtype=I32)[None, :]).astype(I32)
    csum = jnp.cumsum(onehot, axis=0)
    rank = jnp.take_along_axis(csum, e[:, None], axis=1)[:, 0] - 1
    counts = csum[-1]
    padded = ((counts + MOE_TM - 1) // MOE_TM) * MOE_TM
    seg_end = jnp.cumsum(padded)
    pos = (seg_end - padded)[e] + rank
    nrows = 2 * n + N_EXPERTS * MOE_TM
    row_token = jnp.zeros((nrows,), I32).at[pos].set(jnp.arange(2 * n, dtype=I32) // 2)
    tile_start = jnp.arange(nrows // MOE_TM, dtype=I32) * MOE_TM
    tile_expert = jnp.minimum(jnp.searchsorted(seg_end, tile_start, side="right"), N_EXPERTS - 1).astype(I32)
    return row_token, tile_expert, pos.reshape(n, 2)


def _pack_w_in(w):
    d = w.shape[0]
    o_kv, o_qi, o_ki, o_wi = 512, 768, 1024, 1088
    o_dil = o_wi + IDX_HEADS
    o_gate = o_dil + 9 * 256
    pad = jnp.zeros((d, LANES - IDX_DIM - IDX_HEADS), w.dtype)
    packed = jnp.concatenate([w[:, :o_ki], w[:, o_ki:o_wi], w[:, o_wi:o_dil], pad, w[:, o_dil:o_gate],
                              w[:, o_gate:]], axis=1)
    assert packed.shape[1] == C_END
    return packed.astype(BF16)


def kernel(x, attn_norm, w_in, kv_norm, w_uk, w_uv, rel_bias, w_branch_a, w_branch_b, w_out, ffn_norm,
           w_router_group, b_router_group, w_router_expert, b_router_expert, w_gate, w_up, w_down,
           final_norm):
    b, seq, d = x.shape
    n = b * seq
    nkc = seq // BLK
    assert w_in.shape[0] == 1, "one layer"
    x2 = x.reshape(n, d)

    outs = _proj(x2, attn_norm[0], _pack_w_in(w_in[0]), kv_norm[0])
    qa, ckv, qi, kw = outs[:4]
    dil = outs[4:13]
    gates = outs[13]

    dsa_bias, dil_bias = _bias_tiles(rel_bias, nkc)

    t3 = lambda a: jnp.swapaxes(a.reshape(b, seq, a.shape[-1]), 1, 2)
    kw3 = kw.reshape(b, seq, LANES)
    wiT = jnp.swapaxes(kw3[:, :, IDX_DIM:IDX_DIM + 8], 1, 2)
    kidx = kw3[:, :, :IDX_DIM].astype(BF16)
    ckv3 = ckv.reshape(b, seq, KV_LATENT)
    ckvT = jnp.swapaxes(ckv3, 1, 2)
    ya = _dsa(t3(qi), wiT, t3(qa), kidx, ckv3, ckvT, w_uk[0].astype(BF16), w_uv[0].astype(BF16), dsa_bias)

    os_, lses = [], []
    for g, (_, dilation) in enumerate(DIL_GROUPS):
        r3 = lambda a: a.reshape(b, seq, DIL_OUT)
        o, lse = _dilated_group(r3(dil[g]), r3(dil[3 + g]), r3(dil[6 + g]), dil_bias, g, dilation)
        os_.append(o.reshape(n, DIL_OUT))
        lses.append(lse.reshape(n, DIL_OUT))

    wr = jnp.concatenate([w_router_group[0],
                          jnp.swapaxes(w_router_expert[0], 0, 1).reshape(d, N_EXPERTS),
                          jnp.zeros((d, ROUTE_COLS - N_GROUPS - N_EXPERTS), F32)], axis=1)
    br = jnp.concatenate([b_router_group[0], b_router_expert[0].reshape(-1),
                          jnp.zeros((ROUTE_COLS - N_GROUPS - N_EXPERTS,), F32)]).reshape(1, ROUTE_COLS)
    h, t, rw, ri = _mix(x2, ya.reshape(n, -1), os_, lses, gates,
                        w_branch_a[0].astype(BF16), w_branch_b[0].astype(BF16), w_out[0].astype(BF16),
                        ffn_norm[0], wr, br)

    row_token, tile_expert, pos = _route_plan(ri[:, :2])
    y_sorted = _experts(tile_expert, row_token, t, w_gate[0].astype(BF16), w_up[0].astype(BF16),
                        w_down[0].astype(BF16), d)

    pos_tiles = jnp.swapaxes(pos.reshape(n // FIN_TM, FIN_TM, 2), 1, 2).reshape(n // FIN_TM, 1, 2 * FIN_TM)
    out = _final(pos_tiles, y_sorted, h, rw, final_norm)
    return out.reshape(b, seq, d)
```

```python
import functools
import math

import numpy as np
import jax
import jax.numpy as jnp
from jax import lax
from jax.experimental import pallas as pl
from jax.experimental.pallas import tpu as pltpu

F32 = jnp.float32
BF16 = jnp.bfloat16
I32 = jnp.int32
I16 = jnp.int16

LANES = 128
VMEM_LIMIT_BYTES = 56 * 1024 * 1024

HEAD_DIM = 64
DSA_HEADS = 8
KV_LATENT = 256
IDX_HEADS = 4
IDX_DIM = 64
TOPK_MAX = 256
DIL_GROUPS = ((128, 1), (512, 4), (2048, 16))
DIL_HPG = 4
DIL_OUT = DIL_HPG * HEAD_DIM
NUM_BUCKETS = 32
MAX_DISTANCE = 2048
N_GROUPS = 4
EXPERTS_PER_GROUP = 8
N_EXPERTS = N_GROUPS * EXPERTS_PER_GROUP
RMS_EPS = 1e-6
NEG = -1e30
INT_MIN = -2 ** 31
I16_MIN = -2 ** 15

BLK = 128
PROJ_TM = 512
MIX_TM = 256
MOE_TM = 256
FIN_TM = 256
ROUTE_COLS = 128


def _cparams(sem):
    return pltpu.CompilerParams(dimension_semantics=sem, vmem_limit_bytes=VMEM_LIMIT_BYTES)


C_QA = 0
C_KV = 512
C_QI = 768
C_KW = 1024
C_DIL = 1152
C_GATE = C_DIL + 9 * 256
C_END = C_GATE + 2048


def _proj_kernel(x_ref, g_ref, w_ref, kvg_ref, qa_ref, ckv_ref, qi_ref, kw_ref, *rest):
    dil_refs = rest[:9]
    gate_ref = rest[9]
    x = x_ref[...]
    u = x * lax.rsqrt(jnp.mean(x * x, axis=-1, keepdims=True) + RMS_EPS) * g_ref[...]
    u = u.astype(BF16)

    def mm(a, b):
        return jnp.dot(u, w_ref[:, a:b], preferred_element_type=F32)

    qa_ref[...] = mm(C_QA, C_KV).astype(BF16)
    c = mm(C_KV, C_QI)
    c = c * lax.rsqrt(jnp.mean(c * c, axis=-1, keepdims=True) + RMS_EPS) * kvg_ref[...]
    ckv_ref[...] = c.astype(BF16)
    qi_ref[...] = mm(C_QI, C_KW).astype(BF16)
    kw_ref[...] = mm(C_KW, C_DIL)
    for j in range(9):
        dil_refs[j][...] = mm(C_DIL + 256 * j, C_DIL + 256 * (j + 1)).astype(BF16)
    for j in range(4):
        gate_ref[:, 512 * j:512 * (j + 1)] = mm(C_GATE + 512 * j, C_GATE + 512 * (j + 1)).astype(BF16)


def _proj(x2, attn_norm, w_packed, kv_norm):
    n, d = x2.shape
    tm = PROJ_TM
    row = lambda i: (i, 0)
    const = lambda i: (0, 0)
    outs = [jax.ShapeDtypeStruct((n, 512), BF16), jax.ShapeDtypeStruct((n, 256), BF16),
            jax.ShapeDtypeStruct((n, 256), BF16), jax.ShapeDtypeStruct((n, 128), F32)]
    outs += [jax.ShapeDtypeStruct((n, 256), BF16)] * 9
    outs += [jax.ShapeDtypeStruct((n, 2048), BF16)]
    out_specs = [pl.BlockSpec((tm, s.shape[1]), row) for s in outs]
    return pl.pallas_call(
        _proj_kernel,
        out_shape=outs,
        grid=(n // tm,),
        in_specs=[pl.BlockSpec((tm, d), row), pl.BlockSpec((1, d), const),
                  pl.BlockSpec((d, C_END), const), pl.BlockSpec((1, KV_LATENT), const)],
        out_specs=out_specs,
        compiler_params=_cparams(("parallel",)),
        name="proj",
    )(x2, attn_norm.reshape(1, d), w_packed, kv_norm.reshape(1, KV_LATENT))


def _bucket_thresholds():
    max_exact = NUM_BUCKETS // 2
    d = np.arange(0, MAX_DISTANCE + 1)
    nf = np.maximum(d, 1).astype(np.float32)
    large = max_exact + (np.log(nf / np.float32(max_exact)) / np.float32(math.log(MAX_DISTANCE / max_exact))
                         * np.float32(NUM_BUCKETS - max_exact)).astype(np.int32)
    large = np.minimum(large, NUM_BUCKETS - 1)
    bucket = np.where(d < max_exact, d, large)
    assert np.all(np.diff(bucket) >= 0)
    return [int(np.argmax(bucket >= b)) for b in range(1, NUM_BUCKETS)]


_BUCKET_THR = _bucket_thresholds()


def _bias_from_distance(dist, tab_ref, heads):
    masks = [dist >= t for t in _BUCKET_THR]
    out = []
    for h in heads:
        v = jnp.full(dist.shape, tab_ref[0, h], F32)
        for b in range(1, NUM_BUCKETS):
            v = jnp.where(masks[b - 1], tab_ref[b, h], v)
        out.append(v)
    return out


def _dsa_bias_kernel(tab_ref, o_ref):
    delta = pl.program_id(0)
    j = lax.broadcasted_iota(I32, (BLK, BLK), 0)
    i = lax.broadcasted_iota(I32, (BLK, BLK), 1)
    dist = jnp.maximum(delta * BLK + i - j, 0)
    tiles = _bias_from_distance(dist, tab_ref, range(DSA_HEADS))
    for h in range(DSA_HEADS):
        o_ref[0, h] = tiles[h]


def _dil_bias_kernel(tab_ref, o_ref, *, dilations):
    g = pl.program_id(0)
    j = lax.broadcasted_iota(I32, (2 * BLK, BLK), 0)
    i = lax.broadcasted_iota(I32, (2 * BLK, BLK), 1)
    step = i + BLK - j
    valid = (step >= 0) & (step <= BLK)
    for gi, r in enumerate(dilations):
        @pl.when(g == gi)
        def _():
            dist = jnp.maximum(step, 0) * r
            heads = [DSA_HEADS + gi * DIL_HPG + hh for hh in range(DIL_HPG)]
            tiles = _bias_from_distance(dist, tab_ref, heads)
            for hh in range(DIL_HPG):
                o_ref[0, :, hh * BLK:(hh + 1) * BLK] = jnp.where(valid, tiles[hh], NEG)


def _bias_tiles(rel_bias, nkc):
    smem = pl.BlockSpec(memory_space=pltpu.SMEM)
    dsa = pl.pallas_call(
        _dsa_bias_kernel,
        out_shape=jax.ShapeDtypeStruct((nkc, DSA_HEADS, BLK, BLK), F32),
        grid=(nkc,),
        in_specs=[smem],
        out_specs=pl.BlockSpec((1, DSA_HEADS, BLK, BLK), lambda d: (d, 0, 0, 0)),
        compiler_params=_cparams(("parallel",)),
        name="dsa_bias",
    )(rel_bias)
    dil = pl.pallas_call(
        functools.partial(_dil_bias_kernel, dilations=tuple(r for _, r in DIL_GROUPS)),
        out_shape=jax.ShapeDtypeStruct((len(DIL_GROUPS), 2 * BLK, DIL_HPG * BLK), F32),
        grid=(len(DIL_GROUPS),),
        in_specs=[smem],
        out_specs=pl.BlockSpec((1, 2 * BLK, DIL_HPG * BLK), lambda g: (g, 0, 0)),
        compiler_params=_cparams(("parallel",)),
        name="dil_bias",
    )(rel_bias)
    return dsa, dil


SUP = 4
SROWS = SUP * BLK


def _fold_rows(x, rows):
    parts = [x[i:i + rows] for i in range(0, x.shape[0], rows)]
    while len(parts) > 1:
        parts = [parts[i] + parts[i + 1] for i in range(0, len(parts), 2)]
    return parts[0]


def _dsa_kernel(qiT_ref, wiT_ref, qaT_ref, kidx_ref, ckv_ref, ckvT_ref, wuk_ref, wuv_ref, bias_ref, tri_ref,
                y_ref, keys_ref, hi_ref, lo_ref, qlT_ref, x_ref, pT_ref, *, topk, nsc, qb0):
    qb = qb0 + pl.program_id(1)
    row = lax.broadcasted_iota(I32, (SROWS, BLK), 0)
    col = lax.broadcasted_iota(I32, (SROWS, BLK), 1)
    trips = [slice(sc * SROWS, (sc + 1) * SROWS) for sc in range(nsc)]

    def causal(sc):
        return row <= col + (qb * BLK - sc * SROWS)

    for h in range(DSA_HEADS):
        ql = jnp.dot(wuk_ref[h], qaT_ref[0, h * HEAD_DIM:(h + 1) * HEAD_DIM, :],
                     preferred_element_type=F32) * (HEAD_DIM ** -0.5)
        qlT_ref[:, h * BLK:(h + 1) * BLK] = ql.astype(BF16)

    wq = wiT_ref[0] * (IDX_HEADS ** -0.5)
    for sc, ts in enumerate(trips):
        kx = kidx_ref[0, ts, :]
        acc = jnp.zeros((SROWS, BLK), F32)
        for h in range(IDX_HEADS):
            s = jnp.dot(kx, qiT_ref[0, h * IDX_DIM:(h + 1) * IDX_DIM, :],
                        preferred_element_type=F32) * (IDX_DIM ** -0.5)
            acc = acc + wq[h:h + 1, :] * jnp.maximum(s, 0.0)
        acc = jnp.where(causal(sc), acc, NEG)
        bits = pltpu.bitcast(acc, I32)
        bits = jnp.where(bits == INT_MIN, 0, bits)
        key = bits ^ ((bits >> 31) & 0x7FFFFFFF)
        keys_ref[ts, :] = key
        hi_ref[ts, :] = (key >> 16).astype(I16)

    def count16(ref, pred):
        cnt = jnp.zeros((16, BLK), I16)
        for ts in trips:
            cnt = cnt + _fold_rows(jnp.where(pred(ref[ts, :]), jnp.int16(1), jnp.int16(0)), 16)
        return jnp.sum(cnt.astype(I32), axis=0, keepdims=True)

    def kth_largest16(ref, need):
        def body(it, lo):
            cand = lo + jnp.left_shift(jnp.int32(1), 15 - it)
            cand16 = cand.astype(I16)
            return jnp.where(count16(ref, lambda v: v >= cand16) >= need, cand, lo)
        return lax.fori_loop(0, 16, body, jnp.full((1, BLK), I16_MIN, I32))

    t_hi = kth_largest16(hi_ref, topk)
    t_hi16 = t_hi.astype(I16)
    n_gt_hi = count16(hi_ref, lambda v: v > t_hi16)
    for ts in trips:
        k = keys_ref[ts, :]
        low = (k & 0xFFFF) + I16_MIN
        lo_ref[ts, :] = jnp.where((k >> 16) == t_hi, low, I16_MIN).astype(I16)
    t_lo = kth_largest16(lo_ref, topk - n_gt_hi)
    t_lo16 = t_lo.astype(I16)
    thr = jnp.left_shift(t_hi, 16) | (t_lo - I16_MIN)
    ties_wanted = (topk - n_gt_hi - count16(lo_ref, lambda v: v > t_lo16)).astype(F32)

    m = [jnp.full((8, BLK), NEG, F32) for _ in range(DSA_HEADS)]
    ties_before = jnp.zeros((1, BLK), F32)
    for sc, ts in enumerate(trips):
        k = keys_ref[ts, :]
        tie = k == thr
        tie_rank = jnp.dot(tri_ref[...], jnp.where(tie, 1.0, 0.0).astype(BF16),
                           preferred_element_type=F32) + ties_before
        ties_before = tie_rank[SROWS - 1:SROWS, :]
        sel = ((k > thr) | (tie & (tie_rank <= ties_wanted))) & causal(sc)
        am = jnp.where(sel, 0.0, NEG)
        ck = ckv_ref[0, ts, :]
        for hp in range(DSA_HEADS // 2):
            lg2 = jnp.dot(ck, qlT_ref[:, 2 * hp * BLK:(2 * hp + 2) * BLK], preferred_element_type=F32)
            for h in (2 * hp, 2 * hp + 1):
                lg = lg2[:, (h % 2) * BLK:(h % 2 + 1) * BLK]
                for j in range(SUP):
                    rs = slice(j * BLK, (j + 1) * BLK)
                    delta = jnp.maximum(qb - (sc * SUP + j), 0)
                    x = lg[rs] + bias_ref[delta, h] + am[rs]
                    x_ref[sc * SROWS + j * BLK:sc * SROWS + (j + 1) * BLK, h * BLK:(h + 1) * BLK] = x
                    m[h] = jnp.maximum(m[h], jnp.max(x.reshape(BLK // 8, 8, BLK), axis=0))
    m = [jnp.max(v, axis=0, keepdims=True) for v in m]

    l = [jnp.zeros((8, BLK), F32) for _ in range(DSA_HEADS)]
    for sc, ts in enumerate(trips):
        for h in range(DSA_HEADS):
            hs = slice(h * BLK, (h + 1) * BLK)
            p = jnp.exp(x_ref[ts, hs] - m[h])
            pT_ref[ts, hs] = p.astype(BF16)
            l[h] = l[h] + jnp.sum(p.reshape(SROWS // 8, 8, BLK), axis=0)

    for hp in range(DSA_HEADS // 2):
        o2 = jnp.dot(ckvT_ref[0], pT_ref[:, 2 * hp * BLK:(2 * hp + 2) * BLK],
                     preferred_element_type=F32)
        for h in (2 * hp, 2 * hp + 1):
            inv = 1.0 / jnp.sum(l[h], axis=0, keepdims=True)
            oh = (o2[:, (h % 2) * BLK:(h % 2 + 1) * BLK] * inv).T.astype(BF16)
            yh = jnp.dot(oh, wuv_ref[h], preferred_element_type=F32)
            y_ref[0, :, h * HEAD_DIM:(h + 1) * HEAD_DIM] = yh.astype(BF16)


def _dsa_group(g, qiT, wiT, qaT, kidx, ckv, ckvT, wuk, wuv, bias_tiles, tri):
    b, seq, _ = ckv.shape
    nkc = seq // BLK
    nsc = g + 1
    nk = nsc * SROWS
    topk = min(TOPK_MAX, seq // 4)
    qblk = lambda rows: pl.BlockSpec((1, rows, BLK), lambda bi, qi: (bi, 0, g * SUP + qi))
    head3 = lambda cols: pl.BlockSpec((1, nk, cols), lambda bi, qi: (bi, 0, 0))
    const = lambda s: pl.BlockSpec(s, lambda bi, qi: (0,) * len(s))
    hl = DSA_HEADS * BLK
    return pl.pallas_call(
        functools.partial(_dsa_kernel, topk=topk, nsc=nsc, qb0=g * SUP),
        out_shape=jax.ShapeDtypeStruct((b, SROWS, DSA_HEADS * HEAD_DIM), BF16),
        grid=(b, SUP),
        in_specs=[qblk(IDX_HEADS * IDX_DIM), qblk(8), qblk(DSA_HEADS * HEAD_DIM),
                  head3(IDX_DIM), head3(KV_LATENT),
                  pl.BlockSpec((1, KV_LATENT, nk), lambda bi, qi: (bi, 0, 0)),
                  const((DSA_HEADS, KV_LATENT, HEAD_DIM)), const((DSA_HEADS, KV_LATENT, HEAD_DIM)),
                  const((nkc, DSA_HEADS, BLK, BLK)), const((SROWS, SROWS))],
        out_specs=pl.BlockSpec((1, BLK, DSA_HEADS * HEAD_DIM), lambda bi, qi: (bi, qi, 0)),
        scratch_shapes=[pltpu.VMEM((nk, BLK), I32),
                        pltpu.VMEM((nk, BLK), I16),
                        pltpu.VMEM((nk, BLK), I16),
                        pltpu.VMEM((KV_LATENT, hl), BF16),
                        pltpu.VMEM((nk, hl), F32),
                        pltpu.VMEM((nk, hl), BF16)],
        compiler_params=_cparams(("parallel", "arbitrary")),
        name=f"dsa_g{g}",
    )(qiT, wiT, qaT, kidx, ckv, ckvT, wuk, wuv, bias_tiles, tri)


def _dsa(qiT, wiT, qaT, kidx, ckv, ckvT, wuk, wuv, bias_tiles):
    seq = ckv.shape[1]
    assert seq % SROWS == 0 and seq >= 4 * TOPK_MAX
    tri = jnp.tril(jnp.ones((SROWS, SROWS), BF16))
    groups = [_dsa_group(g, qiT, wiT, qaT, kidx, ckv, ckvT, wuk, wuv, bias_tiles, tri)
              for g in range(seq // SROWS)]
    return jnp.concatenate(groups, axis=1)


DIL_UNROLL = 5


def _dil_kernel(q_ref, k_ref, v_ref, bm_ref, o_ref, lse_ref, vT_ref, *, nblk):
    hq = DIL_HPG * BLK
    rowh = lax.broadcasted_iota(I32, (hq, DIL_OUT), 0) // BLK
    colh = lax.broadcasted_iota(I32, (hq, DIL_OUT), 1) // HEAD_DIM
    same_head = rowh == colh

    for n in range(nblk):
        vT_ref[n] = v_ref[0, n * BLK:(n + 1) * BLK, :].astype(F32).T.astype(BF16)

    def block(qo, kw, vT, bm):
        q = q_ref[0, pl.ds(qo, BLK), :]
        qd = jnp.where(same_head, jnp.concatenate([q] * DIL_HPG, axis=0), jnp.zeros((), BF16))
        s = lax.dot_general(kw, qd, (((1,), (1,)), ((), ())), preferred_element_type=F32)
        s = s * (HEAD_DIM ** -0.5) + bm
        m = jnp.max(s, axis=0, keepdims=True)
        e = jnp.exp(s - m)
        l = jnp.sum(e, axis=0, keepdims=True)
        oT = jnp.dot(vT, e.astype(BF16), preferred_element_type=F32)
        inv = 1.0 / l
        lse = m + jnp.log(l)
        outs, lses = [], []
        for hh in range(DIL_HPG):
            qs = slice(hh * BLK, (hh + 1) * BLK)
            outs.append(oT[hh * HEAD_DIM:(hh + 1) * HEAD_DIM, qs] * inv[:, qs])
            lses.append(jnp.broadcast_to(lse[:, qs], (HEAD_DIM, BLK)))
        o_ref[0, pl.ds(qo, BLK), :] = jnp.concatenate(outs, axis=0).T.astype(BF16)
        lse_ref[0, pl.ds(qo, BLK), :] = jnp.concatenate(lses, axis=0).T

    block(0, k_ref[0, 0:BLK, :], vT_ref[0], bm_ref[0, BLK:, :])

    def body(n, carry):
        ko = pl.multiple_of((n - 1) * BLK, BLK)
        vT = jnp.concatenate([vT_ref[n - 1], vT_ref[n]], axis=1)
        block(pl.multiple_of(n * BLK, BLK), k_ref[0, pl.ds(ko, 2 * BLK), :], vT, bm_ref[0])
        return carry

    lax.fori_loop(1, nblk, body, 0, unroll=DIL_UNROLL)


def _dilated_group(q, k, v, bm, g, dilation):
    b, seq, c = q.shape
    ls = seq // dilation
    nblk = ls // BLK
    view = lambda a: a.reshape(b, ls, dilation * c)
    blk = pl.BlockSpec((1, ls, c), lambda bi, ri: (bi, 0, ri))
    o, lse = pl.pallas_call(
        functools.partial(_dil_kernel, nblk=nblk),
        out_shape=[jax.ShapeDtypeStruct((b, ls, dilation * c), BF16),
                   jax.ShapeDtypeStruct((b, ls, dilation * c), F32)],
        grid=(b, dilation),
        in_specs=[blk, blk, blk, pl.BlockSpec((1, 2 * BLK, DIL_HPG * BLK), lambda bi, ri: (g, 0, 0))],
        out_specs=[blk, blk],
        scratch_shapes=[pltpu.VMEM((nblk, c, BLK), BF16)],
        compiler_params=_cparams(("parallel", "parallel")),
        name=f"dilated_g{g}",
    )(view(q), view(k), view(v), bm)
    return o.reshape(b, seq, c), lse.reshape(b, seq, c)


def _store_row_tiles(ref, base, val):
    rows, d = val.shape
    dt = d // LANES
    for s in range(dt):
        ref[pl.ds(base * dt + s, rows, stride=dt), :] = val[:, s * LANES:(s + 1) * LANES]


def _load_row_tiles(ref, base, rows, dt):
    return jnp.concatenate([ref[pl.ds(base * dt + s, rows, stride=dt), :] for s in range(dt)], axis=1)


def _mix_kernel(x_ref, ya_ref, o1_ref, o2_ref, o3_ref, l1_ref, l2_ref, l3_ref, gate_ref,
                wa_ref, wb_ref, wo_ref, fg_ref, wr_ref, br_ref,
                h_ref, t_ref, rw_ref, ri_ref):
    l1, l2, l3 = l1_ref[...], l2_ref[...], l3_ref[...]
    mx = jnp.maximum(jnp.maximum(l1, l2), l3)
    e1, e2, e3 = jnp.exp(l1 - mx), jnp.exp(l2 - mx), jnp.exp(l3 - mx)
    inv = 1.0 / (e1 + e2 + e3)
    yb = ((e1 * inv) * o1_ref[...].astype(F32) + (e2 * inv) * o2_ref[...].astype(F32)
          + (e3 * inv) * o3_ref[...].astype(F32))
    a = jnp.dot(ya_ref[...], wa_ref[...], preferred_element_type=F32)
    bmix = jnp.dot(yb.astype(BF16), wb_ref[...], preferred_element_type=F32)
    d = a.shape[1]
    g0 = jax.nn.sigmoid(gate_ref[:, :d].astype(F32))
    g1 = jax.nn.sigmoid(gate_ref[:, d:].astype(F32))
    mixed = g0 * a + g1 * bmix
    h = x_ref[...] + jnp.dot(mixed.astype(BF16), wo_ref[...], preferred_element_type=F32)
    h_ref[...] = h
    t = h * lax.rsqrt(jnp.mean(h * h, axis=-1, keepdims=True) + RMS_EPS) * fg_ref[...]
    _store_row_tiles(t_ref, 0, t)

    logits = jnp.dot(t, wr_ref[...], preferred_element_type=F32,
                     precision=lax.Precision.HIGHEST) + br_ref[...]
    lane = lax.broadcasted_iota(I32, logits.shape, 1)
    ninf = -jnp.inf
    big = jnp.int32(10 ** 6)

    def first_argmax(v, vmax):
        return jnp.min(jnp.where(v == vmax, lane, big), axis=-1, keepdims=True)

    gl = jnp.where(lane < N_GROUPS, logits, ninf)
    gmax = jnp.max(gl, axis=-1, keepdims=True)
    gsel = first_argmax(gl, gmax)
    p_g = 1.0 / jnp.sum(jnp.exp(gl - gmax), axis=-1, keepdims=True)
    lo = N_GROUPS + gsel * EXPERTS_PER_GROUP
    el = jnp.where((lane >= lo) & (lane < lo + EXPERTS_PER_GROUP), logits, ninf)
    v1 = jnp.max(el, axis=-1, keepdims=True)
    i1 = first_argmax(el, v1)
    el2 = jnp.where(lane == i1, ninf, el)
    v2 = jnp.max(el2, axis=-1, keepdims=True)
    i2 = first_argmax(el2, v2)
    e2 = jnp.exp(v2 - v1)
    w1 = p_g / (1.0 + e2)
    w2 = p_g * e2 / (1.0 + e2)
    rw_ref[...] = jnp.where(lane == 0, w1, jnp.where(lane == 1, w2, 0.0))
    ri_ref[...] = jnp.where(lane == 0, i1 - N_GROUPS, jnp.where(lane == 1, i2 - N_GROUPS, 0))


def _mix(x2, ya, os_, lses, gates, wa, wb, wo, ffn_norm, wr, br):
    n, d = x2.shape
    tm = MIX_TM
    row = lambda c: pl.BlockSpec((tm, c), lambda i: (i, 0))
    const = lambda s: pl.BlockSpec(s, lambda i: (0, 0))
    return pl.pallas_call(
        _mix_kernel,
        out_shape=[jax.ShapeDtypeStruct((n, d), F32), jax.ShapeDtypeStruct((n * (d // LANES), LANES), F32),
                   jax.ShapeDtypeStruct((n, ROUTE_COLS), F32), jax.ShapeDtypeStruct((n, ROUTE_COLS), I32)],
        grid=(n // tm,),
        in_specs=[row(d), row(512), row(DIL_OUT), row(DIL_OUT), row(DIL_OUT),
                  row(DIL_OUT), row(DIL_OUT), row(DIL_OUT), row(2 * d),
                  const(wa.shape), const(wb.shape), const(wo.shape), const((1, d)),
                  const(wr.shape), const((1, ROUTE_COLS))],
        out_specs=[row(d), pl.BlockSpec((tm * (d // LANES), LANES), lambda i: (i, 0)),
                   row(ROUTE_COLS), row(ROUTE_COLS)],
        compiler_params=_cparams(("parallel",)),
        name="mix",
    )(x2, ya, *os_, *lses, gates, wa, wb, wo, ffn_norm.reshape(1, d), wr, br)


GATHER_UNROLL = 8


def _start_row_gather(src_hbm, idx_ref, nrows, dt, buf, sem, slot):
    def body(g, c):
        for u in range(GATHER_UNROLL):
            r = g * GATHER_UNROLL + u
            src = pl.multiple_of(idx_ref[0, 0, r] * dt, dt)
            dst = pl.multiple_of((slot * nrows + r) * dt, dt)
            pltpu.make_async_copy(src_hbm.at[pl.ds(src, dt)], buf.at[pl.ds(dst, dt)], sem.at[slot]).start()
        return c
    lax.fori_loop(0, nrows // GATHER_UNROLL, body, 0)


def _wait_row_gather(src_hbm, nrows, dt, buf, sem, slot):
    dst = pl.multiple_of(slot * nrows * dt, dt)
    pltpu.make_async_copy(src_hbm.at[pl.ds(0, nrows * dt)], buf.at[pl.ds(dst, nrows * dt)], sem.at[slot]).wait()


def _gather_pipeline(i, nsteps, src_hbm, cur_ref, nxt_ref, nrows, dt, buf, sem):
    slot = i % 2

    @pl.when(i == 0)
    def _():
        _start_row_gather(src_hbm, cur_ref, nrows, dt, buf, sem, 0)

    @pl.when(i + 1 < nsteps)
    def _():
        _start_row_gather(src_hbm, nxt_ref, nrows, dt, buf, sem, 1 - slot)

    _wait_row_gather(src_hbm, nrows, dt, buf, sem, slot)
    return slot


def _expert_kernel(te_ref, cur_ref, nxt_ref, t_hbm, wg_ref, wu_ref, wd_ref, y_ref, buf, sem):
    i = pl.program_id(0)
    dt = wg_ref.shape[1] // LANES
    slot = _gather_pipeline(i, pl.num_programs(0), t_hbm, cur_ref, nxt_ref, MOE_TM, dt, buf, sem)
    xt = _load_row_tiles(buf, slot * MOE_TM, MOE_TM, dt).astype(BF16)
    hg = jnp.dot(xt, wg_ref[0], preferred_element_type=F32)
    hu = jnp.dot(xt, wu_ref[0], preferred_element_type=F32)
    hid = (hg * jax.nn.sigmoid(hg)) * hu
    _store_row_tiles(y_ref, 0, jnp.dot(hid.astype(BF16), wd_ref[0], preferred_element_type=F32))


def _experts(tile_expert, row_token, t, wg, wu, wd, d):
    dt = d // LANES
    ntiles = tile_expert.shape[0]
    ff = wg.shape[2]
    tok3 = row_token.reshape(ntiles, 1, MOE_TM)
    smem_cur = pl.BlockSpec((1, 1, MOE_TM), lambda i, te: (i, 0, 0), memory_space=pltpu.SMEM)
    smem_nxt = pl.BlockSpec((1, 1, MOE_TM), lambda i, te: (jnp.minimum(i + 1, ntiles - 1), 0, 0),
                            memory_space=pltpu.SMEM)
    wspec = lambda s: pl.BlockSpec((1,) + s, lambda i, te: (te[i], 0, 0))
    return pl.pallas_call(
        _expert_kernel,
        out_shape=jax.ShapeDtypeStruct((ntiles * MOE_TM * dt, LANES), F32),
        grid_spec=pltpu.PrefetchScalarGridSpec(
            num_scalar_prefetch=1,
            grid=(ntiles,),
            in_specs=[smem_cur, smem_nxt, pl.BlockSpec(memory_space=pl.ANY),
                      wspec((d, ff)), wspec((d, ff)), wspec((ff, d))],
            out_specs=pl.BlockSpec((MOE_TM * dt, LANES), lambda i, te: (i, 0)),
            scratch_shapes=[pltpu.VMEM((2 * MOE_TM * dt, LANES), F32), pltpu.SemaphoreType.DMA((2,))],
        ),
        compiler_params=_cparams(("arbitrary",)),
        name="experts",
    )(tile_expert, tok3, tok3, t, wg, wu, wd)


def _final_kernel(cur_ref, nxt_ref, y_hbm, h_ref, rw_ref, fn_ref, o_ref, buf, sem):
    i = pl.program_id(0)
    dt = h_ref.shape[1] // LANES
    slot = _gather_pipeline(i, pl.num_programs(0), y_hbm, cur_ref, nxt_ref, 2 * FIN_TM, dt, buf, sem)
    rw = rw_ref[...]
    y0 = _load_row_tiles(buf, slot * 2 * FIN_TM, FIN_TM, dt)
    y1 = _load_row_tiles(buf, slot * 2 * FIN_TM + FIN_TM, FIN_TM, dt)
    h = h_ref[...] + rw[:, 0:1] * y0 + rw[:, 1:2] * y1
    o_ref[...] = h * lax.rsqrt(jnp.mean(h * h, axis=-1, keepdims=True) + RMS_EPS) * fn_ref[...]


def _final(pos_tiles, y_sorted, h, rw, final_norm):
    n, d = h.shape
    tm = FIN_TM
    nt = n // tm
    smem_cur = pl.BlockSpec((1, 1, 2 * tm), lambda i: (i, 0, 0), memory_space=pltpu.SMEM)
    smem_nxt = pl.BlockSpec((1, 1, 2 * tm), lambda i: (jnp.minimum(i + 1, nt - 1), 0, 0),
                            memory_space=pltpu.SMEM)
    return pl.pallas_call(
        _final_kernel,
        out_shape=jax.ShapeDtypeStruct((n, d), F32),
        grid=(nt,),
        in_specs=[smem_cur, smem_nxt, pl.BlockSpec(memory_space=pl.ANY),
                  pl.BlockSpec((tm, d), lambda i: (i, 0)), pl.BlockSpec((tm, ROUTE_COLS), lambda i: (i, 0)),
                  pl.BlockSpec((1, d), lambda i: (0, 0))],
        out_specs=pl.BlockSpec((tm, d), lambda i: (i, 0)),
        scratch_shapes=[pltpu.VMEM((2 * 2 * tm * (d // LANES), LANES), F32), pltpu.SemaphoreType.DMA((2,))],
        compiler_params=_cparams(("arbitrary",)),
        name="final",
    )(pos_tiles, pos_tiles, y_sorted, h, rw, final_norm.reshape(1, d))


def _route_plan(gid):
    n = gid.shape[0]
    e = gid.reshape(-1)
    onehot = (e[:, None] == jnp.arange(N_EXPERTS, dtype=I32)[None, :]).astype(F32)
    chunk = 256
    oh3 = onehot.reshape(-1, chunk, N_EXPERTS)
    within = jnp.einsum("ij,tjk->tik", jnp.tril(jnp.ones((chunk, chunk), F32)), oh3)
    totals = within[:, -1, :]
    before = jnp.cumsum(totals, axis=0) - totals
    csum = (within + before[:, None, :]).reshape(-1, N_EXPERTS)
    rank = jnp.sum(csum * onehot, axis=1).astype(I32) - 1
    counts = (before[-1] + totals[-1]).astype(I32)
    padded = ((counts + MOE_TM - 1) // MOE_TM) * MOE_TM
    seg_end = jnp.cumsum(padded)
    pos = (seg_end - padded)[e] + rank
    nrows = 2 * n + N_EXPERTS * MOE_TM
    row_token = jnp.zeros((nrows,), I32).at[pos].set(jnp.arange(2 * n, dtype=I32) // 2, unique_indices=True)
    tile_start = jnp.arange(nrows // MOE_TM, dtype=I32) * MOE_TM
    tile_expert = jnp.minimum(jnp.searchsorted(seg_end, tile_start, side="right"), N_EXPERTS - 1).astype(I32)
    return row_token, tile_expert, pos.reshape(n, 2)


def _pack_w_in(w):
    d = w.shape[0]
    o_kv, o_qi, o_ki, o_wi = 512, 768, 1024, 1088
    o_dil = o_wi + IDX_HEADS
    o_gate = o_dil + 9 * 256
    pad = jnp.zeros((d, LANES - IDX_DIM - IDX_HEADS), w.dtype)
    packed = jnp.concatenate([w[:, :o_ki], w[:, o_ki:o_wi], w[:, o_wi:o_dil], pad, w[:, o_dil:o_gate],
                              w[:, o_gate:]], axis=1)
    assert packed.shape[1] == C_END
    return packed.astype(BF16)


def kernel(x, attn_norm, w_in, kv_norm, w_uk, w_uv, rel_bias, w_branch_a, w_branch_b, w_out, ffn_norm,
           w_router_group, b_router_group, w_router_expert, b_router_expert, w_gate, w_up, w_down,
           final_norm):
    b, seq, d = x.shape
    n = b * seq
    nkc = seq // BLK
    assert w_in.shape[0] == 1, "one layer"
    x2 = x.reshape(n, d)

    outs = _proj(x2, attn_norm[0], _pack_w_in(w_in[0]), kv_norm[0])
    qa, ckv, qi, kw = outs[:4]
    dil = outs[4:13]
    gates = outs[13]

    dsa_bias, dil_bias = _bias_tiles(rel_bias, nkc)

    t3 = lambda a: jnp.swapaxes(a.reshape(b, seq, a.shape[-1]), 1, 2)
    kw3 = kw.reshape(b, seq, LANES)
    wiT = jnp.swapaxes(kw3[:, :, IDX_DIM:IDX_DIM + 8], 1, 2)
    kidx = kw3[:, :, :IDX_DIM].astype(BF16)
    ckv3 = ckv.reshape(b, seq, KV_LATENT)
    ckvT = jnp.swapaxes(ckv3, 1, 2)
    ya = _dsa(t3(qi), wiT, t3(qa), kidx, ckv3, ckvT, w_uk[0].astype(BF16), w_uv[0].astype(BF16), dsa_bias)

    os_, lses = [], []
    for g, (_, dilation) in enumerate(DIL_GROUPS):
        r3 = lambda a: a.reshape(b, seq, DIL_OUT)
        o, lse = _dilated_group(r3(dil[g]), r3(dil[3 + g]), r3(dil[6 + g]), dil_bias, g, dilation)
        os_.append(o.reshape(n, DIL_OUT))
        lses.append(lse.reshape(n, DIL_OUT))

    wr = jnp.concatenate([w_router_group[0],
                          jnp.swapaxes(w_router_expert[0], 0, 1).reshape(d, N_EXPERTS),
                          jnp.zeros((d, ROUTE_COLS - N_GROUPS - N_EXPERTS), F32)], axis=1)
    br = jnp.concatenate([b_router_group[0], b_router_expert[0].reshape(-1),
                          jnp.zeros((ROUTE_COLS - N_GROUPS - N_EXPERTS,), F32)]).reshape(1, ROUTE_COLS)
    h, t, rw, ri = _mix(x2, ya.reshape(n, -1), os_, lses, gates,
                        w_branch_a[0].astype(BF16), w_branch_b[0].astype(BF16), w_out[0].astype(BF16),
                        ffn_norm[0], wr, br)

    row_token, tile_expert, pos = _route_plan(ri[:, :2])
    y_sorted = _experts(tile_expert, row_token, t, w_gate[0].astype(BF16), w_up[0].astype(BF16),
                        w_down[0].astype(BF16), d)

    pos_tiles = jnp.swapaxes(pos.reshape(n // FIN_TM, FIN_TM, 2), 1, 2).reshape(n // FIN_TM, 1, 2 * FIN_TM)
    out = _final(pos_tiles, y_sorted, h, rw, final_norm)
    return out.reshape(b, seq, d)
```

```python
import functools
import math

import numpy as np
import jax
import jax.numpy as jnp
from jax import lax
from jax.experimental import pallas as pl
from jax.experimental.pallas import tpu as pltpu

F32 = jnp.float32
BF16 = jnp.bfloat16
I32 = jnp.int32
I16 = jnp.int16

LANES = 128
VMEM_LIMIT_BYTES = 56 * 1024 * 1024

HEAD_DIM = 64
DSA_HEADS = 8
KV_LATENT = 256
IDX_HEADS = 4
IDX_DIM = 64
TOPK_MAX = 256
DIL_GROUPS = ((128, 1), (512, 4), (2048, 16))
DIL_HPG = 4
DIL_OUT = DIL_HPG * HEAD_DIM
NUM_BUCKETS = 32
MAX_DISTANCE = 2048
N_GROUPS = 4
EXPERTS_PER_GROUP = 8
N_EXPERTS = N_GROUPS * EXPERTS_PER_GROUP
RMS_EPS = 1e-6
NEG = -1e30
INT_MIN = -2 ** 31
I16_MIN = -2 ** 15

BLK = 128
PROJ_TM = 512
MIX_TM = 256
MOE_TM = 256
FIN_TM = 256
ROUTE_COLS = 128


def _cparams(sem):
    return pltpu.CompilerParams(dimension_semantics=sem, vmem_limit_bytes=VMEM_LIMIT_BYTES)


C_QA = 0
C_KV = 512
C_QI = 768
C_KW = 1024
C_DIL = 1152
C_GATE = C_DIL + 9 * 256
C_END = C_GATE + 2048


def _proj_kernel(x_ref, g_ref, w_ref, kvg_ref, qa_ref, ckv_ref, qi_ref, kw_ref, *rest):
    dil_refs = rest[:9]
    gate_ref = rest[9]
    stage_ref = rest[10]
    x = x_ref[...]
    u = x * lax.rsqrt(jnp.mean(x * x, axis=-1, keepdims=True) + RMS_EPS) * g_ref[...]
    u = u.astype(BF16)

    def mm(a, b):
        return jnp.dot(u, w_ref[:, a:b], preferred_element_type=F32)

    qa_ref[...] = mm(C_QA, C_KV).astype(BF16)
    c = mm(C_KV, C_QI)
    c = c * lax.rsqrt(jnp.mean(c * c, axis=-1, keepdims=True) + RMS_EPS) * kvg_ref[...]
    ckv_ref[...] = c.astype(BF16)
    qi_ref[...] = mm(C_QI, C_KW).astype(BF16)
    kw_ref[...] = mm(C_KW, C_DIL)
    for j in range(9):
        val = mm(C_DIL + 256 * j, C_DIL + 256 * (j + 1))
        r = DIL_GROUPS[j % 3][1]
        if r == 1:
            dil_refs[j][...] = val.astype(BF16)
        else:
            for hf in range(2):
                stage_ref[hf] = val[:, hf * LANES:(hf + 1) * LANES]
            for rho in range(r):
                for hf in range(2):
                    dil_refs[j][:, rho * 256 + hf * LANES:rho * 256 + (hf + 1) * LANES] = (
                        stage_ref[hf, pl.ds(rho, val.shape[0] // r, stride=r), :].astype(BF16))
    for j in range(4):
        gate_ref[:, 512 * j:512 * (j + 1)] = mm(C_GATE + 512 * j, C_GATE + 512 * (j + 1)).astype(BF16)


def _proj(x2, attn_norm, w_packed, kv_norm):
    n, d = x2.shape
    tm = PROJ_TM
    row = lambda i: (i, 0)
    const = lambda i: (0, 0)
    outs = [jax.ShapeDtypeStruct((n, 512), BF16), jax.ShapeDtypeStruct((n, 256), BF16),
            jax.ShapeDtypeStruct((n, 256), BF16), jax.ShapeDtypeStruct((n, 128), F32)]
    for j in range(9):
        r = DIL_GROUPS[j % 3][1]
        outs.append(jax.ShapeDtypeStruct((n // r, r * 256), BF16))
    outs += [jax.ShapeDtypeStruct((n, 2048), BF16)]
    out_specs = [pl.BlockSpec((tm * s.shape[0] // n, s.shape[1]), row) for s in outs]
    return pl.pallas_call(
        _proj_kernel,
        out_shape=outs,
        grid=(n // tm,),
        in_specs=[pl.BlockSpec((tm, d), row), pl.BlockSpec((1, d), const),
                  pl.BlockSpec((d, C_END), const), pl.BlockSpec((1, KV_LATENT), const)],
        out_specs=out_specs,
        scratch_shapes=[pltpu.VMEM((2, tm, LANES), F32)],
        compiler_params=_cparams(("parallel",)),
        name="proj",
    )(x2, attn_norm.reshape(1, d), w_packed, kv_norm.reshape(1, KV_LATENT))


def _bucket_thresholds():
    max_exact = NUM_BUCKETS // 2
    d = np.arange(0, MAX_DISTANCE + 1)
    nf = np.maximum(d, 1).astype(np.float32)
    large = max_exact + (np.log(nf / np.float32(max_exact)) / np.float32(math.log(MAX_DISTANCE / max_exact))
                         * np.float32(NUM_BUCKETS - max_exact)).astype(np.int32)
    large = np.minimum(large, NUM_BUCKETS - 1)
    bucket = np.where(d < max_exact, d, large)
    assert np.all(np.diff(bucket) >= 0)
    return [int(np.argmax(bucket >= b)) for b in range(1, NUM_BUCKETS)]


_BUCKET_THR = _bucket_thresholds()


def _bias_from_distance(dist, tab_ref, heads):
    masks = [dist >= t for t in _BUCKET_THR]
    out = []
    for h in heads:
        v = jnp.full(dist.shape, tab_ref[0, h], F32)
        for b in range(1, NUM_BUCKETS):
            v = jnp.where(masks[b - 1], tab_ref[b, h], v)
        out.append(v)
    return out


def _dsa_bias_kernel(tab_ref, o_ref):
    delta = pl.program_id(0)
    j = lax.broadcasted_iota(I32, (BLK, BLK), 0)
    i = lax.broadcasted_iota(I32, (BLK, BLK), 1)
    dist = jnp.maximum(delta * BLK + i - j, 0)
    tiles = _bias_from_distance(dist, tab_ref, range(DSA_HEADS))
    for h in range(DSA_HEADS):
        o_ref[0, h] = tiles[h]


def _dil_bias_kernel(tab_ref, o_ref, *, dilations):
    g = pl.program_id(0)
    j = lax.broadcasted_iota(I32, (2 * BLK, BLK), 0)
    i = lax.broadcasted_iota(I32, (2 * BLK, BLK), 1)
    step = i + BLK - j
    valid = (step >= 0) & (step <= BLK)
    for gi, r in enumerate(dilations):
        @pl.when(g == gi)
        def _():
            dist = jnp.maximum(step, 0) * r
            heads = [DSA_HEADS + gi * DIL_HPG + hh for hh in range(DIL_HPG)]
            tiles = _bias_from_distance(dist, tab_ref, heads)
            for hh in range(DIL_HPG):
                o_ref[0, :, hh * BLK:(hh + 1) * BLK] = jnp.where(valid, tiles[hh], NEG)


def _bias_tiles(rel_bias, nkc):
    smem = pl.BlockSpec(memory_space=pltpu.SMEM)
    dsa = pl.pallas_call(
        _dsa_bias_kernel,
        out_shape=jax.ShapeDtypeStruct((nkc, DSA_HEADS, BLK, BLK), F32),
        grid=(nkc,),
        in_specs=[smem],
        out_specs=pl.BlockSpec((1, DSA_HEADS, BLK, BLK), lambda d: (d, 0, 0, 0)),
        compiler_params=_cparams(("parallel",)),
        name="dsa_bias",
    )(rel_bias)
    dil = pl.pallas_call(
        functools.partial(_dil_bias_kernel, dilations=tuple(r for _, r in DIL_GROUPS)),
        out_shape=jax.ShapeDtypeStruct((len(DIL_GROUPS), 2 * BLK, DIL_HPG * BLK), F32),
        grid=(len(DIL_GROUPS),),
        in_specs=[smem],
        out_specs=pl.BlockSpec((1, 2 * BLK, DIL_HPG * BLK), lambda g: (g, 0, 0)),
        compiler_params=_cparams(("parallel",)),
        name="dil_bias",
    )(rel_bias)
    return dsa, dil


SUP = 4
SROWS = SUP * BLK


def _fold_rows(x, rows):
    parts = [x[i:i + rows] for i in range(0, x.shape[0], rows)]
    while len(parts) > 1:
        parts = [parts[i] + parts[i + 1] for i in range(0, len(parts), 2)]
    return parts[0]


def _dsa_kernel(qiT_ref, wiT_ref, qaT_ref, kidx_ref, ckv_ref, ckvT_ref, wuk_ref, wuv_ref, bias_ref, tri_ref,
                y_ref, keys_ref, hi_ref, lo_ref, qlT_ref, x_ref, pT_ref, *, topk, nsc, qb0):
    qb = qb0 + pl.program_id(1)
    row = lax.broadcasted_iota(I32, (SROWS, BLK), 0)
    col = lax.broadcasted_iota(I32, (SROWS, BLK), 1)
    trips = [slice(sc * SROWS, (sc + 1) * SROWS) for sc in range(nsc)]

    def causal(sc):
        return row <= col + (qb * BLK - sc * SROWS)

    for h in range(DSA_HEADS):
        ql = jnp.dot(wuk_ref[h], qaT_ref[0, h * HEAD_DIM:(h + 1) * HEAD_DIM, :],
                     preferred_element_type=F32) * (HEAD_DIM ** -0.5)
        qlT_ref[:, h * BLK:(h + 1) * BLK] = ql.astype(BF16)

    wq = wiT_ref[0] * (IDX_HEADS ** -0.5)
    for sc, ts in enumerate(trips):
        kx = kidx_ref[0, ts, :]
        acc = jnp.zeros((SROWS, BLK), F32)
        for h in range(IDX_HEADS):
            s = jnp.dot(kx, qiT_ref[0, h * IDX_DIM:(h + 1) * IDX_DIM, :],
                        preferred_element_type=F32) * (IDX_DIM ** -0.5)
            acc = acc + wq[h:h + 1, :] * jnp.maximum(s, 0.0)
        acc = jnp.where(causal(sc), acc, NEG)
        bits = pltpu.bitcast(acc, I32)
        bits = jnp.where(bits == INT_MIN, 0, bits)
        key = bits ^ ((bits >> 31) & 0x7FFFFFFF)
        keys_ref[ts, :] = key
        hi_ref[ts, :] = (key >> 16).astype(I16)

    def count16(ref, pred):
        cnt = jnp.zeros((16, BLK), I16)
        for ts in trips:
            cnt = cnt + _fold_rows(jnp.where(pred(ref[ts, :]), jnp.int16(1), jnp.int16(0)), 16)
        return jnp.sum(cnt.astype(I32), axis=0, keepdims=True)

    def kth_largest16(ref, need):
        def body(it, lo):
            cand = lo + jnp.left_shift(jnp.int32(1), 15 - it)
            cand16 = cand.astype(I16)
            return jnp.where(count16(ref, lambda v: v >= cand16) >= need, cand, lo)
        return lax.fori_loop(0, 16, body, jnp.full((1, BLK), I16_MIN, I32))

    t_hi = kth_largest16(hi_ref, topk)
    t_hi16 = t_hi.astype(I16)
    n_gt_hi = count16(hi_ref, lambda v: v > t_hi16)
    for ts in trips:
        k = keys_ref[ts, :]
        low = (k & 0xFFFF) + I16_MIN
        lo_ref[ts, :] = jnp.where((k >> 16) == t_hi, low, I16_MIN).astype(I16)
    t_lo = kth_largest16(lo_ref, topk - n_gt_hi)
    t_lo16 = t_lo.astype(I16)
    thr = jnp.left_shift(t_hi, 16) | (t_lo - I16_MIN)
    ties_wanted = (topk - n_gt_hi - count16(lo_ref, lambda v: v > t_lo16)).astype(F32)

    m = [jnp.full((8, BLK), NEG, F32) for _ in range(DSA_HEADS)]
    ties_before = jnp.zeros((1, BLK), F32)
    for sc, ts in enumerate(trips):
        k = keys_ref[ts, :]
        tie = k == thr
        tie_rank = jnp.dot(tri_ref[...], jnp.where(tie, 1.0, 0.0).astype(BF16),
                           preferred_element_type=F32) + ties_before
        ties_before = tie_rank[SROWS - 1:SROWS, :]
        sel = ((k > thr) | (tie & (tie_rank <= ties_wanted))) & causal(sc)
        am = jnp.where(sel, 0.0, NEG)
        ck = ckv_ref[0, ts, :]
        for hp in range(DSA_HEADS // 2):
            lg2 = jnp.dot(ck, qlT_ref[:, 2 * hp * BLK:(2 * hp + 2) * BLK], preferred_element_type=F32)
            for h in (2 * hp, 2 * hp + 1):
                lg = lg2[:, (h % 2) * BLK:(h % 2 + 1) * BLK]
                for j in range(SUP):
                    rs = slice(j * BLK, (j + 1) * BLK)
                    delta = jnp.maximum(qb - (sc * SUP + j), 0)
                    x = lg[rs] + bias_ref[delta, h] + am[rs]
                    x_ref[sc * SROWS + j * BLK:sc * SROWS + (j + 1) * BLK, h * BLK:(h + 1) * BLK] = x
                    m[h] = jnp.maximum(m[h], jnp.max(x.reshape(BLK // 8, 8, BLK), axis=0))
    m = [jnp.max(v, axis=0, keepdims=True) for v in m]

    l = [jnp.zeros((8, BLK), F32) for _ in range(DSA_HEADS)]
    for sc, ts in enumerate(trips):
        for h in range(DSA_HEADS):
            hs = slice(h * BLK, (h + 1) * BLK)
            p = jnp.exp(x_ref[ts, hs] - m[h])
            pT_ref[ts, hs] = p.astype(BF16)
            l[h] = l[h] + jnp.sum(p.reshape(SROWS // 8, 8, BLK), axis=0)

    for hp in range(DSA_HEADS // 2):
        o2 = jnp.dot(ckvT_ref[0], pT_ref[:, 2 * hp * BLK:(2 * hp + 2) * BLK],
                     preferred_element_type=F32)
        for h in (2 * hp, 2 * hp + 1):
            inv = 1.0 / jnp.sum(l[h], axis=0, keepdims=True)
            oh = (o2[:, (h % 2) * BLK:(h % 2 + 1) * BLK] * inv).T.astype(BF16)
            yh = jnp.dot(oh, wuv_ref[h], preferred_element_type=F32)
            y_ref[0, :, h * HEAD_DIM:(h + 1) * HEAD_DIM] = yh.astype(BF16)


def _dsa_group(g, qiT, wiT, qaT, kidx, ckv, ckvT, wuk, wuv, bias_tiles, tri):
    b, seq, _ = ckv.shape
    nkc = seq // BLK
    nsc = g + 1
    nk = nsc * SROWS
    topk = min(TOPK_MAX, seq // 4)
    qblk = lambda rows: pl.BlockSpec((1, rows, BLK), lambda bi, qi: (bi, 0, g * SUP + qi))
    head3 = lambda cols: pl.BlockSpec((1, nk, cols), lambda bi, qi: (bi, 0, 0))
    const = lambda s: pl.BlockSpec(s, lambda bi, qi: (0,) * len(s))
    hl = DSA_HEADS * BLK
    return pl.pallas_call(
        functools.partial(_dsa_kernel, topk=topk, nsc=nsc, qb0=g * SUP),
        out_shape=jax.ShapeDtypeStruct((b, SROWS, DSA_HEADS * HEAD_DIM), BF16),
        grid=(b, SUP),
        in_specs=[qblk(IDX_HEADS * IDX_DIM), qblk(8), qblk(DSA_HEADS * HEAD_DIM),
                  head3(IDX_DIM), head3(KV_LATENT),
                  pl.BlockSpec((1, KV_LATENT, nk), lambda bi, qi: (bi, 0, 0)),
                  const((DSA_HEADS, KV_LATENT, HEAD_DIM)), const((DSA_HEADS, KV_LATENT, HEAD_DIM)),
                  const((nkc, DSA_HEADS, BLK, BLK)), const((SROWS, SROWS))],
        out_specs=pl.BlockSpec((1, BLK, DSA_HEADS * HEAD_DIM), lambda bi, qi: (bi, qi, 0)),
        scratch_shapes=[pltpu.VMEM((nk, BLK), I32),
                        pltpu.VMEM((nk, BLK), I16),
                        pltpu.VMEM((nk, BLK), I16),
                        pltpu.VMEM((KV_LATENT, hl), BF16),
                        pltpu.VMEM((nk, hl), F32),
                        pltpu.VMEM((nk, hl), BF16)],
        compiler_params=_cparams(("parallel", "arbitrary")),
        name=f"dsa_g{g}",
    )(qiT, wiT, qaT, kidx, ckv, ckvT, wuk, wuv, bias_tiles, tri)


def _dsa(qiT, wiT, qaT, kidx, ckv, ckvT, wuk, wuv, bias_tiles):
    seq = ckv.shape[1]
    assert seq % SROWS == 0 and seq >= 4 * TOPK_MAX
    tri = jnp.tril(jnp.ones((SROWS, SROWS), BF16))
    groups = [_dsa_group(g, qiT, wiT, qaT, kidx, ckv, ckvT, wuk, wuv, bias_tiles, tri)
              for g in range(seq // SROWS)]
    return jnp.concatenate(groups, axis=1)


DIL_UNROLL = 5


def _dil_kernel(q_ref, k_ref, v_ref, bm_ref, o_ref, lse_ref, vT_ref, *, nblk):
    hq = DIL_HPG * BLK
    rowh = lax.broadcasted_iota(I32, (hq, DIL_OUT), 0) // BLK
    colh = lax.broadcasted_iota(I32, (hq, DIL_OUT), 1) // HEAD_DIM
    same_head = rowh == colh

    for n in range(nblk):
        vT_ref[n] = v_ref[0, n * BLK:(n + 1) * BLK, :].astype(F32).T.astype(BF16)

    def block(qo, kw, vT, bm):
        q = q_ref[0, pl.ds(qo, BLK), :]
        qd = jnp.where(same_head, jnp.concatenate([q] * DIL_HPG, axis=0), jnp.zeros((), BF16))
        s = lax.dot_general(kw, qd, (((1,), (1,)), ((), ())), preferred_element_type=F32)
        s = s * (HEAD_DIM ** -0.5) + bm
        m = jnp.max(s, axis=0, keepdims=True)
        e = jnp.exp(s - m)
        l = jnp.sum(e, axis=0, keepdims=True)
        oT = jnp.dot(vT, e.astype(BF16), preferred_element_type=F32)
        inv = 1.0 / l
        lse = m + jnp.log(l)
        outs, lses = [], []
        for hh in range(DIL_HPG):
            qs = slice(hh * BLK, (hh + 1) * BLK)
            outs.append(oT[hh * HEAD_DIM:(hh + 1) * HEAD_DIM, qs] * inv[:, qs])
            lses.append(jnp.broadcast_to(lse[:, qs], (HEAD_DIM, BLK)))
        o_ref[0, pl.ds(qo, BLK), :] = jnp.concatenate(outs, axis=0).T.astype(BF16)
        lse_ref[0, pl.ds(qo, BLK), :] = jnp.concatenate(lses, axis=0).T

    block(0, k_ref[0, 0:BLK, :], vT_ref[0], bm_ref[0, BLK:, :])

    def body(n, carry):
        ko = pl.multiple_of((n - 1) * BLK, BLK)
        vT = jnp.concatenate([vT_ref[n - 1], vT_ref[n]], axis=1)
        block(pl.multiple_of(n * BLK, BLK), k_ref[0, pl.ds(ko, 2 * BLK), :], vT, bm_ref[0])
        return carry

    lax.fori_loop(1, nblk, body, 0, unroll=DIL_UNROLL)


def _dilated_group(q, k, v, bm, g, dilation, b):
    c = DIL_OUT
    ls = q.shape[0] // b
    nblk = ls // BLK
    view = lambda a: a.reshape(b, ls, dilation * c)
    blk = pl.BlockSpec((1, ls, c), lambda bi, ri: (bi, 0, ri))
    o, lse = pl.pallas_call(
        functools.partial(_dil_kernel, nblk=nblk),
        out_shape=[jax.ShapeDtypeStruct((b, ls, dilation * c), BF16),
                   jax.ShapeDtypeStruct((b, ls, dilation * c), F32)],
        grid=(b, dilation),
        in_specs=[blk, blk, blk, pl.BlockSpec((1, 2 * BLK, DIL_HPG * BLK), lambda bi, ri: (g, 0, 0))],
        out_specs=[blk, blk],
        scratch_shapes=[pltpu.VMEM((nblk, c, BLK), BF16)],
        compiler_params=_cparams(("parallel", "parallel")),
        name=f"dilated_g{g}",
    )(view(q), view(k), view(v), bm)
    return o.reshape(b * ls, dilation * c), lse.reshape(b * ls, dilation * c)


def _store_row_tiles(ref, base, val):
    rows, d = val.shape
    dt = d // LANES
    for s in range(dt):
        ref[pl.ds(base * dt + s, rows, stride=dt), :] = val[:, s * LANES:(s + 1) * LANES]


def _load_row_tiles(ref, base, rows, dt):
    return jnp.concatenate([ref[pl.ds(base * dt + s, rows, stride=dt), :] for s in range(dt)], axis=1)


MIX_COLS = 256


def _token_major(ref, stage_ref, r):
    if r == 1:
        return ref[...].astype(F32)
    rows = ref.shape[0]
    for rho in range(r):
        for hf in range(DIL_OUT // LANES):
            c0 = rho * DIL_OUT + hf * LANES
            stage_ref[hf, pl.ds(rho, rows, stride=r), :] = ref[:, c0:c0 + LANES].astype(F32)
    return jnp.concatenate([stage_ref[hf] for hf in range(DIL_OUT // LANES)], axis=1)


def _sigmoid(v):
    return 0.5 * jnp.tanh(0.5 * v) + 0.5


def _mix_kernel(x_ref, ya_ref, o1_ref, o2_ref, o3_ref, l1_ref, l2_ref, l3_ref, gate_ref,
                wa_ref, wb_ref, wo_ref, fg_ref, wr_ref, br_ref,
                h_ref, t_ref, rw_ref, ri_ref, mixed_ref, *stage_refs):
    dils = [r for _, r in DIL_GROUPS]
    o = [_token_major(ref, st, r) for ref, st, r in zip((o1_ref, o2_ref, o3_ref), stage_refs[:3], dils)]
    l1, l2, l3 = [_token_major(ref, st, r) for ref, st, r in zip((l1_ref, l2_ref, l3_ref), stage_refs[3:], dils)]
    mx = jnp.maximum(jnp.maximum(l1, l2), l3)
    e1, e2, e3 = jnp.exp(l1 - mx), jnp.exp(l2 - mx), jnp.exp(l3 - mx)
    inv = 1.0 / (e1 + e2 + e3)
    yb = ((e1 * inv) * o[0] + (e2 * inv) * o[1] + (e3 * inv) * o[2]).astype(BF16)
    ya = ya_ref[...]
    d = x_ref.shape[1]
    for c in range(0, d, MIX_COLS):
        cs = slice(c, c + MIX_COLS)
        a = jnp.dot(ya, wa_ref[:, cs], preferred_element_type=F32)
        bmix = jnp.dot(yb, wb_ref[:, cs], preferred_element_type=F32)
        g0 = _sigmoid(gate_ref[:, cs].astype(F32))
        g1 = _sigmoid(gate_ref[:, d + c:d + c + MIX_COLS].astype(F32))
        mixed_ref[:, cs] = (g0 * a + g1 * bmix).astype(BF16)
    h = x_ref[...] + jnp.dot(mixed_ref[...], wo_ref[...], preferred_element_type=F32)
    h_ref[...] = h
    t = h * lax.rsqrt(jnp.mean(h * h, axis=-1, keepdims=True) + RMS_EPS) * fg_ref[...]
    _store_row_tiles(t_ref, 0, t)

    t_hi = t.astype(BF16)
    t_lo = (t - t_hi.astype(F32)).astype(BF16)
    r1 = jnp.dot(t_hi, wr_ref[...], preferred_element_type=F32)
    r2 = jnp.dot(t_lo, wr_ref[:, :ROUTE_COLS], preferred_element_type=F32)
    logits = r1[:, :ROUTE_COLS] + (r1[:, ROUTE_COLS:] + r2) + br_ref[...]
    lane = lax.broadcasted_iota(I32, logits.shape, 1)
    ninf = -jnp.inf
    big = jnp.int32(10 ** 6)

    def first_argmax(v, vmax):
        return jnp.min(jnp.where(v == vmax, lane, big), axis=-1, keepdims=True)

    gl = jnp.where(lane < N_GROUPS, logits, ninf)
    gmax = jnp.max(gl, axis=-1, keepdims=True)
    gsel = first_argmax(gl, gmax)
    p_g = 1.0 / jnp.sum(jnp.exp(gl - gmax), axis=-1, keepdims=True)
    lo = N_GROUPS + gsel * EXPERTS_PER_GROUP
    el = jnp.where((lane >= lo) & (lane < lo + EXPERTS_PER_GROUP), logits, ninf)
    v1 = jnp.max(el, axis=-1, keepdims=True)
    i1 = first_argmax(el, v1)
    el2 = jnp.where(lane == i1, ninf, el)
    v2 = jnp.max(el2, axis=-1, keepdims=True)
    i2 = first_argmax(el2, v2)
    e2 = jnp.exp(v2 - v1)
    w1 = p_g / (1.0 + e2)
    w2 = p_g * e2 / (1.0 + e2)
    rw_ref[...] = jnp.where(lane == 0, w1, jnp.where(lane == 1, w2, 0.0))
    ri_ref[...] = jnp.where(lane == 0, i1 - N_GROUPS, jnp.where(lane == 1, i2 - N_GROUPS, 0))


def _mix(x2, ya, os_, lses, gates, wa, wb, wo, ffn_norm, wr, br):
    n, d = x2.shape
    tm = MIX_TM
    row = lambda c: pl.BlockSpec((tm, c), lambda i: (i, 0))
    res = lambda a: pl.BlockSpec((tm * a.shape[0] // n, a.shape[1]), lambda i: (i, 0))
    const = lambda s: pl.BlockSpec(s, lambda i: (0, 0))
    return pl.pallas_call(
        _mix_kernel,
        out_shape=[jax.ShapeDtypeStruct((n, d), F32), jax.ShapeDtypeStruct((n * (d // LANES), LANES), F32),
                   jax.ShapeDtypeStruct((n, ROUTE_COLS), F32), jax.ShapeDtypeStruct((n, ROUTE_COLS), I32)],
        grid=(n // tm,),
        in_specs=[row(d), row(512)] + [res(a) for a in os_] + [res(a) for a in lses] + [row(2 * d),
                  const(wa.shape), const(wb.shape), const(wo.shape), const((1, d)),
                  const(wr.shape), const((1, ROUTE_COLS))],
        out_specs=[row(d), pl.BlockSpec((tm * (d // LANES), LANES), lambda i: (i, 0)),
                   row(ROUTE_COLS), row(ROUTE_COLS)],
        scratch_shapes=[pltpu.VMEM((tm, d), BF16)] + [pltpu.VMEM((DIL_OUT // LANES, tm, LANES), F32)] * 6,
        compiler_params=_cparams(("parallel",)),
        name="mix",
    )(x2, ya, *os_, *lses, gates, wa, wb, wo, ffn_norm.reshape(1, d), wr, br)


GATHER_UNROLL = 8


def _start_row_gather(src_hbm, idx_ref, nrows, dt, buf, sem, slot):
    def body(g, c):
        for u in range(GATHER_UNROLL):
            r = g * GATHER_UNROLL + u
            src = pl.multiple_of(idx_ref[0, 0, r] * dt, dt)
            dst = pl.multiple_of((slot * nrows + r) * dt, dt)
            pltpu.make_async_copy(src_hbm.at[pl.ds(src, dt)], buf.at[pl.ds(dst, dt)], sem.at[slot]).start()
        return c
    lax.fori_loop(0, nrows // GATHER_UNROLL, body, 0)


def _wait_row_gather(src_hbm, nrows, dt, buf, sem, slot):
    dst = pl.multiple_of(slot * nrows * dt, dt)
    pltpu.make_async_copy(src_hbm.at[pl.ds(0, nrows * dt)], buf.at[pl.ds(dst, nrows * dt)], sem.at[slot]).wait()


def _gather_pipeline(i, nsteps, src_hbm, cur_ref, nxt_ref, nrows, dt, buf, sem):
    slot = i % 2

    @pl.when(i == 0)
    def _():
        _start_row_gather(src_hbm, cur_ref, nrows, dt, buf, sem, 0)

    @pl.when(i + 1 < nsteps)
    def _():
        _start_row_gather(src_hbm, nxt_ref, nrows, dt, buf, sem, 1 - slot)

    _wait_row_gather(src_hbm, nrows, dt, buf, sem, slot)
    return slot


def _expert_kernel(te_ref, cur_ref, nxt_ref, t_hbm, wg_ref, wu_ref, wd_ref, y_ref, buf, sem):
    i = pl.program_id(0)
    dt = wg_ref.shape[1] // LANES
    slot = _gather_pipeline(i, pl.num_programs(0), t_hbm, cur_ref, nxt_ref, MOE_TM, dt, buf, sem)
    xt = _load_row_tiles(buf, slot * MOE_TM, MOE_TM, dt).astype(BF16)
    hg = jnp.dot(xt, wg_ref[0], preferred_element_type=F32)
    hu = jnp.dot(xt, wu_ref[0], preferred_element_type=F32)
    hid = (hg * jax.nn.sigmoid(hg)) * hu
    _store_row_tiles(y_ref, 0, jnp.dot(hid.astype(BF16), wd_ref[0], preferred_element_type=F32))


def _experts(tile_expert, row_token, t, wg, wu, wd, d):
    dt = d // LANES
    ntiles = tile_expert.shape[0]
    ff = wg.shape[2]
    tok3 = row_token.reshape(ntiles, 1, MOE_TM)
    smem_cur = pl.BlockSpec((1, 1, MOE_TM), lambda i, te: (i, 0, 0), memory_space=pltpu.SMEM)
    smem_nxt = pl.BlockSpec((1, 1, MOE_TM), lambda i, te: (jnp.minimum(i + 1, ntiles - 1), 0, 0),
                            memory_space=pltpu.SMEM)
    wspec = lambda s: pl.BlockSpec((1,) + s, lambda i, te: (te[i], 0, 0))
    return pl.pallas_call(
        _expert_kernel,
        out_shape=jax.ShapeDtypeStruct((ntiles * MOE_TM * dt, LANES), F32),
        grid_spec=pltpu.PrefetchScalarGridSpec(
            num_scalar_prefetch=1,
            grid=(ntiles,),
            in_specs=[smem_cur, smem_nxt, pl.BlockSpec(memory_space=pl.ANY),
                      wspec((d, ff)), wspec((d, ff)), wspec((ff, d))],
            out_specs=pl.BlockSpec((MOE_TM * dt, LANES), lambda i, te: (i, 0)),
            scratch_shapes=[pltpu.VMEM((2 * MOE_TM * dt, LANES), F32), pltpu.SemaphoreType.DMA((2,))],
        ),
        compiler_params=_cparams(("arbitrary",)),
        name="experts",
    )(tile_expert, tok3, tok3, t, wg, wu, wd)


def _final_kernel(cur_ref, nxt_ref, y_hbm, h_ref, rw_ref, fn_ref, o_ref, buf, sem):
    i = pl.program_id(0)
    dt = h_ref.shape[1] // LANES
    slot = _gather_pipeline(i, pl.num_programs(0), y_hbm, cur_ref, nxt_ref, 2 * FIN_TM, dt, buf, sem)
    rw = rw_ref[...]
    y0 = _load_row_tiles(buf, slot * 2 * FIN_TM, FIN_TM, dt)
    y1 = _load_row_tiles(buf, slot * 2 * FIN_TM + FIN_TM, FIN_TM, dt)
    h = h_ref[...] + rw[:, 0:1] * y0 + rw[:, 1:2] * y1
    o_ref[...] = h * lax.rsqrt(jnp.mean(h * h, axis=-1, keepdims=True) + RMS_EPS) * fn_ref[...]


def _final(pos_tiles, y_sorted, h, rw, final_norm):
    n, d = h.shape
    tm = FIN_TM
    nt = n // tm
    smem_cur = pl.BlockSpec((1, 1, 2 * tm), lambda i: (i, 0, 0), memory_space=pltpu.SMEM)
    smem_nxt = pl.BlockSpec((1, 1, 2 * tm), lambda i: (jnp.minimum(i + 1, nt - 1), 0, 0),
                            memory_space=pltpu.SMEM)
    return pl.pallas_call(
        _final_kernel,
        out_shape=jax.ShapeDtypeStruct((n, d), F32),
        grid=(nt,),
        in_specs=[smem_cur, smem_nxt, pl.BlockSpec(memory_space=pl.ANY),
                  pl.BlockSpec((tm, d), lambda i: (i, 0)), pl.BlockSpec((tm, ROUTE_COLS), lambda i: (i, 0)),
                  pl.BlockSpec((1, d), lambda i: (0, 0))],
        out_specs=pl.BlockSpec((tm, d), lambda i: (i, 0)),
        scratch_shapes=[pltpu.VMEM((2 * 2 * tm * (d // LANES), LANES), F32), pltpu.SemaphoreType.DMA((2,))],
        compiler_params=_cparams(("arbitrary",)),
        name="final",
    )(pos_tiles, pos_tiles, y_sorted, h, rw, final_norm.reshape(1, d))


def _route_plan(gid):
    n = gid.shape[0]
    e = gid.reshape(-1)
    onehot = (e[:, None] == jnp.arange(N_EXPERTS, dtype=I32)[None, :]).astype(F32)
    chunk = 256
    oh3 = onehot.reshape(-1, chunk, N_EXPERTS)
    within = jnp.einsum("ij,tjk->tik", jnp.tril(jnp.ones((chunk, chunk), F32)), oh3)
    totals = within[:, -1, :]
    before = jnp.cumsum(totals, axis=0) - totals
    csum = (within + before[:, None, :]).reshape(-1, N_EXPERTS)
    rank = jnp.sum(csum * onehot, axis=1).astype(I32) - 1
    counts = (before[-1] + totals[-1]).astype(I32)
    padded = ((counts + MOE_TM - 1) // MOE_TM) * MOE_TM
    seg_end = jnp.cumsum(padded)
    pos = (seg_end - padded)[e] + rank
    nrows = 2 * n + N_EXPERTS * MOE_TM
    row_token = jnp.zeros((nrows,), I32).at[pos].set(jnp.arange(2 * n, dtype=I32) // 2, unique_indices=True)
    tile_start = jnp.arange(nrows // MOE_TM, dtype=I32) * MOE_TM
    tile_expert = jnp.minimum(jnp.sum(tile_start[:, None] >= seg_end[None, :], axis=1), N_EXPERTS - 1).astype(I32)
    return row_token, tile_expert, pos.reshape(n, 2)


def _pack_w_in(w):
    d = w.shape[0]
    o_kv, o_qi, o_ki, o_wi = 512, 768, 1024, 1088
    o_dil = o_wi + IDX_HEADS
    o_gate = o_dil + 9 * 256
    pad = jnp.zeros((d, LANES - IDX_DIM - IDX_HEADS), w.dtype)
    packed = jnp.concatenate([w[:, :o_ki], w[:, o_ki:o_wi], w[:, o_wi:o_dil], pad, w[:, o_dil:o_gate],
                              w[:, o_gate:]], axis=1)
    assert packed.shape[1] == C_END
    return packed.astype(BF16)


def kernel(x, attn_norm, w_in, kv_norm, w_uk, w_uv, rel_bias, w_branch_a, w_branch_b, w_out, ffn_norm,
           w_router_group, b_router_group, w_router_expert, b_router_expert, w_gate, w_up, w_down,
           final_norm):
    b, seq, d = x.shape
    n = b * seq
    nkc = seq // BLK
    assert w_in.shape[0] == 1, "one layer"
    x2 = x.reshape(n, d)

    outs = _proj(x2, attn_norm[0], _pack_w_in(w_in[0]), kv_norm[0])
    qa, ckv, qi, kw = outs[:4]
    dil = outs[4:13]
    gates = outs[13]

    dsa_bias, dil_bias = _bias_tiles(rel_bias, nkc)

    t3 = lambda a: jnp.swapaxes(a.reshape(b, seq, a.shape[-1]), 1, 2)
    kw3 = kw.reshape(b, seq, LANES)
    wiT = jnp.swapaxes(kw3[:, :, IDX_DIM:IDX_DIM + 8], 1, 2)
    kidx = kw3[:, :, :IDX_DIM].astype(BF16)
    ckv3 = ckv.reshape(b, seq, KV_LATENT)
    ckvT = jnp.swapaxes(ckv3, 1, 2)
    ya = _dsa(t3(qi), wiT, t3(qa), kidx, ckv3, ckvT, w_uk[0].astype(BF16), w_uv[0].astype(BF16), dsa_bias)

    os_, lses = [], []
    for g, (_, dilation) in enumerate(DIL_GROUPS):
        o, lse = _dilated_group(dil[g], dil[3 + g], dil[6 + g], dil_bias, g, dilation, b)
        os_.append(o)
        lses.append(lse)

    wr = jnp.concatenate([w_router_group[0],
                          jnp.swapaxes(w_router_expert[0], 0, 1).reshape(d, N_EXPERTS),
                          jnp.zeros((d, ROUTE_COLS - N_GROUPS - N_EXPERTS), F32)], axis=1)
    wr_hi = wr.astype(BF16)
    wr = jnp.concatenate([wr_hi, (wr - wr_hi.astype(F32)).astype(BF16)], axis=1)
    br = jnp.concatenate([b_router_group[0], b_router_expert[0].reshape(-1),
                          jnp.zeros((ROUTE_COLS - N_GROUPS - N_EXPERTS,), F32)]).reshape(1, ROUTE_COLS)
    h, t, rw, ri = _mix(x2, ya.reshape(n, -1), os_, lses, gates,
                        w_branch_a[0].astype(BF16), w_branch_b[0].astype(BF16), w_out[0].astype(BF16),
                        ffn_norm[0], wr, br)

    row_token, tile_expert, pos = _route_plan(ri[:, :2])
    y_sorted = _experts(tile_expert, row_token, t, w_gate[0].astype(BF16), w_up[0].astype(BF16),
                        w_down[0].astype(BF16), d)

    pos_tiles = jnp.swapaxes(pos.reshape(n // FIN_TM, FIN_TM, 2), 1, 2).reshape(n // FIN_TM, 1, 2 * FIN_TM)
    out = _final(pos_tiles, y_sorted, h, rw, final_norm)
    return out.reshape(b, seq, d)
```

```python
import functools
import math

import numpy as np
import jax
import jax.numpy as jnp
from jax import lax
from jax.experimental import pallas as pl
from jax.experimental.pallas import tpu as pltpu

F32 = jnp.float32
BF16 = jnp.bfloat16
I32 = jnp.int32

LANES = 128
VMEM_LIMIT_BYTES = 56 * 1024 * 1024

HEAD_DIM = 64
DSA_HEADS = 8
KV_LATENT = 256
IDX_HEADS = 4
IDX_DIM = 64
TOPK_MAX = 256
DIL_GROUPS = ((128, 1), (512, 4), (2048, 16))
DIL_HPG = 4
DIL_OUT = DIL_HPG * HEAD_DIM
NUM_BUCKETS = 32
MAX_DISTANCE = 2048
N_GROUPS = 4
EXPERTS_PER_GROUP = 8
N_EXPERTS = N_GROUPS * EXPERTS_PER_GROUP
RMS_EPS = 1e-6
NEG = -1e30
INT_MIN = -2 ** 31

BLK = 128
PROJ_TM = 512
MIX_TM = 256
MOE_TM = 256
FIN_TM = 256
ROUTE_COLS = 128


def _cparams(sem):
    return pltpu.CompilerParams(dimension_semantics=sem, vmem_limit_bytes=VMEM_LIMIT_BYTES)


C_QA = 0
C_KV = 512
C_QI = 768
C_KW = 1024
C_DIL = 1152
C_GATE = C_DIL + 9 * 256
C_END = C_GATE + 2048


def _proj_kernel(x_ref, g_ref, w_ref, kvg_ref, qa_ref, ckv_ref, qi_ref, kw_ref, *rest):
    dil_refs = rest[:9]
    gate_ref = rest[9]
    stage_ref = rest[10]
    x = x_ref[...]
    u = x * lax.rsqrt(jnp.mean(x * x, axis=-1, keepdims=True) + RMS_EPS) * g_ref[...]
    u = u.astype(BF16)

    def mm(a, b):
        return jnp.dot(u, w_ref[:, a:b], preferred_element_type=F32)

    qa_ref[...] = mm(C_QA, C_KV).astype(BF16)
    c = mm(C_KV, C_QI)
    c = c * lax.rsqrt(jnp.mean(c * c, axis=-1, keepdims=True) + RMS_EPS) * kvg_ref[...]
    ckv_ref[...] = c.astype(BF16)
    qi_ref[...] = mm(C_QI, C_KW).astype(BF16)
    kw_ref[...] = mm(C_KW, C_DIL)
    for j in range(9):
        val = mm(C_DIL + 256 * j, C_DIL + 256 * (j + 1))
        r = DIL_GROUPS[j % 3][1]
        if r == 1:
            dil_refs[j][...] = val.astype(BF16)
        else:
            for hf in range(2):
                stage_ref[hf] = val[:, hf * LANES:(hf + 1) * LANES]
            for rho in range(r):
                for hf in range(2):
                    dil_refs[j][:, rho * 256 + hf * LANES:rho * 256 + (hf + 1) * LANES] = (
                        stage_ref[hf, pl.ds(rho, val.shape[0] // r, stride=r), :].astype(BF16))
    for j in range(4):
        gate_ref[:, 512 * j:512 * (j + 1)] = mm(C_GATE + 512 * j, C_GATE + 512 * (j + 1)).astype(BF16)


def _proj(x2, attn_norm, w_packed, kv_norm):
    n, d = x2.shape
    tm = PROJ_TM
    row = lambda i: (i, 0)
    const = lambda i: (0, 0)
    outs = [jax.ShapeDtypeStruct((n, 512), BF16), jax.ShapeDtypeStruct((n, 256), BF16),
            jax.ShapeDtypeStruct((n, 256), BF16), jax.ShapeDtypeStruct((n, 128), F32)]
    for j in range(9):
        r = DIL_GROUPS[j % 3][1]
        outs.append(jax.ShapeDtypeStruct((n // r, r * 256), BF16))
    outs += [jax.ShapeDtypeStruct((n, 2048), BF16)]
    out_specs = [pl.BlockSpec((tm * s.shape[0] // n, s.shape[1]), row) for s in outs]
    return pl.pallas_call(
        _proj_kernel,
        out_shape=outs,
        grid=(n // tm,),
        in_specs=[pl.BlockSpec((tm, d), row), pl.BlockSpec((1, d), const),
                  pl.BlockSpec((d, C_END), const), pl.BlockSpec((1, KV_LATENT), const)],
        out_specs=out_specs,
        scratch_shapes=[pltpu.VMEM((2, tm, LANES), F32)],
        compiler_params=_cparams(("parallel",)),
        name="proj",
    )(x2, attn_norm.reshape(1, d), w_packed, kv_norm.reshape(1, KV_LATENT))


def _bucket_thresholds():
    max_exact = NUM_BUCKETS // 2
    d = np.arange(0, MAX_DISTANCE + 1)
    nf = np.maximum(d, 1).astype(np.float32)
    large = max_exact + (np.log(nf / np.float32(max_exact)) / np.float32(math.log(MAX_DISTANCE / max_exact))
                         * np.float32(NUM_BUCKETS - max_exact)).astype(np.int32)
    large = np.minimum(large, NUM_BUCKETS - 1)
    bucket = np.where(d < max_exact, d, large)
    assert np.all(np.diff(bucket) >= 0)
    return [int(np.argmax(bucket >= b)) for b in range(1, NUM_BUCKETS)]


_BUCKET_THR = _bucket_thresholds()


def _bias_from_distance(dist, tab_ref, heads):
    masks = [dist >= t for t in _BUCKET_THR]
    out = []
    for h in heads:
        v = jnp.full(dist.shape, tab_ref[0, h], F32)
        for b in range(1, NUM_BUCKETS):
            v = jnp.where(masks[b - 1], tab_ref[b, h], v)
        out.append(v)
    return out


def _dsa_bias_kernel(tab_ref, o_ref):
    delta = pl.program_id(0)
    j = lax.broadcasted_iota(I32, (BLK, BLK), 0)
    i = lax.broadcasted_iota(I32, (BLK, BLK), 1)
    dist = jnp.maximum(delta * BLK + i - j, 0)
    tiles = _bias_from_distance(dist, tab_ref, range(DSA_HEADS))
    for h in range(DSA_HEADS):
        o_ref[0, h] = tiles[h]


def _dil_bias_kernel(tab_ref, o_ref, *, dilations):
    g = pl.program_id(0)
    j = lax.broadcasted_iota(I32, (2 * BLK, BLK), 0)
    i = lax.broadcasted_iota(I32, (2 * BLK, BLK), 1)
    step = i + BLK - j
    valid = (step >= 0) & (step <= BLK)
    for gi, r in enumerate(dilations):
        @pl.when(g == gi)
        def _():
            dist = jnp.maximum(step, 0) * r
            heads = [DSA_HEADS + gi * DIL_HPG + hh for hh in range(DIL_HPG)]
            tiles = _bias_from_distance(dist, tab_ref, heads)
            for hh in range(DIL_HPG):
                o_ref[0, :, hh * BLK:(hh + 1) * BLK] = jnp.where(valid, tiles[hh], NEG)


def _bias_tiles(rel_bias, nkc):
    smem = pl.BlockSpec(memory_space=pltpu.SMEM)
    dsa = pl.pallas_call(
        _dsa_bias_kernel,
        out_shape=jax.ShapeDtypeStruct((nkc, DSA_HEADS, BLK, BLK), F32),
        grid=(nkc,),
        in_specs=[smem],
        out_specs=pl.BlockSpec((1, DSA_HEADS, BLK, BLK), lambda d: (d, 0, 0, 0)),
        compiler_params=_cparams(("parallel",)),
        name="dsa_bias",
    )(rel_bias)
    dil = pl.pallas_call(
        functools.partial(_dil_bias_kernel, dilations=tuple(r for _, r in DIL_GROUPS)),
        out_shape=jax.ShapeDtypeStruct((len(DIL_GROUPS), 2 * BLK, DIL_HPG * BLK), F32),
        grid=(len(DIL_GROUPS),),
        in_specs=[smem],
        out_specs=pl.BlockSpec((1, 2 * BLK, DIL_HPG * BLK), lambda g: (g, 0, 0)),
        compiler_params=_cparams(("parallel",)),
        name="dil_bias",
    )(rel_bias)
    return dsa, dil


SUP = 4
SROWS = SUP * BLK


PLANE_KEYS = 32 * 8


def _bit_planes(words):
    x = list(words)
    j, m = 16, 0x0000FFFF
    while j:
        k = 0
        while k < 32:
            t = (x[k] ^ lax.shift_right_logical(x[k + j], jnp.int32(j))) & m
            x[k] = x[k] ^ t
            x[k + j] = x[k + j] ^ jnp.left_shift(t, jnp.int32(j))
            k = (k + j + 1) & ~j
        j >>= 1
        m = (m ^ (m << j)) & 0xFFFFFFFF
        m = m - (1 << 32) if m >= (1 << 31) else m
    return x


def _dsa_kernel(qiT_ref, wiT_ref, qaT_ref, kidx_ref, ckv_ref, ckvT_ref, wuk_ref, wuv_ref, bias_ref, tri_ref,
                y_ref, sc_ref, planes_ref, qlT_ref, x_ref, pT_ref, *, topk, nsc, qb0):
    qb = qb0 + pl.program_id(1)
    row = lax.broadcasted_iota(I32, (SROWS, BLK), 0)
    col = lax.broadcasted_iota(I32, (SROWS, BLK), 1)
    trips = [slice(sc * SROWS, (sc + 1) * SROWS) for sc in range(nsc)]

    def causal(sc):
        return row <= col + (qb * BLK - sc * SROWS)

    for h in range(DSA_HEADS):
        ql = jnp.dot(wuk_ref[h], qaT_ref[0, h * HEAD_DIM:(h + 1) * HEAD_DIM, :],
                     preferred_element_type=F32) * (HEAD_DIM ** -0.5)
        qlT_ref[:, h * BLK:(h + 1) * BLK] = ql.astype(BF16)

    wq = wiT_ref[0] * (IDX_HEADS ** -0.5)
    for sc, ts in enumerate(trips):
        kx = kidx_ref[0, ts, :]
        acc = jnp.zeros((SROWS, BLK), F32)
        for h in range(IDX_HEADS):
            s = jnp.dot(kx, qiT_ref[0, h * IDX_DIM:(h + 1) * IDX_DIM, :],
                        preferred_element_type=F32) * (IDX_DIM ** -0.5)
            acc = acc + wq[h:h + 1, :] * jnp.maximum(s, 0.0)
        acc = jnp.where(causal(sc), acc, NEG)
        sc_ref[ts, :] = acc
        bits = pltpu.bitcast(acc, I32)
        bits = jnp.where(bits == INT_MIN, 0, bits)
        ukey = bits ^ ((bits >> 31) & 0x7FFFFFFF) ^ INT_MIN
        for grp in range(SROWS // PLANE_KEYS):
            tiles = [ukey[grp * PLANE_KEYS + j * 8:grp * PLANE_KEYS + (j + 1) * 8] for j in range(32)]
            for b, plane in enumerate(_bit_planes(tiles)):
                planes_ref[b, sc * (SROWS // PLANE_KEYS) + grp] = plane

    def count(pred):
        cnt = jnp.zeros((8, BLK), I32)
        for ts in trips:
            cnt = cnt + jnp.sum(jnp.where(pred(sc_ref[ts, :]), 1, 0).reshape(SROWS // 8, 8, BLK), axis=0)
        return jnp.sum(cnt, axis=0, keepdims=True)

    def as_float(key):
        return pltpu.bitcast(key ^ ((key >> 31) & 0x7FFFFFFF), F32)

    def bit_body(it, carry):
        alive, above, code = carry
        ones = alive & planes_ref[it]
        c = jnp.sum(jnp.sum(lax.population_count(ones), axis=0), axis=0, keepdims=True)
        take = above + c >= topk
        alive = jnp.where(take, ones, alive ^ ones)
        above = jnp.where(take, above, above + c)
        code = code | jnp.where(take, jnp.left_shift(jnp.int32(1), 31 - it), 0)
        return alive, above, code

    nw = nsc * (SROWS // PLANE_KEYS)
    _, _, code = lax.fori_loop(
        0, 32, bit_body,
        (jnp.full((nw, 8, BLK), -1, I32), jnp.zeros((1, BLK), I32), jnp.zeros((1, BLK), I32)))
    guess = as_float(code ^ INT_MIN)
    n_gt_guess = count(lambda v: v > guess)
    proven = (n_gt_guess < topk) & (count(lambda v: v >= guess) >= topk)

    def bisect():
        def thr_body(it, lo):
            cand = lo + jnp.left_shift(jnp.int32(1), 31 - it)
            cand_f = as_float(cand)
            return jnp.where(count(lambda v: v >= cand_f) >= topk, cand, lo)
        t = as_float(lax.fori_loop(0, 32, thr_body, jnp.full((1, BLK), INT_MIN, I32)))
        return t, count(lambda v: v > t)

    thr, n_gt = lax.cond(jnp.min(jnp.where(proven, 1, 0)) > 0, lambda: (guess, n_gt_guess), bisect)
    ties_wanted = (topk - n_gt).astype(F32)

    m = [jnp.full((8, BLK), NEG, F32) for _ in range(DSA_HEADS)]
    ties_before = jnp.zeros((1, BLK), F32)
    for sc, ts in enumerate(trips):
        k = sc_ref[ts, :]
        tie = k == thr
        tie_rank = jnp.dot(tri_ref[...], jnp.where(tie, 1.0, 0.0).astype(BF16),
                           preferred_element_type=F32) + ties_before
        ties_before = tie_rank[SROWS - 1:SROWS, :]
        sel = ((k > thr) | (tie & (tie_rank <= ties_wanted))) & causal(sc)
        am = jnp.where(sel, 0.0, NEG)
        ck = ckv_ref[0, ts, :]
        for hp in range(DSA_HEADS // 2):
            lg2 = jnp.dot(ck, qlT_ref[:, 2 * hp * BLK:(2 * hp + 2) * BLK], preferred_element_type=F32)
            for h in (2 * hp, 2 * hp + 1):
                lg = lg2[:, (h % 2) * BLK:(h % 2 + 1) * BLK]
                for j in range(SUP):
                    rs = slice(j * BLK, (j + 1) * BLK)
                    delta = jnp.maximum(qb - (sc * SUP + j), 0)
                    x = lg[rs] + bias_ref[delta, h] + am[rs]
                    x_ref[sc * SROWS + j * BLK:sc * SROWS + (j + 1) * BLK, h * BLK:(h + 1) * BLK] = x
                    m[h] = jnp.maximum(m[h], jnp.max(x.reshape(BLK // 8, 8, BLK), axis=0))
    m = [jnp.max(v, axis=0, keepdims=True) for v in m]

    l = [jnp.zeros((8, BLK), F32) for _ in range(DSA_HEADS)]
    for sc, ts in enumerate(trips):
        for h in range(DSA_HEADS):
            hs = slice(h * BLK, (h + 1) * BLK)
            p = jnp.exp(x_ref[ts, hs] - m[h])
            pT_ref[ts, hs] = p.astype(BF16)
            l[h] = l[h] + jnp.sum(p.reshape(SROWS // 8, 8, BLK), axis=0)

    for hp in range(DSA_HEADS // 2):
        o2 = jnp.dot(ckvT_ref[0], pT_ref[:, 2 * hp * BLK:(2 * hp + 2) * BLK],
                     preferred_element_type=F32)
        for h in (2 * hp, 2 * hp + 1):
            inv = 1.0 / jnp.sum(l[h], axis=0, keepdims=True)
            oh = (o2[:, (h % 2) * BLK:(h % 2 + 1) * BLK] * inv).T.astype(BF16)
            yh = jnp.dot(oh, wuv_ref[h], preferred_element_type=F32)
            y_ref[0, :, h * HEAD_DIM:(h + 1) * HEAD_DIM] = yh.astype(BF16)


def _dsa_group(g, qiT, wiT, qaT, kidx, ckv, ckvT, wuk, wuv, bias_tiles, tri):
    b, seq, _ = ckv.shape
    nkc = seq // BLK
    nsc = g + 1
    nk = nsc * SROWS
    topk = min(TOPK_MAX, seq // 4)
    qblk = lambda rows: pl.BlockSpec((1, rows, BLK), lambda bi, qi: (bi, 0, g * SUP + qi))
    head3 = lambda cols: pl.BlockSpec((1, nk, cols), lambda bi, qi: (bi, 0, 0))
    const = lambda s: pl.BlockSpec(s, lambda bi, qi: (0,) * len(s))
    hl = DSA_HEADS * BLK
    return pl.pallas_call(
        functools.partial(_dsa_kernel, topk=topk, nsc=nsc, qb0=g * SUP),
        out_shape=jax.ShapeDtypeStruct((b, SROWS, DSA_HEADS * HEAD_DIM), BF16),
        grid=(b, SUP),
        in_specs=[qblk(IDX_HEADS * IDX_DIM), qblk(8), qblk(DSA_HEADS * HEAD_DIM),
                  head3(IDX_DIM), head3(KV_LATENT),
                  pl.BlockSpec((1, KV_LATENT, nk), lambda bi, qi: (bi, 0, 0)),
                  const((DSA_HEADS, KV_LATENT, HEAD_DIM)), const((DSA_HEADS, KV_LATENT, HEAD_DIM)),
                  const((nkc, DSA_HEADS, BLK, BLK)), const((SROWS, SROWS))],
        out_specs=pl.BlockSpec((1, BLK, DSA_HEADS * HEAD_DIM), lambda bi, qi: (bi, qi, 0)),
        scratch_shapes=[pltpu.VMEM((nk, BLK), F32),
                        pltpu.VMEM((32, nk // PLANE_KEYS, 8, BLK), I32),
                        pltpu.VMEM((KV_LATENT, hl), BF16),
                        pltpu.VMEM((nk, hl), F32),
                        pltpu.VMEM((nk, hl), BF16)],
        compiler_params=_cparams(("parallel", "arbitrary")),
        name=f"dsa_g{g}",
    )(qiT, wiT, qaT, kidx, ckv, ckvT, wuk, wuv, bias_tiles, tri)


def _dsa(qiT, wiT, qaT, kidx, ckv, ckvT, wuk, wuv, bias_tiles):
    seq = ckv.shape[1]
    assert seq % SROWS == 0 and seq >= 4 * TOPK_MAX
    tri = jnp.tril(jnp.ones((SROWS, SROWS), BF16))
    groups = [_dsa_group(g, qiT, wiT, qaT, kidx, ckv, ckvT, wuk, wuv, bias_tiles, tri)
              for g in range(seq // SROWS)]
    return jnp.concatenate(groups, axis=1)


DIL_UNROLL = 5


def _dil_kernel(q_ref, k_ref, v_ref, bm_ref, o_ref, lse_ref, vT_ref, *, nblk):
    hq = DIL_HPG * BLK
    rowh = lax.broadcasted_iota(I32, (hq, DIL_OUT), 0) // BLK
    colh = lax.broadcasted_iota(I32, (hq, DIL_OUT), 1) // HEAD_DIM
    same_head = rowh == colh

    for n in range(nblk):
        vT_ref[n] = v_ref[0, n * BLK:(n + 1) * BLK, :].astype(F32).T.astype(BF16)

    def block(qo, kw, vT, bm):
        q = q_ref[0, pl.ds(qo, BLK), :]
        qd = jnp.where(same_head, jnp.concatenate([q] * DIL_HPG, axis=0), jnp.zeros((), BF16))
        s = lax.dot_general(kw, qd, (((1,), (1,)), ((), ())), preferred_element_type=F32)
        s = s * (HEAD_DIM ** -0.5) + bm
        m = jnp.max(s, axis=0, keepdims=True)
        e = jnp.exp(s - m)
        l = jnp.sum(e, axis=0, keepdims=True)
        oT = jnp.dot(vT, e.astype(BF16), preferred_element_type=F32)
        inv = 1.0 / l
        lse = m + jnp.log(l)
        outs, lses = [], []
        for hh in range(DIL_HPG):
            qs = slice(hh * BLK, (hh + 1) * BLK)
            outs.append(oT[hh * HEAD_DIM:(hh + 1) * HEAD_DIM, qs] * inv[:, qs])
            lses.append(jnp.broadcast_to(lse[:, qs], (HEAD_DIM, BLK)))
        o_ref[0, pl.ds(qo, BLK), :] = jnp.concatenate(outs, axis=0).T.astype(BF16)
        lse_ref[0, pl.ds(qo, BLK), :] = jnp.concatenate(lses, axis=0).T

    block(0, k_ref[0, 0:BLK, :], vT_ref[0], bm_ref[0, BLK:, :])

    def body(n, carry):
        ko = pl.multiple_of((n - 1) * BLK, BLK)
        vT = jnp.concatenate([vT_ref[n - 1], vT_ref[n]], axis=1)
        block(pl.multiple_of(n * BLK, BLK), k_ref[0, pl.ds(ko, 2 * BLK), :], vT, bm_ref[0])
        return carry

    lax.fori_loop(1, nblk, body, 0, unroll=DIL_UNROLL)


def _dilated_group(q, k, v, bm, g, dilation, b):
    c = DIL_OUT
    ls = q.shape[0] // b
    nblk = ls // BLK
    view = lambda a: a.reshape(b, ls, dilation * c)
    blk = pl.BlockSpec((1, ls, c), lambda bi, ri: (bi, 0, ri))
    o, lse = pl.pallas_call(
        functools.partial(_dil_kernel, nblk=nblk),
        out_shape=[jax.ShapeDtypeStruct((b, ls, dilation * c), BF16),
                   jax.ShapeDtypeStruct((b, ls, dilation * c), F32)],
        grid=(b, dilation),
        in_specs=[blk, blk, blk, pl.BlockSpec((1, 2 * BLK, DIL_HPG * BLK), lambda bi, ri: (g, 0, 0))],
        out_specs=[blk, blk],
        scratch_shapes=[pltpu.VMEM((nblk, c, BLK), BF16)],
        compiler_params=_cparams(("parallel", "parallel")),
        name=f"dilated_g{g}",
    )(view(q), view(k), view(v), bm)
    return o.reshape(b * ls, dilation * c), lse.reshape(b * ls, dilation * c)


def _store_row_tiles(ref, base, val):
    rows, d = val.shape
    dt = d // LANES
    for s in range(dt):
        ref[pl.ds(base * dt + s, rows, stride=dt), :] = val[:, s * LANES:(s + 1) * LANES]


def _load_row_tiles(ref, base, rows, dt):
    return jnp.concatenate([ref[pl.ds(base * dt + s, rows, stride=dt), :] for s in range(dt)], axis=1)


MIX_COLS = 256


def _token_major(ref, stage_ref, r):
    if r == 1:
        return ref[...].astype(F32)
    rows = ref.shape[0]
    for rho in range(r):
        for hf in range(DIL_OUT // LANES):
            c0 = rho * DIL_OUT + hf * LANES
            stage_ref[hf, pl.ds(rho, rows, stride=r), :] = ref[:, c0:c0 + LANES].astype(F32)
    return jnp.concatenate([stage_ref[hf] for hf in range(DIL_OUT // LANES)], axis=1)


def _sigmoid(v):
    return 0.5 * jnp.tanh(0.5 * v) + 0.5


def _mix_kernel(x_ref, ya_ref, o1_ref, o2_ref, o3_ref, l1_ref, l2_ref, l3_ref, gate_ref,
                wa_ref, wb_ref, wo_ref, fg_ref, wr_ref, br_ref,
                h_ref, t_ref, rw_ref, ri_ref, mixed_ref, *stage_refs):
    dils = [r for _, r in DIL_GROUPS]
    o = [_token_major(ref, st, r) for ref, st, r in zip((o1_ref, o2_ref, o3_ref), stage_refs[:3], dils)]
    l1, l2, l3 = [_token_major(ref, st, r) for ref, st, r in zip((l1_ref, l2_ref, l3_ref), stage_refs[3:], dils)]
    mx = jnp.maximum(jnp.maximum(l1, l2), l3)
    e1, e2, e3 = jnp.exp(l1 - mx), jnp.exp(l2 - mx), jnp.exp(l3 - mx)
    inv = 1.0 / (e1 + e2 + e3)
    yb = ((e1 * inv) * o[0] + (e2 * inv) * o[1] + (e3 * inv) * o[2]).astype(BF16)
    ya = ya_ref[...]
    d = x_ref.shape[1]
    for c in range(0, d, MIX_COLS):
        cs = slice(c, c + MIX_COLS)
        a = jnp.dot(ya, wa_ref[:, cs], preferred_element_type=F32)
        bmix = jnp.dot(yb, wb_ref[:, cs], preferred_element_type=F32)
        g0 = _sigmoid(gate_ref[:, cs].astype(F32))
        g1 = _sigmoid(gate_ref[:, d + c:d + c + MIX_COLS].astype(F32))
        mixed_ref[:, cs] = (g0 * a + g1 * bmix).astype(BF16)
    h = x_ref[...] + jnp.dot(mixed_ref[...], wo_ref[...], preferred_element_type=F32)
    h_ref[...] = h
    t = h * lax.rsqrt(jnp.mean(h * h, axis=-1, keepdims=True) + RMS_EPS) * fg_ref[...]
    _store_row_tiles(t_ref, 0, t)

    t_hi = t.astype(BF16)
    t_lo = (t - t_hi.astype(F32)).astype(BF16)
    r1 = jnp.dot(t_hi, wr_ref[...], preferred_element_type=F32)
    r2 = jnp.dot(t_lo, wr_ref[:, :ROUTE_COLS], preferred_element_type=F32)
    logits = r1[:, :ROUTE_COLS] + (r1[:, ROUTE_COLS:] + r2) + br_ref[...]
    lane = lax.broadcasted_iota(I32, logits.shape, 1)
    ninf = -jnp.inf
    big = jnp.int32(10 ** 6)

    def first_argmax(v, vmax):
        return jnp.min(jnp.where(v == vmax, lane, big), axis=-1, keepdims=True)

    gl = jnp.where(lane < N_GROUPS, logits, ninf)
    gmax = jnp.max(gl, axis=-1, keepdims=True)
    gsel = first_argmax(gl, gmax)
    p_g = 1.0 / jnp.sum(jnp.exp(gl - gmax), axis=-1, keepdims=True)
    lo = N_GROUPS + gsel * EXPERTS_PER_GROUP
    el = jnp.where((lane >= lo) & (lane < lo + EXPERTS_PER_GROUP), logits, ninf)
    v1 = jnp.max(el, axis=-1, keepdims=True)
    i1 = first_argmax(el, v1)
    el2 = jnp.where(lane == i1, ninf, el)
    v2 = jnp.max(el2, axis=-1, keepdims=True)
    i2 = first_argmax(el2, v2)
    e2 = jnp.exp(v2 - v1)
    w1 = p_g / (1.0 + e2)
    w2 = p_g * e2 / (1.0 + e2)
    rw_ref[...] = jnp.where(lane == 0, w1, jnp.where(lane == 1, w2, 0.0))
    ri_ref[...] = jnp.where(lane == 0, i1 - N_GROUPS, jnp.where(lane == 1, i2 - N_GROUPS, 0))


def _mix(x2, ya, os_, lses, gates, wa, wb, wo, ffn_norm, wr, br):
    n, d = x2.shape
    tm = MIX_TM
    row = lambda c: pl.BlockSpec((tm, c), lambda i: (i, 0))
    res = lambda a: pl.BlockSpec((tm * a.shape[0] // n, a.shape[1]), lambda i: (i, 0))
    const = lambda s: pl.BlockSpec(s, lambda i: (0, 0))
    return pl.pallas_call(
        _mix_kernel,
        out_shape=[jax.ShapeDtypeStruct((n, d), F32), jax.ShapeDtypeStruct((n * (d // LANES), LANES), F32),
                   jax.ShapeDtypeStruct((n, ROUTE_COLS), F32), jax.ShapeDtypeStruct((n, ROUTE_COLS), I32)],
        grid=(n // tm,),
        in_specs=[row(d), row(512)] + [res(a) for a in os_] + [res(a) for a in lses] + [row(2 * d),
                  const(wa.shape), const(wb.shape), const(wo.shape), const((1, d)),
                  const(wr.shape), const((1, ROUTE_COLS))],
        out_specs=[row(d), pl.BlockSpec((tm * (d // LANES), LANES), lambda i: (i, 0)),
                   row(ROUTE_COLS), row(ROUTE_COLS)],
        scratch_shapes=[pltpu.VMEM((tm, d), BF16)] + [pltpu.VMEM((DIL_OUT // LANES, tm, LANES), F32)] * 6,
        compiler_params=_cparams(("parallel",)),
        name="mix",
    )(x2, ya, *os_, *lses, gates, wa, wb, wo, ffn_norm.reshape(1, d), wr, br)


GATHER_UNROLL = 8


def _start_row_gather(src_hbm, idx_ref, nrows, dt, buf, sem, slot):
    def body(g, c):
        for u in range(GATHER_UNROLL):
            r = g * GATHER_UNROLL + u
            src = pl.multiple_of(idx_ref[0, 0, r] * dt, dt)
            dst = pl.multiple_of((slot * nrows + r) * dt, dt)
            pltpu.make_async_copy(src_hbm.at[pl.ds(src, dt)], buf.at[pl.ds(dst, dt)], sem.at[slot]).start()
        return c
    lax.fori_loop(0, nrows // GATHER_UNROLL, body, 0)


def _wait_row_gather(src_hbm, nrows, dt, buf, sem, slot):
    dst = pl.multiple_of(slot * nrows * dt, dt)
    pltpu.make_async_copy(src_hbm.at[pl.ds(0, nrows * dt)], buf.at[pl.ds(dst, nrows * dt)], sem.at[slot]).wait()


def _gather_pipeline(i, nsteps, src_hbm, cur_ref, nxt_ref, nrows, dt, buf, sem):
    slot = i % 2

    @pl.when(i == 0)
    def _():
        _start_row_gather(src_hbm, cur_ref, nrows, dt, buf, sem, 0)

    @pl.when(i + 1 < nsteps)
    def _():
        _start_row_gather(src_hbm, nxt_ref, nrows, dt, buf, sem, 1 - slot)

    _wait_row_gather(src_hbm, nrows, dt, buf, sem, slot)
    return slot


def _expert_kernel(te_ref, cur_ref, nxt_ref, t_hbm, wg_ref, wu_ref, wd_ref, y_ref, buf, sem):
    i = pl.program_id(0)
    dt = wg_ref.shape[1] // LANES
    slot = _gather_pipeline(i, pl.num_programs(0), t_hbm, cur_ref, nxt_ref, MOE_TM, dt, buf, sem)
    xt = _load_row_tiles(buf, slot * MOE_TM, MOE_TM, dt).astype(BF16)
    hg = jnp.dot(xt, wg_ref[0], preferred_element_type=F32)
    hu = jnp.dot(xt, wu_ref[0], preferred_element_type=F32)
    hid = (hg * jax.nn.sigmoid(hg)) * hu
    _store_row_tiles(y_ref, 0, jnp.dot(hid.astype(BF16), wd_ref[0], preferred_element_type=F32))


def _experts(tile_expert, row_token, t, wg, wu, wd, d):
    dt = d // LANES
    ntiles = tile_expert.shape[0]
    ff = wg.shape[2]
    tok3 = row_token.reshape(ntiles, 1, MOE_TM)
    smem_cur = pl.BlockSpec((1, 1, MOE_TM), lambda i, te: (i, 0, 0), memory_space=pltpu.SMEM)
    smem_nxt = pl.BlockSpec((1, 1, MOE_TM), lambda i, te: (jnp.minimum(i + 1, ntiles - 1), 0, 0),
                            memory_space=pltpu.SMEM)
    wspec = lambda s: pl.BlockSpec((1,) + s, lambda i, te: (te[i], 0, 0))
    return pl.pallas_call(
        _expert_kernel,
        out_shape=jax.ShapeDtypeStruct((ntiles * MOE_TM * dt, LANES), F32),
        grid_spec=pltpu.PrefetchScalarGridSpec(
            num_scalar_prefetch=1,
            grid=(ntiles,),
            in_specs=[smem_cur, smem_nxt, pl.BlockSpec(memory_space=pl.ANY),
                      wspec((d, ff)), wspec((d, ff)), wspec((ff, d))],
            out_specs=pl.BlockSpec((MOE_TM * dt, LANES), lambda i, te: (i, 0)),
            scratch_shapes=[pltpu.VMEM((2 * MOE_TM * dt, LANES), F32), pltpu.SemaphoreType.DMA((2,))],
        ),
        compiler_params=_cparams(("arbitrary",)),
        name="experts",
    )(tile_expert, tok3, tok3, t, wg, wu, wd)


def _final_kernel(cur_ref, nxt_ref, y_hbm, h_ref, rw_ref, fn_ref, o_ref, buf, sem):
    i = pl.program_id(0)
    dt = h_ref.shape[1] // LANES
    slot = _gather_pipeline(i, pl.num_programs(0), y_hbm, cur_ref, nxt_ref, 2 * FIN_TM, dt, buf, sem)
    rw = rw_ref[...]
    y0 = _load_row_tiles(buf, slot * 2 * FIN_TM, FIN_TM, dt)
    y1 = _load_row_tiles(buf, slot * 2 * FIN_TM + FIN_TM, FIN_TM, dt)
    h = h_ref[...] + rw[:, 0:1] * y0 + rw[:, 1:2] * y1
    o_ref[...] = h * lax.rsqrt(jnp.mean(h * h, axis=-1, keepdims=True) + RMS_EPS) * fn_ref[...]


def _final(pos_tiles, y_sorted, h, rw, final_norm):
    n, d = h.shape
    tm = FIN_TM
    nt = n // tm
    smem_cur = pl.BlockSpec((1, 1, 2 * tm), lambda i: (i, 0, 0), memory_space=pltpu.SMEM)
    smem_nxt = pl.BlockSpec((1, 1, 2 * tm), lambda i: (jnp.minimum(i + 1, nt - 1), 0, 0),
                            memory_space=pltpu.SMEM)
    return pl.pallas_call(
        _final_kernel,
        out_shape=jax.ShapeDtypeStruct((n, d), F32),
        grid=(nt,),
        in_specs=[smem_cur, smem_nxt, pl.BlockSpec(memory_space=pl.ANY),
                  pl.BlockSpec((tm, d), lambda i: (i, 0)), pl.BlockSpec((tm, ROUTE_COLS), lambda i: (i, 0)),
                  pl.BlockSpec((1, d), lambda i: (0, 0))],
        out_specs=pl.BlockSpec((tm, d), lambda i: (i, 0)),
        scratch_shapes=[pltpu.VMEM((2 * 2 * tm * (d // LANES), LANES), F32), pltpu.SemaphoreType.DMA((2,))],
        compiler_params=_cparams(("arbitrary",)),
        name="final",
    )(pos_tiles, pos_tiles, y_sorted, h, rw, final_norm.reshape(1, d))


def _route_plan(gid):
    n = gid.shape[0]
    e = gid.reshape(-1)
    onehot = (e[:, None] == jnp.arange(N_EXPERTS, dtype=I32)[None, :]).astype(F32)
    chunk = 256
    oh3 = onehot.reshape(-1, chunk, N_EXPERTS)
    within = jnp.einsum("ij,tjk->tik", jnp.tril(jnp.ones((chunk, chunk), F32)), oh3)
    totals = within[:, -1, :]
    before = jnp.cumsum(totals, axis=0) - totals
    csum = (within + before[:, None, :]).reshape(-1, N_EXPERTS)
    rank = jnp.sum(csum * onehot, axis=1).astype(I32) - 1
    counts = (before[-1] + totals[-1]).astype(I32)
    padded = ((counts + MOE_TM - 1) // MOE_TM) * MOE_TM
    seg_end = jnp.cumsum(padded)
    pos = (seg_end - padded)[e] + rank
    nrows = 2 * n + N_EXPERTS * MOE_TM
    row_token = jnp.zeros((nrows,), I32).at[pos].set(jnp.arange(2 * n, dtype=I32) // 2, unique_indices=True)
    tile_start = jnp.arange(nrows // MOE_TM, dtype=I32) * MOE_TM
    tile_expert = jnp.minimum(jnp.sum(tile_start[:, None] >= seg_end[None, :], axis=1), N_EXPERTS - 1).astype(I32)
    return row_token, tile_expert, pos.reshape(n, 2)


def _pack_w_in(w):
    d = w.shape[0]
    o_kv, o_qi, o_ki, o_wi = 512, 768, 1024, 1088
    o_dil = o_wi + IDX_HEADS
    o_gate = o_dil + 9 * 256
    pad = jnp.zeros((d, LANES - IDX_DIM - IDX_HEADS), w.dtype)
    packed = jnp.concatenate([w[:, :o_ki], w[:, o_ki:o_wi], w[:, o_wi:o_dil], pad, w[:, o_dil:o_gate],
                              w[:, o_gate:]], axis=1)
    assert packed.shape[1] == C_END
    return packed.astype(BF16)


def kernel(x, attn_norm, w_in, kv_norm, w_uk, w_uv, rel_bias, w_branch_a, w_branch_b, w_out, ffn_norm,
           w_router_group, b_router_group, w_router_expert, b_router_expert, w_gate, w_up, w_down,
           final_norm):
    b, seq, d = x.shape
    n = b * seq
    nkc = seq // BLK
    assert w_in.shape[0] == 1, "one layer"
    x2 = x.reshape(n, d)

    outs = _proj(x2, attn_norm[0], _pack_w_in(w_in[0]), kv_norm[0])
    qa, ckv, qi, kw = outs[:4]
    dil = outs[4:13]
    gates = outs[13]

    dsa_bias, dil_bias = _bias_tiles(rel_bias, nkc)

    t3 = lambda a: jnp.swapaxes(a.reshape(b, seq, a.shape[-1]), 1, 2)
    kw3 = kw.reshape(b, seq, LANES)
    wiT = jnp.swapaxes(kw3[:, :, IDX_DIM:IDX_DIM + 8], 1, 2)
    kidx = kw3[:, :, :IDX_DIM].astype(BF16)
    ckv3 = ckv.reshape(b, seq, KV_LATENT)
    ckvT = jnp.swapaxes(ckv3, 1, 2)
    ya = _dsa(t3(qi), wiT, t3(qa), kidx, ckv3, ckvT, w_uk[0].astype(BF16), w_uv[0].astype(BF16), dsa_bias)

    os_, lses = [], []
    for g, (_, dilation) in enumerate(DIL_GROUPS):
        o, lse = _dilated_group(dil[g], dil[3 + g], dil[6 + g], dil_bias, g, dilation, b)
        os_.append(o)
        lses.append(lse)

    wr = jnp.concatenate([w_router_group[0],
                          jnp.swapaxes(w_router_expert[0], 0, 1).reshape(d, N_EXPERTS),
                          jnp.zeros((d, ROUTE_COLS - N_GROUPS - N_EXPERTS), F32)], axis=1)
    wr_hi = wr.astype(BF16)
    wr = jnp.concatenate([wr_hi, (wr - wr_hi.astype(F32)).astype(BF16)], axis=1)
    br = jnp.concatenate([b_router_group[0], b_router_expert[0].reshape(-1),
                          jnp.zeros((ROUTE_COLS - N_GROUPS - N_EXPERTS,), F32)]).reshape(1, ROUTE_COLS)
    h, t, rw, ri = _mix(x2, ya.reshape(n, -1), os_, lses, gates,
                        w_branch_a[0].astype(BF16), w_branch_b[0].astype(BF16), w_out[0].astype(BF16),
                        ffn_norm[0], wr, br)

    row_token, tile_expert, pos = _route_plan(ri[:, :2])
    y_sorted = _experts(tile_expert, row_token, t, w_gate[0].astype(BF16), w_up[0].astype(BF16),
                        w_down[0].astype(BF16), d)

    pos_tiles = jnp.swapaxes(pos.reshape(n // FIN_TM, FIN_TM, 2), 1, 2).reshape(n // FIN_TM, 1, 2 * FIN_TM)
    out = _final(pos_tiles, y_sorted, h, rw, final_norm)
    return out.reshape(b, seq, d)
```

```python
import functools
import math

import numpy as np
import jax
import jax.numpy as jnp
from jax import lax
from jax.experimental import pallas as pl
from jax.experimental.pallas import tpu as pltpu

F32 = jnp.float32
BF16 = jnp.bfloat16
I32 = jnp.int32

LANES = 128
VMEM_LIMIT_BYTES = 56 * 1024 * 1024

HEAD_DIM = 64
DSA_HEADS = 8
KV_LATENT = 256
IDX_HEADS = 4
IDX_DIM = 64
TOPK_MAX = 256
DIL_GROUPS = ((128, 1), (512, 4), (2048, 16))
DIL_HPG = 4
DIL_OUT = DIL_HPG * HEAD_DIM
NUM_BUCKETS = 32
MAX_DISTANCE = 2048
N_GROUPS = 4
EXPERTS_PER_GROUP = 8
N_EXPERTS = N_GROUPS * EXPERTS_PER_GROUP
RMS_EPS = 1e-6
NEG = -1e30
INT_MIN = -2 ** 31

BLK = 128
PROJ_TM = 512
MIX_TM = 256
MOE_TM = 256
FIN_TM = 256
ROUTE_COLS = 128


def _cparams(sem):
    return pltpu.CompilerParams(dimension_semantics=sem, vmem_limit_bytes=VMEM_LIMIT_BYTES)


C_QA = 0
C_KV = 512
C_QI = 768
C_KW = 1024
C_DIL = 1152
C_GATE = C_DIL + 9 * 256
C_END = C_GATE + 2048


def _proj_kernel(x_ref, g_ref, w_ref, kvg_ref, qa_ref, ckv_ref, qi_ref, kw_ref, *rest):
    dil_refs = rest[:9]
    gate_ref = rest[9]
    stage_ref = rest[10]
    x = x_ref[...]
    u = x * lax.rsqrt(jnp.mean(x * x, axis=-1, keepdims=True) + RMS_EPS) * g_ref[...]
    u = u.astype(BF16)

    def mm(a, b):
        return jnp.dot(u, w_ref[:, a:b], preferred_element_type=F32)

    qa_ref[...] = mm(C_QA, C_KV).astype(BF16)
    c = mm(C_KV, C_QI)
    c = c * lax.rsqrt(jnp.mean(c * c, axis=-1, keepdims=True) + RMS_EPS) * kvg_ref[...]
    ckv_ref[...] = c.astype(BF16)
    qi_ref[...] = mm(C_QI, C_KW).astype(BF16)
    kw_ref[...] = mm(C_KW, C_DIL)
    for j in range(9):
        val = mm(C_DIL + 256 * j, C_DIL + 256 * (j + 1))
        r = DIL_GROUPS[j % 3][1]
        if r == 1:
            dil_refs[j][...] = val.astype(BF16)
        else:
            for hf in range(2):
                stage_ref[hf] = val[:, hf * LANES:(hf + 1) * LANES]
            for rho in range(r):
                for hf in range(2):
                    dil_refs[j][:, rho * 256 + hf * LANES:rho * 256 + (hf + 1) * LANES] = (
                        stage_ref[hf, pl.ds(rho, val.shape[0] // r, stride=r), :].astype(BF16))
    for j in range(4):
        gate_ref[:, 512 * j:512 * (j + 1)] = mm(C_GATE + 512 * j, C_GATE + 512 * (j + 1)).astype(BF16)


def _proj(x2, attn_norm, w_packed, kv_norm):
    n, d = x2.shape
    tm = PROJ_TM
    row = lambda i: (i, 0)
    const = lambda i: (0, 0)
    outs = [jax.ShapeDtypeStruct((n, 512), BF16), jax.ShapeDtypeStruct((n, 256), BF16),
            jax.ShapeDtypeStruct((n, 256), BF16), jax.ShapeDtypeStruct((n, 128), F32)]
    for j in range(9):
        r = DIL_GROUPS[j % 3][1]
        outs.append(jax.ShapeDtypeStruct((n // r, r * 256), BF16))
    outs += [jax.ShapeDtypeStruct((n, 2048), BF16)]
    out_specs = [pl.BlockSpec((tm * s.shape[0] // n, s.shape[1]), row) for s in outs]
    return pl.pallas_call(
        _proj_kernel,
        out_shape=outs,
        grid=(n // tm,),
        in_specs=[pl.BlockSpec((tm, d), row), pl.BlockSpec((1, d), const),
                  pl.BlockSpec((d, C_END), const), pl.BlockSpec((1, KV_LATENT), const)],
        out_specs=out_specs,
        scratch_shapes=[pltpu.VMEM((2, tm, LANES), F32)],
        compiler_params=_cparams(("parallel",)),
        name="proj",
    )(x2, attn_norm.reshape(1, d), w_packed, kv_norm.reshape(1, KV_LATENT))


def _bucket_thresholds():
    max_exact = NUM_BUCKETS // 2
    d = np.arange(0, MAX_DISTANCE + 1)
    nf = np.maximum(d, 1).astype(np.float32)
    large = max_exact + (np.log(nf / np.float32(max_exact)) / np.float32(math.log(MAX_DISTANCE / max_exact))
                         * np.float32(NUM_BUCKETS - max_exact)).astype(np.int32)
    large = np.minimum(large, NUM_BUCKETS - 1)
    bucket = np.where(d < max_exact, d, large)
    assert np.all(np.diff(bucket) >= 0)
    return [int(np.argmax(bucket >= b)) for b in range(1, NUM_BUCKETS)]


_BUCKET_THR = _bucket_thresholds()


def _bias_from_distance(dist, tab_ref, heads):
    masks = [dist >= t for t in _BUCKET_THR]
    out = []
    for h in heads:
        v = jnp.full(dist.shape, tab_ref[0, h], F32)
        for b in range(1, NUM_BUCKETS):
            v = jnp.where(masks[b - 1], tab_ref[b, h], v)
        out.append(v)
    return out


def _dsa_bias_kernel(tab_ref, o_ref):
    delta = pl.program_id(0)
    j = lax.broadcasted_iota(I32, (BLK, BLK), 0)
    i = lax.broadcasted_iota(I32, (BLK, BLK), 1)
    dist = jnp.maximum(delta * BLK + i - j, 0)
    tiles = _bias_from_distance(dist, tab_ref, range(DSA_HEADS))
    for h in range(DSA_HEADS):
        o_ref[0, h] = tiles[h]


def _dil_bias_kernel(tab_ref, o_ref, *, dilations):
    g = pl.program_id(0)
    j = lax.broadcasted_iota(I32, (2 * BLK, BLK), 0)
    i = lax.broadcasted_iota(I32, (2 * BLK, BLK), 1)
    step = i + BLK - j
    valid = (step >= 0) & (step <= BLK)
    for gi, r in enumerate(dilations):
        @pl.when(g == gi)
        def _():
            dist = jnp.maximum(step, 0) * r
            heads = [DSA_HEADS + gi * DIL_HPG + hh for hh in range(DIL_HPG)]
            tiles = _bias_from_distance(dist, tab_ref, heads)
            for hh in range(DIL_HPG):
                o_ref[0, :, hh * BLK:(hh + 1) * BLK] = jnp.where(valid, tiles[hh], NEG)


def _bias_tiles(rel_bias, nkc):
    smem = pl.BlockSpec(memory_space=pltpu.SMEM)
    dsa = pl.pallas_call(
        _dsa_bias_kernel,
        out_shape=jax.ShapeDtypeStruct((nkc, DSA_HEADS, BLK, BLK), F32),
        grid=(nkc,),
        in_specs=[smem],
        out_specs=pl.BlockSpec((1, DSA_HEADS, BLK, BLK), lambda d: (d, 0, 0, 0)),
        compiler_params=_cparams(("parallel",)),
        name="dsa_bias",
    )(rel_bias)
    dil = pl.pallas_call(
        functools.partial(_dil_bias_kernel, dilations=tuple(r for _, r in DIL_GROUPS)),
        out_shape=jax.ShapeDtypeStruct((len(DIL_GROUPS), 2 * BLK, DIL_HPG * BLK), F32),
        grid=(len(DIL_GROUPS),),
        in_specs=[smem],
        out_specs=pl.BlockSpec((1, 2 * BLK, DIL_HPG * BLK), lambda g: (g, 0, 0)),
        compiler_params=_cparams(("parallel",)),
        name="dil_bias",
    )(rel_bias)
    return dsa, dil


SUP = 4
SROWS = SUP * BLK


PLANE_KEYS = 32 * 8


def _bit_planes(words):
    x = list(words)
    j, m = 16, 0x0000FFFF
    while j:
        k = 0
        while k < 32:
            t = (x[k] ^ lax.shift_right_logical(x[k + j], jnp.int32(j))) & m
            x[k] = x[k] ^ t
            x[k + j] = x[k + j] ^ jnp.left_shift(t, jnp.int32(j))
            k = (k + j + 1) & ~j
        j >>= 1
        m = (m ^ (m << j)) & 0xFFFFFFFF
        m = m - (1 << 32) if m >= (1 << 31) else m
    return x


def _dsa_kernel(qiT_ref, wiT_ref, qaT_ref, kidx_ref, ckv_ref, ckvT_ref, wuk_ref, wuv_ref, bias_ref, tri_ref,
                y_ref, sc_ref, planes_ref, qlT_ref, x_ref, pT_ref, *, topk, nsc, qb0):
    qb = qb0 + pl.program_id(1)
    row = lax.broadcasted_iota(I32, (SROWS, BLK), 0)
    col = lax.broadcasted_iota(I32, (SROWS, BLK), 1)
    trips = [slice(sc * SROWS, (sc + 1) * SROWS) for sc in range(nsc)]

    def causal(sc):
        return row <= col + (qb * BLK - sc * SROWS)

    for h in range(DSA_HEADS):
        ql = jnp.dot(wuk_ref[h], qaT_ref[0, h * HEAD_DIM:(h + 1) * HEAD_DIM, :],
                     preferred_element_type=F32) * (HEAD_DIM ** -0.5)
        qlT_ref[:, h * BLK:(h + 1) * BLK] = ql.astype(BF16)

    wq = wiT_ref[0] * (IDX_HEADS ** -0.5)
    for sc, ts in enumerate(trips):
        kx = kidx_ref[0, ts, :]
        acc = jnp.zeros((SROWS, BLK), F32)
        for h in range(IDX_HEADS):
            s = jnp.dot(kx, qiT_ref[0, h * IDX_DIM:(h + 1) * IDX_DIM, :],
                        preferred_element_type=F32) * (IDX_DIM ** -0.5)
            acc = acc + wq[h:h + 1, :] * jnp.maximum(s, 0.0)
        acc = jnp.where(causal(sc), acc, NEG)
        sc_ref[ts, :] = acc
        bits = pltpu.bitcast(acc, I32)
        bits = jnp.where(bits == INT_MIN, 0, bits)
        ukey = bits ^ ((bits >> 31) & 0x7FFFFFFF) ^ INT_MIN
        for grp in range(SROWS // PLANE_KEYS):
            tiles = [ukey[grp * PLANE_KEYS + j * 8:grp * PLANE_KEYS + (j + 1) * 8] for j in range(32)]
            for b, plane in enumerate(_bit_planes(tiles)):
                planes_ref[b, sc * (SROWS // PLANE_KEYS) + grp] = plane

    def count(pred):
        cnt = jnp.zeros((8, BLK), I32)
        for ts in trips:
            cnt = cnt + jnp.sum(jnp.where(pred(sc_ref[ts, :]), 1, 0).reshape(SROWS // 8, 8, BLK), axis=0)
        return jnp.sum(cnt, axis=0, keepdims=True)

    def as_float(key):
        return pltpu.bitcast(key ^ ((key >> 31) & 0x7FFFFFFF), F32)

    def bit_body(it, carry):
        alive, above, code = carry
        ones = alive & planes_ref[it]
        c = jnp.sum(jnp.sum(lax.population_count(ones), axis=0), axis=0, keepdims=True)
        take = above + c >= topk
        alive = jnp.where(take, ones, alive ^ ones)
        above = jnp.where(take, above, above + c)
        code = code | jnp.where(take, jnp.left_shift(jnp.int32(1), 31 - it), 0)
        return alive, above, code

    nw = nsc * (SROWS // PLANE_KEYS)
    _, _, code = lax.fori_loop(
        0, 32, bit_body,
        (jnp.full((nw, 8, BLK), -1, I32), jnp.zeros((1, BLK), I32), jnp.zeros((1, BLK), I32)))
    guess = as_float(code ^ INT_MIN)
    n_gt_guess = count(lambda v: v > guess)
    proven = (n_gt_guess < topk) & (count(lambda v: v >= guess) >= topk)

    def bisect():
        def thr_body(it, lo):
            cand = lo + jnp.left_shift(jnp.int32(1), 31 - it)
            cand_f = as_float(cand)
            return jnp.where(count(lambda v: v >= cand_f) >= topk, cand, lo)
        t = as_float(lax.fori_loop(0, 32, thr_body, jnp.full((1, BLK), INT_MIN, I32)))
        return t, count(lambda v: v > t)

    thr, n_gt = lax.cond(jnp.min(jnp.where(proven, 1, 0)) > 0, lambda: (guess, n_gt_guess), bisect)
    ties_wanted = (topk - n_gt).astype(F32)

    m = [jnp.full((8, BLK), NEG, F32) for _ in range(DSA_HEADS)]
    ties_before = jnp.zeros((1, BLK), F32)
    for sc, ts in enumerate(trips):
        k = sc_ref[ts, :]
        tie = k == thr
        tie_rank = jnp.dot(tri_ref[...], jnp.where(tie, 1.0, 0.0).astype(BF16),
                           preferred_element_type=F32) + ties_before
        ties_before = tie_rank[SROWS - 1:SROWS, :]
        sel = ((k > thr) | (tie & (tie_rank <= ties_wanted))) & causal(sc)
        am = jnp.where(sel, 0.0, NEG)
        ck = ckv_ref[0, ts, :]
        for hp in range(DSA_HEADS // 2):
            lg2 = jnp.dot(ck, qlT_ref[:, 2 * hp * BLK:(2 * hp + 2) * BLK], preferred_element_type=F32)
            for h in (2 * hp, 2 * hp + 1):
                lg = lg2[:, (h % 2) * BLK:(h % 2 + 1) * BLK]
                for j in range(SUP):
                    rs = slice(j * BLK, (j + 1) * BLK)
                    delta = jnp.maximum(qb - (sc * SUP + j), 0)
                    x = lg[rs] + bias_ref[delta, h] + am[rs]
                    x_ref[sc * SROWS + j * BLK:sc * SROWS + (j + 1) * BLK, h * BLK:(h + 1) * BLK] = x
                    m[h] = jnp.maximum(m[h], jnp.max(x.reshape(BLK // 8, 8, BLK), axis=0))
    m = [jnp.max(v, axis=0, keepdims=True) for v in m]

    l = [jnp.zeros((8, BLK), F32) for _ in range(DSA_HEADS)]
    for sc, ts in enumerate(trips):
        for h in range(DSA_HEADS):
            hs = slice(h * BLK, (h + 1) * BLK)
            p = jnp.exp(x_ref[ts, hs] - m[h])
            pT_ref[ts, hs] = p.astype(BF16)
            l[h] = l[h] + jnp.sum(p.reshape(SROWS // 8, 8, BLK), axis=0)

    for hp in range(DSA_HEADS // 2):
        o2 = jnp.dot(ckvT_ref[0], pT_ref[:, 2 * hp * BLK:(2 * hp + 2) * BLK],
                     preferred_element_type=F32)
        for h in (2 * hp, 2 * hp + 1):
            inv = 1.0 / jnp.sum(l[h], axis=0, keepdims=True)
            oh = (o2[:, (h % 2) * BLK:(h % 2 + 1) * BLK] * inv).T.astype(BF16)
            yh = jnp.dot(oh, wuv_ref[h], preferred_element_type=F32)
            y_ref[0, :, h * HEAD_DIM:(h + 1) * HEAD_DIM] = yh.astype(BF16)


def _dsa_group(g, qiT, wiT, qaT, kidx, ckv, ckvT, wuk, wuv, bias_tiles, tri):
    b, seq, _ = ckv.shape
    nkc = seq // BLK
    nsc = g + 1
    nk = nsc * SROWS
    topk = min(TOPK_MAX, seq // 4)
    qblk = lambda rows: pl.BlockSpec((1, rows, BLK), lambda bi, qi: (bi, 0, g * SUP + qi))
    head3 = lambda cols: pl.BlockSpec((1, nk, cols), lambda bi, qi: (bi, 0, 0))
    const = lambda s: pl.BlockSpec(s, lambda bi, qi: (0,) * len(s))
    hl = DSA_HEADS * BLK
    return pl.pallas_call(
        functools.partial(_dsa_kernel, topk=topk, nsc=nsc, qb0=g * SUP),
        out_shape=jax.ShapeDtypeStruct((b, SROWS, DSA_HEADS * HEAD_DIM), BF16),
        grid=(b, SUP),
        in_specs=[qblk(IDX_HEADS * IDX_DIM), qblk(8), qblk(DSA_HEADS * HEAD_DIM),
                  head3(IDX_DIM), head3(KV_LATENT),
                  pl.BlockSpec((1, KV_LATENT, nk), lambda bi, qi: (bi, 0, 0)),
                  const((DSA_HEADS, KV_LATENT, HEAD_DIM)), const((DSA_HEADS, KV_LATENT, HEAD_DIM)),
                  const((nkc, DSA_HEADS, BLK, BLK)), const((SROWS, SROWS))],
        out_specs=pl.BlockSpec((1, BLK, DSA_HEADS * HEAD_DIM), lambda bi, qi: (bi, qi, 0)),
        scratch_shapes=[pltpu.VMEM((nk, BLK), F32),
                        pltpu.VMEM((32, nk // PLANE_KEYS, 8, BLK), I32),
                        pltpu.VMEM((KV_LATENT, hl), BF16),
                        pltpu.VMEM((nk, hl), F32),
                        pltpu.VMEM((nk, hl), BF16)],
        compiler_params=_cparams(("parallel", "arbitrary")),
        name=f"dsa_g{g}",
    )(qiT, wiT, qaT, kidx, ckv, ckvT, wuk, wuv, bias_tiles, tri)


def _dsa(qiT, wiT, qaT, kidx, ckv, ckvT, wuk, wuv, bias_tiles):
    seq = ckv.shape[1]
    assert seq % SROWS == 0 and seq >= 4 * TOPK_MAX
    tri = jnp.tril(jnp.ones((SROWS, SROWS), BF16))
    groups = [_dsa_group(g, qiT, wiT, qaT, kidx, ckv, ckvT, wuk, wuv, bias_tiles, tri)
              for g in range(seq // SROWS)]
    return jnp.concatenate(groups, axis=1)


DIL_UNROLL = 5
DIL_STEP_ROWS = 1024


def _dil_kernel(q_ref, k_ref, v_ref, bm_ref, o_ref, lse_ref, vT_ref, *, nblk, nres):
    hq = DIL_HPG * BLK
    rowh = lax.broadcasted_iota(I32, (hq, DIL_OUT), 0) // BLK
    colh = lax.broadcasted_iota(I32, (hq, DIL_OUT), 1) // HEAD_DIM
    same_head = rowh == colh

    for res in range(nres):
        cs = slice(res * DIL_OUT, (res + 1) * DIL_OUT)

        for n in range(nblk):
            vT_ref[n] = v_ref[0, n * BLK:(n + 1) * BLK, cs].astype(F32).T.astype(BF16)

        def block(qo, kw, vT, bm, cs=cs):
            q = q_ref[0, pl.ds(qo, BLK), cs]
            qd = jnp.where(same_head, jnp.concatenate([q] * DIL_HPG, axis=0), jnp.zeros((), BF16))
            s = lax.dot_general(kw, qd, (((1,), (1,)), ((), ())), preferred_element_type=F32)
            s = s * (HEAD_DIM ** -0.5) + bm
            m = jnp.max(s, axis=0, keepdims=True)
            e = jnp.exp(s - m)
            l = jnp.sum(e, axis=0, keepdims=True)
            oT = jnp.dot(vT, e.astype(BF16), preferred_element_type=F32)
            inv = 1.0 / l
            lse = m + jnp.log(l)
            outs, lses = [], []
            for hh in range(DIL_HPG):
                qs = slice(hh * BLK, (hh + 1) * BLK)
                outs.append(oT[hh * HEAD_DIM:(hh + 1) * HEAD_DIM, qs] * inv[:, qs])
                lses.append(jnp.broadcast_to(lse[:, qs], (HEAD_DIM, BLK)))
            o_ref[0, pl.ds(qo, BLK), cs] = jnp.concatenate(outs, axis=0).T.astype(BF16)
            lse_ref[0, pl.ds(qo, BLK), cs] = jnp.concatenate(lses, axis=0).T

        block(0, k_ref[0, 0:BLK, cs], vT_ref[0], bm_ref[0, BLK:, :])

        def body(n, carry, cs=cs, block=block):
            ko = pl.multiple_of((n - 1) * BLK, BLK)
            vT = jnp.concatenate([vT_ref[n - 1], vT_ref[n]], axis=1)
            block(pl.multiple_of(n * BLK, BLK), k_ref[0, pl.ds(ko, 2 * BLK), cs], vT, bm_ref[0])
            return carry

        lax.fori_loop(1, nblk, body, 0, unroll=DIL_UNROLL)


def _dilated_group(q, k, v, bm, g, dilation, b):
    c = DIL_OUT
    ls = q.shape[0] // b
    nblk = ls // BLK
    nres = max(1, min(dilation, DIL_STEP_ROWS // ls))
    view = lambda a: a.reshape(b, ls, dilation * c)
    blk = pl.BlockSpec((1, ls, nres * c), lambda bi, ri: (bi, 0, ri))
    o, lse = pl.pallas_call(
        functools.partial(_dil_kernel, nblk=nblk, nres=nres),
        out_shape=[jax.ShapeDtypeStruct((b, ls, dilation * c), BF16),
                   jax.ShapeDtypeStruct((b, ls, dilation * c), F32)],
        grid=(b, dilation // nres),
        in_specs=[blk, blk, blk, pl.BlockSpec((1, 2 * BLK, DIL_HPG * BLK), lambda bi, ri: (g, 0, 0))],
        out_specs=[blk, blk],
        scratch_shapes=[pltpu.VMEM((nblk, c, BLK), BF16)],
        compiler_params=_cparams(("parallel", "parallel")),
        name=f"dilated_g{g}",
    )(view(q), view(k), view(v), bm)
    return o.reshape(b * ls, dilation * c), lse.reshape(b * ls, dilation * c)


def _store_row_tiles(ref, base, val):
    rows, d = val.shape
    dt = d // LANES
    for s in range(dt):
        ref[pl.ds(base * dt + s, rows, stride=dt), :] = val[:, s * LANES:(s + 1) * LANES]


def _load_row_tiles(ref, base, rows, dt):
    return jnp.concatenate([ref[pl.ds(base * dt + s, rows, stride=dt), :] for s in range(dt)], axis=1)


MIX_COLS = 256


def _token_major(ref, stage_ref, r):
    if r == 1:
        return ref[...].astype(F32)
    rows = ref.shape[0]
    for rho in range(r):
        for hf in range(DIL_OUT // LANES):
            c0 = rho * DIL_OUT + hf * LANES
            stage_ref[hf, pl.ds(rho, rows, stride=r), :] = ref[:, c0:c0 + LANES].astype(F32)
    return jnp.concatenate([stage_ref[hf] for hf in range(DIL_OUT // LANES)], axis=1)


def _sigmoid(v):
    return 0.5 * jnp.tanh(0.5 * v) + 0.5


def _mix_kernel(x_ref, ya_ref, o1_ref, o2_ref, o3_ref, l1_ref, l2_ref, l3_ref, gate_ref,
                wa_ref, wb_ref, wo_ref, fg_ref, wr_ref, br_ref,
                h_ref, t_ref, rw_ref, ri_ref, mixed_ref, *stage_refs):
    dils = [r for _, r in DIL_GROUPS]
    o = [_token_major(ref, st, r) for ref, st, r in zip((o1_ref, o2_ref, o3_ref), stage_refs[:3], dils)]
    l1, l2, l3 = [_token_major(ref, st, r) for ref, st, r in zip((l1_ref, l2_ref, l3_ref), stage_refs[3:], dils)]
    mx = jnp.maximum(jnp.maximum(l1, l2), l3)
    e1, e2, e3 = jnp.exp(l1 - mx), jnp.exp(l2 - mx), jnp.exp(l3 - mx)
    inv = 1.0 / (e1 + e2 + e3)
    yb = ((e1 * inv) * o[0] + (e2 * inv) * o[1] + (e3 * inv) * o[2]).astype(BF16)
    ya = ya_ref[...]
    d = x_ref.shape[1]
    for c in range(0, d, MIX_COLS):
        cs = slice(c, c + MIX_COLS)
        a = jnp.dot(ya, wa_ref[:, cs], preferred_element_type=F32)
        bmix = jnp.dot(yb, wb_ref[:, cs], preferred_element_type=F32)
        g0 = _sigmoid(gate_ref[:, cs].astype(F32))
        g1 = _sigmoid(gate_ref[:, d + c:d + c + MIX_COLS].astype(F32))
        mixed_ref[:, cs] = (g0 * a + g1 * bmix).astype(BF16)
    h = x_ref[...] + jnp.dot(mixed_ref[...], wo_ref[...], preferred_element_type=F32)
    h_ref[...] = h
    t = h * lax.rsqrt(jnp.mean(h * h, axis=-1, keepdims=True) + RMS_EPS) * fg_ref[...]
    _store_row_tiles(t_ref, 0, t)

    t_hi = t.astype(BF16)
    t_lo = (t - t_hi.astype(F32)).astype(BF16)
    r1 = jnp.dot(t_hi, wr_ref[...], preferred_element_type=F32)
    r2 = jnp.dot(t_lo, wr_ref[:, :ROUTE_COLS], preferred_element_type=F32)
    logits = r1[:, :ROUTE_COLS] + (r1[:, ROUTE_COLS:] + r2) + br_ref[...]
    lane = lax.broadcasted_iota(I32, logits.shape, 1)
    ninf = -jnp.inf
    big = jnp.int32(10 ** 6)

    def first_argmax(v, vmax):
        return jnp.min(jnp.where(v == vmax, lane, big), axis=-1, keepdims=True)

    gl = jnp.where(lane < N_GROUPS, logits, ninf)
    gmax = jnp.max(gl, axis=-1, keepdims=True)
    gsel = first_argmax(gl, gmax)
    p_g = 1.0 / jnp.sum(jnp.exp(gl - gmax), axis=-1, keepdims=True)
    lo = N_GROUPS + gsel * EXPERTS_PER_GROUP
    el = jnp.where((lane >= lo) & (lane < lo + EXPERTS_PER_GROUP), logits, ninf)
    v1 = jnp.max(el, axis=-1, keepdims=True)
    i1 = first_argmax(el, v1)
    el2 = jnp.where(lane == i1, ninf, el)
    v2 = jnp.max(el2, axis=-1, keepdims=True)
    i2 = first_argmax(el2, v2)
    e2 = jnp.exp(v2 - v1)
    w1 = p_g / (1.0 + e2)
    w2 = p_g * e2 / (1.0 + e2)
    rw_ref[...] = jnp.where(lane == 0, w1, jnp.where(lane == 1, w2, 0.0))
    ri_ref[...] = jnp.where(lane == 0, i1 - N_GROUPS, jnp.where(lane == 1, i2 - N_GROUPS, 0))


def _mix(x2, ya, os_, lses, gates, wa, wb, wo, ffn_norm, wr, br):
    n, d = x2.shape
    tm = MIX_TM
    row = lambda c: pl.BlockSpec((tm, c), lambda i: (i, 0))
    res = lambda a: pl.BlockSpec((tm * a.shape[0] // n, a.shape[1]), lambda i: (i, 0))
    const = lambda s: pl.BlockSpec(s, lambda i: (0, 0))
    return pl.pallas_call(
        _mix_kernel,
        out_shape=[jax.ShapeDtypeStruct((n, d), F32), jax.ShapeDtypeStruct((n * (d // LANES), LANES), F32),
                   jax.ShapeDtypeStruct((n, ROUTE_COLS), F32), jax.ShapeDtypeStruct((n, ROUTE_COLS), I32)],
        grid=(n // tm,),
        in_specs=[row(d), row(512)] + [res(a) for a in os_] + [res(a) for a in lses] + [row(2 * d),
                  const(wa.shape), const(wb.shape), const(wo.shape), const((1, d)),
                  const(wr.shape), const((1, ROUTE_COLS))],
        out_specs=[row(d), pl.BlockSpec((tm * (d // LANES), LANES), lambda i: (i, 0)),
                   row(ROUTE_COLS), row(ROUTE_COLS)],
        scratch_shapes=[pltpu.VMEM((tm, d), BF16)] + [pltpu.VMEM((DIL_OUT // LANES, tm, LANES), F32)] * 6,
        compiler_params=_cparams(("parallel",)),
        name="mix",
    )(x2, ya, *os_, *lses, gates, wa, wb, wo, ffn_norm.reshape(1, d), wr, br)


GATHER_UNROLL = 8


def _start_row_gather(src_hbm, idx_ref, nrows, dt, buf, sem, slot):
    def body(g, c):
        for u in range(GATHER_UNROLL):
            r = g * GATHER_UNROLL + u
            src = pl.multiple_of(idx_ref[0, 0, r] * dt, dt)
            dst = pl.multiple_of((slot * nrows + r) * dt, dt)
            pltpu.make_async_copy(src_hbm.at[pl.ds(src, dt)], buf.at[pl.ds(dst, dt)], sem.at[slot]).start()
        return c
    lax.fori_loop(0, nrows // GATHER_UNROLL, body, 0)


def _wait_row_gather(src_hbm, nrows, dt, buf, sem, slot):
    dst = pl.multiple_of(slot * nrows * dt, dt)
    pltpu.make_async_copy(src_hbm.at[pl.ds(0, nrows * dt)], buf.at[pl.ds(dst, nrows * dt)], sem.at[slot]).wait()


def _row_gather_chunks(src_hbm, idx_ref, nrows, dt, buf, sem, slot, nchunks):
    per = nrows // nchunks

    def chunk(c):
        for r in range(c * per, (c + 1) * per):
            src = pl.multiple_of(idx_ref[0, 0, r] * dt, dt)
            dst = pl.multiple_of((slot * nrows + r) * dt, dt)
            pltpu.make_async_copy(src_hbm.at[pl.ds(src, dt)], buf.at[pl.ds(dst, dt)], sem.at[slot]).start()

    return [functools.partial(chunk, c) for c in range(nchunks)]


EXPERT_COLS = 256


def _expert_kernel(te_ref, cur_ref, nxt_ref, t_hbm, wg_ref, wu_ref, wd_ref, y_ref, buf, sem, hid_ref):
    i = pl.program_id(0)
    nsteps = pl.num_programs(0)
    d, ff = wg_ref.shape[1], wg_ref.shape[2]
    dt = d // LANES
    slot = i % 2

    @pl.when(i == 0)
    def _():
        _start_row_gather(t_hbm, cur_ref, MOE_TM, dt, buf, sem, 0)

    _wait_row_gather(t_hbm, MOE_TM, dt, buf, sem, slot)
    xt = _load_row_tiles(buf, slot * MOE_TM, MOE_TM, dt).astype(BF16)

    stages = 2 * (ff // EXPERT_COLS) + d // EXPERT_COLS
    fetch = iter(_row_gather_chunks(t_hbm, nxt_ref, MOE_TM, dt, buf, sem, 1 - slot, stages))
    for c in range(0, ff, EXPERT_COLS):
        cs = slice(c, c + EXPERT_COLS)
        hg = jnp.dot(xt, wg_ref[0, :, cs], preferred_element_type=F32)
        next(fetch)()
        hu = jnp.dot(xt, wu_ref[0, :, cs], preferred_element_type=F32)
        next(fetch)()
        hid_ref[:, cs] = ((hg * jax.nn.sigmoid(hg)) * hu).astype(BF16)
    hid = hid_ref[...]
    for c in range(0, d, EXPERT_COLS):
        y = jnp.dot(hid, wd_ref[0, :, c:c + EXPERT_COLS], preferred_element_type=F32)
        for s in range(c // LANES, (c + EXPERT_COLS) // LANES):
            y_ref[pl.ds(s, MOE_TM, stride=dt), :] = y[:, s * LANES - c:(s + 1) * LANES - c]
        next(fetch)()

    @pl.when(i == nsteps - 1)
    def _():
        _wait_row_gather(t_hbm, MOE_TM, dt, buf, sem, 1 - slot)


def _experts(tile_expert, row_token, t, wg, wu, wd, d):
    dt = d // LANES
    ntiles = tile_expert.shape[0]
    ff = wg.shape[2]
    tok3 = row_token.reshape(ntiles, 1, MOE_TM)
    smem_cur = pl.BlockSpec((1, 1, MOE_TM), lambda i, te: (i, 0, 0), memory_space=pltpu.SMEM)
    smem_nxt = pl.BlockSpec((1, 1, MOE_TM), lambda i, te: (jnp.minimum(i + 1, ntiles - 1), 0, 0),
                            memory_space=pltpu.SMEM)
    wspec = lambda s: pl.BlockSpec((1,) + s, lambda i, te: (te[i], 0, 0))
    return pl.pallas_call(
        _expert_kernel,
        out_shape=jax.ShapeDtypeStruct((ntiles * MOE_TM * dt, LANES), F32),
        grid_spec=pltpu.PrefetchScalarGridSpec(
            num_scalar_prefetch=1,
            grid=(ntiles,),
            in_specs=[smem_cur, smem_nxt, pl.BlockSpec(memory_space=pl.ANY),
                      wspec((d, ff)), wspec((d, ff)), wspec((ff, d))],
            out_specs=pl.BlockSpec((MOE_TM * dt, LANES), lambda i, te: (i, 0)),
            scratch_shapes=[pltpu.VMEM((2 * MOE_TM * dt, LANES), F32), pltpu.SemaphoreType.DMA((2,)),
                            pltpu.VMEM((MOE_TM, ff), BF16)],
        ),
        compiler_params=_cparams(("arbitrary",)),
        name="experts",
    )(tile_expert, tok3, tok3, t, wg, wu, wd)


def _final_kernel(cur_ref, nxt_ref, y_hbm, h_ref, rw_ref, fn_ref, o_ref, buf, sem):
    i = pl.program_id(0)
    nsteps = pl.num_programs(0)
    d = h_ref.shape[1]
    dt = d // LANES
    nrows = 2 * FIN_TM
    slot = i % 2

    @pl.when(i == 0)
    def _():
        _start_row_gather(y_hbm, cur_ref, nrows, dt, buf, sem, 0)

    _wait_row_gather(y_hbm, nrows, dt, buf, sem, slot)
    fetch = iter(_row_gather_chunks(y_hbm, nxt_ref, nrows, dt, buf, sem, 1 - slot, dt))
    rw = rw_ref[...]
    w0, w1 = rw[:, 0:1], rw[:, 1:2]
    base = slot * nrows
    ssq = jnp.zeros((FIN_TM, 1), F32)
    for s in range(dt):
        cs = slice(s * LANES, (s + 1) * LANES)
        y0 = buf[pl.ds(base * dt + s, FIN_TM, stride=dt), :]
        y1 = buf[pl.ds((base + FIN_TM) * dt + s, FIN_TM, stride=dt), :]
        hs = h_ref[:, cs] + w0 * y0 + w1 * y1
        o_ref[:, cs] = hs
        ssq = ssq + jnp.sum(hs * hs, axis=-1, keepdims=True)
        next(fetch)()
    o_ref[...] = o_ref[...] * lax.rsqrt(ssq * (1.0 / d) + RMS_EPS) * fn_ref[...]

    @pl.when(i == nsteps - 1)
    def _():
        _wait_row_gather(y_hbm, nrows, dt, buf, sem, 1 - slot)


def _final(pos_tiles, y_sorted, h, rw, final_norm):
    n, d = h.shape
    tm = FIN_TM
    nt = n // tm
    smem_cur = pl.BlockSpec((1, 1, 2 * tm), lambda i: (i, 0, 0), memory_space=pltpu.SMEM)
    smem_nxt = pl.BlockSpec((1, 1, 2 * tm), lambda i: (jnp.minimum(i + 1, nt - 1), 0, 0),
                            memory_space=pltpu.SMEM)
    return pl.pallas_call(
        _final_kernel,
        out_shape=jax.ShapeDtypeStruct((n, d), F32),
        grid=(nt,),
        in_specs=[smem_cur, smem_nxt, pl.BlockSpec(memory_space=pl.ANY),
                  pl.BlockSpec((tm, d), lambda i: (i, 0)), pl.BlockSpec((tm, ROUTE_COLS), lambda i: (i, 0)),
                  pl.BlockSpec((1, d), lambda i: (0, 0))],
        out_specs=pl.BlockSpec((tm, d), lambda i: (i, 0)),
        scratch_shapes=[pltpu.VMEM((2 * 2 * tm * (d // LANES), LANES), F32), pltpu.SemaphoreType.DMA((2,))],
        compiler_params=_cparams(("arbitrary",)),
        name="final",
    )(pos_tiles, pos_tiles, y_sorted, h, rw, final_norm.reshape(1, d))


def _route_plan(gid):
    n = gid.shape[0]
    e = gid.reshape(-1)
    onehot = (e[:, None] == jnp.arange(N_EXPERTS, dtype=I32)[None, :]).astype(F32)
    chunk = 256
    oh3 = onehot.reshape(-1, chunk, N_EXPERTS)
    within = jnp.einsum("ij,tjk->tik", jnp.tril(jnp.ones((chunk, chunk), F32)), oh3)
    totals = within[:, -1, :]
    before = jnp.cumsum(totals, axis=0) - totals
    csum = (within + before[:, None, :]).reshape(-1, N_EXPERTS)
    rank = jnp.sum(csum * onehot, axis=1).astype(I32) - 1
    counts = (before[-1] + totals[-1]).astype(I32)
    padded = ((counts + MOE_TM - 1) // MOE_TM) * MOE_TM
    seg_end = jnp.cumsum(padded)
    pos = (seg_end - padded)[e] + rank
    nrows = 2 * n + N_EXPERTS * MOE_TM
    row_token = jnp.zeros((nrows,), I32).at[pos].set(jnp.arange(2 * n, dtype=I32) // 2, unique_indices=True)
    tile_start = jnp.arange(nrows // MOE_TM, dtype=I32) * MOE_TM
    tile_expert = jnp.minimum(jnp.sum(tile_start[:, None] >= seg_end[None, :], axis=1), N_EXPERTS - 1).astype(I32)
    return row_token, tile_expert, pos.reshape(n, 2)


def _pack_w_in(w):
    d = w.shape[0]
    o_kv, o_qi, o_ki, o_wi = 512, 768, 1024, 1088
    o_dil = o_wi + IDX_HEADS
    o_gate = o_dil + 9 * 256
    pad = jnp.zeros((d, LANES - IDX_DIM - IDX_HEADS), w.dtype)
    packed = jnp.concatenate([w[:, :o_ki], w[:, o_ki:o_wi], w[:, o_wi:o_dil], pad, w[:, o_dil:o_gate],
                              w[:, o_gate:]], axis=1)
    assert packed.shape[1] == C_END
    return packed.astype(BF16)


def kernel(x, attn_norm, w_in, kv_norm, w_uk, w_uv, rel_bias, w_branch_a, w_branch_b, w_out, ffn_norm,
           w_router_group, b_router_group, w_router_expert, b_router_expert, w_gate, w_up, w_down,
           final_norm):
    b, seq, d = x.shape
    n = b * seq
    nkc = seq // BLK
    assert w_in.shape[0] == 1, "one layer"
    x2 = x.reshape(n, d)

    outs = _proj(x2, attn_norm[0], _pack_w_in(w_in[0]), kv_norm[0])
    qa, ckv, qi, kw = outs[:4]
    dil = outs[4:13]
    gates = outs[13]

    dsa_bias, dil_bias = _bias_tiles(rel_bias, nkc)

    t3 = lambda a: jnp.swapaxes(a.reshape(b, seq, a.shape[-1]), 1, 2)
    kw3 = kw.reshape(b, seq, LANES)
    wiT = jnp.swapaxes(kw3[:, :, IDX_DIM:IDX_DIM + 8], 1, 2)
    kidx = kw3[:, :, :IDX_DIM].astype(BF16)
    ckv3 = ckv.reshape(b, seq, KV_LATENT)
    ckvT = jnp.swapaxes(ckv3, 1, 2)
    ya = _dsa(t3(qi), wiT, t3(qa), kidx, ckv3, ckvT, w_uk[0].astype(BF16), w_uv[0].astype(BF16), dsa_bias)

    os_, lses = [], []
    for g, (_, dilation) in enumerate(DIL_GROUPS):
        o, lse = _dilated_group(dil[g], dil[3 + g], dil[6 + g], dil_bias, g, dilation, b)
        os_.append(o)
        lses.append(lse)

    wr = jnp.concatenate([w_router_group[0],
                          jnp.swapaxes(w_router_expert[0], 0, 1).reshape(d, N_EXPERTS),
                          jnp.zeros((d, ROUTE_COLS - N_GROUPS - N_EXPERTS), F32)], axis=1)
    wr_hi = wr.astype(BF16)
    wr = jnp.concatenate([wr_hi, (wr - wr_hi.astype(F32)).astype(BF16)], axis=1)
    br = jnp.concatenate([b_router_group[0], b_router_expert[0].reshape(-1),
                          jnp.zeros((ROUTE_COLS - N_GROUPS - N_EXPERTS,), F32)]).reshape(1, ROUTE_COLS)
    h, t, rw, ri = _mix(x2, ya.reshape(n, -1), os_, lses, gates,
                        w_branch_a[0].astype(BF16), w_branch_b[0].astype(BF16), w_out[0].astype(BF16),
                        ffn_norm[0], wr, br)

    row_token, tile_expert, pos = _route_plan(ri[:, :2])
    y_sorted = _experts(tile_expert, row_token, t, w_gate[0].astype(BF16), w_up[0].astype(BF16),
                        w_down[0].astype(BF16), d)

    pos_tiles = jnp.swapaxes(pos.reshape(n // FIN_TM, FIN_TM, 2), 1, 2).reshape(n // FIN_TM, 1, 2 * FIN_TM)
    out = _final(pos_tiles, y_sorted, h, rw, final_norm)
    return out.reshape(b, seq, d)
```

```python
import functools
import math

import numpy as np
import jax
import jax.numpy as jnp
from jax import lax
from jax.experimental import pallas as pl
from jax.experimental.pallas import tpu as pltpu

F32 = jnp.float32
BF16 = jnp.bfloat16
I32 = jnp.int32

LANES = 128
VMEM_LIMIT_BYTES = 56 * 1024 * 1024

HEAD_DIM = 64
DSA_HEADS = 8
KV_LATENT = 256
IDX_HEADS = 4
IDX_DIM = 64
TOPK_MAX = 256
DIL_GROUPS = ((128, 1), (512, 4), (2048, 16))
DIL_HPG = 4
DIL_OUT = DIL_HPG * HEAD_DIM
NUM_BUCKETS = 32
MAX_DISTANCE = 2048
N_GROUPS = 4
EXPERTS_PER_GROUP = 8
N_EXPERTS = N_GROUPS * EXPERTS_PER_GROUP
RMS_EPS = 1e-6
NEG = -1e30
INT_MIN = -2 ** 31

BLK = 128
PROJ_TM = 512
MIX_TM = 256
MOE_TM = 256
FIN_TM = 256
ROUTE_COLS = 128


def _cparams(sem):
    return pltpu.CompilerParams(dimension_semantics=sem, vmem_limit_bytes=VMEM_LIMIT_BYTES)


C_QA = 0
C_KV = 512
C_QI = 768
C_KW = 1024
C_DIL = 1152
C_GATE = C_DIL + 9 * 256
C_END = C_GATE + 2048


def _proj_kernel(x_ref, g_ref, w_ref, kvg_ref, qa_ref, ckv_ref, qi_ref, kw_ref, *rest):
    dil_refs = rest[:9]
    gate_ref = rest[9]
    stage_ref = rest[10]
    x = x_ref[...]
    u = x * lax.rsqrt(jnp.mean(x * x, axis=-1, keepdims=True) + RMS_EPS) * g_ref[...]
    u = u.astype(BF16)

    def mm(a, b):
        return jnp.dot(u, w_ref[:, a:b], preferred_element_type=F32)

    qa_ref[...] = mm(C_QA, C_KV).astype(BF16)
    c = mm(C_KV, C_QI)
    c = c * lax.rsqrt(jnp.mean(c * c, axis=-1, keepdims=True) + RMS_EPS) * kvg_ref[...]
    ckv_ref[...] = c.astype(BF16)
    qi_ref[...] = mm(C_QI, C_KW).astype(BF16)
    kw_ref[...] = mm(C_KW, C_DIL)
    for j in range(9):
        val = mm(C_DIL + 256 * j, C_DIL + 256 * (j + 1))
        r = DIL_GROUPS[j % 3][1]
        if r == 1:
            dil_refs[j][...] = val.astype(BF16)
        else:
            for hf in range(2):
                stage_ref[hf] = val[:, hf * LANES:(hf + 1) * LANES]
            for rho in range(r):
                for hf in range(2):
                    dil_refs[j][:, rho * 256 + hf * LANES:rho * 256 + (hf + 1) * LANES] = (
                        stage_ref[hf, pl.ds(rho, val.shape[0] // r, stride=r), :].astype(BF16))
    for j in range(4):
        gate_ref[:, 512 * j:512 * (j + 1)] = mm(C_GATE + 512 * j, C_GATE + 512 * (j + 1)).astype(BF16)


def _proj(x2, attn_norm, w_packed, kv_norm):
    n, d = x2.shape
    tm = PROJ_TM
    row = lambda i: (i, 0)
    const = lambda i: (0, 0)
    outs = [jax.ShapeDtypeStruct((n, 512), BF16), jax.ShapeDtypeStruct((n, 256), BF16),
            jax.ShapeDtypeStruct((n, 256), BF16), jax.ShapeDtypeStruct((n, 128), F32)]
    for j in range(9):
        r = DIL_GROUPS[j % 3][1]
        outs.append(jax.ShapeDtypeStruct((n // r, r * 256), BF16))
    outs += [jax.ShapeDtypeStruct((n, 2048), BF16)]
    out_specs = [pl.BlockSpec((tm * s.shape[0] // n, s.shape[1]), row) for s in outs]
    return pl.pallas_call(
        _proj_kernel,
        out_shape=outs,
        grid=(n // tm,),
        in_specs=[pl.BlockSpec((tm, d), row), pl.BlockSpec((1, d), const),
                  pl.BlockSpec((d, C_END), const), pl.BlockSpec((1, KV_LATENT), const)],
        out_specs=out_specs,
        scratch_shapes=[pltpu.VMEM((2, tm, LANES), F32)],
        compiler_params=_cparams(("parallel",)),
        name="proj",
    )(x2, attn_norm.reshape(1, d), w_packed, kv_norm.reshape(1, KV_LATENT))


def _bucket_thresholds():
    max_exact = NUM_BUCKETS // 2
    d = np.arange(0, MAX_DISTANCE + 1)
    nf = np.maximum(d, 1).astype(np.float32)
    large = max_exact + (np.log(nf / np.float32(max_exact)) / np.float32(math.log(MAX_DISTANCE / max_exact))
                         * np.float32(NUM_BUCKETS - max_exact)).astype(np.int32)
    large = np.minimum(large, NUM_BUCKETS - 1)
    bucket = np.where(d < max_exact, d, large)
    assert np.all(np.diff(bucket) >= 0)
    return [int(np.argmax(bucket >= b)) for b in range(1, NUM_BUCKETS)]


_BUCKET_THR = _bucket_thresholds()


def _bias_from_distance(dist, tab_ref, heads):
    masks = [dist >= t for t in _BUCKET_THR]
    out = []
    for h in heads:
        v = jnp.full(dist.shape, tab_ref[0, h], F32)
        for b in range(1, NUM_BUCKETS):
            v = jnp.where(masks[b - 1], tab_ref[b, h], v)
        out.append(v)
    return out


def _dsa_bias_kernel(tab_ref, o_ref):
    delta = pl.program_id(0)
    j = lax.broadcasted_iota(I32, (BLK, BLK), 0)
    i = lax.broadcasted_iota(I32, (BLK, BLK), 1)
    dist = jnp.maximum(delta * BLK + i - j, 0)
    tiles = _bias_from_distance(dist, tab_ref, range(DSA_HEADS))
    for h in range(DSA_HEADS):
        o_ref[0, h] = tiles[h]


def _dil_bias_kernel(tab_ref, o_ref, *, dilations):
    g = pl.program_id(0)
    j = lax.broadcasted_iota(I32, (2 * BLK, BLK), 0)
    i = lax.broadcasted_iota(I32, (2 * BLK, BLK), 1)
    step = i + BLK - j
    valid = (step >= 0) & (step <= BLK)
    for gi, r in enumerate(dilations):
        @pl.when(g == gi)
        def _():
            dist = jnp.maximum(step, 0) * r
            heads = [DSA_HEADS + gi * DIL_HPG + hh for hh in range(DIL_HPG)]
            tiles = _bias_from_distance(dist, tab_ref, heads)
            for hh in range(DIL_HPG):
                o_ref[0, :, hh * BLK:(hh + 1) * BLK] = jnp.where(valid, tiles[hh], NEG)


def _bias_tiles(rel_bias, nkc):
    smem = pl.BlockSpec(memory_space=pltpu.SMEM)
    dsa = pl.pallas_call(
        _dsa_bias_kernel,
        out_shape=jax.ShapeDtypeStruct((nkc, DSA_HEADS, BLK, BLK), F32),
        grid=(nkc,),
        in_specs=[smem],
        out_specs=pl.BlockSpec((1, DSA_HEADS, BLK, BLK), lambda d: (d, 0, 0, 0)),
        compiler_params=_cparams(("parallel",)),
        name="dsa_bias",
    )(rel_bias)
    dil = pl.pallas_call(
        functools.partial(_dil_bias_kernel, dilations=tuple(r for _, r in DIL_GROUPS)),
        out_shape=jax.ShapeDtypeStruct((len(DIL_GROUPS), 2 * BLK, DIL_HPG * BLK), F32),
        grid=(len(DIL_GROUPS),),
        in_specs=[smem],
        out_specs=pl.BlockSpec((1, 2 * BLK, DIL_HPG * BLK), lambda g: (g, 0, 0)),
        compiler_params=_cparams(("parallel",)),
        name="dil_bias",
    )(rel_bias)
    return dsa, dil


SUP = 4
SROWS = SUP * BLK


PLANE_KEYS = 32 * 8


def _bit_planes(words):
    x = list(words)
    j, m = 16, 0x0000FFFF
    while j:
        k = 0
        while k < 32:
            t = (x[k] ^ lax.shift_right_logical(x[k + j], jnp.int32(j))) & m
            x[k] = x[k] ^ t
            x[k + j] = x[k + j] ^ jnp.left_shift(t, jnp.int32(j))
            k = (k + j + 1) & ~j
        j >>= 1
        m = (m ^ (m << j)) & 0xFFFFFFFF
        m = m - (1 << 32) if m >= (1 << 31) else m
    return x


def _dsa_kernel(qiT_ref, wiT_ref, qaT_ref, kidx_ref, ckv_ref, ckvT_ref, wuk_ref, wuv_ref, bias_ref, tri_ref,
                y_ref, sc_ref, planes_ref, qlT_ref, x_ref, pT_ref, *, topk, nsc, qb0):
    qb = qb0 + pl.program_id(1)
    row = lax.broadcasted_iota(I32, (SROWS, BLK), 0)
    col = lax.broadcasted_iota(I32, (SROWS, BLK), 1)
    trips = [slice(sc * SROWS, (sc + 1) * SROWS) for sc in range(nsc)]

    def causal(sc):
        return row <= col + (qb * BLK - sc * SROWS)

    for h in range(DSA_HEADS):
        ql = jnp.dot(wuk_ref[h], qaT_ref[0, h * HEAD_DIM:(h + 1) * HEAD_DIM, :],
                     preferred_element_type=F32) * (HEAD_DIM ** -0.5)
        qlT_ref[:, h * BLK:(h + 1) * BLK] = ql.astype(BF16)

    wq = wiT_ref[0] * (IDX_HEADS ** -0.5)
    for sc, ts in enumerate(trips):
        kx = kidx_ref[0, ts, :]
        acc = jnp.zeros((SROWS, BLK), F32)
        for h in range(IDX_HEADS):
            s = jnp.dot(kx, qiT_ref[0, h * IDX_DIM:(h + 1) * IDX_DIM, :],
                        preferred_element_type=F32) * (IDX_DIM ** -0.5)
            acc = acc + wq[h:h + 1, :] * jnp.maximum(s, 0.0)
        acc = jnp.where(causal(sc), acc, NEG)
        sc_ref[ts, :] = acc
        bits = pltpu.bitcast(acc, I32)
        bits = jnp.where(bits == INT_MIN, 0, bits)
        ukey = bits ^ ((bits >> 31) & 0x7FFFFFFF) ^ INT_MIN
        for grp in range(SROWS // PLANE_KEYS):
            tiles = [ukey[grp * PLANE_KEYS + j * 8:grp * PLANE_KEYS + (j + 1) * 8] for j in range(32)]
            for b, plane in enumerate(_bit_planes(tiles)):
                planes_ref[b, sc * (SROWS // PLANE_KEYS) + grp] = plane

    def count(pred):
        cnt = jnp.zeros((8, BLK), I32)
        for ts in trips:
            cnt = cnt + jnp.sum(jnp.where(pred(sc_ref[ts, :]), 1, 0).reshape(SROWS // 8, 8, BLK), axis=0)
        return jnp.sum(cnt, axis=0, keepdims=True)

    def as_float(key):
        return pltpu.bitcast(key ^ ((key >> 31) & 0x7FFFFFFF), F32)

    def bit_body(it, carry):
        alive, above, code = carry
        ones = alive & planes_ref[it]
        c = jnp.sum(jnp.sum(lax.population_count(ones), axis=0), axis=0, keepdims=True)
        take = above + c >= topk
        alive = jnp.where(take, ones, alive ^ ones)
        above = jnp.where(take, above, above + c)
        code = code | jnp.where(take, jnp.left_shift(jnp.int32(1), 31 - it), 0)
        return alive, above, code

    nw = nsc * (SROWS // PLANE_KEYS)
    _, _, code = lax.fori_loop(
        0, 32, bit_body,
        (jnp.full((nw, 8, BLK), -1, I32), jnp.zeros((1, BLK), I32), jnp.zeros((1, BLK), I32)))
    guess = as_float(code ^ INT_MIN)
    n_gt_guess = count(lambda v: v > guess)
    proven = (n_gt_guess < topk) & (count(lambda v: v >= guess) >= topk)

    def bisect():
        def thr_body(it, lo):
            cand = lo + jnp.left_shift(jnp.int32(1), 31 - it)
            cand_f = as_float(cand)
            return jnp.where(count(lambda v: v >= cand_f) >= topk, cand, lo)
        t = as_float(lax.fori_loop(0, 32, thr_body, jnp.full((1, BLK), INT_MIN, I32)))
        return t, count(lambda v: v > t)

    thr, n_gt = lax.cond(jnp.min(jnp.where(proven, 1, 0)) > 0, lambda: (guess, n_gt_guess), bisect)
    ties_wanted = (topk - n_gt).astype(F32)

    m = [jnp.full((8, BLK), NEG, F32) for _ in range(DSA_HEADS)]
    ties_before = jnp.zeros((1, BLK), F32)
    for sc, ts in enumerate(trips):
        k = sc_ref[ts, :]
        tie = k == thr
        tie_rank = jnp.dot(tri_ref[...], jnp.where(tie, 1.0, 0.0).astype(BF16),
                           preferred_element_type=F32) + ties_before
        ties_before = tie_rank[SROWS - 1:SROWS, :]
        sel = ((k > thr) | (tie & (tie_rank <= ties_wanted))) & causal(sc)
        am = jnp.where(sel, 0.0, NEG)
        ck = ckv_ref[0, ts, :]
        for hp in range(DSA_HEADS // 2):
            lg2 = jnp.dot(ck, qlT_ref[:, 2 * hp * BLK:(2 * hp + 2) * BLK], preferred_element_type=F32)
            for h in (2 * hp, 2 * hp + 1):
                lg = lg2[:, (h % 2) * BLK:(h % 2 + 1) * BLK]
                for j in range(SUP):
                    rs = slice(j * BLK, (j + 1) * BLK)
                    delta = jnp.maximum(qb - (sc * SUP + j), 0)
                    x = lg[rs] + bias_ref[delta, h] + am[rs]
                    x_ref[sc * SROWS + j * BLK:sc * SROWS + (j + 1) * BLK, h * BLK:(h + 1) * BLK] = x
                    m[h] = jnp.maximum(m[h], jnp.max(x.reshape(BLK // 8, 8, BLK), axis=0))
    m = [jnp.max(v, axis=0, keepdims=True) for v in m]

    l = [jnp.zeros((8, BLK), F32) for _ in range(DSA_HEADS)]
    for sc, ts in enumerate(trips):
        for h in range(DSA_HEADS):
            hs = slice(h * BLK, (h + 1) * BLK)
            p = jnp.exp(x_ref[ts, hs] - m[h])
            pT_ref[ts, hs] = p.astype(BF16)
            l[h] = l[h] + jnp.sum(p.reshape(SROWS // 8, 8, BLK), axis=0)

    for hp in range(DSA_HEADS // 2):
        o2 = jnp.dot(ckvT_ref[0], pT_ref[:, 2 * hp * BLK:(2 * hp + 2) * BLK],
                     preferred_element_type=F32)
        for h in (2 * hp, 2 * hp + 1):
            inv = 1.0 / jnp.sum(l[h], axis=0, keepdims=True)
            oh = (o2[:, (h % 2) * BLK:(h % 2 + 1) * BLK] * inv).T.astype(BF16)
            yh = jnp.dot(oh, wuv_ref[h], preferred_element_type=F32)
            y_ref[0, :, h * HEAD_DIM:(h + 1) * HEAD_DIM] = yh.astype(BF16)


def _dsa_group(g, qiT, wiT, qaT, kidx, ckv, ckvT, wuk, wuv, bias_tiles, tri):
    b, seq, _ = ckv.shape
    nkc = seq // BLK
    nsc = g + 1
    nk = nsc * SROWS
    topk = min(TOPK_MAX, seq // 4)
    qblk = lambda rows: pl.BlockSpec((1, rows, BLK), lambda bi, qi: (bi, 0, g * SUP + qi))
    head3 = lambda cols: pl.BlockSpec((1, nk, cols), lambda bi, qi: (bi, 0, 0))
    const = lambda s: pl.BlockSpec(s, lambda bi, qi: (0,) * len(s))
    hl = DSA_HEADS * BLK
    return pl.pallas_call(
        functools.partial(_dsa_kernel, topk=topk, nsc=nsc, qb0=g * SUP),
        out_shape=jax.ShapeDtypeStruct((b, SROWS, DSA_HEADS * HEAD_DIM), BF16),
        grid=(b, SUP),
        in_specs=[qblk(IDX_HEADS * IDX_DIM), qblk(8), qblk(DSA_HEADS * HEAD_DIM),
                  head3(IDX_DIM), head3(KV_LATENT),
                  pl.BlockSpec((1, KV_LATENT, nk), lambda bi, qi: (bi, 0, 0)),
                  const((DSA_HEADS, KV_LATENT, HEAD_DIM)), const((DSA_HEADS, KV_LATENT, HEAD_DIM)),
                  const((nkc, DSA_HEADS, BLK, BLK)), const((SROWS, SROWS))],
        out_specs=pl.BlockSpec((1, BLK, DSA_HEADS * HEAD_DIM), lambda bi, qi: (bi, qi, 0)),
        scratch_shapes=[pltpu.VMEM((nk, BLK), F32),
                        pltpu.VMEM((32, nk // PLANE_KEYS, 8, BLK), I32),
                        pltpu.VMEM((KV_LATENT, hl), BF16),
                        pltpu.VMEM((nk, hl), F32),
                        pltpu.VMEM((nk, hl), BF16)],
        compiler_params=_cparams(("parallel", "arbitrary")),
        name=f"dsa_g{g}",
    )(qiT, wiT, qaT, kidx, ckv, ckvT, wuk, wuv, bias_tiles, tri)


def _dsa(qiT, wiT, qaT, kidx, ckv, ckvT, wuk, wuv, bias_tiles):
    seq = ckv.shape[1]
    assert seq % SROWS == 0 and seq >= 4 * TOPK_MAX
    tri = jnp.tril(jnp.ones((SROWS, SROWS), BF16))
    groups = [_dsa_group(g, qiT, wiT, qaT, kidx, ckv, ckvT, wuk, wuv, bias_tiles, tri)
              for g in range(seq // SROWS)]
    return jnp.concatenate(groups, axis=1)


DIL_UNROLL = 5
DIL_STEP_ROWS = 1024


def _dil_kernel(q_ref, k_ref, v_ref, bm_ref, o_ref, lse_ref, vT_ref, *, nblk, nres):
    hq = DIL_HPG * BLK
    rowh = lax.broadcasted_iota(I32, (hq, DIL_OUT), 0) // BLK
    colh = lax.broadcasted_iota(I32, (hq, DIL_OUT), 1) // HEAD_DIM
    same_head = rowh == colh

    for res in range(nres):
        cs = slice(res * DIL_OUT, (res + 1) * DIL_OUT)

        for n in range(nblk):
            vT_ref[n] = v_ref[0, n * BLK:(n + 1) * BLK, cs].astype(F32).T.astype(BF16)

        def block(qo, kw, vT, bm, cs=cs):
            q = q_ref[0, pl.ds(qo, BLK), cs]
            qd = jnp.where(same_head, jnp.concatenate([q] * DIL_HPG, axis=0), jnp.zeros((), BF16))
            s = lax.dot_general(kw, qd, (((1,), (1,)), ((), ())), preferred_element_type=F32)
            s = s * (HEAD_DIM ** -0.5) + bm
            m = jnp.max(s, axis=0, keepdims=True)
            e = jnp.exp(s - m)
            l = jnp.sum(e, axis=0, keepdims=True)
            oT = jnp.dot(vT, e.astype(BF16), preferred_element_type=F32)
            inv = 1.0 / l
            lse = m + jnp.log(l)
            outs, lses = [], []
            for hh in range(DIL_HPG):
                qs = slice(hh * BLK, (hh + 1) * BLK)
                outs.append(oT[hh * HEAD_DIM:(hh + 1) * HEAD_DIM, qs] * inv[:, qs])
                lses.append(jnp.broadcast_to(lse[:, qs], (HEAD_DIM, BLK)))
            o_ref[0, pl.ds(qo, BLK), cs] = jnp.concatenate(outs, axis=0).T.astype(BF16)
            lse_ref[0, pl.ds(qo, BLK), cs] = jnp.concatenate(lses, axis=0).T

        block(0, k_ref[0, 0:BLK, cs], vT_ref[0], bm_ref[0, BLK:, :])

        def body(n, carry, cs=cs, block=block):
            ko = pl.multiple_of((n - 1) * BLK, BLK)
            vT = jnp.concatenate([vT_ref[n - 1], vT_ref[n]], axis=1)
            block(pl.multiple_of(n * BLK, BLK), k_ref[0, pl.ds(ko, 2 * BLK), cs], vT, bm_ref[0])
            return carry

        lax.fori_loop(1, nblk, body, 0, unroll=DIL_UNROLL)


def _dilated_group(q, k, v, bm, g, dilation, b):
    c = DIL_OUT
    ls = q.shape[0] // b
    nblk = ls // BLK
    nres = max(1, min(dilation, DIL_STEP_ROWS // ls))
    view = lambda a: a.reshape(b, ls, dilation * c)
    blk = pl.BlockSpec((1, ls, nres * c), lambda bi, ri: (bi, 0, ri))
    o, lse = pl.pallas_call(
        functools.partial(_dil_kernel, nblk=nblk, nres=nres),
        out_shape=[jax.ShapeDtypeStruct((b, ls, dilation * c), BF16),
                   jax.ShapeDtypeStruct((b, ls, dilation * c), F32)],
        grid=(b, dilation // nres),
        in_specs=[blk, blk, blk, pl.BlockSpec((1, 2 * BLK, DIL_HPG * BLK), lambda bi, ri: (g, 0, 0))],
        out_specs=[blk, blk],
        scratch_shapes=[pltpu.VMEM((nblk, c, BLK), BF16)],
        compiler_params=_cparams(("parallel", "parallel")),
        name=f"dilated_g{g}",
    )(view(q), view(k), view(v), bm)
    return o.reshape(b * ls, dilation * c), lse.reshape(b * ls, dilation * c)


def _store_row_tiles(ref, base, val):
    rows, d = val.shape
    dt = d // LANES
    for s in range(dt):
        ref[pl.ds(base * dt + s, rows, stride=dt), :] = val[:, s * LANES:(s + 1) * LANES]


def _load_row_tiles(ref, base, rows, dt):
    return jnp.concatenate([ref[pl.ds(base * dt + s, rows, stride=dt), :] for s in range(dt)], axis=1)


MIX_COLS = 256


def _token_major(ref, stage_ref, r):
    if r == 1:
        return ref[...].astype(F32)
    rows = ref.shape[0]
    for rho in range(r):
        for hf in range(DIL_OUT // LANES):
            c0 = rho * DIL_OUT + hf * LANES
            stage_ref[hf, pl.ds(rho, rows, stride=r), :] = ref[:, c0:c0 + LANES].astype(F32)
    return jnp.concatenate([stage_ref[hf] for hf in range(DIL_OUT // LANES)], axis=1)


def _sigmoid(v):
    return 0.5 * jnp.tanh(0.5 * v) + 0.5


def _mix_kernel(x_ref, ya_ref, o1_ref, o2_ref, o3_ref, l1_ref, l2_ref, l3_ref, gate_ref,
                wa_ref, wb_ref, wo_ref, fg_ref, wr_ref, br_ref,
                h_ref, t_ref, rw_ref, ri_ref, mixed_ref, *stage_refs):
    dils = [r for _, r in DIL_GROUPS]
    o = [_token_major(ref, st, r) for ref, st, r in zip((o1_ref, o2_ref, o3_ref), stage_refs[:3], dils)]
    l1, l2, l3 = [_token_major(ref, st, r) for ref, st, r in zip((l1_ref, l2_ref, l3_ref), stage_refs[3:], dils)]
    mx = jnp.maximum(jnp.maximum(l1, l2), l3)
    e1, e2, e3 = jnp.exp(l1 - mx), jnp.exp(l2 - mx), jnp.exp(l3 - mx)
    inv = 1.0 / (e1 + e2 + e3)
    yb = ((e1 * inv) * o[0] + (e2 * inv) * o[1] + (e3 * inv) * o[2]).astype(BF16)
    ya = ya_ref[...]
    d = x_ref.shape[1]
    for c in range(0, d, MIX_COLS):
        cs = slice(c, c + MIX_COLS)
        a = jnp.dot(ya, wa_ref[:, cs], preferred_element_type=F32)
        bmix = jnp.dot(yb, wb_ref[:, cs], preferred_element_type=F32)
        g0 = _sigmoid(gate_ref[:, cs].astype(F32))
        g1 = _sigmoid(gate_ref[:, d + c:d + c + MIX_COLS].astype(F32))
        mixed_ref[:, cs] = (g0 * a + g1 * bmix).astype(BF16)
    h = x_ref[...] + jnp.dot(mixed_ref[...], wo_ref[...], preferred_element_type=F32)
    h_ref[...] = h
    t = h * lax.rsqrt(jnp.mean(h * h, axis=-1, keepdims=True) + RMS_EPS) * fg_ref[...]
    _store_row_tiles(t_ref, 0, t)

    t_hi = t.astype(BF16)
    t_lo = (t - t_hi.astype(F32)).astype(BF16)
    r1 = jnp.dot(t_hi, wr_ref[...], preferred_element_type=F32)
    r2 = jnp.dot(t_lo, wr_ref[:, :ROUTE_COLS], preferred_element_type=F32)
    logits = r1[:, :ROUTE_COLS] + (r1[:, ROUTE_COLS:] + r2) + br_ref[...]
    lane = lax.broadcasted_iota(I32, logits.shape, 1)
    ninf = -jnp.inf
    big = jnp.int32(10 ** 6)

    def first_argmax(v, vmax):
        return jnp.min(jnp.where(v == vmax, lane, big), axis=-1, keepdims=True)

    gl = jnp.where(lane < N_GROUPS, logits, ninf)
    gmax = jnp.max(gl, axis=-1, keepdims=True)
    gsel = first_argmax(gl, gmax)
    p_g = 1.0 / jnp.sum(jnp.exp(gl - gmax), axis=-1, keepdims=True)
    lo = N_GROUPS + gsel * EXPERTS_PER_GROUP
    el = jnp.where((lane >= lo) & (lane < lo + EXPERTS_PER_GROUP), logits, ninf)
    v1 = jnp.max(el, axis=-1, keepdims=True)
    i1 = first_argmax(el, v1)
    el2 = jnp.where(lane == i1, ninf, el)
    v2 = jnp.max(el2, axis=-1, keepdims=True)
    i2 = first_argmax(el2, v2)
    e2 = jnp.exp(v2 - v1)
    w1 = p_g / (1.0 + e2)
    w2 = p_g * e2 / (1.0 + e2)
    rw_ref[...] = jnp.where(lane == 0, w1, jnp.where(lane == 1, w2, 0.0))
    ri_ref[...] = jnp.where(lane == 0, i1 - N_GROUPS, jnp.where(lane == 1, i2 - N_GROUPS, 0))


def _mix(x2, ya, os_, lses, gates, wa, wb, wo, ffn_norm, wr, br):
    n, d = x2.shape
    tm = MIX_TM
    row = lambda c: pl.BlockSpec((tm, c), lambda i: (i, 0))
    res = lambda a: pl.BlockSpec((tm * a.shape[0] // n, a.shape[1]), lambda i: (i, 0))
    const = lambda s: pl.BlockSpec(s, lambda i: (0, 0))
    return pl.pallas_call(
        _mix_kernel,
        out_shape=[jax.ShapeDtypeStruct((n, d), F32), jax.ShapeDtypeStruct((n * (d // LANES), LANES), F32),
                   jax.ShapeDtypeStruct((n, ROUTE_COLS), F32), jax.ShapeDtypeStruct((n, ROUTE_COLS), I32)],
        grid=(n // tm,),
        in_specs=[row(d), row(512)] + [res(a) for a in os_] + [res(a) for a in lses] + [row(2 * d),
                  const(wa.shape), const(wb.shape), const(wo.shape), const((1, d)),
                  const(wr.shape), const((1, ROUTE_COLS))],
        out_specs=[row(d), pl.BlockSpec((tm * (d // LANES), LANES), lambda i: (i, 0)),
                   row(ROUTE_COLS), row(ROUTE_COLS)],
        scratch_shapes=[pltpu.VMEM((tm, d), BF16)] + [pltpu.VMEM((DIL_OUT // LANES, tm, LANES), F32)] * 6,
        compiler_params=_cparams(("parallel",)),
        name="mix",
    )(x2, ya, *os_, *lses, gates, wa, wb, wo, ffn_norm.reshape(1, d), wr, br)


GATHER_UNROLL = 8


def _start_row_gather(src_hbm, idx_ref, nrows, dt, buf, sem, slot):
    def body(g, c):
        for u in range(GATHER_UNROLL):
            r = g * GATHER_UNROLL + u
            src = pl.multiple_of(idx_ref[0, 0, r] * dt, dt)
            dst = pl.multiple_of((slot * nrows + r) * dt, dt)
            pltpu.make_async_copy(src_hbm.at[pl.ds(src, dt)], buf.at[pl.ds(dst, dt)], sem.at[slot]).start()
        return c
    lax.fori_loop(0, nrows // GATHER_UNROLL, body, 0)


def _wait_row_gather(src_hbm, nrows, dt, buf, sem, slot):
    dst = pl.multiple_of(slot * nrows * dt, dt)
    pltpu.make_async_copy(src_hbm.at[pl.ds(0, nrows * dt)], buf.at[pl.ds(dst, nrows * dt)], sem.at[slot]).wait()


def _gather_pipeline(i, nsteps, src_hbm, cur_ref, nxt_ref, nrows, dt, buf, sem):
    slot = i % 2

    @pl.when(i == 0)
    def _():
        _start_row_gather(src_hbm, cur_ref, nrows, dt, buf, sem, 0)

    @pl.when(i + 1 < nsteps)
    def _():
        _start_row_gather(src_hbm, nxt_ref, nrows, dt, buf, sem, 1 - slot)

    _wait_row_gather(src_hbm, nrows, dt, buf, sem, slot)
    return slot


def _expert_kernel(te_ref, cur_ref, nxt_ref, t_hbm, wg_ref, wu_ref, wd_ref, y_ref, buf, sem):
    i = pl.program_id(0)
    dt = wg_ref.shape[1] // LANES
    slot = _gather_pipeline(i, pl.num_programs(0), t_hbm, cur_ref, nxt_ref, MOE_TM, dt, buf, sem)
    xt = _load_row_tiles(buf, slot * MOE_TM, MOE_TM, dt).astype(BF16)
    hg = jnp.dot(xt, wg_ref[0], preferred_element_type=F32)
    hu = jnp.dot(xt, wu_ref[0], preferred_element_type=F32)
    hid = (hg * jax.nn.sigmoid(hg)) * hu
    _store_row_tiles(y_ref, 0, jnp.dot(hid.astype(BF16), wd_ref[0], preferred_element_type=F32))


def _experts(tile_expert, row_token, t, wg, wu, wd, d):
    dt = d // LANES
    ntiles = tile_expert.shape[0]
    ff = wg.shape[2]
    tok3 = row_token.reshape(ntiles, 1, MOE_TM)
    smem_cur = pl.BlockSpec((1, 1, MOE_TM), lambda i, te: (i, 0, 0), memory_space=pltpu.SMEM)
    smem_nxt = pl.BlockSpec((1, 1, MOE_TM), lambda i, te: (jnp.minimum(i + 1, ntiles - 1), 0, 0),
                            memory_space=pltpu.SMEM)
    wspec = lambda s: pl.BlockSpec((1,) + s, lambda i, te: (te[i], 0, 0))
    return pl.pallas_call(
        _expert_kernel,
        out_shape=jax.ShapeDtypeStruct((ntiles * MOE_TM * dt, LANES), F32),
        grid_spec=pltpu.PrefetchScalarGridSpec(
            num_scalar_prefetch=1,
            grid=(ntiles,),
            in_specs=[smem_cur, smem_nxt, pl.BlockSpec(memory_space=pl.ANY),
                      wspec((d, ff)), wspec((d, ff)), wspec((ff, d))],
            out_specs=pl.BlockSpec((MOE_TM * dt, LANES), lambda i, te: (i, 0)),
            scratch_shapes=[pltpu.VMEM((2 * MOE_TM * dt, LANES), F32), pltpu.SemaphoreType.DMA((2,))],
        ),
        compiler_params=_cparams(("arbitrary",)),
        name="experts",
    )(tile_expert, tok3, tok3, t, wg, wu, wd)


def _final_kernel(cur_ref, nxt_ref, y_hbm, h_ref, rw_ref, fn_ref, o_ref, buf, sem):
    i = pl.program_id(0)
    dt = h_ref.shape[1] // LANES
    slot = _gather_pipeline(i, pl.num_programs(0), y_hbm, cur_ref, nxt_ref, 2 * FIN_TM, dt, buf, sem)
    rw = rw_ref[...]
    y0 = _load_row_tiles(buf, slot * 2 * FIN_TM, FIN_TM, dt)
    y1 = _load_row_tiles(buf, slot * 2 * FIN_TM + FIN_TM, FIN_TM, dt)
    h = h_ref[...] + rw[:, 0:1] * y0 + rw[:, 1:2] * y1
    o_ref[...] = h * lax.rsqrt(jnp.mean(h * h, axis=-1, keepdims=True) + RMS_EPS) * fn_ref[...]


def _final(pos_tiles, y_sorted, h, rw, final_norm):
    n, d = h.shape
    tm = FIN_TM
    nt = n // tm
    smem_cur = pl.BlockSpec((1, 1, 2 * tm), lambda i: (i, 0, 0), memory_space=pltpu.SMEM)
    smem_nxt = pl.BlockSpec((1, 1, 2 * tm), lambda i: (jnp.minimum(i + 1, nt - 1), 0, 0),
                            memory_space=pltpu.SMEM)
    return pl.pallas_call(
        _final_kernel,
        out_shape=jax.ShapeDtypeStruct((n, d), F32),
        grid=(nt,),
        in_specs=[smem_cur, smem_nxt, pl.BlockSpec(memory_space=pl.ANY),
                  pl.BlockSpec((tm, d), lambda i: (i, 0)), pl.BlockSpec((tm, ROUTE_COLS), lambda i: (i, 0)),
                  pl.BlockSpec((1, d), lambda i: (0, 0))],
        out_specs=pl.BlockSpec((tm, d), lambda i: (i, 0)),
        scratch_shapes=[pltpu.VMEM((2 * 2 * tm * (d // LANES), LANES), F32), pltpu.SemaphoreType.DMA((2,))],
        compiler_params=_cparams(("arbitrary",)),
        name="final",
    )(pos_tiles, pos_tiles, y_sorted, h, rw, final_norm.reshape(1, d))


def _route_plan(gid):
    n = gid.shape[0]
    e = gid.reshape(-1)
    onehot = (e[:, None] == jnp.arange(N_EXPERTS, dtype=I32)[None, :]).astype(F32)
    chunk = 256
    oh3 = onehot.reshape(-1, chunk, N_EXPERTS)
    within = jnp.einsum("ij,tjk->tik", jnp.tril(jnp.ones((chunk, chunk), F32)), oh3)
    totals = within[:, -1, :]
    before = jnp.cumsum(totals, axis=0) - totals
    csum = (within + before[:, None, :]).reshape(-1, N_EXPERTS)
    rank = jnp.sum(csum * onehot, axis=1).astype(I32) - 1
    counts = (before[-1] + totals[-1]).astype(I32)
    padded = ((counts + MOE_TM - 1) // MOE_TM) * MOE_TM
    seg_end = jnp.cumsum(padded)
    pos = (seg_end - padded)[e] + rank
    nrows = 2 * n + N_EXPERTS * MOE_TM
    row_token = jnp.zeros((nrows,), I32).at[pos].set(jnp.arange(2 * n, dtype=I32) // 2, unique_indices=True)
    tile_start = jnp.arange(nrows // MOE_TM, dtype=I32) * MOE_TM
    tile_expert = jnp.minimum(jnp.sum(tile_start[:, None] >= seg_end[None, :], axis=1), N_EXPERTS - 1).astype(I32)
    return row_token, tile_expert, pos.reshape(n, 2)


def _pack_w_in(w):
    d = w.shape[0]
    o_kv, o_qi, o_ki, o_wi = 512, 768, 1024, 1088
    o_dil = o_wi + IDX_HEADS
    o_gate = o_dil + 9 * 256
    pad = jnp.zeros((d, LANES - IDX_DIM - IDX_HEADS), w.dtype)
    packed = jnp.concatenate([w[:, :o_ki], w[:, o_ki:o_wi], w[:, o_wi:o_dil], pad, w[:, o_dil:o_gate],
                              w[:, o_gate:]], axis=1)
    assert packed.shape[1] == C_END
    return packed.astype(BF16)


def kernel(x, attn_norm, w_in, kv_norm, w_uk, w_uv, rel_bias, w_branch_a, w_branch_b, w_out, ffn_norm,
           w_router_group, b_router_group, w_router_expert, b_router_expert, w_gate, w_up, w_down,
           final_norm):
    b, seq, d = x.shape
    n = b * seq
    nkc = seq // BLK
    assert w_in.shape[0] == 1, "one layer"
    x2 = x.reshape(n, d)

    outs = _proj(x2, attn_norm[0], _pack_w_in(w_in[0]), kv_norm[0])
    qa, ckv, qi, kw = outs[:4]
    dil = outs[4:13]
    gates = outs[13]

    dsa_bias, dil_bias = _bias_tiles(rel_bias, nkc)

    t3 = lambda a: jnp.swapaxes(a.reshape(b, seq, a.shape[-1]), 1, 2)
    kw3 = kw.reshape(b, seq, LANES)
    wiT = jnp.swapaxes(kw3[:, :, IDX_DIM:IDX_DIM + 8], 1, 2)
    kidx = kw3[:, :, :IDX_DIM].astype(BF16)
    ckv3 = ckv.reshape(b, seq, KV_LATENT)
    ckvT = jnp.swapaxes(ckv3, 1, 2)
    ya = _dsa(t3(qi), wiT, t3(qa), kidx, ckv3, ckvT, w_uk[0].astype(BF16), w_uv[0].astype(BF16), dsa_bias)

    os_, lses = [], []
    for g, (_, dilation) in enumerate(DIL_GROUPS):
        o, lse = _dilated_group(dil[g], dil[3 + g], dil[6 + g], dil_bias, g, dilation, b)
        os_.append(o)
        lses.append(lse)

    wr = jnp.concatenate([w_router_group[0],
                          jnp.swapaxes(w_router_expert[0], 0, 1).reshape(d, N_EXPERTS),
                          jnp.zeros((d, ROUTE_COLS - N_GROUPS - N_EXPERTS), F32)], axis=1)
    wr_hi = wr.astype(BF16)
    wr = jnp.concatenate([wr_hi, (wr - wr_hi.astype(F32)).astype(BF16)], axis=1)
    br = jnp.concatenate([b_router_group[0], b_router_expert[0].reshape(-1),
                          jnp.zeros((ROUTE_COLS - N_GROUPS - N_EXPERTS,), F32)]).reshape(1, ROUTE_COLS)
    h, t, rw, ri = _mix(x2, ya.reshape(n, -1), os_, lses, gates,
                        w_branch_a[0].astype(BF16), w_branch_b[0].astype(BF16), w_out[0].astype(BF16),
                        ffn_norm[0], wr, br)

    row_token, tile_expert, pos = _route_plan(ri[:, :2])
    y_sorted = _experts(tile_expert, row_token, t, w_gate[0].astype(BF16), w_up[0].astype(BF16),
                        w_down[0].astype(BF16), d)

    pos_tiles = jnp.swapaxes(pos.reshape(n // FIN_TM, FIN_TM, 2), 1, 2).reshape(n // FIN_TM, 1, 2 * FIN_TM)
    out = _final(pos_tiles, y_sorted, h, rw, final_norm)
    return out.reshape(b, seq, d)
```

```python
import functools
import math

import numpy as np
import jax
import jax.numpy as jnp
from jax import lax
from jax.experimental import pallas as pl
from jax.experimental.pallas import tpu as pltpu

F32 = jnp.float32
BF16 = jnp.bfloat16
I32 = jnp.int32

LANES = 128
VMEM_LIMIT_BYTES = 56 * 1024 * 1024

HEAD_DIM = 64
DSA_HEADS = 8
KV_LATENT = 256
IDX_HEADS = 4
IDX_DIM = 64
TOPK_MAX = 256
DIL_GROUPS = ((128, 1), (512, 4), (2048, 16))
DIL_HPG = 4
DIL_OUT = DIL_HPG * HEAD_DIM
NUM_BUCKETS = 32
MAX_DISTANCE = 2048
N_GROUPS = 4
EXPERTS_PER_GROUP = 8
N_EXPERTS = N_GROUPS * EXPERTS_PER_GROUP
RMS_EPS = 1e-6
NEG = -1e30
INT_MIN = -2 ** 31

BLK = 128
PROJ_TM = 512
MIX_TM = 256
MOE_TM = 256
FIN_TM = 256
ROUTE_COLS = 128


def _cparams(sem):
    return pltpu.CompilerParams(dimension_semantics=sem, vmem_limit_bytes=VMEM_LIMIT_BYTES)


C_QA = 0
C_KV = 512
C_QI = 768
C_KW = 1024
C_DIL = 1152
C_GATE = C_DIL + 9 * 256
C_END = C_GATE + 2048


def _proj_kernel(x_ref, g_ref, w_ref, kvg_ref, qa_ref, ckv_ref, qi_ref, kw_ref, *rest):
    dil_refs = rest[:9]
    gate_ref = rest[9]
    stage_ref = rest[10]
    x = x_ref[...]
    u = x * lax.rsqrt(jnp.mean(x * x, axis=-1, keepdims=True) + RMS_EPS) * g_ref[...]
    u = u.astype(BF16)

    def mm(a, b):
        return jnp.dot(u, w_ref[:, a:b], preferred_element_type=F32)

    qa_ref[...] = mm(C_QA, C_KV).astype(BF16)
    c = mm(C_KV, C_QI)
    c = c * lax.rsqrt(jnp.mean(c * c, axis=-1, keepdims=True) + RMS_EPS) * kvg_ref[...]
    ckv_ref[...] = c.astype(BF16)
    qi_ref[...] = mm(C_QI, C_KW).astype(BF16)
    kw_ref[...] = mm(C_KW, C_DIL)
    for j in range(9):
        val = mm(C_DIL + 256 * j, C_DIL + 256 * (j + 1))
        r = DIL_GROUPS[j % 3][1]
        if r == 1:
            dil_refs[j][...] = val.astype(BF16)
        else:
            for hf in range(2):
                stage_ref[hf] = val[:, hf * LANES:(hf + 1) * LANES]
            for rho in range(r):
                for hf in range(2):
                    dil_refs[j][:, rho * 256 + hf * LANES:rho * 256 + (hf + 1) * LANES] = (
                        stage_ref[hf, pl.ds(rho, val.shape[0] // r, stride=r), :].astype(BF16))
    for j in range(4):
        gate_ref[:, 512 * j:512 * (j + 1)] = mm(C_GATE + 512 * j, C_GATE + 512 * (j + 1)).astype(BF16)


def _proj(x2, attn_norm, w_packed, kv_norm):
    n, d = x2.shape
    tm = PROJ_TM
    row = lambda i: (i, 0)
    const = lambda i: (0, 0)
    outs = [jax.ShapeDtypeStruct((n, 512), BF16), jax.ShapeDtypeStruct((n, 256), BF16),
            jax.ShapeDtypeStruct((n, 256), BF16), jax.ShapeDtypeStruct((n, 128), F32)]
    for j in range(9):
        r = DIL_GROUPS[j % 3][1]
        outs.append(jax.ShapeDtypeStruct((n // r, r * 256), BF16))
    outs += [jax.ShapeDtypeStruct((n, 2048), BF16)]
    out_specs = [pl.BlockSpec((tm * s.shape[0] // n, s.shape[1]), row) for s in outs]
    return pl.pallas_call(
        _proj_kernel,
        out_shape=outs,
        grid=(n // tm,),
        in_specs=[pl.BlockSpec((tm, d), row), pl.BlockSpec((1, d), const),
                  pl.BlockSpec((d, C_END), const), pl.BlockSpec((1, KV_LATENT), const)],
        out_specs=out_specs,
        scratch_shapes=[pltpu.VMEM((2, tm, LANES), F32)],
        compiler_params=_cparams(("parallel",)),
        name="proj",
    )(x2, attn_norm.reshape(1, d), w_packed, kv_norm.reshape(1, KV_LATENT))


def _bucket_thresholds():
    max_exact = NUM_BUCKETS // 2
    d = np.arange(0, MAX_DISTANCE + 1)
    nf = np.maximum(d, 1).astype(np.float32)
    large = max_exact + (np.log(nf / np.float32(max_exact)) / np.float32(math.log(MAX_DISTANCE / max_exact))
                         * np.float32(NUM_BUCKETS - max_exact)).astype(np.int32)
    large = np.minimum(large, NUM_BUCKETS - 1)
    bucket = np.where(d < max_exact, d, large)
    assert np.all(np.diff(bucket) >= 0)
    return [int(np.argmax(bucket >= b)) for b in range(1, NUM_BUCKETS)]


_BUCKET_THR = _bucket_thresholds()


def _bias_from_distance(dist, tab_ref, heads):
    masks = [dist >= t for t in _BUCKET_THR]
    out = []
    for h in heads:
        v = jnp.full(dist.shape, tab_ref[0, h], F32)
        for b in range(1, NUM_BUCKETS):
            v = jnp.where(masks[b - 1], tab_ref[b, h], v)
        out.append(v)
    return out


def _dsa_bias_kernel(tab_ref, o_ref):
    delta = pl.program_id(0)
    j = lax.broadcasted_iota(I32, (BLK, BLK), 0)
    i = lax.broadcasted_iota(I32, (BLK, BLK), 1)
    dist = jnp.maximum(delta * BLK + i - j, 0)
    tiles = _bias_from_distance(dist, tab_ref, range(DSA_HEADS))
    for h in range(DSA_HEADS):
        o_ref[0, h] = tiles[h]


def _dil_bias_kernel(tab_ref, o_ref, *, dilations):
    g = pl.program_id(0)
    j = lax.broadcasted_iota(I32, (2 * BLK, BLK), 0)
    i = lax.broadcasted_iota(I32, (2 * BLK, BLK), 1)
    step = i + BLK - j
    valid = (step >= 0) & (step <= BLK)
    for gi, r in enumerate(dilations):
        @pl.when(g == gi)
        def _():
            dist = jnp.maximum(step, 0) * r
            heads = [DSA_HEADS + gi * DIL_HPG + hh for hh in range(DIL_HPG)]
            tiles = _bias_from_distance(dist, tab_ref, heads)
            for hh in range(DIL_HPG):
                o_ref[0, :, hh * BLK:(hh + 1) * BLK] = jnp.where(valid, tiles[hh], NEG)


def _bias_tiles(rel_bias, nkc):
    smem = pl.BlockSpec(memory_space=pltpu.SMEM)
    dsa = pl.pallas_call(
        _dsa_bias_kernel,
        out_shape=jax.ShapeDtypeStruct((nkc, DSA_HEADS, BLK, BLK), F32),
        grid=(nkc,),
        in_specs=[smem],
        out_specs=pl.BlockSpec((1, DSA_HEADS, BLK, BLK), lambda d: (d, 0, 0, 0)),
        compiler_params=_cparams(("parallel",)),
        name="dsa_bias",
    )(rel_bias)
    dil = pl.pallas_call(
        functools.partial(_dil_bias_kernel, dilations=tuple(r for _, r in DIL_GROUPS)),
        out_shape=jax.ShapeDtypeStruct((len(DIL_GROUPS), 2 * BLK, DIL_HPG * BLK), F32),
        grid=(len(DIL_GROUPS),),
        in_specs=[smem],
        out_specs=pl.BlockSpec((1, 2 * BLK, DIL_HPG * BLK), lambda g: (g, 0, 0)),
        compiler_params=_cparams(("parallel",)),
        name="dil_bias",
    )(rel_bias)
    return dsa, dil


SUP = 4
SROWS = SUP * BLK


PLANE_KEYS = 32 * 8


def _bit_planes(words):
    x = list(words)
    j, m = 16, 0x0000FFFF
    while j:
        k = 0
        while k < 32:
            t = (x[k] ^ lax.shift_right_logical(x[k + j], jnp.int32(j))) & m
            x[k] = x[k] ^ t
            x[k + j] = x[k + j] ^ jnp.left_shift(t, jnp.int32(j))
            k = (k + j + 1) & ~j
        j >>= 1
        m = (m ^ (m << j)) & 0xFFFFFFFF
        m = m - (1 << 32) if m >= (1 << 31) else m
    return x


def _dsa_kernel(qiT_ref, wiT_ref, qaT_ref, kidx_ref, ckv_ref, ckvT_ref, wuk_ref, wuv_ref, bias_ref, tri_ref,
                y_ref, sc_ref, planes_ref, qlT_ref, x_ref, pT_ref, *, topk, nsc, qb0):
    qb = qb0 + pl.program_id(1)
    row = lax.broadcasted_iota(I32, (SROWS, BLK), 0)
    col = lax.broadcasted_iota(I32, (SROWS, BLK), 1)
    trips = [slice(sc * SROWS, (sc + 1) * SROWS) for sc in range(nsc)]

    def causal(sc):
        return row <= col + (qb * BLK - sc * SROWS)

    for h in range(DSA_HEADS):
        ql = jnp.dot(wuk_ref[h], qaT_ref[0, h * HEAD_DIM:(h + 1) * HEAD_DIM, :],
                     preferred_element_type=F32) * (HEAD_DIM ** -0.5)
        qlT_ref[:, h * BLK:(h + 1) * BLK] = ql.astype(BF16)

    wq = wiT_ref[0] * (IDX_HEADS ** -0.5)
    for sc, ts in enumerate(trips):
        kx = kidx_ref[0, ts, :]
        acc = jnp.zeros((SROWS, BLK), F32)
        for h in range(IDX_HEADS):
            s = jnp.dot(kx, qiT_ref[0, h * IDX_DIM:(h + 1) * IDX_DIM, :],
                        preferred_element_type=F32) * (IDX_DIM ** -0.5)
            acc = acc + wq[h:h + 1, :] * jnp.maximum(s, 0.0)
        acc = jnp.where(causal(sc), acc, NEG)
        sc_ref[ts, :] = acc
        bits = pltpu.bitcast(acc, I32)
        bits = jnp.where(bits == INT_MIN, 0, bits)
        ukey = bits ^ ((bits >> 31) & 0x7FFFFFFF) ^ INT_MIN
        for grp in range(SROWS // PLANE_KEYS):
            tiles = [ukey[grp * PLANE_KEYS + j * 8:grp * PLANE_KEYS + (j + 1) * 8] for j in range(32)]
            for b, plane in enumerate(_bit_planes(tiles)):
                planes_ref[b, sc * (SROWS // PLANE_KEYS) + grp] = plane

    def count(pred):
        cnt = jnp.zeros((8, BLK), I32)
        for ts in trips:
            cnt = cnt + jnp.sum(jnp.where(pred(sc_ref[ts, :]), 1, 0).reshape(SROWS // 8, 8, BLK), axis=0)
        return jnp.sum(cnt, axis=0, keepdims=True)

    def as_float(key):
        return pltpu.bitcast(key ^ ((key >> 31) & 0x7FFFFFFF), F32)

    def bit_body(it, carry):
        alive, above, code = carry
        ones = alive & planes_ref[it]
        c = jnp.sum(jnp.sum(lax.population_count(ones), axis=0), axis=0, keepdims=True)
        take = above + c >= topk
        alive = jnp.where(take, ones, alive ^ ones)
        above = jnp.where(take, above, above + c)
        code = code | jnp.where(take, jnp.left_shift(jnp.int32(1), 31 - it), 0)
        return alive, above, code

    nw = nsc * (SROWS // PLANE_KEYS)
    _, _, code = lax.fori_loop(
        0, 32, bit_body,
        (jnp.full((nw, 8, BLK), -1, I32), jnp.zeros((1, BLK), I32), jnp.zeros((1, BLK), I32)))
    guess = as_float(code ^ INT_MIN)
    n_gt_guess = count(lambda v: v > guess)
    proven = (n_gt_guess < topk) & (count(lambda v: v >= guess) >= topk)

    def bisect():
        def thr_body(it, lo):
            cand = lo + jnp.left_shift(jnp.int32(1), 31 - it)
            cand_f = as_float(cand)
            return jnp.where(count(lambda v: v >= cand_f) >= topk, cand, lo)
        t = as_float(lax.fori_loop(0, 32, thr_body, jnp.full((1, BLK), INT_MIN, I32)))
        return t, count(lambda v: v > t)

    thr, n_gt = lax.cond(jnp.min(jnp.where(proven, 1, 0)) > 0, lambda: (guess, n_gt_guess), bisect)
    ties_wanted = (topk - n_gt).astype(F32)

    m = [jnp.full((8, BLK), NEG, F32) for _ in range(DSA_HEADS)]
    ties_before = jnp.zeros((1, BLK), F32)
    for sc, ts in enumerate(trips):
        k = sc_ref[ts, :]
        tie = k == thr
        tie_rank = jnp.dot(tri_ref[...], jnp.where(tie, 1.0, 0.0).astype(BF16),
                           preferred_element_type=F32) + ties_before
        ties_before = tie_rank[SROWS - 1:SROWS, :]
        sel = ((k > thr) | (tie & (tie_rank <= ties_wanted))) & causal(sc)
        am = jnp.where(sel, 0.0, NEG)
        ck = ckv_ref[0, ts, :]
        for hp in range(DSA_HEADS // 2):
            lg2 = jnp.dot(ck, qlT_ref[:, 2 * hp * BLK:(2 * hp + 2) * BLK], preferred_element_type=F32)
            for h in (2 * hp, 2 * hp + 1):
                lg = lg2[:, (h % 2) * BLK:(h % 2 + 1) * BLK]
                for j in range(SUP):
                    rs = slice(j * BLK, (j + 1) * BLK)
                    delta = jnp.maximum(qb - (sc * SUP + j), 0)
                    x = lg[rs] + bias_ref[delta, h] + am[rs]
                    x_ref[sc * SROWS + j * BLK:sc * SROWS + (j + 1) * BLK, h * BLK:(h + 1) * BLK] = x
                    m[h] = jnp.maximum(m[h], jnp.max(x.reshape(BLK // 8, 8, BLK), axis=0))
    m = [jnp.max(v, axis=0, keepdims=True) for v in m]

    l = [jnp.zeros((8, BLK), F32) for _ in range(DSA_HEADS)]
    for sc, ts in enumerate(trips):
        for h in range(DSA_HEADS):
            hs = slice(h * BLK, (h + 1) * BLK)
            p = jnp.exp(x_ref[ts, hs] - m[h])
            pT_ref[ts, hs] = p.astype(BF16)
            l[h] = l[h] + jnp.sum(p.reshape(SROWS // 8, 8, BLK), axis=0)

    for hp in range(DSA_HEADS // 2):
        o2 = jnp.dot(ckvT_ref[0], pT_ref[:, 2 * hp * BLK:(2 * hp + 2) * BLK],
                     preferred_element_type=F32)
        for h in (2 * hp, 2 * hp + 1):
            inv = 1.0 / jnp.sum(l[h], axis=0, keepdims=True)
            oh = (o2[:, (h % 2) * BLK:(h % 2 + 1) * BLK] * inv).T.astype(BF16)
            yh = jnp.dot(oh, wuv_ref[h], preferred_element_type=F32)
            y_ref[0, :, h * HEAD_DIM:(h + 1) * HEAD_DIM] = yh.astype(BF16)


def _dsa_group(g, qiT, wiT, qaT, kidx, ckv, ckvT, wuk, wuv, bias_tiles, tri):
    b, seq, _ = ckv.shape
    nkc = seq // BLK
    nsc = g + 1
    nk = nsc * SROWS
    topk = min(TOPK_MAX, seq // 4)
    qblk = lambda rows: pl.BlockSpec((1, rows, BLK), lambda bi, qi: (bi, 0, g * SUP + qi))
    head3 = lambda cols: pl.BlockSpec((1, nk, cols), lambda bi, qi: (bi, 0, 0))
    const = lambda s: pl.BlockSpec(s, lambda bi, qi: (0,) * len(s))
    hl = DSA_HEADS * BLK
    return pl.pallas_call(
        functools.partial(_dsa_kernel, topk=topk, nsc=nsc, qb0=g * SUP),
        out_shape=jax.ShapeDtypeStruct((b, SROWS, DSA_HEADS * HEAD_DIM), BF16),
        grid=(b, SUP),
        in_specs=[qblk(IDX_HEADS * IDX_DIM), qblk(8), qblk(DSA_HEADS * HEAD_DIM),
                  head3(IDX_DIM), head3(KV_LATENT),
                  pl.BlockSpec((1, KV_LATENT, nk), lambda bi, qi: (bi, 0, 0)),
                  const((DSA_HEADS, KV_LATENT, HEAD_DIM)), const((DSA_HEADS, KV_LATENT, HEAD_DIM)),
                  const((nkc, DSA_HEADS, BLK, BLK)), const((SROWS, SROWS))],
        out_specs=pl.BlockSpec((1, BLK, DSA_HEADS * HEAD_DIM), lambda bi, qi: (bi, qi, 0)),
        scratch_shapes=[pltpu.VMEM((nk, BLK), F32),
                        pltpu.VMEM((32, nk // PLANE_KEYS, 8, BLK), I32),
                        pltpu.VMEM((KV_LATENT, hl), BF16),
                        pltpu.VMEM((nk, hl), F32),
                        pltpu.VMEM((nk, hl), BF16)],
        compiler_params=_cparams(("parallel", "arbitrary")),
        name=f"dsa_g{g}",
    )(qiT, wiT, qaT, kidx, ckv, ckvT, wuk, wuv, bias_tiles, tri)


def _dsa(qiT, wiT, qaT, kidx, ckv, ckvT, wuk, wuv, bias_tiles):
    seq = ckv.shape[1]
    assert seq % SROWS == 0 and seq >= 4 * TOPK_MAX
    tri = jnp.tril(jnp.ones((SROWS, SROWS), BF16))
    groups = [_dsa_group(g, qiT, wiT, qaT, kidx, ckv, ckvT, wuk, wuv, bias_tiles, tri)
              for g in range(seq // SROWS)]
    return jnp.concatenate(groups, axis=1)


DIL_UNROLL = 5
DIL_STEP_ROWS = 1024


def _dil_kernel(q_ref, k_ref, v_ref, bm_ref, o_ref, lse_ref, vT_ref, *, nblk, nres):
    hq = DIL_HPG * BLK
    rowh = lax.broadcasted_iota(I32, (hq, DIL_OUT), 0) // BLK
    colh = lax.broadcasted_iota(I32, (hq, DIL_OUT), 1) // HEAD_DIM
    same_head = rowh == colh

    for res in range(nres):
        cs = slice(res * DIL_OUT, (res + 1) * DIL_OUT)

        for n in range(nblk):
            vT_ref[n] = v_ref[0, n * BLK:(n + 1) * BLK, cs].astype(F32).T.astype(BF16)

        def block(qo, kw, vT, bm, cs=cs):
            q = q_ref[0, pl.ds(qo, BLK), cs]
            qd = jnp.where(same_head, jnp.concatenate([q] * DIL_HPG, axis=0), jnp.zeros((), BF16))
            s = lax.dot_general(kw, qd, (((1,), (1,)), ((), ())), preferred_element_type=F32)
            s = s * (HEAD_DIM ** -0.5) + bm
            m = jnp.max(s, axis=0, keepdims=True)
            e = jnp.exp(s - m)
            l = jnp.sum(e, axis=0, keepdims=True)
            oT = jnp.dot(vT, e.astype(BF16), preferred_element_type=F32)
            inv = 1.0 / l
            lse = m + jnp.log(l)
            outs, lses = [], []
            for hh in range(DIL_HPG):
                qs = slice(hh * BLK, (hh + 1) * BLK)
                outs.append(oT[hh * HEAD_DIM:(hh + 1) * HEAD_DIM, qs] * inv[:, qs])
                lses.append(jnp.broadcast_to(lse[:, qs], (HEAD_DIM, BLK)))
            o_ref[0, pl.ds(qo, BLK), cs] = jnp.concatenate(outs, axis=0).T.astype(BF16)
            lse_ref[0, pl.ds(qo, BLK), cs] = jnp.concatenate(lses, axis=0).T

        block(0, k_ref[0, 0:BLK, cs], vT_ref[0], bm_ref[0, BLK:, :])

        def body(n, carry, cs=cs, block=block):
            ko = pl.multiple_of((n - 1) * BLK, BLK)
            vT = jnp.concatenate([vT_ref[n - 1], vT_ref[n]], axis=1)
            block(pl.multiple_of(n * BLK, BLK), k_ref[0, pl.ds(ko, 2 * BLK), cs], vT, bm_ref[0])
            return carry

        lax.fori_loop(1, nblk, body, 0, unroll=DIL_UNROLL)


def _dilated_group(q, k, v, bm, g, dilation, b):
    c = DIL_OUT
    ls = q.shape[0] // b
    nblk = ls // BLK
    nres = max(1, min(dilation, DIL_STEP_ROWS // ls))
    view = lambda a: a.reshape(b, ls, dilation * c)
    blk = pl.BlockSpec((1, ls, nres * c), lambda bi, ri: (bi, 0, ri))
    o, lse = pl.pallas_call(
        functools.partial(_dil_kernel, nblk=nblk, nres=nres),
        out_shape=[jax.ShapeDtypeStruct((b, ls, dilation * c), BF16),
                   jax.ShapeDtypeStruct((b, ls, dilation * c), F32)],
        grid=(b, dilation // nres),
        in_specs=[blk, blk, blk, pl.BlockSpec((1, 2 * BLK, DIL_HPG * BLK), lambda bi, ri: (g, 0, 0))],
        out_specs=[blk, blk],
        scratch_shapes=[pltpu.VMEM((nblk, c, BLK), BF16)],
        compiler_params=_cparams(("parallel", "parallel")),
        name=f"dilated_g{g}",
    )(view(q), view(k), view(v), bm)
    return o.reshape(b * ls, dilation * c), lse.reshape(b * ls, dilation * c)


def _store_row_tiles(ref, base, val):
    rows, d = val.shape
    dt = d // LANES
    for s in range(dt):
        ref[pl.ds(base * dt + s, rows, stride=dt), :] = val[:, s * LANES:(s + 1) * LANES]


def _load_row_tiles(ref, base, rows, dt):
    return jnp.concatenate([ref[pl.ds(base * dt + s, rows, stride=dt), :] for s in range(dt)], axis=1)


MIX_COLS = 256


def _token_major(ref, stage_ref, r):
    if r == 1:
        return ref[...].astype(F32)
    rows = ref.shape[0]
    for rho in range(r):
        for hf in range(DIL_OUT // LANES):
            c0 = rho * DIL_OUT + hf * LANES
            stage_ref[hf, pl.ds(rho, rows, stride=r), :] = ref[:, c0:c0 + LANES].astype(F32)
    return jnp.concatenate([stage_ref[hf] for hf in range(DIL_OUT // LANES)], axis=1)


def _sigmoid(v):
    return 0.5 * jnp.tanh(0.5 * v) + 0.5


def _mix_kernel(x_ref, ya_ref, o1_ref, o2_ref, o3_ref, l1_ref, l2_ref, l3_ref, gate_ref,
                wa_ref, wb_ref, wo_ref, fg_ref, wr_ref, br_ref,
                h_ref, t_ref, rw_ref, ri_ref, mixed_ref, *stage_refs):
    dils = [r for _, r in DIL_GROUPS]
    o = [_token_major(ref, st, r) for ref, st, r in zip((o1_ref, o2_ref, o3_ref), stage_refs[:3], dils)]
    l1, l2, l3 = [_token_major(ref, st, r) for ref, st, r in zip((l1_ref, l2_ref, l3_ref), stage_refs[3:], dils)]
    mx = jnp.maximum(jnp.maximum(l1, l2), l3)
    e1, e2, e3 = jnp.exp(l1 - mx), jnp.exp(l2 - mx), jnp.exp(l3 - mx)
    inv = 1.0 / (e1 + e2 + e3)
    yb = ((e1 * inv) * o[0] + (e2 * inv) * o[1] + (e3 * inv) * o[2]).astype(BF16)
    ya = ya_ref[...]
    d = x_ref.shape[1]
    for c in range(0, d, MIX_COLS):
        cs = slice(c, c + MIX_COLS)
        a = jnp.dot(ya, wa_ref[:, cs], preferred_element_type=F32)
        bmix = jnp.dot(yb, wb_ref[:, cs], preferred_element_type=F32)
        g0 = _sigmoid(gate_ref[:, cs].astype(F32))
        g1 = _sigmoid(gate_ref[:, d + c:d + c + MIX_COLS].astype(F32))
        mixed_ref[:, cs] = (g0 * a + g1 * bmix).astype(BF16)
    h = x_ref[...] + jnp.dot(mixed_ref[...], wo_ref[...], preferred_element_type=F32)
    h_ref[...] = h
    t = h * lax.rsqrt(jnp.mean(h * h, axis=-1, keepdims=True) + RMS_EPS) * fg_ref[...]
    _store_row_tiles(t_ref, 0, t)

    t_hi = t.astype(BF16)
    t_lo = (t - t_hi.astype(F32)).astype(BF16)
    r1 = jnp.dot(t_hi, wr_ref[...], preferred_element_type=F32)
    r2 = jnp.dot(t_lo, wr_ref[:, :ROUTE_COLS], preferred_element_type=F32)
    logits = r1[:, :ROUTE_COLS] + (r1[:, ROUTE_COLS:] + r2) + br_ref[...]
    lane = lax.broadcasted_iota(I32, logits.shape, 1)
    ninf = -jnp.inf
    big = jnp.int32(10 ** 6)

    def first_argmax(v, vmax):
        return jnp.min(jnp.where(v == vmax, lane, big), axis=-1, keepdims=True)

    gl = jnp.where(lane < N_GROUPS, logits, ninf)
    gmax = jnp.max(gl, axis=-1, keepdims=True)
    gsel = first_argmax(gl, gmax)
    p_g = 1.0 / jnp.sum(jnp.exp(gl - gmax), axis=-1, keepdims=True)
    lo = N_GROUPS + gsel * EXPERTS_PER_GROUP
    el = jnp.where((lane >= lo) & (lane < lo + EXPERTS_PER_GROUP), logits, ninf)
    v1 = jnp.max(el, axis=-1, keepdims=True)
    i1 = first_argmax(el, v1)
    el2 = jnp.where(lane == i1, ninf, el)
    v2 = jnp.max(el2, axis=-1, keepdims=True)
    i2 = first_argmax(el2, v2)
    e2 = jnp.exp(v2 - v1)
    w1 = p_g / (1.0 + e2)
    w2 = p_g * e2 / (1.0 + e2)
    rw_ref[...] = jnp.where(lane == 0, w1, jnp.where(lane == 1, w2, 0.0))
    ri_ref[...] = jnp.where(lane == 0, i1 - N_GROUPS, jnp.where(lane == 1, i2 - N_GROUPS, 0))


def _mix(x2, ya, os_, lses, gates, wa, wb, wo, ffn_norm, wr, br):
    n, d = x2.shape
    tm = MIX_TM
    row = lambda c: pl.BlockSpec((tm, c), lambda i: (i, 0))
    res = lambda a: pl.BlockSpec((tm * a.shape[0] // n, a.shape[1]), lambda i: (i, 0))
    const = lambda s: pl.BlockSpec(s, lambda i: (0, 0))
    return pl.pallas_call(
        _mix_kernel,
        out_shape=[jax.ShapeDtypeStruct((n, d), F32), jax.ShapeDtypeStruct((n * (d // LANES), LANES), F32),
                   jax.ShapeDtypeStruct((n, ROUTE_COLS), F32), jax.ShapeDtypeStruct((n, ROUTE_COLS), I32)],
        grid=(n // tm,),
        in_specs=[row(d), row(512)] + [res(a) for a in os_] + [res(a) for a in lses] + [row(2 * d),
                  const(wa.shape), const(wb.shape), const(wo.shape), const((1, d)),
                  const(wr.shape), const((1, ROUTE_COLS))],
        out_specs=[row(d), pl.BlockSpec((tm * (d // LANES), LANES), lambda i: (i, 0)),
                   row(ROUTE_COLS), row(ROUTE_COLS)],
        scratch_shapes=[pltpu.VMEM((tm, d), BF16)] + [pltpu.VMEM((DIL_OUT // LANES, tm, LANES), F32)] * 6,
        compiler_params=_cparams(("parallel",)),
        name="mix",
    )(x2, ya, *os_, *lses, gates, wa, wb, wo, ffn_norm.reshape(1, d), wr, br)


GATHER_UNROLL = 8


def _start_row_gather(src_hbm, idx_ref, nrows, dt, buf, sem, slot):
    def body(g, c):
        for u in range(GATHER_UNROLL):
            r = g * GATHER_UNROLL + u
            src = pl.multiple_of(idx_ref[0, 0, r] * dt, dt)
            dst = pl.multiple_of((slot * nrows + r) * dt, dt)
            pltpu.make_async_copy(src_hbm.at[pl.ds(src, dt)], buf.at[pl.ds(dst, dt)], sem.at[slot]).start()
        return c
    lax.fori_loop(0, nrows // GATHER_UNROLL, body, 0)


def _wait_row_gather(src_hbm, nrows, dt, buf, sem, slot):
    dst = pl.multiple_of(slot * nrows * dt, dt)
    pltpu.make_async_copy(src_hbm.at[pl.ds(0, nrows * dt)], buf.at[pl.ds(dst, nrows * dt)], sem.at[slot]).wait()


def _gather_pipeline(i, nsteps, src_hbm, cur_ref, nxt_ref, nrows, dt, buf, sem):
    slot = i % 2

    @pl.when(i == 0)
    def _():
        _start_row_gather(src_hbm, cur_ref, nrows, dt, buf, sem, 0)

    @pl.when(i + 1 < nsteps)
    def _():
        _start_row_gather(src_hbm, nxt_ref, nrows, dt, buf, sem, 1 - slot)

    _wait_row_gather(src_hbm, nrows, dt, buf, sem, slot)
    return slot


def _expert_kernel(te_ref, cur_ref, nxt_ref, t_hbm, wg_ref, wu_ref, wd_ref, y_ref, buf, sem):
    i = pl.program_id(0)
    dt = wg_ref.shape[1] // LANES
    slot = _gather_pipeline(i, pl.num_programs(0), t_hbm, cur_ref, nxt_ref, MOE_TM, dt, buf, sem)
    xt = _load_row_tiles(buf, slot * MOE_TM, MOE_TM, dt).astype(BF16)
    hg = jnp.dot(xt, wg_ref[0], preferred_element_type=F32)
    hu = jnp.dot(xt, wu_ref[0], preferred_element_type=F32)
    hid = (hg * jax.nn.sigmoid(hg)) * hu
    _store_row_tiles(y_ref, 0, jnp.dot(hid.astype(BF16), wd_ref[0], preferred_element_type=F32))


def _experts(tile_expert, row_token, t, wg, wu, wd, d):
    dt = d // LANES
    ntiles = tile_expert.shape[0]
    ff = wg.shape[2]
    tok3 = row_token.reshape(ntiles, 1, MOE_TM)
    smem_cur = pl.BlockSpec((1, 1, MOE_TM), lambda i, te: (i, 0, 0), memory_space=pltpu.SMEM)
    smem_nxt = pl.BlockSpec((1, 1, MOE_TM), lambda i, te: (jnp.minimum(i + 1, ntiles - 1), 0, 0),
                            memory_space=pltpu.SMEM)
    wspec = lambda s: pl.BlockSpec((1,) + s, lambda i, te: (te[i], 0, 0))
    return pl.pallas_call(
        _expert_kernel,
        out_shape=jax.ShapeDtypeStruct((ntiles * MOE_TM * dt, LANES), F32),
        grid_spec=pltpu.PrefetchScalarGridSpec(
            num_scalar_prefetch=1,
            grid=(ntiles,),
            in_specs=[smem_cur, smem_nxt, pl.BlockSpec(memory_space=pl.ANY),
                      wspec((d, ff)), wspec((d, ff)), wspec((ff, d))],
            out_specs=pl.BlockSpec((MOE_TM * dt, LANES), lambda i, te: (i, 0)),
            scratch_shapes=[pltpu.VMEM((2 * MOE_TM * dt, LANES), F32), pltpu.SemaphoreType.DMA((2,))],
        ),
        compiler_params=_cparams(("arbitrary",)),
        name="experts",
    )(tile_expert, tok3, tok3, t, wg, wu, wd)


def _final_kernel(cur_ref, nxt_ref, y_hbm, h_ref, rw_ref, fn_ref, o_ref, buf, sem):
    i = pl.program_id(0)
    dt = h_ref.shape[1] // LANES
    slot = _gather_pipeline(i, pl.num_programs(0), y_hbm, cur_ref, nxt_ref, 2 * FIN_TM, dt, buf, sem)
    rw = rw_ref[...]
    y0 = _load_row_tiles(buf, slot * 2 * FIN_TM, FIN_TM, dt)
    y1 = _load_row_tiles(buf, slot * 2 * FIN_TM + FIN_TM, FIN_TM, dt)
    h = h_ref[...] + rw[:, 0:1] * y0 + rw[:, 1:2] * y1
    o_ref[...] = h * lax.rsqrt(jnp.mean(h * h, axis=-1, keepdims=True) + RMS_EPS) * fn_ref[...]


def _final(pos_tiles, y_sorted, h, rw, final_norm):
    n, d = h.shape
    tm = FIN_TM
    nt = n // tm
    smem_cur = pl.BlockSpec((1, 1, 2 * tm), lambda i: (i, 0, 0), memory_space=pltpu.SMEM)
    smem_nxt = pl.BlockSpec((1, 1, 2 * tm), lambda i: (jnp.minimum(i + 1, nt - 1), 0, 0),
                            memory_space=pltpu.SMEM)
    return pl.pallas_call(
        _final_kernel,
        out_shape=jax.ShapeDtypeStruct((n, d), F32),
        grid=(nt,),
        in_specs=[smem_cur, smem_nxt, pl.BlockSpec(memory_space=pl.ANY),
                  pl.BlockSpec((tm, d), lambda i: (i, 0)), pl.BlockSpec((tm, ROUTE_COLS), lambda i: (i, 0)),
                  pl.BlockSpec((1, d), lambda i: (0, 0))],
        out_specs=pl.BlockSpec((tm, d), lambda i: (i, 0)),
        scratch_shapes=[pltpu.VMEM((2 * 2 * tm * (d // LANES), LANES), F32), pltpu.SemaphoreType.DMA((2,))],
        compiler_params=_cparams(("arbitrary",)),
        name="final",
    )(pos_tiles, pos_tiles, y_sorted, h, rw, final_norm.reshape(1, d))


def _route_plan(gid):
    n = gid.shape[0]
    e = gid.reshape(-1)
    onehot = (e[:, None] == jnp.arange(N_EXPERTS, dtype=I32)[None, :]).astype(F32)
    chunk = 256
    oh3 = onehot.reshape(-1, chunk, N_EXPERTS)
    within = jnp.einsum("ij,tjk->tik", jnp.tril(jnp.ones((chunk, chunk), F32)), oh3)
    totals = within[:, -1, :]
    before = jnp.cumsum(totals, axis=0) - totals
    csum = (within + before[:, None, :]).reshape(-1, N_EXPERTS)
    rank = jnp.sum(csum * onehot, axis=1).astype(I32) - 1
    counts = (before[-1] + totals[-1]).astype(I32)
    padded = ((counts + MOE_TM - 1) // MOE_TM) * MOE_TM
    seg_end = jnp.cumsum(padded)
    pos = (seg_end - padded)[e] + rank
    nrows = 2 * n + N_EXPERTS * MOE_TM
    tile_start = jnp.arange(nrows // MOE_TM, dtype=I32) * MOE_TM
    tile_expert = jnp.minimum(jnp.sum(tile_start[:, None] >= seg_end[None, :], axis=1), N_EXPERTS - 1).astype(I32)
    order = jnp.argsort(e, stable=True).astype(I32)
    first = jnp.cumsum(counts) - counts
    row_expert = jnp.repeat(tile_expert, MOE_TM)
    src = jnp.arange(nrows, dtype=I32) - (seg_end - padded)[row_expert] + first[row_expert]
    row_token = order[jnp.clip(src, 0, 2 * n - 1)] // 2
    return row_token, tile_expert, pos.reshape(n, 2)


def _pack_w_in(w):
    d = w.shape[0]
    o_kv, o_qi, o_ki, o_wi = 512, 768, 1024, 1088
    o_dil = o_wi + IDX_HEADS
    o_gate = o_dil + 9 * 256
    pad = jnp.zeros((d, LANES - IDX_DIM - IDX_HEADS), w.dtype)
    packed = jnp.concatenate([w[:, :o_ki], w[:, o_ki:o_wi], w[:, o_wi:o_dil], pad, w[:, o_dil:o_gate],
                              w[:, o_gate:]], axis=1)
    assert packed.shape[1] == C_END
    return packed.astype(BF16)


def kernel(x, attn_norm, w_in, kv_norm, w_uk, w_uv, rel_bias, w_branch_a, w_branch_b, w_out, ffn_norm,
           w_router_group, b_router_group, w_router_expert, b_router_expert, w_gate, w_up, w_down,
           final_norm):
    b, seq, d = x.shape
    n = b * seq
    nkc = seq // BLK
    assert w_in.shape[0] == 1, "one layer"
    x2 = x.reshape(n, d)

    outs = _proj(x2, attn_norm[0], _pack_w_in(w_in[0]), kv_norm[0])
    qa, ckv, qi, kw = outs[:4]
    dil = outs[4:13]
    gates = outs[13]

    dsa_bias, dil_bias = _bias_tiles(rel_bias, nkc)

    t3 = lambda a: jnp.swapaxes(a.reshape(b, seq, a.shape[-1]), 1, 2)
    kw3 = kw.reshape(b, seq, LANES)
    wiT = jnp.swapaxes(kw3[:, :, IDX_DIM:IDX_DIM + 8], 1, 2)
    kidx = kw3[:, :, :IDX_DIM].astype(BF16)
    ckv3 = ckv.reshape(b, seq, KV_LATENT)
    ckvT = jnp.swapaxes(ckv3, 1, 2)
    ya = _dsa(t3(qi), wiT, t3(qa), kidx, ckv3, ckvT, w_uk[0].astype(BF16), w_uv[0].astype(BF16), dsa_bias)

    os_, lses = [], []
    for g, (_, dilation) in enumerate(DIL_GROUPS):
        o, lse = _dilated_group(dil[g], dil[3 + g], dil[6 + g], dil_bias, g, dilation, b)
        os_.append(o)
        lses.append(lse)

    wr = jnp.concatenate([w_router_group[0],
                          jnp.swapaxes(w_router_expert[0], 0, 1).reshape(d, N_EXPERTS),
                          jnp.zeros((d, ROUTE_COLS - N_GROUPS - N_EXPERTS), F32)], axis=1)
    wr_hi = wr.astype(BF16)
    wr = jnp.concatenate([wr_hi, (wr - wr_hi.astype(F32)).astype(BF16)], axis=1)
    br = jnp.concatenate([b_router_group[0], b_router_expert[0].reshape(-1),
                          jnp.zeros((ROUTE_COLS - N_GROUPS - N_EXPERTS,), F32)]).reshape(1, ROUTE_COLS)
    h, t, rw, ri = _mix(x2, ya.reshape(n, -1), os_, lses, gates,
                        w_branch_a[0].astype(BF16), w_branch_b[0].astype(BF16), w_out[0].astype(BF16),
                        ffn_norm[0], wr, br)

    row_token, tile_expert, pos = _route_plan(ri[:, :2])
    y_sorted = _experts(tile_expert, row_token, t, w_gate[0].astype(BF16), w_up[0].astype(BF16),
                        w_down[0].astype(BF16), d)

    pos_tiles = jnp.swapaxes(pos.reshape(n // FIN_TM, FIN_TM, 2), 1, 2).reshape(n // FIN_TM, 1, 2 * FIN_TM)
    out = _final(pos_tiles, y_sorted, h, rw, final_norm)
    return out.reshape(b, seq, d)
```

```python
import functools
import math

import numpy as np
import jax
import jax.numpy as jnp
from jax import lax
from jax.experimental import pallas as pl
from jax.experimental.pallas import tpu as pltpu

F32 = jnp.float32
BF16 = jnp.bfloat16
I32 = jnp.int32

LANES = 128
VMEM_LIMIT_BYTES = 56 * 1024 * 1024

HEAD_DIM = 64
DSA_HEADS = 8
KV_LATENT = 256
IDX_HEADS = 4
IDX_DIM = 64
TOPK_MAX = 256
DIL_GROUPS = ((128, 1), (512, 4), (2048, 16))
DIL_HPG = 4
DIL_OUT = DIL_HPG * HEAD_DIM
NUM_BUCKETS = 32
MAX_DISTANCE = 2048
N_GROUPS = 4
EXPERTS_PER_GROUP = 8
N_EXPERTS = N_GROUPS * EXPERTS_PER_GROUP
RMS_EPS = 1e-6
NEG = -1e30
LOG2E = math.log2(math.e)
INT_MIN = -2 ** 31

BLK = 128
PROJ_TM = 512
MIX_TM = 256
MOE_TM = 256
FIN_TM = 256
ROUTE_COLS = 128


def _cparams(sem):
    return pltpu.CompilerParams(dimension_semantics=sem, vmem_limit_bytes=VMEM_LIMIT_BYTES)


C_QA = 0
C_KV = 512
C_QI = 768
C_KW = 1024
C_DIL = 1152
C_GATE = C_DIL + 9 * 256
C_END = C_GATE + 2048


def _proj_kernel(x_ref, g_ref, w_ref, kvg_ref, qa_ref, ckv_ref, qi_ref, kw_ref, *rest):
    dil_refs = rest[:9]
    gate_ref = rest[9]
    stage_ref = rest[10]
    x = x_ref[...]
    u = x * lax.rsqrt(jnp.mean(x * x, axis=-1, keepdims=True) + RMS_EPS) * g_ref[...]
    u = u.astype(BF16)

    def mm(a, b):
        return jnp.dot(u, w_ref[:, a:b], preferred_element_type=F32)

    qa_ref[...] = mm(C_QA, C_KV).astype(BF16)
    c = mm(C_KV, C_QI)
    c = c * lax.rsqrt(jnp.mean(c * c, axis=-1, keepdims=True) + RMS_EPS) * kvg_ref[...]
    ckv_ref[...] = c.astype(BF16)
    qi_ref[...] = mm(C_QI, C_KW).astype(BF16)
    kw_ref[...] = mm(C_KW, C_DIL)
    for j in range(9):
        val = mm(C_DIL + 256 * j, C_DIL + 256 * (j + 1))
        r = DIL_GROUPS[j % 3][1]
        if r == 1:
            dil_refs[j][...] = val.astype(BF16)
        else:
            for hf in range(2):
                stage_ref[hf] = val[:, hf * LANES:(hf + 1) * LANES]
            for rho in range(r):
                for hf in range(2):
                    dil_refs[j][:, rho * 256 + hf * LANES:rho * 256 + (hf + 1) * LANES] = (
                        stage_ref[hf, pl.ds(rho, val.shape[0] // r, stride=r), :].astype(BF16))
    for j in range(4):
        gate_ref[:, 512 * j:512 * (j + 1)] = mm(C_GATE + 512 * j, C_GATE + 512 * (j + 1)).astype(BF16)


def _proj(x2, attn_norm, w_packed, kv_norm):
    n, d = x2.shape
    tm = PROJ_TM
    row = lambda i: (i, 0)
    const = lambda i: (0, 0)
    outs = [jax.ShapeDtypeStruct((n, 512), BF16), jax.ShapeDtypeStruct((n, 256), BF16),
            jax.ShapeDtypeStruct((n, 256), BF16), jax.ShapeDtypeStruct((n, 128), F32)]
    for j in range(9):
        r = DIL_GROUPS[j % 3][1]
        outs.append(jax.ShapeDtypeStruct((n // r, r * 256), BF16))
    outs += [jax.ShapeDtypeStruct((n, 2048), BF16)]
    out_specs = [pl.BlockSpec((tm * s.shape[0] // n, s.shape[1]), row) for s in outs]
    return pl.pallas_call(
        _proj_kernel,
        out_shape=outs,
        grid=(n // tm,),
        in_specs=[pl.BlockSpec((tm, d), row), pl.BlockSpec((1, d), const),
                  pl.BlockSpec((d, C_END), const), pl.BlockSpec((1, KV_LATENT), const)],
        out_specs=out_specs,
        scratch_shapes=[pltpu.VMEM((2, tm, LANES), F32)],
        compiler_params=_cparams(("parallel",)),
        name="proj",
    )(x2, attn_norm.reshape(1, d), w_packed, kv_norm.reshape(1, KV_LATENT))


def _bucket_thresholds():
    max_exact = NUM_BUCKETS // 2
    d = np.arange(0, MAX_DISTANCE + 1)
    nf = np.maximum(d, 1).astype(np.float32)
    large = max_exact + (np.log(nf / np.float32(max_exact)) / np.float32(math.log(MAX_DISTANCE / max_exact))
                         * np.float32(NUM_BUCKETS - max_exact)).astype(np.int32)
    large = np.minimum(large, NUM_BUCKETS - 1)
    bucket = np.where(d < max_exact, d, large)
    assert np.all(np.diff(bucket) >= 0)
    return [int(np.argmax(bucket >= b)) for b in range(1, NUM_BUCKETS)]


_BUCKET_THR = _bucket_thresholds()


def _bias_from_distance(dist, tab_ref, heads):
    masks = [dist >= t for t in _BUCKET_THR]
    out = []
    for h in heads:
        v = jnp.full(dist.shape, tab_ref[0, h], F32)
        for b in range(1, NUM_BUCKETS):
            v = jnp.where(masks[b - 1], tab_ref[b, h], v)
        out.append(v)
    return out


def _dsa_bias_kernel(tab_ref, o_ref):
    delta = pl.program_id(0)
    j = lax.broadcasted_iota(I32, (BLK, BLK), 0)
    i = lax.broadcasted_iota(I32, (BLK, BLK), 1)
    dist = jnp.maximum(delta * BLK + i - j, 0)
    tiles = _bias_from_distance(dist, tab_ref, range(DSA_HEADS))
    for h in range(DSA_HEADS):
        o_ref[0, h] = tiles[h] * LOG2E


def _dil_bias_kernel(tab_ref, o_ref, *, dilations):
    g = pl.program_id(0)
    j = lax.broadcasted_iota(I32, (2 * BLK, BLK), 0)
    i = lax.broadcasted_iota(I32, (2 * BLK, BLK), 1)
    step = i + BLK - j
    valid = (step >= 0) & (step <= BLK)
    for gi, r in enumerate(dilations):
        @pl.when(g == gi)
        def _():
            dist = jnp.maximum(step, 0) * r
            heads = [DSA_HEADS + gi * DIL_HPG + hh for hh in range(DIL_HPG)]
            tiles = _bias_from_distance(dist, tab_ref, heads)
            for hh in range(DIL_HPG):
                o_ref[0, :, hh * BLK:(hh + 1) * BLK] = jnp.where(valid, tiles[hh], NEG)


def _bias_tiles(rel_bias, nkc):
    smem = pl.BlockSpec(memory_space=pltpu.SMEM)
    dsa = pl.pallas_call(
        _dsa_bias_kernel,
        out_shape=jax.ShapeDtypeStruct((nkc, DSA_HEADS, BLK, BLK), F32),
        grid=(nkc,),
        in_specs=[smem],
        out_specs=pl.BlockSpec((1, DSA_HEADS, BLK, BLK), lambda d: (d, 0, 0, 0)),
        compiler_params=_cparams(("parallel",)),
        name="dsa_bias",
    )(rel_bias)
    dil = pl.pallas_call(
        functools.partial(_dil_bias_kernel, dilations=tuple(r for _, r in DIL_GROUPS)),
        out_shape=jax.ShapeDtypeStruct((len(DIL_GROUPS), 2 * BLK, DIL_HPG * BLK), F32),
        grid=(len(DIL_GROUPS),),
        in_specs=[smem],
        out_specs=pl.BlockSpec((1, 2 * BLK, DIL_HPG * BLK), lambda g: (g, 0, 0)),
        compiler_params=_cparams(("parallel",)),
        name="dil_bias",
    )(rel_bias)
    return dsa, dil


SUP = 4
SROWS = SUP * BLK


PLANE_KEYS = 32 * 8


def _bit_planes(words):
    x = list(words)
    j, m = 16, 0x0000FFFF
    while j:
        k = 0
        while k < 32:
            t = (x[k] ^ lax.shift_right_logical(x[k + j], jnp.int32(j))) & m
            x[k] = x[k] ^ t
            x[k + j] = x[k + j] ^ jnp.left_shift(t, jnp.int32(j))
            k = (k + j + 1) & ~j
        j >>= 1
        m = (m ^ (m << j)) & 0xFFFFFFFF
        m = m - (1 << 32) if m >= (1 << 31) else m
    return x


def _dsa_kernel(qiT_ref, wiT_ref, qaT_ref, kidx_ref, ckv_ref, ckvT_ref, wuk_ref, wuv_ref, bias_ref, tri_ref,
                y_ref, sc_ref, planes_ref, qlT_ref, x_ref, pT_ref, *, topk, nsc, qb0):
    qb = qb0 + pl.program_id(1)
    row = lax.broadcasted_iota(I32, (SROWS, BLK), 0)
    col = lax.broadcasted_iota(I32, (SROWS, BLK), 1)
    trips = [slice(sc * SROWS, (sc + 1) * SROWS) for sc in range(nsc)]

    def causal(sc):
        return row <= col + (qb * BLK - sc * SROWS)

    for h in range(DSA_HEADS):
        ql = jnp.dot(wuk_ref[h], qaT_ref[0, h * HEAD_DIM:(h + 1) * HEAD_DIM, :],
                     preferred_element_type=F32) * (HEAD_DIM ** -0.5 * LOG2E)
        qlT_ref[:, h * BLK:(h + 1) * BLK] = ql.astype(BF16)

    wq = wiT_ref[0] * (IDX_HEADS ** -0.5)
    for sc, ts in enumerate(trips):
        kx = kidx_ref[0, ts, :]
        acc = jnp.zeros((SROWS, BLK), F32)
        for h in range(IDX_HEADS):
            s = jnp.dot(kx, qiT_ref[0, h * IDX_DIM:(h + 1) * IDX_DIM, :],
                        preferred_element_type=F32) * (IDX_DIM ** -0.5)
            acc = acc + wq[h:h + 1, :] * jnp.maximum(s, 0.0)
        acc = jnp.where(causal(sc), acc, NEG)
        sc_ref[ts, :] = acc
        bits = pltpu.bitcast(acc, I32)
        bits = jnp.where(bits == INT_MIN, 0, bits)
        ukey = bits ^ ((bits >> 31) & 0x7FFFFFFF) ^ INT_MIN
        for grp in range(SROWS // PLANE_KEYS):
            tiles = [ukey[grp * PLANE_KEYS + j * 8:grp * PLANE_KEYS + (j + 1) * 8] for j in range(32)]
            for b, plane in enumerate(_bit_planes(tiles)):
                planes_ref[b, sc * (SROWS // PLANE_KEYS) + grp] = plane

    def count(pred):
        cnt = jnp.zeros((8, BLK), I32)
        for ts in trips:
            cnt = cnt + jnp.sum(jnp.where(pred(sc_ref[ts, :]), 1, 0).reshape(SROWS // 8, 8, BLK), axis=0)
        return jnp.sum(cnt, axis=0, keepdims=True)

    def as_float(key):
        return pltpu.bitcast(key ^ ((key >> 31) & 0x7FFFFFFF), F32)

    def bit_body(it, carry):
        alive, above, code = carry
        ones = alive & planes_ref[it]
        c = jnp.sum(jnp.sum(lax.population_count(ones), axis=0), axis=0, keepdims=True)
        take = above + c >= topk
        alive = jnp.where(take, ones, alive ^ ones)
        above = jnp.where(take, above, above + c)
        code = code | jnp.where(take, jnp.left_shift(jnp.int32(1), 31 - it), 0)
        return alive, above, code

    nw = nsc * (SROWS // PLANE_KEYS)
    _, _, code = lax.fori_loop(
        0, 32, bit_body,
        (jnp.full((nw, 8, BLK), -1, I32), jnp.zeros((1, BLK), I32), jnp.zeros((1, BLK), I32)))
    guess = as_float(code ^ INT_MIN)
    n_gt_guess = count(lambda v: v > guess)
    proven = (n_gt_guess < topk) & (count(lambda v: v >= guess) >= topk)

    def bisect():
        def thr_body(it, lo):
            cand = lo + jnp.left_shift(jnp.int32(1), 31 - it)
            cand_f = as_float(cand)
            return jnp.where(count(lambda v: v >= cand_f) >= topk, cand, lo)
        t = as_float(lax.fori_loop(0, 32, thr_body, jnp.full((1, BLK), INT_MIN, I32)))
        return t, count(lambda v: v > t)

    thr, n_gt = lax.cond(jnp.min(jnp.where(proven, 1, 0)) > 0, lambda: (guess, n_gt_guess), bisect)
    ties_wanted = (topk - n_gt).astype(F32)

    m = [jnp.full((8, BLK), NEG, F32) for _ in range(DSA_HEADS)]
    ties_before = jnp.zeros((1, BLK), F32)
    for sc, ts in enumerate(trips):
        k = sc_ref[ts, :]
        tie = k == thr
        tie_rank = jnp.dot(tri_ref[...], jnp.where(tie, 1.0, 0.0).astype(BF16),
                           preferred_element_type=F32) + ties_before
        ties_before = tie_rank[SROWS - 1:SROWS, :]
        sel = ((k > thr) | (tie & (tie_rank <= ties_wanted))) & causal(sc)
        am = jnp.where(sel, 0.0, NEG)
        ck = ckv_ref[0, ts, :]
        for hp in range(DSA_HEADS // 2):
            lg2 = jnp.dot(ck, qlT_ref[:, 2 * hp * BLK:(2 * hp + 2) * BLK], preferred_element_type=F32)
            for h in (2 * hp, 2 * hp + 1):
                lg = lg2[:, (h % 2) * BLK:(h % 2 + 1) * BLK]
                for j in range(SUP):
                    rs = slice(j * BLK, (j + 1) * BLK)
                    delta = jnp.maximum(qb - (sc * SUP + j), 0)
                    x = lg[rs] + bias_ref[delta, h] + am[rs]
                    x_ref[sc * SROWS + j * BLK:sc * SROWS + (j + 1) * BLK, h * BLK:(h + 1) * BLK] = x
                    m[h] = jnp.maximum(m[h], jnp.max(x.reshape(BLK // 8, 8, BLK), axis=0))
    m = [jnp.max(v, axis=0, keepdims=True) for v in m]

    l = [jnp.zeros((8, BLK), F32) for _ in range(DSA_HEADS)]
    for sc, ts in enumerate(trips):
        for h in range(DSA_HEADS):
            hs = slice(h * BLK, (h + 1) * BLK)
            p = jnp.exp2(x_ref[ts, hs] - m[h])
            pT_ref[ts, hs] = p.astype(BF16)
            l[h] = l[h] + jnp.sum(p.reshape(SROWS // 8, 8, BLK), axis=0)

    for hp in range(DSA_HEADS // 2):
        o2 = jnp.dot(ckvT_ref[0], pT_ref[:, 2 * hp * BLK:(2 * hp + 2) * BLK],
                     preferred_element_type=F32)
        for h in (2 * hp, 2 * hp + 1):
            inv = 1.0 / jnp.sum(l[h], axis=0, keepdims=True)
            oh = (o2[:, (h % 2) * BLK:(h % 2 + 1) * BLK] * inv).T.astype(BF16)
            yh = jnp.dot(oh, wuv_ref[h], preferred_element_type=F32)
            y_ref[0, :, h * HEAD_DIM:(h + 1) * HEAD_DIM] = yh.astype(BF16)


def _dsa_group(g, qiT, wiT, qaT, kidx, ckv, ckvT, wuk, wuv, bias_tiles, tri):
    b, seq, _ = ckv.shape
    nkc = seq // BLK
    nsc = g + 1
    nk = nsc * SROWS
    topk = min(TOPK_MAX, seq // 4)
    qblk = lambda rows: pl.BlockSpec((1, rows, BLK), lambda bi, qi: (bi, 0, g * SUP + qi))
    head3 = lambda cols: pl.BlockSpec((1, nk, cols), lambda bi, qi: (bi, 0, 0))
    const = lambda s: pl.BlockSpec(s, lambda bi, qi: (0,) * len(s))
    hl = DSA_HEADS * BLK
    return pl.pallas_call(
        functools.partial(_dsa_kernel, topk=topk, nsc=nsc, qb0=g * SUP),
        out_shape=jax.ShapeDtypeStruct((b, SROWS, DSA_HEADS * HEAD_DIM), BF16),
        grid=(b, SUP),
        in_specs=[qblk(IDX_HEADS * IDX_DIM), qblk(8), qblk(DSA_HEADS * HEAD_DIM),
                  head3(IDX_DIM), head3(KV_LATENT),
                  pl.BlockSpec((1, KV_LATENT, nk), lambda bi, qi: (bi, 0, 0)),
                  const((DSA_HEADS, KV_LATENT, HEAD_DIM)), const((DSA_HEADS, KV_LATENT, HEAD_DIM)),
                  const((nkc, DSA_HEADS, BLK, BLK)), const((SROWS, SROWS))],
        out_specs=pl.BlockSpec((1, BLK, DSA_HEADS * HEAD_DIM), lambda bi, qi: (bi, qi, 0)),
        scratch_shapes=[pltpu.VMEM((nk, BLK), F32),
                        pltpu.VMEM((32, nk // PLANE_KEYS, 8, BLK), I32),
                        pltpu.VMEM((KV_LATENT, hl), BF16),
                        pltpu.VMEM((nk, hl), F32),
                        pltpu.VMEM((nk, hl), BF16)],
        compiler_params=_cparams(("parallel", "arbitrary")),
        name=f"dsa_g{g}",
    )(qiT, wiT, qaT, kidx, ckv, ckvT, wuk, wuv, bias_tiles, tri)


def _dsa(qiT, wiT, qaT, kidx, ckv, ckvT, wuk, wuv, bias_tiles):
    seq = ckv.shape[1]
    assert seq % SROWS == 0 and seq >= 4 * TOPK_MAX
    tri = jnp.tril(jnp.ones((SROWS, SROWS), BF16))
    groups = [_dsa_group(g, qiT, wiT, qaT, kidx, ckv, ckvT, wuk, wuv, bias_tiles, tri)
              for g in range(seq // SROWS)]
    return jnp.concatenate(groups, axis=1)


DIL_UNROLL = 5
DIL_STEP_ROWS = 1024


def _dil_kernel(q_ref, k_ref, v_ref, bm_ref, o_ref, lse_ref, vT_ref, *, nblk, nres):
    hq = DIL_HPG * BLK
    rowh = lax.broadcasted_iota(I32, (hq, DIL_OUT), 0) // BLK
    colh = lax.broadcasted_iota(I32, (hq, DIL_OUT), 1) // HEAD_DIM
    same_head = rowh == colh

    for res in range(nres):
        cs = slice(res * DIL_OUT, (res + 1) * DIL_OUT)

        for n in range(nblk):
            vT_ref[n] = v_ref[0, n * BLK:(n + 1) * BLK, cs].astype(F32).T.astype(BF16)

        def block(qo, kw, vT, bm, cs=cs):
            q = q_ref[0, pl.ds(qo, BLK), cs]
            qd = jnp.where(same_head, jnp.concatenate([q] * DIL_HPG, axis=0), jnp.zeros((), BF16))
            s = lax.dot_general(kw, qd, (((1,), (1,)), ((), ())), preferred_element_type=F32)
            s = s * (HEAD_DIM ** -0.5) + bm
            m = jnp.max(s, axis=0, keepdims=True)
            e = jnp.exp(s - m)
            l = jnp.sum(e, axis=0, keepdims=True)
            oT = jnp.dot(vT, e.astype(BF16), preferred_element_type=F32)
            inv = 1.0 / l
            lse = m + jnp.log(l)
            outs, lses = [], []
            for hh in range(DIL_HPG):
                qs = slice(hh * BLK, (hh + 1) * BLK)
                outs.append(oT[hh * HEAD_DIM:(hh + 1) * HEAD_DIM, qs] * inv[:, qs])
                lses.append(jnp.broadcast_to(lse[:, qs], (HEAD_DIM, BLK)))
            o_ref[0, pl.ds(qo, BLK), cs] = jnp.concatenate(outs, axis=0).T.astype(BF16)
            lse_ref[0, pl.ds(qo, BLK), cs] = jnp.concatenate(lses, axis=0).T

        block(0, k_ref[0, 0:BLK, cs], vT_ref[0], bm_ref[0, BLK:, :])

        def body(n, carry, cs=cs, block=block):
            ko = pl.multiple_of((n - 1) * BLK, BLK)
            vT = jnp.concatenate([vT_ref[n - 1], vT_ref[n]], axis=1)
            block(pl.multiple_of(n * BLK, BLK), k_ref[0, pl.ds(ko, 2 * BLK), cs], vT, bm_ref[0])
            return carry

        lax.fori_loop(1, nblk, body, 0, unroll=DIL_UNROLL)


def _dilated_group(q, k, v, bm, g, dilation, b):
    c = DIL_OUT
    ls = q.shape[0] // b
    nblk = ls // BLK
    nres = max(1, min(dilation, DIL_STEP_ROWS // ls))
    view = lambda a: a.reshape(b, ls, dilation * c)
    blk = pl.BlockSpec((1, ls, nres * c), lambda bi, ri: (bi, 0, ri))
    o, lse = pl.pallas_call(
        functools.partial(_dil_kernel, nblk=nblk, nres=nres),
        out_shape=[jax.ShapeDtypeStruct((b, ls, dilation * c), BF16),
                   jax.ShapeDtypeStruct((b, ls, dilation * c), F32)],
        grid=(b, dilation // nres),
        in_specs=[blk, blk, blk, pl.BlockSpec((1, 2 * BLK, DIL_HPG * BLK), lambda bi, ri: (g, 0, 0))],
        out_specs=[blk, blk],
        scratch_shapes=[pltpu.VMEM((nblk, c, BLK), BF16)],
        compiler_params=_cparams(("parallel", "parallel")),
        name=f"dilated_g{g}",
    )(view(q), view(k), view(v), bm)
    return o.reshape(b * ls, dilation * c), lse.reshape(b * ls, dilation * c)


def _store_row_tiles(ref, base, val):
    rows, d = val.shape
    dt = d // LANES
    for s in range(dt):
        ref[pl.ds(base * dt + s, rows, stride=dt), :] = val[:, s * LANES:(s + 1) * LANES]


def _load_row_tiles(ref, base, rows, dt):
    return jnp.concatenate([ref[pl.ds(base * dt + s, rows, stride=dt), :] for s in range(dt)], axis=1)


MIX_COLS = 256


def _token_major(ref, stage_ref, r):
    if r == 1:
        return ref[...].astype(F32)
    rows = ref.shape[0]
    for rho in range(r):
        for hf in range(DIL_OUT // LANES):
            c0 = rho * DIL_OUT + hf * LANES
            stage_ref[hf, pl.ds(rho, rows, stride=r), :] = ref[:, c0:c0 + LANES].astype(F32)
    return jnp.concatenate([stage_ref[hf] for hf in range(DIL_OUT // LANES)], axis=1)


def _sigmoid(v):
    return 0.5 * jnp.tanh(0.5 * v) + 0.5


def _mix_kernel(x_ref, ya_ref, o1_ref, o2_ref, o3_ref, l1_ref, l2_ref, l3_ref, gate_ref,
                wa_ref, wb_ref, wo_ref, fg_ref, wr_ref, br_ref,
                h_ref, t_ref, rw_ref, ri_ref, mixed_ref, *stage_refs):
    dils = [r for _, r in DIL_GROUPS]
    o = [_token_major(ref, st, r) for ref, st, r in zip((o1_ref, o2_ref, o3_ref), stage_refs[:3], dils)]
    l1, l2, l3 = [_token_major(ref, st, r) for ref, st, r in zip((l1_ref, l2_ref, l3_ref), stage_refs[3:], dils)]
    mx = jnp.maximum(jnp.maximum(l1, l2), l3)
    e1, e2, e3 = jnp.exp(l1 - mx), jnp.exp(l2 - mx), jnp.exp(l3 - mx)
    inv = 1.0 / (e1 + e2 + e3)
    yb = ((e1 * inv) * o[0] + (e2 * inv) * o[1] + (e3 * inv) * o[2]).astype(BF16)
    ya = ya_ref[...]
    d = x_ref.shape[1]
    for c in range(0, d, MIX_COLS):
        cs = slice(c, c + MIX_COLS)
        a = jnp.dot(ya, wa_ref[:, cs], preferred_element_type=F32)
        bmix = jnp.dot(yb, wb_ref[:, cs], preferred_element_type=F32)
        g0 = _sigmoid(gate_ref[:, cs].astype(F32))
        g1 = _sigmoid(gate_ref[:, d + c:d + c + MIX_COLS].astype(F32))
        mixed_ref[:, cs] = (g0 * a + g1 * bmix).astype(BF16)
    h = x_ref[...] + jnp.dot(mixed_ref[...], wo_ref[...], preferred_element_type=F32)
    h_ref[...] = h
    t = h * lax.rsqrt(jnp.mean(h * h, axis=-1, keepdims=True) + RMS_EPS) * fg_ref[...]
    _store_row_tiles(t_ref, 0, t)

    t_hi = t.astype(BF16)
    t_lo = (t - t_hi.astype(F32)).astype(BF16)
    r1 = jnp.dot(t_hi, wr_ref[...], preferred_element_type=F32)
    r2 = jnp.dot(t_lo, wr_ref[:, :ROUTE_COLS], preferred_element_type=F32)
    logits = r1[:, :ROUTE_COLS] + (r1[:, ROUTE_COLS:] + r2) + br_ref[...]
    lane = lax.broadcasted_iota(I32, logits.shape, 1)
    ninf = -jnp.inf
    big = jnp.int32(10 ** 6)

    def first_argmax(v, vmax):
        return jnp.min(jnp.where(v == vmax, lane, big), axis=-1, keepdims=True)

    gl = jnp.where(lane < N_GROUPS, logits, ninf)
    gmax = jnp.max(gl, axis=-1, keepdims=True)
    gsel = first_argmax(gl, gmax)
    p_g = 1.0 / jnp.sum(jnp.exp(gl - gmax), axis=-1, keepdims=True)
    lo = N_GROUPS + gsel * EXPERTS_PER_GROUP
    el = jnp.where((lane >= lo) & (lane < lo + EXPERTS_PER_GROUP), logits, ninf)
    v1 = jnp.max(el, axis=-1, keepdims=True)
    i1 = first_argmax(el, v1)
    el2 = jnp.where(lane == i1, ninf, el)
    v2 = jnp.max(el2, axis=-1, keepdims=True)
    i2 = first_argmax(el2, v2)
    e2 = jnp.exp(v2 - v1)
    w1 = p_g / (1.0 + e2)
    w2 = p_g * e2 / (1.0 + e2)
    rw_ref[...] = jnp.where(lane == 0, w1, jnp.where(lane == 1, w2, 0.0))
    ri_ref[...] = jnp.where(lane == 0, i1 - N_GROUPS, jnp.where(lane == 1, i2 - N_GROUPS, 0))


def _mix(x2, ya, os_, lses, gates, wa, wb, wo, ffn_norm, wr, br):
    n, d = x2.shape
    tm = MIX_TM
    row = lambda c: pl.BlockSpec((tm, c), lambda i: (i, 0))
    res = lambda a: pl.BlockSpec((tm * a.shape[0] // n, a.shape[1]), lambda i: (i, 0))
    const = lambda s: pl.BlockSpec(s, lambda i: (0, 0))
    return pl.pallas_call(
        _mix_kernel,
        out_shape=[jax.ShapeDtypeStruct((n, d), F32), jax.ShapeDtypeStruct((n * (d // LANES), LANES), F32),
                   jax.ShapeDtypeStruct((n, ROUTE_COLS), F32), jax.ShapeDtypeStruct((n, ROUTE_COLS), I32)],
        grid=(n // tm,),
        in_specs=[row(d), row(512)] + [res(a) for a in os_] + [res(a) for a in lses] + [row(2 * d),
                  const(wa.shape), const(wb.shape), const(wo.shape), const((1, d)),
                  const(wr.shape), const((1, ROUTE_COLS))],
        out_specs=[row(d), pl.BlockSpec((tm * (d // LANES), LANES), lambda i: (i, 0)),
                   row(ROUTE_COLS), row(ROUTE_COLS)],
        scratch_shapes=[pltpu.VMEM((tm, d), BF16)] + [pltpu.VMEM((DIL_OUT // LANES, tm, LANES), F32)] * 6,
        compiler_params=_cparams(("parallel",)),
        name="mix",
    )(x2, ya, *os_, *lses, gates, wa, wb, wo, ffn_norm.reshape(1, d), wr, br)


GATHER_UNROLL = 8


def _start_row_gather(src_hbm, idx_ref, nrows, dt, buf, sem, slot):
    def body(g, c):
        for u in range(GATHER_UNROLL):
            r = g * GATHER_UNROLL + u
            src = pl.multiple_of(idx_ref[0, 0, r] * dt, dt)
            dst = pl.multiple_of((slot * nrows + r) * dt, dt)
            pltpu.make_async_copy(src_hbm.at[pl.ds(src, dt)], buf.at[pl.ds(dst, dt)],
                                  sem.at[slot]).start(priority=u % 2)
        return c
    lax.fori_loop(0, nrows // GATHER_UNROLL, body, 0)


def _wait_row_gather(src_hbm, nrows, dt, buf, sem, slot):
    dst = pl.multiple_of(slot * nrows * dt, dt)
    pltpu.make_async_copy(src_hbm.at[pl.ds(0, nrows * dt)], buf.at[pl.ds(dst, nrows * dt)], sem.at[slot]).wait()


def _gather_pipeline(i, nsteps, src_hbm, cur_ref, nxt_ref, nrows, dt, buf, sem):
    slot = i % 2

    @pl.when(i == 0)
    def _():
        _start_row_gather(src_hbm, cur_ref, nrows, dt, buf, sem, 0)

    @pl.when(i + 1 < nsteps)
    def _():
        _start_row_gather(src_hbm, nxt_ref, nrows, dt, buf, sem, 1 - slot)

    _wait_row_gather(src_hbm, nrows, dt, buf, sem, slot)
    return slot


def _expert_kernel(te_ref, cur_ref, nxt_ref, t_hbm, wg_ref, wu_ref, wd_ref, y_ref, buf, sem):
    i = pl.program_id(0)
    dt = wg_ref.shape[1] // LANES
    slot = _gather_pipeline(i, pl.num_programs(0), t_hbm, cur_ref, nxt_ref, MOE_TM, dt, buf, sem)
    xt = _load_row_tiles(buf, slot * MOE_TM, MOE_TM, dt).astype(BF16)
    hg = jnp.dot(xt, wg_ref[0], preferred_element_type=F32)
    hu = jnp.dot(xt, wu_ref[0], preferred_element_type=F32)
    hid = (hg * jax.nn.sigmoid(hg)) * hu
    _store_row_tiles(y_ref, 0, jnp.dot(hid.astype(BF16), wd_ref[0], preferred_element_type=F32))


def _experts(tile_expert, row_token, t, wg, wu, wd, d):
    dt = d // LANES
    ntiles = tile_expert.shape[0]
    ff = wg.shape[2]
    tok3 = row_token.reshape(ntiles, 1, MOE_TM)
    smem_cur = pl.BlockSpec((1, 1, MOE_TM), lambda i, te: (i, 0, 0), memory_space=pltpu.SMEM)
    smem_nxt = pl.BlockSpec((1, 1, MOE_TM), lambda i, te: (jnp.minimum(i + 1, ntiles - 1), 0, 0),
                            memory_space=pltpu.SMEM)
    wspec = lambda s: pl.BlockSpec((1,) + s, lambda i, te: (te[i], 0, 0))
    return pl.pallas_call(
        _expert_kernel,
        out_shape=jax.ShapeDtypeStruct((ntiles * MOE_TM * dt, LANES), F32),
        grid_spec=pltpu.PrefetchScalarGridSpec(
            num_scalar_prefetch=1,
            grid=(ntiles,),
            in_specs=[smem_cur, smem_nxt, pl.BlockSpec(memory_space=pl.ANY),
                      wspec((d, ff)), wspec((d, ff)), wspec((ff, d))],
            out_specs=pl.BlockSpec((MOE_TM * dt, LANES), lambda i, te: (i, 0)),
            scratch_shapes=[pltpu.VMEM((2 * MOE_TM * dt, LANES), F32), pltpu.SemaphoreType.DMA((2,))],
        ),
        compiler_params=_cparams(("arbitrary",)),
        name="experts",
    )(tile_expert, tok3, tok3, t, wg, wu, wd)


def _final_kernel(cur_ref, nxt_ref, y_hbm, h_ref, rw_ref, fn_ref, o_ref, buf, sem):
    i = pl.program_id(0)
    dt = h_ref.shape[1] // LANES
    slot = _gather_pipeline(i, pl.num_programs(0), y_hbm, cur_ref, nxt_ref, 2 * FIN_TM, dt, buf, sem)
    rw = rw_ref[...]
    y0 = _load_row_tiles(buf, slot * 2 * FIN_TM, FIN_TM, dt)
    y1 = _load_row_tiles(buf, slot * 2 * FIN_TM + FIN_TM, FIN_TM, dt)
    h = h_ref[...] + rw[:, 0:1] * y0 + rw[:, 1:2] * y1
    o_ref[...] = h * lax.rsqrt(jnp.mean(h * h, axis=-1, keepdims=True) + RMS_EPS) * fn_ref[...]


def _final(pos_tiles, y_sorted, h, rw, final_norm):
    n, d = h.shape
    tm = FIN_TM
    nt = n // tm
    smem_cur = pl.BlockSpec((1, 1, 2 * tm), lambda i: (i, 0, 0), memory_space=pltpu.SMEM)
    smem_nxt = pl.BlockSpec((1, 1, 2 * tm), lambda i: (jnp.minimum(i + 1, nt - 1), 0, 0),
                            memory_space=pltpu.SMEM)
    return pl.pallas_call(
        _final_kernel,
        out_shape=jax.ShapeDtypeStruct((n, d), F32),
        grid=(nt,),
        in_specs=[smem_cur, smem_nxt, pl.BlockSpec(memory_space=pl.ANY),
                  pl.BlockSpec((tm, d), lambda i: (i, 0)), pl.BlockSpec((tm, ROUTE_COLS), lambda i: (i, 0)),
                  pl.BlockSpec((1, d), lambda i: (0, 0))],
        out_specs=pl.BlockSpec((tm, d), lambda i: (i, 0)),
        scratch_shapes=[pltpu.VMEM((2 * 2 * tm * (d // LANES), LANES), F32), pltpu.SemaphoreType.DMA((2,))],
        compiler_params=_cparams(("arbitrary",)),
        name="final",
    )(pos_tiles, pos_tiles, y_sorted, h, rw, final_norm.reshape(1, d))


def _route_plan(gid):
    n = gid.shape[0]
    e = gid.reshape(-1)
    onehot = (e[:, None] == jnp.arange(N_EXPERTS, dtype=I32)[None, :]).astype(F32)
    chunk = 256
    oh3 = onehot.reshape(-1, chunk, N_EXPERTS)
    within = jnp.einsum("ij,tjk->tik", jnp.tril(jnp.ones((chunk, chunk), F32)), oh3)
    totals = within[:, -1, :]
    before = jnp.cumsum(totals, axis=0) - totals
    csum = (within + before[:, None, :]).reshape(-1, N_EXPERTS)
    rank = jnp.sum(csum * onehot, axis=1).astype(I32) - 1
    counts = (before[-1] + totals[-1]).astype(I32)
    padded = ((counts + MOE_TM - 1) // MOE_TM) * MOE_TM
    seg_end = jnp.cumsum(padded)
    pos = (seg_end - padded)[e] + rank
    nrows = 2 * n + N_EXPERTS * MOE_TM
    tile_start = jnp.arange(nrows // MOE_TM, dtype=I32) * MOE_TM
    tile_expert = jnp.minimum(jnp.sum(tile_start[:, None] >= seg_end[None, :], axis=1), N_EXPERTS - 1).astype(I32)
    order = jnp.argsort(e, stable=True).astype(I32)
    first = jnp.cumsum(counts) - counts
    row_expert = jnp.repeat(tile_expert, MOE_TM)
    src = jnp.arange(nrows, dtype=I32) - (seg_end - padded)[row_expert] + first[row_expert]
    row_token = order[jnp.clip(src, 0, 2 * n - 1)] // 2
    return row_token, tile_expert, pos.reshape(n, 2)


def _pack_w_in(w):
    d = w.shape[0]
    o_kv, o_qi, o_ki, o_wi = 512, 768, 1024, 1088
    o_dil = o_wi + IDX_HEADS
    o_gate = o_dil + 9 * 256
    pad = jnp.zeros((d, LANES - IDX_DIM - IDX_HEADS), w.dtype)
    packed = jnp.concatenate([w[:, :o_ki], w[:, o_ki:o_wi], w[:, o_wi:o_dil], pad, w[:, o_dil:o_gate],
                              w[:, o_gate:]], axis=1)
    assert packed.shape[1] == C_END
    return packed.astype(BF16)


def kernel(x, attn_norm, w_in, kv_norm, w_uk, w_uv, rel_bias, w_branch_a, w_branch_b, w_out, ffn_norm,
           w_router_group, b_router_group, w_router_expert, b_router_expert, w_gate, w_up, w_down,
           final_norm):
    b, seq, d = x.shape
    n = b * seq
    nkc = seq // BLK
    assert w_in.shape[0] == 1, "one layer"
    x2 = x.reshape(n, d)

    outs = _proj(x2, attn_norm[0], _pack_w_in(w_in[0]), kv_norm[0])
    qa, ckv, qi, kw = outs[:4]
    dil = outs[4:13]
    gates = outs[13]

    dsa_bias, dil_bias = _bias_tiles(rel_bias, nkc)

    t3 = lambda a: jnp.swapaxes(a.reshape(b, seq, a.shape[-1]), 1, 2)
    kw3 = kw.reshape(b, seq, LANES)
    wiT = jnp.swapaxes(kw3[:, :, IDX_DIM:IDX_DIM + 8], 1, 2)
    kidx = kw3[:, :, :IDX_DIM].astype(BF16)
    ckv3 = ckv.reshape(b, seq, KV_LATENT)
    ckvT = jnp.swapaxes(ckv3, 1, 2)
    ya = _dsa(t3(qi), wiT, t3(qa), kidx, ckv3, ckvT, w_uk[0].astype(BF16), w_uv[0].astype(BF16), dsa_bias)

    os_, lses = [], []
    for g, (_, dilation) in enumerate(DIL_GROUPS):
        o, lse = _dilated_group(dil[g], dil[3 + g], dil[6 + g], dil_bias, g, dilation, b)
        os_.append(o)
        lses.append(lse)

    wr = jnp.concatenate([w_router_group[0],
                          jnp.swapaxes(w_router_expert[0], 0, 1).reshape(d, N_EXPERTS),
                          jnp.zeros((d, ROUTE_COLS - N_GROUPS - N_EXPERTS), F32)], axis=1)
    wr_hi = wr.astype(BF16)
    wr = jnp.concatenate([wr_hi, (wr - wr_hi.astype(F32)).astype(BF16)], axis=1)
    br = jnp.concatenate([b_router_group[0], b_router_expert[0].reshape(-1),
                          jnp.zeros((ROUTE_COLS - N_GROUPS - N_EXPERTS,), F32)]).reshape(1, ROUTE_COLS)
    h, t, rw, ri = _mix(x2, ya.reshape(n, -1), os_, lses, gates,
                        w_branch_a[0].astype(BF16), w_branch_b[0].astype(BF16), w_out[0].astype(BF16),
                        ffn_norm[0], wr, br)

    row_token, tile_expert, pos = _route_plan(ri[:, :2])
    y_sorted = _experts(tile_expert, row_token, t, w_gate[0].astype(BF16), w_up[0].astype(BF16),
                        w_down[0].astype(BF16), d)

    pos_tiles = jnp.swapaxes(pos.reshape(n // FIN_TM, FIN_TM, 2), 1, 2).reshape(n // FIN_TM, 1, 2 * FIN_TM)
    out = _final(pos_tiles, y_sorted, h, rw, final_norm)
    return out.reshape(b, seq, d)
```

```python
import functools
import math

import numpy as np
import jax
import jax.numpy as jnp
from jax import lax
from jax.experimental import pallas as pl
from jax.experimental.pallas import tpu as pltpu

F32 = jnp.float32
BF16 = jnp.bfloat16
I32 = jnp.int32

LANES = 128
VMEM_LIMIT_BYTES = 56 * 1024 * 1024

HEAD_DIM = 64
DSA_HEADS = 8
KV_LATENT = 256
IDX_HEADS = 4
IDX_DIM = 64
TOPK_MAX = 256
DIL_GROUPS = ((128, 1), (512, 4), (2048, 16))
DIL_HPG = 4
DIL_OUT = DIL_HPG * HEAD_DIM
NUM_BUCKETS = 32
MAX_DISTANCE = 2048
N_GROUPS = 4
EXPERTS_PER_GROUP = 8
N_EXPERTS = N_GROUPS * EXPERTS_PER_GROUP
RMS_EPS = 1e-6
NEG = -1e30
LOG2E = math.log2(math.e)
INT_MIN = -2 ** 31

BLK = 128
PROJ_TM = 512
MIX_TM = 512
MOE_TM = 256
FIN_TM = 256
ROUTE_COLS = 128


def _cparams(sem):
    return pltpu.CompilerParams(dimension_semantics=sem, vmem_limit_bytes=VMEM_LIMIT_BYTES)


C_QA = 0
C_KV = 512
C_QI = 768
C_KW = 1024
C_DIL = 1152
C_GATE = C_DIL + 9 * 256
C_END = C_GATE + 2048


def _proj_kernel(x_ref, g_ref, w_ref, kvg_ref, qa_ref, ckv_ref, qi_ref, kw_ref, *rest):
    dil_refs = rest[:9]
    gate_ref = rest[9]
    stage_ref = rest[10]
    x = x_ref[...]
    u = x * lax.rsqrt(jnp.mean(x * x, axis=-1, keepdims=True) + RMS_EPS) * g_ref[...]
    u = u.astype(BF16)

    def mm(a, b):
        return jnp.dot(u, w_ref[:, a:b], preferred_element_type=F32)

    qa_ref[...] = mm(C_QA, C_KV).astype(BF16)
    c = mm(C_KV, C_QI)
    c = c * lax.rsqrt(jnp.mean(c * c, axis=-1, keepdims=True) + RMS_EPS) * kvg_ref[...]
    ckv_ref[...] = c.astype(BF16)
    qi_ref[...] = mm(C_QI, C_KW).astype(BF16)
    kw_ref[...] = mm(C_KW, C_DIL)
    for j in range(9):
        val = mm(C_DIL + 256 * j, C_DIL + 256 * (j + 1))
        r = DIL_GROUPS[j % 3][1]
        if r == 1:
            dil_refs[j][...] = val.astype(BF16)
        else:
            for hf in range(2):
                stage_ref[hf] = val[:, hf * LANES:(hf + 1) * LANES]
            for rho in range(r):
                for hf in range(2):
                    dil_refs[j][:, rho * 256 + hf * LANES:rho * 256 + (hf + 1) * LANES] = (
                        stage_ref[hf, pl.ds(rho, val.shape[0] // r, stride=r), :].astype(BF16))
    for j in range(4):
        gate_ref[:, 512 * j:512 * (j + 1)] = mm(C_GATE + 512 * j, C_GATE + 512 * (j + 1)).astype(BF16)


def _proj(x2, attn_norm, w_packed, kv_norm):
    n, d = x2.shape
    tm = PROJ_TM
    row = lambda i: (i, 0)
    const = lambda i: (0, 0)
    outs = [jax.ShapeDtypeStruct((n, 512), BF16), jax.ShapeDtypeStruct((n, 256), BF16),
            jax.ShapeDtypeStruct((n, 256), BF16), jax.ShapeDtypeStruct((n, 128), F32)]
    for j in range(9):
        r = DIL_GROUPS[j % 3][1]
        outs.append(jax.ShapeDtypeStruct((n // r, r * 256), BF16))
    outs += [jax.ShapeDtypeStruct((n, 2048), BF16)]
    out_specs = [pl.BlockSpec((tm * s.shape[0] // n, s.shape[1]), row) for s in outs]
    return pl.pallas_call(
        _proj_kernel,
        out_shape=outs,
        grid=(n // tm,),
        in_specs=[pl.BlockSpec((tm, d), row), pl.BlockSpec((1, d), const),
                  pl.BlockSpec((d, C_END), const), pl.BlockSpec((1, KV_LATENT), const)],
        out_specs=out_specs,
        scratch_shapes=[pltpu.VMEM((2, tm, LANES), F32)],
        compiler_params=_cparams(("parallel",)),
        name="proj",
    )(x2, attn_norm.reshape(1, d), w_packed, kv_norm.reshape(1, KV_LATENT))


def _bucket_thresholds():
    max_exact = NUM_BUCKETS // 2
    d = np.arange(0, MAX_DISTANCE + 1)
    nf = np.maximum(d, 1).astype(np.float32)
    large = max_exact + (np.log(nf / np.float32(max_exact)) / np.float32(math.log(MAX_DISTANCE / max_exact))
                         * np.float32(NUM_BUCKETS - max_exact)).astype(np.int32)
    large = np.minimum(large, NUM_BUCKETS - 1)
    bucket = np.where(d < max_exact, d, large)
    assert np.all(np.diff(bucket) >= 0)
    return [int(np.argmax(bucket >= b)) for b in range(1, NUM_BUCKETS)]


_BUCKET_THR = _bucket_thresholds()


def _bias_from_distance(dist, tab_ref, heads):
    masks = [dist >= t for t in _BUCKET_THR]
    out = []
    for h in heads:
        v = jnp.full(dist.shape, tab_ref[0, h], F32)
        for b in range(1, NUM_BUCKETS):
            v = jnp.where(masks[b - 1], tab_ref[b, h], v)
        out.append(v)
    return out


def _dsa_bias_kernel(tab_ref, o_ref):
    delta = pl.program_id(0)
    j = lax.broadcasted_iota(I32, (BLK, BLK), 0)
    i = lax.broadcasted_iota(I32, (BLK, BLK), 1)
    dist = jnp.maximum(delta * BLK + i - j, 0)
    tiles = _bias_from_distance(dist, tab_ref, range(DSA_HEADS))
    for h in range(DSA_HEADS):
        o_ref[0, h] = tiles[h] * LOG2E


def _dil_bias_kernel(tab_ref, o_ref, *, dilations):
    g = pl.program_id(0)
    j = lax.broadcasted_iota(I32, (2 * BLK, BLK), 0)
    i = lax.broadcasted_iota(I32, (2 * BLK, BLK), 1)
    step = i + BLK - j
    valid = (step >= 0) & (step <= BLK)
    for gi, r in enumerate(dilations):
        @pl.when(g == gi)
        def _():
            dist = jnp.maximum(step, 0) * r
            heads = [DSA_HEADS + gi * DIL_HPG + hh for hh in range(DIL_HPG)]
            tiles = _bias_from_distance(dist, tab_ref, heads)
            for hh in range(DIL_HPG):
                o_ref[0, :, hh * BLK:(hh + 1) * BLK] = jnp.where(valid, tiles[hh], NEG)


def _bias_tiles(rel_bias, nkc):
    smem = pl.BlockSpec(memory_space=pltpu.SMEM)
    dsa = pl.pallas_call(
        _dsa_bias_kernel,
        out_shape=jax.ShapeDtypeStruct((nkc, DSA_HEADS, BLK, BLK), F32),
        grid=(nkc,),
        in_specs=[smem],
        out_specs=pl.BlockSpec((1, DSA_HEADS, BLK, BLK), lambda d: (d, 0, 0, 0)),
        compiler_params=_cparams(("parallel",)),
        name="dsa_bias",
    )(rel_bias)
    dil = pl.pallas_call(
        functools.partial(_dil_bias_kernel, dilations=tuple(r for _, r in DIL_GROUPS)),
        out_shape=jax.ShapeDtypeStruct((len(DIL_GROUPS), 2 * BLK, DIL_HPG * BLK), F32),
        grid=(len(DIL_GROUPS),),
        in_specs=[smem],
        out_specs=pl.BlockSpec((1, 2 * BLK, DIL_HPG * BLK), lambda g: (g, 0, 0)),
        compiler_params=_cparams(("parallel",)),
        name="dil_bias",
    )(rel_bias)
    return dsa, dil


SUP = 4
SROWS = SUP * BLK


PLANE_KEYS = 32 * 8


def _bit_planes(words):
    x = list(words)
    j, m = 16, 0x0000FFFF
    while j:
        k = 0
        while k < 32:
            t = (x[k] ^ lax.shift_right_logical(x[k + j], jnp.int32(j))) & m
            x[k] = x[k] ^ t
            x[k + j] = x[k + j] ^ jnp.left_shift(t, jnp.int32(j))
            k = (k + j + 1) & ~j
        j >>= 1
        m = (m ^ (m << j)) & 0xFFFFFFFF
        m = m - (1 << 32) if m >= (1 << 31) else m
    return x


def _dsa_kernel(qiT_ref, wiT_ref, qaT_ref, kidx_ref, ckv_ref, ckvT_ref, wuk_ref, wuv_ref, bias_ref, tri_ref,
                y_ref, sc_ref, planes_ref, qlT_ref, x_ref, pT_ref, *, topk, nsc, qb0):
    qb = qb0 + pl.program_id(1)
    row = lax.broadcasted_iota(I32, (SROWS, BLK), 0)
    col = lax.broadcasted_iota(I32, (SROWS, BLK), 1)
    trips = [slice(sc * SROWS, (sc + 1) * SROWS) for sc in range(nsc)]

    def causal(sc):
        return row <= col + (qb * BLK - sc * SROWS)

    for h in range(DSA_HEADS):
        ql = jnp.dot(wuk_ref[h], qaT_ref[0, h * HEAD_DIM:(h + 1) * HEAD_DIM, :],
                     preferred_element_type=F32) * (HEAD_DIM ** -0.5 * LOG2E)
        qlT_ref[:, h * BLK:(h + 1) * BLK] = ql.astype(BF16)

    wq = wiT_ref[0] * (IDX_HEADS ** -0.5)
    for sc, ts in enumerate(trips):
        kx = kidx_ref[0, ts, :]
        acc = jnp.zeros((SROWS, BLK), F32)
        for h in range(IDX_HEADS):
            s = jnp.dot(kx, qiT_ref[0, h * IDX_DIM:(h + 1) * IDX_DIM, :],
                        preferred_element_type=F32) * (IDX_DIM ** -0.5)
            acc = acc + wq[h:h + 1, :] * jnp.maximum(s, 0.0)
        acc = jnp.where(causal(sc), acc, NEG)
        sc_ref[ts, :] = acc
        bits = pltpu.bitcast(acc, I32)
        bits = jnp.where(bits == INT_MIN, 0, bits)
        ukey = bits ^ ((bits >> 31) & 0x7FFFFFFF) ^ INT_MIN
        for grp in range(SROWS // PLANE_KEYS):
            tiles = [ukey[grp * PLANE_KEYS + j * 8:grp * PLANE_KEYS + (j + 1) * 8] for j in range(32)]
            for b, plane in enumerate(_bit_planes(tiles)):
                planes_ref[b, sc * (SROWS // PLANE_KEYS) + grp] = plane

    def count(pred):
        cnt = jnp.zeros((8, BLK), I32)
        for ts in trips:
            cnt = cnt + jnp.sum(jnp.where(pred(sc_ref[ts, :]), 1, 0).reshape(SROWS // 8, 8, BLK), axis=0)
        return jnp.sum(cnt, axis=0, keepdims=True)

    def as_float(key):
        return pltpu.bitcast(key ^ ((key >> 31) & 0x7FFFFFFF), F32)

    def bit_body(it, carry):
        alive, above, code = carry
        ones = alive & planes_ref[it]
        c = jnp.sum(jnp.sum(lax.population_count(ones), axis=0), axis=0, keepdims=True)
        take = above + c >= topk
        alive = jnp.where(take, ones, alive ^ ones)
        above = jnp.where(take, above, above + c)
        code = code | jnp.where(take, jnp.left_shift(jnp.int32(1), 31 - it), 0)
        return alive, above, code

    nw = nsc * (SROWS // PLANE_KEYS)
    _, _, code = lax.fori_loop(
        0, 32, bit_body,
        (jnp.full((nw, 8, BLK), -1, I32), jnp.zeros((1, BLK), I32), jnp.zeros((1, BLK), I32)))
    guess = as_float(code ^ INT_MIN)
    n_gt_guess = count(lambda v: v > guess)
    proven = (n_gt_guess < topk) & (count(lambda v: v >= guess) >= topk)

    def bisect():
        def thr_body(it, lo):
            cand = lo + jnp.left_shift(jnp.int32(1), 31 - it)
            cand_f = as_float(cand)
            return jnp.where(count(lambda v: v >= cand_f) >= topk, cand, lo)
        t = as_float(lax.fori_loop(0, 32, thr_body, jnp.full((1, BLK), INT_MIN, I32)))
        return t, count(lambda v: v > t)

    thr, n_gt = lax.cond(jnp.min(jnp.where(proven, 1, 0)) > 0, lambda: (guess, n_gt_guess), bisect)
    ties_wanted = (topk - n_gt).astype(F32)

    m = [jnp.full((8, BLK), NEG, F32) for _ in range(DSA_HEADS)]
    ties_before = jnp.zeros((1, BLK), F32)
    for sc, ts in enumerate(trips):
        k = sc_ref[ts, :]
        tie = k == thr
        tie_rank = jnp.dot(tri_ref[...], jnp.where(tie, 1.0, 0.0).astype(BF16),
                           preferred_element_type=F32) + ties_before
        ties_before = tie_rank[SROWS - 1:SROWS, :]
        sel = ((k > thr) | (tie & (tie_rank <= ties_wanted))) & causal(sc)
        am = jnp.where(sel, 0.0, NEG)
        ck = ckv_ref[0, ts, :]
        for hp in range(DSA_HEADS // 2):
            lg2 = jnp.dot(ck, qlT_ref[:, 2 * hp * BLK:(2 * hp + 2) * BLK], preferred_element_type=F32)
            for h in (2 * hp, 2 * hp + 1):
                lg = lg2[:, (h % 2) * BLK:(h % 2 + 1) * BLK]
                for j in range(SUP):
                    rs = slice(j * BLK, (j + 1) * BLK)
                    delta = jnp.maximum(qb - (sc * SUP + j), 0)
                    x = lg[rs] + bias_ref[delta, h] + am[rs]
                    x_ref[sc * SROWS + j * BLK:sc * SROWS + (j + 1) * BLK, h * BLK:(h + 1) * BLK] = x
                    m[h] = jnp.maximum(m[h], jnp.max(x.reshape(BLK // 8, 8, BLK), axis=0))
    m = [jnp.max(v, axis=0, keepdims=True) for v in m]

    l = [jnp.zeros((8, BLK), F32) for _ in range(DSA_HEADS)]
    for sc, ts in enumerate(trips):
        for h in range(DSA_HEADS):
            hs = slice(h * BLK, (h + 1) * BLK)
            p = jnp.exp2(x_ref[ts, hs] - m[h])
            pT_ref[ts, hs] = p.astype(BF16)
            l[h] = l[h] + jnp.sum(p.reshape(SROWS // 8, 8, BLK), axis=0)

    for hp in range(DSA_HEADS // 2):
        o2 = jnp.dot(ckvT_ref[0], pT_ref[:, 2 * hp * BLK:(2 * hp + 2) * BLK],
                     preferred_element_type=F32)
        for h in (2 * hp, 2 * hp + 1):
            inv = 1.0 / jnp.sum(l[h], axis=0, keepdims=True)
            oh = (o2[:, (h % 2) * BLK:(h % 2 + 1) * BLK] * inv).T.astype(BF16)
            yh = jnp.dot(oh, wuv_ref[h], preferred_element_type=F32)
            y_ref[0, :, h * HEAD_DIM:(h + 1) * HEAD_DIM] = yh.astype(BF16)


def _dsa_group(g, qiT, wiT, qaT, kidx, ckv, ckvT, wuk, wuv, bias_tiles, tri):
    b, seq, _ = ckv.shape
    nkc = seq // BLK
    nsc = g + 1
    nk = nsc * SROWS
    topk = min(TOPK_MAX, seq // 4)
    qblk = lambda rows: pl.BlockSpec((1, rows, BLK), lambda bi, qi: (bi, 0, g * SUP + qi))
    head3 = lambda cols: pl.BlockSpec((1, nk, cols), lambda bi, qi: (bi, 0, 0))
    const = lambda s: pl.BlockSpec(s, lambda bi, qi: (0,) * len(s))
    hl = DSA_HEADS * BLK
    return pl.pallas_call(
        functools.partial(_dsa_kernel, topk=topk, nsc=nsc, qb0=g * SUP),
        out_shape=jax.ShapeDtypeStruct((b, SROWS, DSA_HEADS * HEAD_DIM), BF16),
        grid=(b, SUP),
        in_specs=[qblk(IDX_HEADS * IDX_DIM), qblk(8), qblk(DSA_HEADS * HEAD_DIM),
                  head3(IDX_DIM), head3(KV_LATENT),
                  pl.BlockSpec((1, KV_LATENT, nk), lambda bi, qi: (bi, 0, 0)),
                  const((DSA_HEADS, KV_LATENT, HEAD_DIM)), const((DSA_HEADS, KV_LATENT, HEAD_DIM)),
                  const((nkc, DSA_HEADS, BLK, BLK)), const((SROWS, SROWS))],
        out_specs=pl.BlockSpec((1, BLK, DSA_HEADS * HEAD_DIM), lambda bi, qi: (bi, qi, 0)),
        scratch_shapes=[pltpu.VMEM((nk, BLK), F32),
                        pltpu.VMEM((32, nk // PLANE_KEYS, 8, BLK), I32),
                        pltpu.VMEM((KV_LATENT, hl), BF16),
                        pltpu.VMEM((nk, hl), F32),
                        pltpu.VMEM((nk, hl), BF16)],
        compiler_params=_cparams(("parallel", "arbitrary")),
        name=f"dsa_g{g}",
    )(qiT, wiT, qaT, kidx, ckv, ckvT, wuk, wuv, bias_tiles, tri)


def _dsa(qiT, wiT, qaT, kidx, ckv, ckvT, wuk, wuv, bias_tiles):
    seq = ckv.shape[1]
    assert seq % SROWS == 0 and seq >= 4 * TOPK_MAX
    tri = jnp.tril(jnp.ones((SROWS, SROWS), BF16))
    groups = [_dsa_group(g, qiT, wiT, qaT, kidx, ckv, ckvT, wuk, wuv, bias_tiles, tri)
              for g in range(seq // SROWS)]
    return jnp.concatenate(groups, axis=1)


DIL_UNROLL = 5
DIL_STEP_ROWS = 2048


def _dil_kernel(q_ref, k_ref, v_ref, bm_ref, o_ref, lse_ref, vT_ref, *, nblk, nres):
    hq = DIL_HPG * BLK
    rowh = lax.broadcasted_iota(I32, (hq, DIL_OUT), 0) // BLK
    colh = lax.broadcasted_iota(I32, (hq, DIL_OUT), 1) // HEAD_DIM
    same_head = rowh == colh

    for res in range(nres):
        cs = slice(res * DIL_OUT, (res + 1) * DIL_OUT)

        for n in range(nblk):
            vT_ref[n] = v_ref[0, n * BLK:(n + 1) * BLK, cs].astype(F32).T.astype(BF16)

        def block(qo, kw, vT, bm, cs=cs):
            q = q_ref[0, pl.ds(qo, BLK), cs]
            qd = jnp.where(same_head, jnp.concatenate([q] * DIL_HPG, axis=0), jnp.zeros((), BF16))
            s = lax.dot_general(kw, qd, (((1,), (1,)), ((), ())), preferred_element_type=F32)
            s = s * (HEAD_DIM ** -0.5) + bm
            m = jnp.max(s, axis=0, keepdims=True)
            e = jnp.exp(s - m)
            l = jnp.sum(e, axis=0, keepdims=True)
            oT = jnp.dot(vT, e.astype(BF16), preferred_element_type=F32)
            inv = 1.0 / l
            lse = m + jnp.log(l)
            outs, lses = [], []
            for hh in range(DIL_HPG):
                qs = slice(hh * BLK, (hh + 1) * BLK)
                outs.append(oT[hh * HEAD_DIM:(hh + 1) * HEAD_DIM, qs] * inv[:, qs])
                lses.append(jnp.broadcast_to(lse[:, qs], (HEAD_DIM, BLK)))
            o_ref[0, pl.ds(qo, BLK), cs] = jnp.concatenate(outs, axis=0).T.astype(BF16)
            lse_ref[0, pl.ds(qo, BLK), cs] = jnp.concatenate(lses, axis=0).T

        block(0, k_ref[0, 0:BLK, cs], vT_ref[0], bm_ref[0, BLK:, :])

        def body(n, carry, cs=cs, block=block):
            ko = pl.multiple_of((n - 1) * BLK, BLK)
            vT = jnp.concatenate([vT_ref[n - 1], vT_ref[n]], axis=1)
            block(pl.multiple_of(n * BLK, BLK), k_ref[0, pl.ds(ko, 2 * BLK), cs], vT, bm_ref[0])
            return carry

        lax.fori_loop(1, nblk, body, 0, unroll=DIL_UNROLL)


def _dilated_group(q, k, v, bm, g, dilation, b):
    c = DIL_OUT
    ls = q.shape[0] // b
    nblk = ls // BLK
    nres = max(1, min(dilation, DIL_STEP_ROWS // ls))
    view = lambda a: a.reshape(b, ls, dilation * c)
    blk = pl.BlockSpec((1, ls, nres * c), lambda bi, ri: (bi, 0, ri))
    o, lse = pl.pallas_call(
        functools.partial(_dil_kernel, nblk=nblk, nres=nres),
        out_shape=[jax.ShapeDtypeStruct((b, ls, dilation * c), BF16),
                   jax.ShapeDtypeStruct((b, ls, dilation * c), F32)],
        grid=(b, dilation // nres),
        in_specs=[blk, blk, blk, pl.BlockSpec((1, 2 * BLK, DIL_HPG * BLK), lambda bi, ri: (g, 0, 0))],
        out_specs=[blk, blk],
        scratch_shapes=[pltpu.VMEM((nblk, c, BLK), BF16)],
        compiler_params=_cparams(("parallel", "parallel")),
        name=f"dilated_g{g}",
    )(view(q), view(k), view(v), bm)
    return o.reshape(b * ls, dilation * c), lse.reshape(b * ls, dilation * c)


def _store_row_tiles(ref, base, val):
    rows, d = val.shape
    dt = d // LANES
    for s in range(dt):
        ref[pl.ds(base * dt + s, rows, stride=dt), :] = val[:, s * LANES:(s + 1) * LANES]


def _load_row_tiles(ref, base, rows, dt):
    return jnp.concatenate([ref[pl.ds(base * dt + s, rows, stride=dt), :] for s in range(dt)], axis=1)


MIX_COLS = 256


def _token_major(ref, stage_ref, r):
    if r == 1:
        return ref[...].astype(F32)
    rows = ref.shape[0]
    for rho in range(r):
        for hf in range(DIL_OUT // LANES):
            c0 = rho * DIL_OUT + hf * LANES
            stage_ref[hf, pl.ds(rho, rows, stride=r), :] = ref[:, c0:c0 + LANES].astype(F32)
    return jnp.concatenate([stage_ref[hf] for hf in range(DIL_OUT // LANES)], axis=1)


def _sigmoid(v):
    return 0.5 * jnp.tanh(0.5 * v) + 0.5


def _mix_kernel(x_ref, ya_ref, o1_ref, o2_ref, o3_ref, l1_ref, l2_ref, l3_ref, gate_ref,
                wa_ref, wb_ref, wo_ref, fg_ref, wr_ref, br_ref,
                h_ref, t_ref, rw_ref, ri_ref, mixed_ref, *stage_refs):
    dils = [r for _, r in DIL_GROUPS]
    o = [_token_major(ref, st, r) for ref, st, r in zip((o1_ref, o2_ref, o3_ref), stage_refs[:3], dils)]
    l1, l2, l3 = [_token_major(ref, st, r) for ref, st, r in zip((l1_ref, l2_ref, l3_ref), stage_refs[3:], dils)]
    mx = jnp.maximum(jnp.maximum(l1, l2), l3)
    e1, e2, e3 = jnp.exp(l1 - mx), jnp.exp(l2 - mx), jnp.exp(l3 - mx)
    inv = 1.0 / (e1 + e2 + e3)
    yb = ((e1 * inv) * o[0] + (e2 * inv) * o[1] + (e3 * inv) * o[2]).astype(BF16)
    ya = ya_ref[...]
    d = x_ref.shape[1]
    for c in range(0, d, MIX_COLS):
        cs = slice(c, c + MIX_COLS)
        a = jnp.dot(ya, wa_ref[:, cs], preferred_element_type=F32)
        bmix = jnp.dot(yb, wb_ref[:, cs], preferred_element_type=F32)
        g0 = _sigmoid(gate_ref[:, cs].astype(F32))
        g1 = _sigmoid(gate_ref[:, d + c:d + c + MIX_COLS].astype(F32))
        mixed_ref[:, cs] = (g0 * a + g1 * bmix).astype(BF16)
    h = x_ref[...] + jnp.dot(mixed_ref[...], wo_ref[...], preferred_element_type=F32)
    h_ref[...] = h
    t = h * lax.rsqrt(jnp.mean(h * h, axis=-1, keepdims=True) + RMS_EPS) * fg_ref[...]
    _store_row_tiles(t_ref, 0, t)

    t_hi = t.astype(BF16)
    t_lo = (t - t_hi.astype(F32)).astype(BF16)
    r1 = jnp.dot(t_hi, wr_ref[...], preferred_element_type=F32)
    r2 = jnp.dot(t_lo, wr_ref[:, :ROUTE_COLS], preferred_element_type=F32)
    logits = r1[:, :ROUTE_COLS] + (r1[:, ROUTE_COLS:] + r2) + br_ref[...]
    lane = lax.broadcasted_iota(I32, logits.shape, 1)
    ninf = -jnp.inf
    big = jnp.int32(10 ** 6)

    def first_argmax(v, vmax):
        return jnp.min(jnp.where(v == vmax, lane, big), axis=-1, keepdims=True)

    gl = jnp.where(lane < N_GROUPS, logits, ninf)
    gmax = jnp.max(gl, axis=-1, keepdims=True)
    gsel = first_argmax(gl, gmax)
    p_g = 1.0 / jnp.sum(jnp.exp(gl - gmax), axis=-1, keepdims=True)
    lo = N_GROUPS + gsel * EXPERTS_PER_GROUP
    el = jnp.where((lane >= lo) & (lane < lo + EXPERTS_PER_GROUP), logits, ninf)
    v1 = jnp.max(el, axis=-1, keepdims=True)
    i1 = first_argmax(el, v1)
    el2 = jnp.where(lane == i1, ninf, el)
    v2 = jnp.max(el2, axis=-1, keepdims=True)
    i2 = first_argmax(el2, v2)
    e2 = jnp.exp(v2 - v1)
    w1 = p_g / (1.0 + e2)
    w2 = p_g * e2 / (1.0 + e2)
    rw_ref[...] = jnp.where(lane == 0, w1, jnp.where(lane == 1, w2, 0.0))
    ri_ref[...] = jnp.where(lane == 0, i1 - N_GROUPS, jnp.where(lane == 1, i2 - N_GROUPS, 0))


def _mix(x2, ya, os_, lses, gates, wa, wb, wo, ffn_norm, wr, br):
    n, d = x2.shape
    tm = MIX_TM
    row = lambda c: pl.BlockSpec((tm, c), lambda i: (i, 0))
    res = lambda a: pl.BlockSpec((tm * a.shape[0] // n, a.shape[1]), lambda i: (i, 0))
    const = lambda s: pl.BlockSpec(s, lambda i: (0, 0))
    return pl.pallas_call(
        _mix_kernel,
        out_shape=[jax.ShapeDtypeStruct((n, d), F32), jax.ShapeDtypeStruct((n * (d // LANES), LANES), F32),
                   jax.ShapeDtypeStruct((n, ROUTE_COLS), F32), jax.ShapeDtypeStruct((n, ROUTE_COLS), I32)],
        grid=(n // tm,),
        in_specs=[row(d), row(512)] + [res(a) for a in os_] + [res(a) for a in lses] + [row(2 * d),
                  const(wa.shape), const(wb.shape), const(wo.shape), const((1, d)),
                  const(wr.shape), const((1, ROUTE_COLS))],
        out_specs=[row(d), pl.BlockSpec((tm * (d // LANES), LANES), lambda i: (i, 0)),
                   row(ROUTE_COLS), row(ROUTE_COLS)],
        scratch_shapes=[pltpu.VMEM((tm, d), BF16)] + [pltpu.VMEM((DIL_OUT // LANES, tm, LANES), F32)] * 6,
        compiler_params=_cparams(("parallel",)),
        name="mix",
    )(x2, ya, *os_, *lses, gates, wa, wb, wo, ffn_norm.reshape(1, d), wr, br)


GATHER_UNROLL = 8


def _start_row_gather(src_hbm, idx_ref, nrows, dt, buf, sem, slot):
    def body(g, c):
        for u in range(GATHER_UNROLL):
            r = g * GATHER_UNROLL + u
            src = pl.multiple_of(idx_ref[0, 0, r] * dt, dt)
            dst = pl.multiple_of((slot * nrows + r) * dt, dt)
            pltpu.make_async_copy(src_hbm.at[pl.ds(src, dt)], buf.at[pl.ds(dst, dt)],
                                  sem.at[slot]).start(priority=u % 2)
        return c
    lax.fori_loop(0, nrows // GATHER_UNROLL, body, 0)


def _wait_row_gather(src_hbm, nrows, dt, buf, sem, slot):
    dst = pl.multiple_of(slot * nrows * dt, dt)
    pltpu.make_async_copy(src_hbm.at[pl.ds(0, nrows * dt)], buf.at[pl.ds(dst, nrows * dt)], sem.at[slot]).wait()


def _gather_pipeline(i, nsteps, src_hbm, cur_ref, nxt_ref, nrows, dt, buf, sem):
    slot = i % 2

    @pl.when(i == 0)
    def _():
        _start_row_gather(src_hbm, cur_ref, nrows, dt, buf, sem, 0)

    @pl.when(i + 1 < nsteps)
    def _():
        _start_row_gather(src_hbm, nxt_ref, nrows, dt, buf, sem, 1 - slot)

    _wait_row_gather(src_hbm, nrows, dt, buf, sem, slot)
    return slot


def _expert_kernel(te_ref, cur_ref, nxt_ref, t_hbm, wg_ref, wu_ref, wd_ref, y_ref, buf, sem):
    i = pl.program_id(0)
    dt = wg_ref.shape[1] // LANES
    slot = _gather_pipeline(i, pl.num_programs(0), t_hbm, cur_ref, nxt_ref, MOE_TM, dt, buf, sem)
    xt = _load_row_tiles(buf, slot * MOE_TM, MOE_TM, dt).astype(BF16)
    hg = jnp.dot(xt, wg_ref[0], preferred_element_type=F32)
    hu = jnp.dot(xt, wu_ref[0], preferred_element_type=F32)
    hid = (hg * jax.nn.sigmoid(hg)) * hu
    _store_row_tiles(y_ref, 0, jnp.dot(hid.astype(BF16), wd_ref[0], preferred_element_type=F32))


def _experts(tile_expert, row_token, t, wg, wu, wd, d):
    dt = d // LANES
    ntiles = tile_expert.shape[0]
    ff = wg.shape[2]
    tok3 = row_token.reshape(ntiles, 1, MOE_TM)
    smem_cur = pl.BlockSpec((1, 1, MOE_TM), lambda i, te: (i, 0, 0), memory_space=pltpu.SMEM)
    smem_nxt = pl.BlockSpec((1, 1, MOE_TM), lambda i, te: (jnp.minimum(i + 1, ntiles - 1), 0, 0),
                            memory_space=pltpu.SMEM)
    wspec = lambda s: pl.BlockSpec((1,) + s, lambda i, te: (te[i], 0, 0))
    return pl.pallas_call(
        _expert_kernel,
        out_shape=jax.ShapeDtypeStruct((ntiles * MOE_TM * dt, LANES), F32),
        grid_spec=pltpu.PrefetchScalarGridSpec(
            num_scalar_prefetch=1,
            grid=(ntiles,),
            in_specs=[smem_cur, smem_nxt, pl.BlockSpec(memory_space=pl.ANY),
                      wspec((d, ff)), wspec((d, ff)), wspec((ff, d))],
            out_specs=pl.BlockSpec((MOE_TM * dt, LANES), lambda i, te: (i, 0)),
            scratch_shapes=[pltpu.VMEM((2 * MOE_TM * dt, LANES), F32), pltpu.SemaphoreType.DMA((2,))],
        ),
        compiler_params=_cparams(("arbitrary",)),
        name="experts",
    )(tile_expert, tok3, tok3, t, wg, wu, wd)


def _final_kernel(cur_ref, nxt_ref, y_hbm, h_ref, rw_ref, fn_ref, o_ref, buf, sem):
    i = pl.program_id(0)
    dt = h_ref.shape[1] // LANES
    slot = _gather_pipeline(i, pl.num_programs(0), y_hbm, cur_ref, nxt_ref, 2 * FIN_TM, dt, buf, sem)
    rw = rw_ref[...]
    y0 = _load_row_tiles(buf, slot * 2 * FIN_TM, FIN_TM, dt)
    y1 = _load_row_tiles(buf, slot * 2 * FIN_TM + FIN_TM, FIN_TM, dt)
    h = h_ref[...] + rw[:, 0:1] * y0 + rw[:, 1:2] * y1
    o_ref[...] = h * lax.rsqrt(jnp.mean(h * h, axis=-1, keepdims=True) + RMS_EPS) * fn_ref[...]


def _final(pos_tiles, y_sorted, h, rw, final_norm):
    n, d = h.shape
    tm = FIN_TM
    nt = n // tm
    smem_cur = pl.BlockSpec((1, 1, 2 * tm), lambda i: (i, 0, 0), memory_space=pltpu.SMEM)
    smem_nxt = pl.BlockSpec((1, 1, 2 * tm), lambda i: (jnp.minimum(i + 1, nt - 1), 0, 0),
                            memory_space=pltpu.SMEM)
    return pl.pallas_call(
        _final_kernel,
        out_shape=jax.ShapeDtypeStruct((n, d), F32),
        grid=(nt,),
        in_specs=[smem_cur, smem_nxt, pl.BlockSpec(memory_space=pl.ANY),
                  pl.BlockSpec((tm, d), lambda i: (i, 0)), pl.BlockSpec((tm, ROUTE_COLS), lambda i: (i, 0)),
                  pl.BlockSpec((1, d), lambda i: (0, 0))],
        out_specs=pl.BlockSpec((tm, d), lambda i: (i, 0)),
        scratch_shapes=[pltpu.VMEM((2 * 2 * tm * (d // LANES), LANES), F32), pltpu.SemaphoreType.DMA((2,))],
        compiler_params=_cparams(("arbitrary",)),
        name="final",
    )(pos_tiles, pos_tiles, y_sorted, h, rw, final_norm.reshape(1, d))


def _route_plan(gid):
    n = gid.shape[0]
    e = gid.reshape(-1)
    onehot = (e[:, None] == jnp.arange(N_EXPERTS, dtype=I32)[None, :]).astype(F32)
    chunk = 256
    oh3 = onehot.reshape(-1, chunk, N_EXPERTS)
    within = jnp.einsum("ij,tjk->tik", jnp.tril(jnp.ones((chunk, chunk), F32)), oh3)
    totals = within[:, -1, :]
    before = jnp.cumsum(totals, axis=0) - totals
    csum = (within + before[:, None, :]).reshape(-1, N_EXPERTS)
    rank = jnp.sum(csum * onehot, axis=1).astype(I32) - 1
    counts = (before[-1] + totals[-1]).astype(I32)
    padded = ((counts + MOE_TM - 1) // MOE_TM) * MOE_TM
    seg_end = jnp.cumsum(padded)
    pos = (seg_end - padded)[e] + rank
    nrows = 2 * n + N_EXPERTS * MOE_TM
    tile_start = jnp.arange(nrows // MOE_TM, dtype=I32) * MOE_TM
    tile_expert = jnp.minimum(jnp.sum(tile_start[:, None] >= seg_end[None, :], axis=1), N_EXPERTS - 1).astype(I32)
    order = jnp.argsort(e, stable=True).astype(I32)
    first = jnp.cumsum(counts) - counts
    row_expert = jnp.repeat(tile_expert, MOE_TM)
    src = jnp.arange(nrows, dtype=I32) - (seg_end - padded)[row_expert] + first[row_expert]
    row_token = order[jnp.clip(src, 0, 2 * n - 1)] // 2
    return row_token, tile_expert, pos.reshape(n, 2)


def _pack_w_in(w):
    d = w.shape[0]
    o_kv, o_qi, o_ki, o_wi = 512, 768, 1024, 1088
    o_dil = o_wi + IDX_HEADS
    o_gate = o_dil + 9 * 256
    pad = jnp.zeros((d, LANES - IDX_DIM - IDX_HEADS), w.dtype)
    packed = jnp.concatenate([w[:, :o_ki], w[:, o_ki:o_wi], w[:, o_wi:o_dil], pad, w[:, o_dil:o_gate],
                              w[:, o_gate:]], axis=1)
    assert packed.shape[1] == C_END
    return packed.astype(BF16)


def kernel(x, attn_norm, w_in, kv_norm, w_uk, w_uv, rel_bias, w_branch_a, w_branch_b, w_out, ffn_norm,
           w_router_group, b_router_group, w_router_expert, b_router_expert, w_gate, w_up, w_down,
           final_norm):
    b, seq, d = x.shape
    n = b * seq
    nkc = seq // BLK
    assert w_in.shape[0] == 1, "one layer"
    x2 = x.reshape(n, d)

    outs = _proj(x2, attn_norm[0], _pack_w_in(w_in[0]), kv_norm[0])
    qa, ckv, qi, kw = outs[:4]
    dil = outs[4:13]
    gates = outs[13]

    dsa_bias, dil_bias = _bias_tiles(rel_bias, nkc)

    t3 = lambda a: jnp.swapaxes(a.reshape(b, seq, a.shape[-1]), 1, 2)
    kw3 = kw.reshape(b, seq, LANES)
    wiT = jnp.swapaxes(kw3[:, :, IDX_DIM:IDX_DIM + 8], 1, 2)
    kidx = kw3[:, :, :IDX_DIM].astype(BF16)
    ckv3 = ckv.reshape(b, seq, KV_LATENT)
    ckvT = jnp.swapaxes(ckv3, 1, 2)
    ya = _dsa(t3(qi), wiT, t3(qa), kidx, ckv3, ckvT, w_uk[0].astype(BF16), w_uv[0].astype(BF16), dsa_bias)

    os_, lses = [], []
    for g, (_, dilation) in enumerate(DIL_GROUPS):
        o, lse = _dilated_group(dil[g], dil[3 + g], dil[6 + g], dil_bias, g, dilation, b)
        os_.append(o)
        lses.append(lse)

    wr = jnp.concatenate([w_router_group[0],
                          jnp.swapaxes(w_router_expert[0], 0, 1).reshape(d, N_EXPERTS),
                          jnp.zeros((d, ROUTE_COLS - N_GROUPS - N_EXPERTS), F32)], axis=1)
    wr_hi = wr.astype(BF16)
    wr = jnp.concatenate([wr_hi, (wr - wr_hi.astype(F32)).astype(BF16)], axis=1)
    br = jnp.concatenate([b_router_group[0], b_router_expert[0].reshape(-1),
                          jnp.zeros((ROUTE_COLS - N_GROUPS - N_EXPERTS,), F32)]).reshape(1, ROUTE_COLS)
    h, t, rw, ri = _mix(x2, ya.reshape(n, -1), os_, lses, gates,
                        w_branch_a[0].astype(BF16), w_branch_b[0].astype(BF16), w_out[0].astype(BF16),
                        ffn_norm[0], wr, br)

    row_token, tile_expert, pos = _route_plan(ri[:, :2])
    y_sorted = _experts(tile_expert, row_token, t, w_gate[0].astype(BF16), w_up[0].astype(BF16),
                        w_down[0].astype(BF16), d)

    pos_tiles = jnp.swapaxes(pos.reshape(n // FIN_TM, FIN_TM, 2), 1, 2).reshape(n // FIN_TM, 1, 2 * FIN_TM)
    out = _final(pos_tiles, y_sorted, h, rw, final_norm)
    return out.reshape(b, seq, d)
```

```python
import functools
import math

import numpy as np
import jax
import jax.numpy as jnp
from jax import lax
from jax.experimental import pallas as pl
from jax.experimental.pallas import tpu as pltpu

F32 = jnp.float32
BF16 = jnp.bfloat16
I32 = jnp.int32

LANES = 128
VMEM_LIMIT_BYTES = 56 * 1024 * 1024

HEAD_DIM = 64
DSA_HEADS = 8
KV_LATENT = 256
IDX_HEADS = 4
IDX_DIM = 64
TOPK_MAX = 256
DIL_GROUPS = ((128, 1), (512, 4), (2048, 16))
DIL_HPG = 4
DIL_OUT = DIL_HPG * HEAD_DIM
NUM_BUCKETS = 32
MAX_DISTANCE = 2048
N_GROUPS = 4
EXPERTS_PER_GROUP = 8
N_EXPERTS = N_GROUPS * EXPERTS_PER_GROUP
RMS_EPS = 1e-6
NEG = -1e30
LOG2E = math.log2(math.e)
INT_MIN = -2 ** 31

BLK = 128
PROJ_TM = 512
MIX_TM = 512
MOE_TM = 256
FIN_TM = 256
ROUTE_COLS = 128


def _cparams(sem):
    return pltpu.CompilerParams(dimension_semantics=sem, vmem_limit_bytes=VMEM_LIMIT_BYTES)


C_QA = 0
C_KV = 512
C_QI = 768
C_KW = 1024
C_DIL = 1152
C_GATE = C_DIL + 9 * 256
C_END = C_GATE + 2048


def _proj_kernel(x_ref, g_ref, w_ref, kvg_ref, qa_ref, ckv_ref, qi_ref, kw_ref, *rest):
    dil_refs = rest[:9]
    gate_ref = rest[9]
    stage_ref = rest[10]
    x = x_ref[...]
    u = x * lax.rsqrt(jnp.mean(x * x, axis=-1, keepdims=True) + RMS_EPS) * g_ref[...]
    u = u.astype(BF16)

    def mm(a, b):
        return jnp.dot(u, w_ref[:, a:b], preferred_element_type=F32)

    qa_ref[...] = mm(C_QA, C_KV).astype(BF16)
    c = mm(C_KV, C_QI)
    c = c * lax.rsqrt(jnp.mean(c * c, axis=-1, keepdims=True) + RMS_EPS) * kvg_ref[...]
    ckv_ref[...] = c.astype(BF16)
    qi_ref[...] = mm(C_QI, C_KW).astype(BF16)
    kw_ref[...] = mm(C_KW, C_DIL)
    for j in range(9):
        val = mm(C_DIL + 256 * j, C_DIL + 256 * (j + 1))
        r = DIL_GROUPS[j % 3][1]
        if r == 1:
            dil_refs[j][...] = val.astype(BF16)
        else:
            for hf in range(2):
                stage_ref[hf] = val[:, hf * LANES:(hf + 1) * LANES]
            for rho in range(r):
                for hf in range(2):
                    dil_refs[j][:, rho * 256 + hf * LANES:rho * 256 + (hf + 1) * LANES] = (
                        stage_ref[hf, pl.ds(rho, val.shape[0] // r, stride=r), :].astype(BF16))
    for j in range(4):
        gate_ref[:, 512 * j:512 * (j + 1)] = mm(C_GATE + 512 * j, C_GATE + 512 * (j + 1)).astype(BF16)


def _proj(x2, attn_norm, w_packed, kv_norm):
    n, d = x2.shape
    tm = PROJ_TM
    row = lambda i: (i, 0)
    const = lambda i: (0, 0)
    outs = [jax.ShapeDtypeStruct((n, 512), BF16), jax.ShapeDtypeStruct((n, 256), BF16),
            jax.ShapeDtypeStruct((n, 256), BF16), jax.ShapeDtypeStruct((n, 128), F32)]
    for j in range(9):
        r = DIL_GROUPS[j % 3][1]
        outs.append(jax.ShapeDtypeStruct((n // r, r * 256), BF16))
    outs += [jax.ShapeDtypeStruct((n, 2048), BF16)]
    out_specs = [pl.BlockSpec((tm * s.shape[0] // n, s.shape[1]), row) for s in outs]
    return pl.pallas_call(
        _proj_kernel,
        out_shape=outs,
        grid=(n // tm,),
        in_specs=[pl.BlockSpec((tm, d), row), pl.BlockSpec((1, d), const),
                  pl.BlockSpec((d, C_END), const), pl.BlockSpec((1, KV_LATENT), const)],
        out_specs=out_specs,
        scratch_shapes=[pltpu.VMEM((2, tm, LANES), F32)],
        compiler_params=_cparams(("parallel",)),
        name="proj",
    )(x2, attn_norm.reshape(1, d), w_packed, kv_norm.reshape(1, KV_LATENT))


def _bucket_thresholds():
    max_exact = NUM_BUCKETS // 2
    d = np.arange(0, MAX_DISTANCE + 1)
    nf = np.maximum(d, 1).astype(np.float32)
    large = max_exact + (np.log(nf / np.float32(max_exact)) / np.float32(math.log(MAX_DISTANCE / max_exact))
                         * np.float32(NUM_BUCKETS - max_exact)).astype(np.int32)
    large = np.minimum(large, NUM_BUCKETS - 1)
    bucket = np.where(d < max_exact, d, large)
    assert np.all(np.diff(bucket) >= 0)
    return [int(np.argmax(bucket >= b)) for b in range(1, NUM_BUCKETS)]


_BUCKET_THR = _bucket_thresholds()


def _bias_from_distance(dist, tab_ref, heads):
    masks = [dist >= t for t in _BUCKET_THR]
    out = []
    for h in heads:
        v = jnp.full(dist.shape, tab_ref[0, h], F32)
        for b in range(1, NUM_BUCKETS):
            v = jnp.where(masks[b - 1], tab_ref[b, h], v)
        out.append(v)
    return out


def _dsa_bias_kernel(tab_ref, o_ref):
    delta = pl.program_id(0)
    j = lax.broadcasted_iota(I32, (BLK, BLK), 0)
    i = lax.broadcasted_iota(I32, (BLK, BLK), 1)
    dist = jnp.maximum(delta * BLK + i - j, 0)
    tiles = _bias_from_distance(dist, tab_ref, range(DSA_HEADS))
    for h in range(DSA_HEADS):
        o_ref[0, h] = tiles[h] * LOG2E


def _dil_bias_kernel(tab_ref, o_ref, *, dilations):
    g = pl.program_id(0)
    j = lax.broadcasted_iota(I32, (2 * BLK, BLK), 0)
    i = lax.broadcasted_iota(I32, (2 * BLK, BLK), 1)
    step = i + BLK - j
    valid = (step >= 0) & (step <= BLK)
    for gi, r in enumerate(dilations):
        @pl.when(g == gi)
        def _():
            dist = jnp.maximum(step, 0) * r
            heads = [DSA_HEADS + gi * DIL_HPG + hh for hh in range(DIL_HPG)]
            tiles = _bias_from_distance(dist, tab_ref, heads)
            for hh in range(DIL_HPG):
                o_ref[0, :, hh * BLK:(hh + 1) * BLK] = jnp.where(valid, tiles[hh], NEG)


def _bias_tiles(rel_bias, nkc):
    smem = pl.BlockSpec(memory_space=pltpu.SMEM)
    dsa = pl.pallas_call(
        _dsa_bias_kernel,
        out_shape=jax.ShapeDtypeStruct((nkc, DSA_HEADS, BLK, BLK), F32),
        grid=(nkc,),
        in_specs=[smem],
        out_specs=pl.BlockSpec((1, DSA_HEADS, BLK, BLK), lambda d: (d, 0, 0, 0)),
        compiler_params=_cparams(("parallel",)),
        name="dsa_bias",
    )(rel_bias)
    dil = pl.pallas_call(
        functools.partial(_dil_bias_kernel, dilations=tuple(r for _, r in DIL_GROUPS)),
        out_shape=jax.ShapeDtypeStruct((len(DIL_GROUPS), 2 * BLK, DIL_HPG * BLK), F32),
        grid=(len(DIL_GROUPS),),
        in_specs=[smem],
        out_specs=pl.BlockSpec((1, 2 * BLK, DIL_HPG * BLK), lambda g: (g, 0, 0)),
        compiler_params=_cparams(("parallel",)),
        name="dil_bias",
    )(rel_bias)
    return dsa, dil


SUP = 2
SROWS = SUP * BLK


PLANE_KEYS = 32 * 8


def _bit_planes(words):
    x = list(words)
    j, m = 16, 0x0000FFFF
    while j:
        k = 0
        while k < 32:
            t = (x[k] ^ lax.shift_right_logical(x[k + j], jnp.int32(j))) & m
            x[k] = x[k] ^ t
            x[k + j] = x[k + j] ^ jnp.left_shift(t, jnp.int32(j))
            k = (k + j + 1) & ~j
        j >>= 1
        m = (m ^ (m << j)) & 0xFFFFFFFF
        m = m - (1 << 32) if m >= (1 << 31) else m
    return x


def _dsa_kernel(qiT_ref, wiT_ref, qaT_ref, kidx_ref, ckv_ref, ckvT_ref, wuk_ref, wuv_ref, bias_ref, tri_ref,
                y_ref, sc_ref, planes_ref, qlT_ref, x_ref, pT_ref, *, topk, nsc, qb0):
    qb = qb0 + pl.program_id(1)
    row = lax.broadcasted_iota(I32, (SROWS, BLK), 0)
    col = lax.broadcasted_iota(I32, (SROWS, BLK), 1)
    trips = [slice(sc * SROWS, (sc + 1) * SROWS) for sc in range(nsc)]

    def causal(sc):
        return row <= col + (qb * BLK - sc * SROWS)

    for h in range(DSA_HEADS):
        ql = jnp.dot(wuk_ref[h], qaT_ref[0, h * HEAD_DIM:(h + 1) * HEAD_DIM, :],
                     preferred_element_type=F32) * (HEAD_DIM ** -0.5 * LOG2E)
        qlT_ref[:, h * BLK:(h + 1) * BLK] = ql.astype(BF16)

    wq = wiT_ref[0] * (IDX_HEADS ** -0.5)
    for sc, ts in enumerate(trips):
        kx = kidx_ref[0, ts, :]
        acc = jnp.zeros((SROWS, BLK), F32)
        for h in range(IDX_HEADS):
            s = jnp.dot(kx, qiT_ref[0, h * IDX_DIM:(h + 1) * IDX_DIM, :],
                        preferred_element_type=F32) * (IDX_DIM ** -0.5)
            acc = acc + wq[h:h + 1, :] * jnp.maximum(s, 0.0)
        acc = jnp.where(causal(sc), acc, NEG)
        sc_ref[ts, :] = acc
        bits = pltpu.bitcast(acc, I32)
        bits = jnp.where(bits == INT_MIN, 0, bits)
        ukey = bits ^ ((bits >> 31) & 0x7FFFFFFF) ^ INT_MIN
        for grp in range(SROWS // PLANE_KEYS):
            tiles = [ukey[grp * PLANE_KEYS + j * 8:grp * PLANE_KEYS + (j + 1) * 8] for j in range(32)]
            for b, plane in enumerate(_bit_planes(tiles)):
                planes_ref[b, sc * (SROWS // PLANE_KEYS) + grp] = plane

    def count(pred):
        cnt = jnp.zeros((8, BLK), I32)
        for ts in trips:
            cnt = cnt + jnp.sum(jnp.where(pred(sc_ref[ts, :]), 1, 0).reshape(SROWS // 8, 8, BLK), axis=0)
        return jnp.sum(cnt, axis=0, keepdims=True)

    def as_float(key):
        return pltpu.bitcast(key ^ ((key >> 31) & 0x7FFFFFFF), F32)

    def bit_body(it, carry):
        alive, above, code = carry
        ones = alive & planes_ref[it]
        c = jnp.sum(jnp.sum(lax.population_count(ones), axis=0), axis=0, keepdims=True)
        take = above + c >= topk
        alive = jnp.where(take, ones, alive ^ ones)
        above = jnp.where(take, above, above + c)
        code = code | jnp.where(take, jnp.left_shift(jnp.int32(1), 31 - it), 0)
        return alive, above, code

    nw = nsc * (SROWS // PLANE_KEYS)
    _, _, code = lax.fori_loop(
        0, 32, bit_body,
        (jnp.full((nw, 8, BLK), -1, I32), jnp.zeros((1, BLK), I32), jnp.zeros((1, BLK), I32)))
    guess = as_float(code ^ INT_MIN)
    n_gt_guess = count(lambda v: v > guess)
    proven = (n_gt_guess < topk) & (count(lambda v: v >= guess) >= topk)

    def bisect():
        def thr_body(it, lo):
            cand = lo + jnp.left_shift(jnp.int32(1), 31 - it)
            cand_f = as_float(cand)
            return jnp.where(count(lambda v: v >= cand_f) >= topk, cand, lo)
        t = as_float(lax.fori_loop(0, 32, thr_body, jnp.full((1, BLK), INT_MIN, I32)))
        return t, count(lambda v: v > t)

    thr, n_gt = lax.cond(jnp.min(jnp.where(proven, 1, 0)) > 0, lambda: (guess, n_gt_guess), bisect)
    ties_wanted = (topk - n_gt).astype(F32)

    m = [jnp.full((8, BLK), NEG, F32) for _ in range(DSA_HEADS)]
    ties_before = jnp.zeros((1, BLK), F32)
    for sc, ts in enumerate(trips):
        k = sc_ref[ts, :]
        tie = k == thr
        tie_rank = jnp.dot(tri_ref[...], jnp.where(tie, 1.0, 0.0).astype(BF16),
                           preferred_element_type=F32) + ties_before
        ties_before = tie_rank[SROWS - 1:SROWS, :]
        sel = ((k > thr) | (tie & (tie_rank <= ties_wanted))) & causal(sc)
        am = jnp.where(sel, 0.0, NEG)
        ck = ckv_ref[0, ts, :]
        for hp in range(DSA_HEADS // 2):
            lg2 = jnp.dot(ck, qlT_ref[:, 2 * hp * BLK:(2 * hp + 2) * BLK], preferred_element_type=F32)
            for h in (2 * hp, 2 * hp + 1):
                lg = lg2[:, (h % 2) * BLK:(h % 2 + 1) * BLK]
                for j in range(SUP):
                    rs = slice(j * BLK, (j + 1) * BLK)
                    delta = jnp.maximum(qb - (sc * SUP + j), 0)
                    x = lg[rs] + bias_ref[delta, h] + am[rs]
                    x_ref[sc * SROWS + j * BLK:sc * SROWS + (j + 1) * BLK, h * BLK:(h + 1) * BLK] = x
                    m[h] = jnp.maximum(m[h], jnp.max(x.reshape(BLK // 8, 8, BLK), axis=0))
    m = [jnp.max(v, axis=0, keepdims=True) for v in m]

    l = [jnp.zeros((8, BLK), F32) for _ in range(DSA_HEADS)]
    for sc, ts in enumerate(trips):
        for h in range(DSA_HEADS):
            hs = slice(h * BLK, (h + 1) * BLK)
            p = jnp.exp2(x_ref[ts, hs] - m[h])
            pT_ref[ts, hs] = p.astype(BF16)
            l[h] = l[h] + jnp.sum(p.reshape(SROWS // 8, 8, BLK), axis=0)

    for hp in range(DSA_HEADS // 2):
        o2 = jnp.dot(ckvT_ref[0], pT_ref[:, 2 * hp * BLK:(2 * hp + 2) * BLK],
                     preferred_element_type=F32)
        for h in (2 * hp, 2 * hp + 1):
            inv = 1.0 / jnp.sum(l[h], axis=0, keepdims=True)
            oh = (o2[:, (h % 2) * BLK:(h % 2 + 1) * BLK] * inv).T.astype(BF16)
            yh = jnp.dot(oh, wuv_ref[h], preferred_element_type=F32)
            y_ref[0, :, h * HEAD_DIM:(h + 1) * HEAD_DIM] = yh.astype(BF16)


def _dsa_group(g, qiT, wiT, qaT, kidx, ckv, ckvT, wuk, wuv, bias_tiles, tri):
    b, seq, _ = ckv.shape
    nkc = seq // BLK
    nsc = g + 1
    nk = nsc * SROWS
    topk = min(TOPK_MAX, seq // 4)
    qblk = lambda rows: pl.BlockSpec((1, rows, BLK), lambda bi, qi: (bi, 0, g * SUP + qi))
    head3 = lambda cols: pl.BlockSpec((1, nk, cols), lambda bi, qi: (bi, 0, 0))
    const = lambda s: pl.BlockSpec(s, lambda bi, qi: (0,) * len(s))
    hl = DSA_HEADS * BLK
    return pl.pallas_call(
        functools.partial(_dsa_kernel, topk=topk, nsc=nsc, qb0=g * SUP),
        out_shape=jax.ShapeDtypeStruct((b, SROWS, DSA_HEADS * HEAD_DIM), BF16),
        grid=(b, SUP),
        in_specs=[qblk(IDX_HEADS * IDX_DIM), qblk(8), qblk(DSA_HEADS * HEAD_DIM),
                  head3(IDX_DIM), head3(KV_LATENT),
                  pl.BlockSpec((1, KV_LATENT, nk), lambda bi, qi: (bi, 0, 0)),
                  const((DSA_HEADS, KV_LATENT, HEAD_DIM)), const((DSA_HEADS, KV_LATENT, HEAD_DIM)),
                  const((nkc, DSA_HEADS, BLK, BLK)), const((SROWS, SROWS))],
        out_specs=pl.BlockSpec((1, BLK, DSA_HEADS * HEAD_DIM), lambda bi, qi: (bi, qi, 0)),
        scratch_shapes=[pltpu.VMEM((nk, BLK), F32),
                        pltpu.VMEM((32, nk // PLANE_KEYS, 8, BLK), I32),
                        pltpu.VMEM((KV_LATENT, hl), BF16),
                        pltpu.VMEM((nk, hl), F32),
                        pltpu.VMEM((nk, hl), BF16)],
        compiler_params=_cparams(("parallel", "arbitrary")),
        name=f"dsa_g{g}",
    )(qiT, wiT, qaT, kidx, ckv, ckvT, wuk, wuv, bias_tiles, tri)


def _dsa(qiT, wiT, qaT, kidx, ckv, ckvT, wuk, wuv, bias_tiles):
    seq = ckv.shape[1]
    assert seq % SROWS == 0 and seq >= 4 * TOPK_MAX
    tri = jnp.tril(jnp.ones((SROWS, SROWS), BF16))
    groups = [_dsa_group(g, qiT, wiT, qaT, kidx, ckv, ckvT, wuk, wuv, bias_tiles, tri)
              for g in range(seq // SROWS)]
    return jnp.concatenate(groups, axis=1)


DIL_UNROLL = 5
DIL_STEP_ROWS = 2048


def _dil_kernel(q_ref, k_ref, v_ref, bm_ref, o_ref, lse_ref, vT_ref, *, nblk, nres):
    hq = DIL_HPG * BLK
    rowh = lax.broadcasted_iota(I32, (hq, DIL_OUT), 0) // BLK
    colh = lax.broadcasted_iota(I32, (hq, DIL_OUT), 1) // HEAD_DIM
    same_head = rowh == colh

    for res in range(nres):
        cs = slice(res * DIL_OUT, (res + 1) * DIL_OUT)

        for n in range(nblk):
            vT_ref[n] = v_ref[0, n * BLK:(n + 1) * BLK, cs].astype(F32).T.astype(BF16)

        def block(qo, kw, vT, bm, cs=cs):
            q = q_ref[0, pl.ds(qo, BLK), cs]
            qd = jnp.where(same_head, jnp.concatenate([q] * DIL_HPG, axis=0), jnp.zeros((), BF16))
            s = lax.dot_general(kw, qd, (((1,), (1,)), ((), ())), preferred_element_type=F32)
            s = s * (HEAD_DIM ** -0.5) + bm
            m = jnp.max(s, axis=0, keepdims=True)
            e = jnp.exp(s - m)
            l = jnp.sum(e, axis=0, keepdims=True)
            oT = jnp.dot(vT, e.astype(BF16), preferred_element_type=F32)
            inv = 1.0 / l
            lse = m + jnp.log(l)
            outs, lses = [], []
            for hh in range(DIL_HPG):
                qs = slice(hh * BLK, (hh + 1) * BLK)
                outs.append(oT[hh * HEAD_DIM:(hh + 1) * HEAD_DIM, qs] * inv[:, qs])
                lses.append(jnp.broadcast_to(lse[:, qs], (HEAD_DIM, BLK)))
            o_ref[0, pl.ds(qo, BLK), cs] = jnp.concatenate(outs, axis=0).T.astype(BF16)
            lse_ref[0, pl.ds(qo, BLK), cs] = jnp.concatenate(lses, axis=0).T

        block(0, k_ref[0, 0:BLK, cs], vT_ref[0], bm_ref[0, BLK:, :])

        def body(n, carry, cs=cs, block=block):
            ko = pl.multiple_of((n - 1) * BLK, BLK)
            vT = jnp.concatenate([vT_ref[n - 1], vT_ref[n]], axis=1)
            block(pl.multiple_of(n * BLK, BLK), k_ref[0, pl.ds(ko, 2 * BLK), cs], vT, bm_ref[0])
            return carry

        lax.fori_loop(1, nblk, body, 0, unroll=DIL_UNROLL)


def _dilated_group(q, k, v, bm, g, dilation, b):
    c = DIL_OUT
    ls = q.shape[0] // b
    nblk = ls // BLK
    nres = max(1, min(dilation, DIL_STEP_ROWS // ls))
    view = lambda a: a.reshape(b, ls, dilation * c)
    blk = pl.BlockSpec((1, ls, nres * c), lambda bi, ri: (bi, 0, ri))
    o, lse = pl.pallas_call(
        functools.partial(_dil_kernel, nblk=nblk, nres=nres),
        out_shape=[jax.ShapeDtypeStruct((b, ls, dilation * c), BF16),
                   jax.ShapeDtypeStruct((b, ls, dilation * c), F32)],
        grid=(b, dilation // nres),
        in_specs=[blk, blk, blk, pl.BlockSpec((1, 2 * BLK, DIL_HPG * BLK), lambda bi, ri: (g, 0, 0))],
        out_specs=[blk, blk],
        scratch_shapes=[pltpu.VMEM((nblk, c, BLK), BF16)],
        compiler_params=_cparams(("parallel", "parallel")),
        name=f"dilated_g{g}",
    )(view(q), view(k), view(v), bm)
    return o.reshape(b * ls, dilation * c), lse.reshape(b * ls, dilation * c)


def _store_row_tiles(ref, base, val):
    rows, d = val.shape
    dt = d // LANES
    for s in range(dt):
        ref[pl.ds(base * dt + s, rows, stride=dt), :] = val[:, s * LANES:(s + 1) * LANES]


def _load_row_tiles(ref, base, rows, dt):
    return jnp.concatenate([ref[pl.ds(base * dt + s, rows, stride=dt), :] for s in range(dt)], axis=1)


MIX_COLS = 256


def _token_major(ref, stage_ref, r):
    if r == 1:
        return ref[...].astype(F32)
    rows = ref.shape[0]
    for rho in range(r):
        for hf in range(DIL_OUT // LANES):
            c0 = rho * DIL_OUT + hf * LANES
            stage_ref[hf, pl.ds(rho, rows, stride=r), :] = ref[:, c0:c0 + LANES].astype(F32)
    return jnp.concatenate([stage_ref[hf] for hf in range(DIL_OUT // LANES)], axis=1)


def _sigmoid(v):
    return 0.5 * jnp.tanh(0.5 * v) + 0.5


def _mix_kernel(x_ref, ya_ref, o1_ref, o2_ref, o3_ref, l1_ref, l2_ref, l3_ref, gate_ref,
                wa_ref, wb_ref, wo_ref, fg_ref, wr_ref, br_ref,
                h_ref, t_ref, rw_ref, ri_ref, mixed_ref, *stage_refs):
    dils = [r for _, r in DIL_GROUPS]
    o = [_token_major(ref, st, r) for ref, st, r in zip((o1_ref, o2_ref, o3_ref), stage_refs[:3], dils)]
    l1, l2, l3 = [_token_major(ref, st, r) for ref, st, r in zip((l1_ref, l2_ref, l3_ref), stage_refs[3:], dils)]
    mx = jnp.maximum(jnp.maximum(l1, l2), l3)
    e1, e2, e3 = jnp.exp(l1 - mx), jnp.exp(l2 - mx), jnp.exp(l3 - mx)
    inv = 1.0 / (e1 + e2 + e3)
    yb = ((e1 * inv) * o[0] + (e2 * inv) * o[1] + (e3 * inv) * o[2]).astype(BF16)
    ya = ya_ref[...]
    d = x_ref.shape[1]
    for c in range(0, d, MIX_COLS):
        cs = slice(c, c + MIX_COLS)
        a = jnp.dot(ya, wa_ref[:, cs], preferred_element_type=F32)
        bmix = jnp.dot(yb, wb_ref[:, cs], preferred_element_type=F32)
        g0 = _sigmoid(gate_ref[:, cs].astype(F32))
        g1 = _sigmoid(gate_ref[:, d + c:d + c + MIX_COLS].astype(F32))
        mixed_ref[:, cs] = (g0 * a + g1 * bmix).astype(BF16)
    h = x_ref[...] + jnp.dot(mixed_ref[...], wo_ref[...], preferred_element_type=F32)
    h_ref[...] = h
    t = h * lax.rsqrt(jnp.mean(h * h, axis=-1, keepdims=True) + RMS_EPS) * fg_ref[...]
    _store_row_tiles(t_ref, 0, t)

    t_hi = t.astype(BF16)
    t_lo = (t - t_hi.astype(F32)).astype(BF16)
    r1 = jnp.dot(t_hi, wr_ref[...], preferred_element_type=F32)
    r2 = jnp.dot(t_lo, wr_ref[:, :ROUTE_COLS], preferred_element_type=F32)
    logits = r1[:, :ROUTE_COLS] + (r1[:, ROUTE_COLS:] + r2) + br_ref[...]
    lane = lax.broadcasted_iota(I32, logits.shape, 1)
    ninf = -jnp.inf
    big = jnp.int32(10 ** 6)

    def first_argmax(v, vmax):
        return jnp.min(jnp.where(v == vmax, lane, big), axis=-1, keepdims=True)

    gl = jnp.where(lane < N_GROUPS, logits, ninf)
    gmax = jnp.max(gl, axis=-1, keepdims=True)
    gsel = first_argmax(gl, gmax)
    p_g = 1.0 / jnp.sum(jnp.exp(gl - gmax), axis=-1, keepdims=True)
    lo = N_GROUPS + gsel * EXPERTS_PER_GROUP
    el = jnp.where((lane >= lo) & (lane < lo + EXPERTS_PER_GROUP), logits, ninf)
    v1 = jnp.max(el, axis=-1, keepdims=True)
    i1 = first_argmax(el, v1)
    el2 = jnp.where(lane == i1, ninf, el)
    v2 = jnp.max(el2, axis=-1, keepdims=True)
    i2 = first_argmax(el2, v2)
    e2 = jnp.exp(v2 - v1)
    w1 = p_g / (1.0 + e2)
    w2 = p_g * e2 / (1.0 + e2)
    rw_ref[...] = jnp.where(lane == 0, w1, jnp.where(lane == 1, w2, 0.0))
    ri_ref[...] = jnp.where(lane == 0, i1 - N_GROUPS, jnp.where(lane == 1, i2 - N_GROUPS, 0))


def _mix(x2, ya, os_, lses, gates, wa, wb, wo, ffn_norm, wr, br):
    n, d = x2.shape
    tm = MIX_TM
    row = lambda c: pl.BlockSpec((tm, c), lambda i: (i, 0))
    res = lambda a: pl.BlockSpec((tm * a.shape[0] // n, a.shape[1]), lambda i: (i, 0))
    const = lambda s: pl.BlockSpec(s, lambda i: (0, 0))
    return pl.pallas_call(
        _mix_kernel,
        out_shape=[jax.ShapeDtypeStruct((n, d), F32), jax.ShapeDtypeStruct((n * (d // LANES), LANES), F32),
                   jax.ShapeDtypeStruct((n, ROUTE_COLS), F32), jax.ShapeDtypeStruct((n, ROUTE_COLS), I32)],
        grid=(n // tm,),
        in_specs=[row(d), row(512)] + [res(a) for a in os_] + [res(a) for a in lses] + [row(2 * d),
                  const(wa.shape), const(wb.shape), const(wo.shape), const((1, d)),
                  const(wr.shape), const((1, ROUTE_COLS))],
        out_specs=[row(d), pl.BlockSpec((tm * (d // LANES), LANES), lambda i: (i, 0)),
                   row(ROUTE_COLS), row(ROUTE_COLS)],
        scratch_shapes=[pltpu.VMEM((tm, d), BF16)] + [pltpu.VMEM((DIL_OUT // LANES, tm, LANES), F32)] * 6,
        compiler_params=_cparams(("parallel",)),
        name="mix",
    )(x2, ya, *os_, *lses, gates, wa, wb, wo, ffn_norm.reshape(1, d), wr, br)


GATHER_UNROLL = 8


def _start_row_gather(src_hbm, idx_ref, nrows, dt, buf, sem, slot):
    def body(g, c):
        for u in range(GATHER_UNROLL):
            r = g * GATHER_UNROLL + u
            src = pl.multiple_of(idx_ref[0, 0, r] * dt, dt)
            dst = pl.multiple_of((slot * nrows + r) * dt, dt)
            pltpu.make_async_copy(src_hbm.at[pl.ds(src, dt)], buf.at[pl.ds(dst, dt)],
                                  sem.at[slot]).start(priority=u % 2)
        return c
    lax.fori_loop(0, nrows // GATHER_UNROLL, body, 0)


def _wait_row_gather(src_hbm, nrows, dt, buf, sem, slot):
    dst = pl.multiple_of(slot * nrows * dt, dt)
    pltpu.make_async_copy(src_hbm.at[pl.ds(0, nrows * dt)], buf.at[pl.ds(dst, nrows * dt)], sem.at[slot]).wait()


def _gather_pipeline(i, nsteps, src_hbm, cur_ref, nxt_ref, nrows, dt, buf, sem):
    slot = i % 2

    @pl.when(i == 0)
    def _():
        _start_row_gather(src_hbm, cur_ref, nrows, dt, buf, sem, 0)

    @pl.when(i + 1 < nsteps)
    def _():
        _start_row_gather(src_hbm, nxt_ref, nrows, dt, buf, sem, 1 - slot)

    _wait_row_gather(src_hbm, nrows, dt, buf, sem, slot)
    return slot


def _expert_kernel(te_ref, cur_ref, nxt_ref, t_hbm, wg_ref, wu_ref, wd_ref, y_ref, buf, sem):
    i = pl.program_id(0)
    dt = wg_ref.shape[1] // LANES
    slot = _gather_pipeline(i, pl.num_programs(0), t_hbm, cur_ref, nxt_ref, MOE_TM, dt, buf, sem)
    xt = _load_row_tiles(buf, slot * MOE_TM, MOE_TM, dt).astype(BF16)
    hg = jnp.dot(xt, wg_ref[0], preferred_element_type=F32)
    hu = jnp.dot(xt, wu_ref[0], preferred_element_type=F32)
    hid = (hg * jax.nn.sigmoid(hg)) * hu
    _store_row_tiles(y_ref, 0, jnp.dot(hid.astype(BF16), wd_ref[0], preferred_element_type=F32))


def _experts(tile_expert, row_token, t, wg, wu, wd, d):
    dt = d // LANES
    ntiles = tile_expert.shape[0]
    ff = wg.shape[2]
    tok3 = row_token.reshape(ntiles, 1, MOE_TM)
    smem_cur = pl.BlockSpec((1, 1, MOE_TM), lambda i, te: (i, 0, 0), memory_space=pltpu.SMEM)
    smem_nxt = pl.BlockSpec((1, 1, MOE_TM), lambda i, te: (jnp.minimum(i + 1, ntiles - 1), 0, 0),
                            memory_space=pltpu.SMEM)
    wspec = lambda s: pl.BlockSpec((1,) + s, lambda i, te: (te[i], 0, 0))
    return pl.pallas_call(
        _expert_kernel,
        out_shape=jax.ShapeDtypeStruct((ntiles * MOE_TM * dt, LANES), F32),
        grid_spec=pltpu.PrefetchScalarGridSpec(
            num_scalar_prefetch=1,
            grid=(ntiles,),
            in_specs=[smem_cur, smem_nxt, pl.BlockSpec(memory_space=pl.ANY),
                      wspec((d, ff)), wspec((d, ff)), wspec((ff, d))],
            out_specs=pl.BlockSpec((MOE_TM * dt, LANES), lambda i, te: (i, 0)),
            scratch_shapes=[pltpu.VMEM((2 * MOE_TM * dt, LANES), F32), pltpu.SemaphoreType.DMA((2,))],
        ),
        compiler_params=_cparams(("arbitrary",)),
        name="experts",
    )(tile_expert, tok3, tok3, t, wg, wu, wd)


def _final_kernel(cur_ref, nxt_ref, y_hbm, h_ref, rw_ref, fn_ref, o_ref, buf, sem):
    i = pl.program_id(0)
    dt = h_ref.shape[1] // LANES
    slot = _gather_pipeline(i, pl.num_programs(0), y_hbm, cur_ref, nxt_ref, 2 * FIN_TM, dt, buf, sem)
    rw = rw_ref[...]
    y0 = _load_row_tiles(buf, slot * 2 * FIN_TM, FIN_TM, dt)
    y1 = _load_row_tiles(buf, slot * 2 * FIN_TM + FIN_TM, FIN_TM, dt)
    h = h_ref[...] + rw[:, 0:1] * y0 + rw[:, 1:2] * y1
    o_ref[...] = h * lax.rsqrt(jnp.mean(h * h, axis=-1, keepdims=True) + RMS_EPS) * fn_ref[...]


def _final(pos_tiles, y_sorted, h, rw, final_norm):
    n, d = h.shape
    tm = FIN_TM
    nt = n // tm
    smem_cur = pl.BlockSpec((1, 1, 2 * tm), lambda i: (i, 0, 0), memory_space=pltpu.SMEM)
    smem_nxt = pl.BlockSpec((1, 1, 2 * tm), lambda i: (jnp.minimum(i + 1, nt - 1), 0, 0),
                            memory_space=pltpu.SMEM)
    return pl.pallas_call(
        _final_kernel,
        out_shape=jax.ShapeDtypeStruct((n, d), F32),
        grid=(nt,),
        in_specs=[smem_cur, smem_nxt, pl.BlockSpec(memory_space=pl.ANY),
                  pl.BlockSpec((tm, d), lambda i: (i, 0)), pl.BlockSpec((tm, ROUTE_COLS), lambda i: (i, 0)),
                  pl.BlockSpec((1, d), lambda i: (0, 0))],
        out_specs=pl.BlockSpec((tm, d), lambda i: (i, 0)),
        scratch_shapes=[pltpu.VMEM((2 * 2 * tm * (d // LANES), LANES), F32), pltpu.SemaphoreType.DMA((2,))],
        compiler_params=_cparams(("arbitrary",)),
        name="final",
    )(pos_tiles, pos_tiles, y_sorted, h, rw, final_norm.reshape(1, d))


def _route_plan(gid):
    n = gid.shape[0]
    e = gid.reshape(-1)
    onehot = (e[:, None] == jnp.arange(N_EXPERTS, dtype=I32)[None, :]).astype(F32)
    chunk = 256
    oh3 = onehot.reshape(-1, chunk, N_EXPERTS)
    within = jnp.einsum("ij,tjk->tik", jnp.tril(jnp.ones((chunk, chunk), F32)), oh3)
    totals = within[:, -1, :]
    before = jnp.cumsum(totals, axis=0) - totals
    csum = (within + before[:, None, :]).reshape(-1, N_EXPERTS)
    rank = jnp.sum(csum * onehot, axis=1).astype(I32) - 1
    counts = (before[-1] + totals[-1]).astype(I32)
    padded = ((counts + MOE_TM - 1) // MOE_TM) * MOE_TM
    seg_end = jnp.cumsum(padded)
    pos = (seg_end - padded)[e] + rank
    nrows = 2 * n + N_EXPERTS * MOE_TM
    tile_start = jnp.arange(nrows // MOE_TM, dtype=I32) * MOE_TM
    tile_expert = jnp.minimum(jnp.sum(tile_start[:, None] >= seg_end[None, :], axis=1), N_EXPERTS - 1).astype(I32)
    order = jnp.argsort(e, stable=True).astype(I32)
    first = jnp.cumsum(counts) - counts
    row_expert = jnp.repeat(tile_expert, MOE_TM)
    src = jnp.arange(nrows, dtype=I32) - (seg_end - padded)[row_expert] + first[row_expert]
    row_token = order[jnp.clip(src, 0, 2 * n - 1)] // 2
    return row_token, tile_expert, pos.reshape(n, 2)


def _pack_w_in(w):
    d = w.shape[0]
    o_kv, o_qi, o_ki, o_wi = 512, 768, 1024, 1088
    o_dil = o_wi + IDX_HEADS
    o_gate = o_dil + 9 * 256
    pad = jnp.zeros((d, LANES - IDX_DIM - IDX_HEADS), w.dtype)
    packed = jnp.concatenate([w[:, :o_ki], w[:, o_ki:o_wi], w[:, o_wi:o_dil], pad, w[:, o_dil:o_gate],
                              w[:, o_gate:]], axis=1)
    assert packed.shape[1] == C_END
    return packed.astype(BF16)


def kernel(x, attn_norm, w_in, kv_norm, w_uk, w_uv, rel_bias, w_branch_a, w_branch_b, w_out, ffn_norm,
           w_router_group, b_router_group, w_router_expert, b_router_expert, w_gate, w_up, w_down,
           final_norm):
    b, seq, d = x.shape
    n = b * seq
    nkc = seq // BLK
    assert w_in.shape[0] == 1, "one layer"
    x2 = x.reshape(n, d)

    outs = _proj(x2, attn_norm[0], _pack_w_in(w_in[0]), kv_norm[0])
    qa, ckv, qi, kw = outs[:4]
    dil = outs[4:13]
    gates = outs[13]

    dsa_bias, dil_bias = _bias_tiles(rel_bias, nkc)

    t3 = lambda a: jnp.swapaxes(a.reshape(b, seq, a.shape[-1]), 1, 2)
    kw3 = kw.reshape(b, seq, LANES)
    wiT = jnp.swapaxes(kw3[:, :, IDX_DIM:IDX_DIM + 8], 1, 2)
    kidx = kw3[:, :, :IDX_DIM].astype(BF16)
    ckv3 = ckv.reshape(b, seq, KV_LATENT)
    ckvT = jnp.swapaxes(ckv3, 1, 2)
    ya = _dsa(t3(qi), wiT, t3(qa), kidx, ckv3, ckvT, w_uk[0].astype(BF16), w_uv[0].astype(BF16), dsa_bias)

    os_, lses = [], []
    for g, (_, dilation) in enumerate(DIL_GROUPS):
        o, lse = _dilated_group(dil[g], dil[3 + g], dil[6 + g], dil_bias, g, dilation, b)
        os_.append(o)
        lses.append(lse)

    wr = jnp.concatenate([w_router_group[0],
                          jnp.swapaxes(w_router_expert[0], 0, 1).reshape(d, N_EXPERTS),
                          jnp.zeros((d, ROUTE_COLS - N_GROUPS - N_EXPERTS), F32)], axis=1)
    wr_hi = wr.astype(BF16)
    wr = jnp.concatenate([wr_hi, (wr - wr_hi.astype(F32)).astype(BF16)], axis=1)
    br = jnp.concatenate([b_router_group[0], b_router_expert[0].reshape(-1),
                          jnp.zeros((ROUTE_COLS - N_GROUPS - N_EXPERTS,), F32)]).reshape(1, ROUTE_COLS)
    h, t, rw, ri = _mix(x2, ya.reshape(n, -1), os_, lses, gates,
                        w_branch_a[0].astype(BF16), w_branch_b[0].astype(BF16), w_out[0].astype(BF16),
                        ffn_norm[0], wr, br)

    row_token, tile_expert, pos = _route_plan(ri[:, :2])
    y_sorted = _experts(tile_expert, row_token, t, w_gate[0].astype(BF16), w_up[0].astype(BF16),
                        w_down[0].astype(BF16), d)

    pos_tiles = jnp.swapaxes(pos.reshape(n // FIN_TM, FIN_TM, 2), 1, 2).reshape(n // FIN_TM, 1, 2 * FIN_TM)
    out = _final(pos_tiles, y_sorted, h, rw, final_norm)
    return out.reshape(b, seq, d)
```

```python
import functools
import math

import numpy as np
import jax
import jax.numpy as jnp
from jax import lax
from jax.experimental import pallas as pl
from jax.experimental.pallas import tpu as pltpu

F32 = jnp.float32
BF16 = jnp.bfloat16
I32 = jnp.int32

LANES = 128
VMEM_LIMIT_BYTES = 56 * 1024 * 1024

HEAD_DIM = 64
DSA_HEADS = 8
KV_LATENT = 256
IDX_HEADS = 4
IDX_DIM = 64
TOPK_MAX = 256
DIL_GROUPS = ((128, 1), (512, 4), (2048, 16))
DIL_HPG = 4
DIL_OUT = DIL_HPG * HEAD_DIM
NUM_BUCKETS = 32
MAX_DISTANCE = 2048
N_GROUPS = 4
EXPERTS_PER_GROUP = 8
N_EXPERTS = N_GROUPS * EXPERTS_PER_GROUP
RMS_EPS = 1e-6
NEG = -1e30
LOG2E = math.log2(math.e)
INT_MIN = -2 ** 31

BLK = 128
PROJ_TM = 512
MIX_TM = 512
MOE_TM = 256
FIN_TM = 256
ROUTE_COLS = 128


def _cparams(sem):
    return pltpu.CompilerParams(dimension_semantics=sem, vmem_limit_bytes=VMEM_LIMIT_BYTES)


C_QA = 0
C_KV = 512
C_QI = 768
C_KW = 1024
C_DIL = 1152
C_GATE = C_DIL + 9 * 256
C_END = C_GATE + 2048


def _proj_kernel(x_ref, g_ref, w_ref, kvg_ref, qa_ref, ckv_ref, qi_ref, kw_ref, *rest):
    dil_refs = rest[:9]
    gate_ref = rest[9]
    stage_ref = rest[10]
    x = x_ref[...]
    u = x * lax.rsqrt(jnp.mean(x * x, axis=-1, keepdims=True) + RMS_EPS) * g_ref[...]
    u = u.astype(BF16)

    def mm(a, b):
        return jnp.dot(u, w_ref[:, a:b], preferred_element_type=F32)

    qa_ref[...] = mm(C_QA, C_KV).astype(BF16)
    c = mm(C_KV, C_QI)
    c = c * lax.rsqrt(jnp.mean(c * c, axis=-1, keepdims=True) + RMS_EPS) * kvg_ref[...]
    ckv_ref[...] = c.astype(BF16)
    qi_ref[...] = mm(C_QI, C_KW).astype(BF16)
    kw_ref[...] = mm(C_KW, C_DIL)
    for j in range(9):
        val = mm(C_DIL + 256 * j, C_DIL + 256 * (j + 1))
        r = DIL_GROUPS[j % 3][1]
        if r == 1:
            dil_refs[j][...] = val.astype(BF16)
        else:
            for hf in range(2):
                stage_ref[hf] = val[:, hf * LANES:(hf + 1) * LANES]
            for rho in range(r):
                for hf in range(2):
                    dil_refs[j][:, rho * 256 + hf * LANES:rho * 256 + (hf + 1) * LANES] = (
                        stage_ref[hf, pl.ds(rho, val.shape[0] // r, stride=r), :].astype(BF16))
    for j in range(4):
        gate_ref[:, 512 * j:512 * (j + 1)] = mm(C_GATE + 512 * j, C_GATE + 512 * (j + 1)).astype(BF16)


def _proj(x2, attn_norm, w_packed, kv_norm):
    n, d = x2.shape
    tm = PROJ_TM
    row = lambda i: (i, 0)
    const = lambda i: (0, 0)
    outs = [jax.ShapeDtypeStruct((n, 512), BF16), jax.ShapeDtypeStruct((n, 256), BF16),
            jax.ShapeDtypeStruct((n, 256), BF16), jax.ShapeDtypeStruct((n, 128), F32)]
    for j in range(9):
        r = DIL_GROUPS[j % 3][1]
        outs.append(jax.ShapeDtypeStruct((n // r, r * 256), BF16))
    outs += [jax.ShapeDtypeStruct((n, 2048), BF16)]
    out_specs = [pl.BlockSpec((tm * s.shape[0] // n, s.shape[1]), row) for s in outs]
    return pl.pallas_call(
        _proj_kernel,
        out_shape=outs,
        grid=(n // tm,),
        in_specs=[pl.BlockSpec((tm, d), row), pl.BlockSpec((1, d), const),
                  pl.BlockSpec((d, C_END), const), pl.BlockSpec((1, KV_LATENT), const)],
        out_specs=out_specs,
        scratch_shapes=[pltpu.VMEM((2, tm, LANES), F32)],
        compiler_params=_cparams(("parallel",)),
        name="proj",
    )(x2, attn_norm.reshape(1, d), w_packed, kv_norm.reshape(1, KV_LATENT))


def _bucket_thresholds():
    max_exact = NUM_BUCKETS // 2
    d = np.arange(0, MAX_DISTANCE + 1)
    nf = np.maximum(d, 1).astype(np.float32)
    large = max_exact + (np.log(nf / np.float32(max_exact)) / np.float32(math.log(MAX_DISTANCE / max_exact))
                         * np.float32(NUM_BUCKETS - max_exact)).astype(np.int32)
    large = np.minimum(large, NUM_BUCKETS - 1)
    bucket = np.where(d < max_exact, d, large)
    assert np.all(np.diff(bucket) >= 0)
    return [int(np.argmax(bucket >= b)) for b in range(1, NUM_BUCKETS)]


_BUCKET_THR = _bucket_thresholds()


def _bias_from_distance(dist, tab_ref, heads):
    masks = [dist >= t for t in _BUCKET_THR]
    out = []
    for h in heads:
        v = jnp.full(dist.shape, tab_ref[0, h], F32)
        for b in range(1, NUM_BUCKETS):
            v = jnp.where(masks[b - 1], tab_ref[b, h], v)
        out.append(v)
    return out


def _dsa_bias_kernel(tab_ref, o_ref):
    delta = pl.program_id(0)
    j = lax.broadcasted_iota(I32, (BLK, BLK), 0)
    i = lax.broadcasted_iota(I32, (BLK, BLK), 1)
    dist = jnp.maximum(delta * BLK + i - j, 0)
    tiles = _bias_from_distance(dist, tab_ref, range(DSA_HEADS))
    for h in range(DSA_HEADS):
        o_ref[0, h] = tiles[h] * LOG2E


def _dil_bias_kernel(tab_ref, o_ref, *, dilations):
    g = pl.program_id(0)
    j = lax.broadcasted_iota(I32, (2 * BLK, BLK), 0)
    i = lax.broadcasted_iota(I32, (2 * BLK, BLK), 1)
    step = i + BLK - j
    valid = (step >= 0) & (step <= BLK)
    for gi, r in enumerate(dilations):
        @pl.when(g == gi)
        def _():
            dist = jnp.maximum(step, 0) * r
            heads = [DSA_HEADS + gi * DIL_HPG + hh for hh in range(DIL_HPG)]
            tiles = _bias_from_distance(dist, tab_ref, heads)
            for hh in range(DIL_HPG):
                o_ref[0, :, hh * BLK:(hh + 1) * BLK] = jnp.where(valid, tiles[hh], NEG)


def _bias_tiles(rel_bias, nkc):
    smem = pl.BlockSpec(memory_space=pltpu.SMEM)
    dsa = pl.pallas_call(
        _dsa_bias_kernel,
        out_shape=jax.ShapeDtypeStruct((nkc, DSA_HEADS, BLK, BLK), F32),
        grid=(nkc,),
        in_specs=[smem],
        out_specs=pl.BlockSpec((1, DSA_HEADS, BLK, BLK), lambda d: (d, 0, 0, 0)),
        compiler_params=_cparams(("parallel",)),
        name="dsa_bias",
    )(rel_bias)
    dil = pl.pallas_call(
        functools.partial(_dil_bias_kernel, dilations=tuple(r for _, r in DIL_GROUPS)),
        out_shape=jax.ShapeDtypeStruct((len(DIL_GROUPS), 2 * BLK, DIL_HPG * BLK), F32),
        grid=(len(DIL_GROUPS),),
        in_specs=[smem],
        out_specs=pl.BlockSpec((1, 2 * BLK, DIL_HPG * BLK), lambda g: (g, 0, 0)),
        compiler_params=_cparams(("parallel",)),
        name="dil_bias",
    )(rel_bias)
    return dsa, dil


SUP = 2
SROWS = SUP * BLK


PLANE_KEYS = 32 * 8


def _bit_planes(words):
    x = list(words)
    j, m = 16, 0x0000FFFF
    while j:
        k = 0
        while k < 32:
            t = (x[k] ^ lax.shift_right_logical(x[k + j], jnp.int32(j))) & m
            x[k] = x[k] ^ t
            x[k + j] = x[k + j] ^ jnp.left_shift(t, jnp.int32(j))
            k = (k + j + 1) & ~j
        j >>= 1
        m = (m ^ (m << j)) & 0xFFFFFFFF
        m = m - (1 << 32) if m >= (1 << 31) else m
    return x


def _dsa_kernel(qiT_ref, wiT_ref, qaT_ref, kidx_ref, ckv_ref, ckvT_ref, wuk_ref, wuv_ref, bias_ref, tri_ref,
                y_ref, sc_ref, planes_ref, qlT_ref, x_ref, pT_ref, *, topk, nsc, qb0):
    qb = qb0 + pl.program_id(1)
    row = lax.broadcasted_iota(I32, (SROWS, BLK), 0)
    col = lax.broadcasted_iota(I32, (SROWS, BLK), 1)
    trips = [slice(sc * SROWS, (sc + 1) * SROWS) for sc in range(nsc)]

    def causal(sc):
        return row <= col + (qb * BLK - sc * SROWS)

    for h in range(DSA_HEADS):
        ql = jnp.dot(wuk_ref[h], qaT_ref[0, h * HEAD_DIM:(h + 1) * HEAD_DIM, :],
                     preferred_element_type=F32) * (HEAD_DIM ** -0.5 * LOG2E)
        qlT_ref[:, h * BLK:(h + 1) * BLK] = ql.astype(BF16)

    wq = wiT_ref[0] * (IDX_HEADS ** -0.5)
    for sc, ts in enumerate(trips):
        kx = kidx_ref[0, ts, :]
        acc = jnp.zeros((SROWS, BLK), F32)
        for h in range(IDX_HEADS):
            s = jnp.dot(kx, qiT_ref[0, h * IDX_DIM:(h + 1) * IDX_DIM, :],
                        preferred_element_type=F32) * (IDX_DIM ** -0.5)
            acc = acc + wq[h:h + 1, :] * jnp.maximum(s, 0.0)
        acc = jnp.where(causal(sc), acc, NEG)
        sc_ref[ts, :] = acc
        bits = pltpu.bitcast(acc, I32)
        bits = jnp.where(bits == INT_MIN, 0, bits)
        ukey = bits ^ ((bits >> 31) & 0x7FFFFFFF) ^ INT_MIN
        for grp in range(SROWS // PLANE_KEYS):
            tiles = [ukey[grp * PLANE_KEYS + j * 8:grp * PLANE_KEYS + (j + 1) * 8] for j in range(32)]
            for b, plane in enumerate(_bit_planes(tiles)):
                planes_ref[b, sc * (SROWS // PLANE_KEYS) + grp] = plane

    def count(pred):
        cnt = jnp.zeros((8, BLK), I32)
        for ts in trips:
            cnt = cnt + jnp.sum(jnp.where(pred(sc_ref[ts, :]), 1, 0).reshape(SROWS // 8, 8, BLK), axis=0)
        return jnp.sum(cnt, axis=0, keepdims=True)

    def as_float(key):
        return pltpu.bitcast(key ^ ((key >> 31) & 0x7FFFFFFF), F32)

    def bit_body(it, carry):
        alive, above, code = carry
        ones = alive & planes_ref[it]
        c = jnp.sum(jnp.sum(lax.population_count(ones), axis=0), axis=0, keepdims=True)
        take = above + c >= topk
        alive = jnp.where(take, ones, alive ^ ones)
        above = jnp.where(take, above, above + c)
        code = code | jnp.where(take, jnp.left_shift(jnp.int32(1), 31 - it), 0)
        return alive, above, code

    nw = nsc * (SROWS // PLANE_KEYS)
    _, _, code = lax.fori_loop(
        0, 32, bit_body,
        (jnp.full((nw, 8, BLK), -1, I32), jnp.zeros((1, BLK), I32), jnp.zeros((1, BLK), I32)))
    guess = as_float(code ^ INT_MIN)
    n_gt_guess = count(lambda v: v > guess)
    proven = (n_gt_guess < topk) & (count(lambda v: v >= guess) >= topk)

    def bisect():
        def thr_body(it, lo):
            cand = lo + jnp.left_shift(jnp.int32(1), 31 - it)
            cand_f = as_float(cand)
            return jnp.where(count(lambda v: v >= cand_f) >= topk, cand, lo)
        t = as_float(lax.fori_loop(0, 32, thr_body, jnp.full((1, BLK), INT_MIN, I32)))
        return t, count(lambda v: v > t)

    thr, n_gt = lax.cond(jnp.min(jnp.where(proven, 1, 0)) > 0, lambda: (guess, n_gt_guess), bisect)
    ties_wanted = (topk - n_gt).astype(F32)

    m = [jnp.full((8, BLK), NEG, F32) for _ in range(DSA_HEADS)]
    ties_before = jnp.zeros((1, BLK), F32)
    for sc, ts in enumerate(trips):
        k = sc_ref[ts, :]
        tie = k == thr
        tie_rank = jnp.dot(tri_ref[...], jnp.where(tie, 1.0, 0.0).astype(BF16),
                           preferred_element_type=F32) + ties_before
        ties_before = tie_rank[SROWS - 1:SROWS, :]
        sel = ((k > thr) | (tie & (tie_rank <= ties_wanted))) & causal(sc)
        am = jnp.where(sel, 0.0, NEG)
        ck = ckv_ref[0, ts, :]
        for hp in range(DSA_HEADS // 2):
            lg2 = jnp.dot(ck, qlT_ref[:, 2 * hp * BLK:(2 * hp + 2) * BLK], preferred_element_type=F32)
            for h in (2 * hp, 2 * hp + 1):
                lg = lg2[:, (h % 2) * BLK:(h % 2 + 1) * BLK]
                for j in range(SUP):
                    rs = slice(j * BLK, (j + 1) * BLK)
                    delta = jnp.maximum(qb - (sc * SUP + j), 0)
                    x = lg[rs] + bias_ref[delta, h] + am[rs]
                    x_ref[sc * SROWS + j * BLK:sc * SROWS + (j + 1) * BLK, h * BLK:(h + 1) * BLK] = x
                    m[h] = jnp.maximum(m[h], jnp.max(x.reshape(BLK // 8, 8, BLK), axis=0))
    m = [jnp.max(v, axis=0, keepdims=True) for v in m]

    l = [jnp.zeros((8, BLK), F32) for _ in range(DSA_HEADS)]
    for sc, ts in enumerate(trips):
        for h in range(DSA_HEADS):
            hs = slice(h * BLK, (h + 1) * BLK)
            p = jnp.exp2(x_ref[ts, hs] - m[h])
            pT_ref[ts, hs] = p.astype(BF16)
            l[h] = l[h] + jnp.sum(p.reshape(SROWS // 8, 8, BLK), axis=0)

    for hp in range(DSA_HEADS // 2):
        o2 = jnp.dot(ckvT_ref[0], pT_ref[:, 2 * hp * BLK:(2 * hp + 2) * BLK],
                     preferred_element_type=F32)
        for h in (2 * hp, 2 * hp + 1):
            inv = 1.0 / jnp.sum(l[h], axis=0, keepdims=True)
            oh = (o2[:, (h % 2) * BLK:(h % 2 + 1) * BLK] * inv).T.astype(BF16)
            yh = jnp.dot(oh, wuv_ref[h], preferred_element_type=F32)
            y_ref[0, :, h * HEAD_DIM:(h + 1) * HEAD_DIM] = yh.astype(BF16)


def _dsa_group(g, qiT, wiT, qaT, kidx, ckv, ckvT, wuk, wuv, bias_tiles, tri):
    b, seq, _ = ckv.shape
    nkc = seq // BLK
    nsc = g + 1
    nk = nsc * SROWS
    topk = min(TOPK_MAX, seq // 4)
    qblk = lambda rows: pl.BlockSpec((1, rows, BLK), lambda bi, qi: (bi, 0, g * SUP + qi))
    head3 = lambda cols: pl.BlockSpec((1, nk, cols), lambda bi, qi: (bi, 0, 0))
    const = lambda s: pl.BlockSpec(s, lambda bi, qi: (0,) * len(s))
    hl = DSA_HEADS * BLK
    return pl.pallas_call(
        functools.partial(_dsa_kernel, topk=topk, nsc=nsc, qb0=g * SUP),
        out_shape=jax.ShapeDtypeStruct((b, SROWS, DSA_HEADS * HEAD_DIM), BF16),
        grid=(b, SUP),
        in_specs=[qblk(IDX_HEADS * IDX_DIM), qblk(8), qblk(DSA_HEADS * HEAD_DIM),
                  head3(IDX_DIM), head3(KV_LATENT),
                  pl.BlockSpec((1, KV_LATENT, nk), lambda bi, qi: (bi, 0, 0)),
                  const((DSA_HEADS, KV_LATENT, HEAD_DIM)), const((DSA_HEADS, KV_LATENT, HEAD_DIM)),
                  const((nkc, DSA_HEADS, BLK, BLK)), const((SROWS, SROWS))],
        out_specs=pl.BlockSpec((1, BLK, DSA_HEADS * HEAD_DIM), lambda bi, qi: (bi, qi, 0)),
        scratch_shapes=[pltpu.VMEM((nk, BLK), F32),
                        pltpu.VMEM((32, nk // PLANE_KEYS, 8, BLK), I32),
                        pltpu.VMEM((KV_LATENT, hl), BF16),
                        pltpu.VMEM((nk, hl), F32),
                        pltpu.VMEM((nk, hl), BF16)],
        compiler_params=_cparams(("parallel", "arbitrary")),
        name=f"dsa_g{g}",
    )(qiT, wiT, qaT, kidx, ckv, ckvT, wuk, wuv, bias_tiles, tri)


def _dsa(qiT, wiT, qaT, kidx, ckv, ckvT, wuk, wuv, bias_tiles):
    seq = ckv.shape[1]
    assert seq % SROWS == 0 and seq >= 4 * TOPK_MAX
    tri = jnp.tril(jnp.ones((SROWS, SROWS), BF16))
    groups = [_dsa_group(g, qiT, wiT, qaT, kidx, ckv, ckvT, wuk, wuv, bias_tiles, tri)
              for g in range(seq // SROWS)]
    return jnp.concatenate(groups, axis=1)


DIL_UNROLL = 5
DIL_STEP_ROWS = 2048


def _dil_kernel(q_ref, k_ref, v_ref, bm_ref, o_ref, lse_ref, vT_ref, *, nblk, nres):
    hq = DIL_HPG * BLK
    rowh = lax.broadcasted_iota(I32, (hq, DIL_OUT), 0) // BLK
    colh = lax.broadcasted_iota(I32, (hq, DIL_OUT), 1) // HEAD_DIM
    same_head = rowh == colh

    for res in range(nres):
        cs = slice(res * DIL_OUT, (res + 1) * DIL_OUT)

        for n in range(nblk):
            vT_ref[n] = v_ref[0, n * BLK:(n + 1) * BLK, cs].astype(F32).T.astype(BF16)

        def block(qo, kw, vT, bm, cs=cs):
            q = q_ref[0, pl.ds(qo, BLK), cs]
            qd = jnp.where(same_head, jnp.concatenate([q] * DIL_HPG, axis=0), jnp.zeros((), BF16))
            s = lax.dot_general(kw, qd, (((1,), (1,)), ((), ())), preferred_element_type=F32)
            s = s * (HEAD_DIM ** -0.5) + bm
            m = jnp.max(s, axis=0, keepdims=True)
            e = jnp.exp(s - m)
            l = jnp.sum(e, axis=0, keepdims=True)
            oT = jnp.dot(vT, e.astype(BF16), preferred_element_type=F32)
            inv = 1.0 / l
            lse = m + jnp.log(l)
            outs, lses = [], []
            for hh in range(DIL_HPG):
                qs = slice(hh * BLK, (hh + 1) * BLK)
                outs.append(oT[hh * HEAD_DIM:(hh + 1) * HEAD_DIM, qs] * inv[:, qs])
                lses.append(jnp.broadcast_to(lse[:, qs], (HEAD_DIM, BLK)))
            o_ref[0, pl.ds(qo, BLK), cs] = jnp.concatenate(outs, axis=0).T.astype(BF16)
            lse_ref[0, pl.ds(qo, BLK), cs] = jnp.concatenate(lses, axis=0).T

        block(0, k_ref[0, 0:BLK, cs], vT_ref[0], bm_ref[0, BLK:, :])

        def body(n, carry, cs=cs, block=block):
            ko = pl.multiple_of((n - 1) * BLK, BLK)
            vT = jnp.concatenate([vT_ref[n - 1], vT_ref[n]], axis=1)
            block(pl.multiple_of(n * BLK, BLK), k_ref[0, pl.ds(ko, 2 * BLK), cs], vT, bm_ref[0])
            return carry

        lax.fori_loop(1, nblk, body, 0, unroll=DIL_UNROLL)


def _dilated_group(q, k, v, bm, g, dilation, b):
    c = DIL_OUT
    ls = q.shape[0] // b
    nblk = ls // BLK
    nres = max(1, min(dilation, DIL_STEP_ROWS // ls))
    view = lambda a: a.reshape(b, ls, dilation * c)
    blk = pl.BlockSpec((1, ls, nres * c), lambda bi, ri: (bi, 0, ri))
    o, lse = pl.pallas_call(
        functools.partial(_dil_kernel, nblk=nblk, nres=nres),
        out_shape=[jax.ShapeDtypeStruct((b, ls, dilation * c), BF16),
                   jax.ShapeDtypeStruct((b, ls, dilation * c), F32)],
        grid=(b, dilation // nres),
        in_specs=[blk, blk, blk, pl.BlockSpec((1, 2 * BLK, DIL_HPG * BLK), lambda bi, ri: (g, 0, 0))],
        out_specs=[blk, blk],
        scratch_shapes=[pltpu.VMEM((nblk, c, BLK), BF16)],
        compiler_params=_cparams(("parallel", "parallel")),
        name=f"dilated_g{g}",
    )(view(q), view(k), view(v), bm)
    return o.reshape(b * ls, dilation * c), lse.reshape(b * ls, dilation * c)


def _store_row_tiles(ref, base, val):
    rows, d = val.shape
    dt = d // LANES
    for s in range(dt):
        ref[pl.ds(base * dt + s, rows, stride=dt), :] = val[:, s * LANES:(s + 1) * LANES]


def _load_row_tiles(ref, base, rows, dt):
    return jnp.concatenate([ref[pl.ds(base * dt + s, rows, stride=dt), :] for s in range(dt)], axis=1)


MIX_COLS = 256


def _token_major(ref, stage_ref, r):
    if r == 1:
        return ref[...].astype(F32)
    rows = ref.shape[0]
    for rho in range(r):
        for hf in range(DIL_OUT // LANES):
            c0 = rho * DIL_OUT + hf * LANES
            stage_ref[hf, pl.ds(rho, rows, stride=r), :] = ref[:, c0:c0 + LANES].astype(F32)
    return jnp.concatenate([stage_ref[hf] for hf in range(DIL_OUT // LANES)], axis=1)


def _sigmoid(v):
    return 0.5 * jnp.tanh(0.5 * v) + 0.5


def _mix_kernel(x_ref, ya_ref, o1_ref, o2_ref, o3_ref, l1_ref, l2_ref, l3_ref, gate_ref,
                wa_ref, wb_ref, wo_ref, fg_ref, wr_ref, br_ref,
                h_ref, t_ref, rw_ref, ri_ref, mixed_ref, *stage_refs):
    dils = [r for _, r in DIL_GROUPS]
    o = [_token_major(ref, st, r) for ref, st, r in zip((o1_ref, o2_ref, o3_ref), stage_refs[:3], dils)]
    l1, l2, l3 = [_token_major(ref, st, r) for ref, st, r in zip((l1_ref, l2_ref, l3_ref), stage_refs[3:], dils)]
    mx = jnp.maximum(jnp.maximum(l1, l2), l3)
    e1, e2, e3 = jnp.exp(l1 - mx), jnp.exp(l2 - mx), jnp.exp(l3 - mx)
    inv = 1.0 / (e1 + e2 + e3)
    yb = ((e1 * inv) * o[0] + (e2 * inv) * o[1] + (e3 * inv) * o[2]).astype(BF16)
    ya = ya_ref[...]
    d = x_ref.shape[1]
    for c in range(0, d, MIX_COLS):
        cs = slice(c, c + MIX_COLS)
        a = jnp.dot(ya, wa_ref[:, cs], preferred_element_type=F32)
        bmix = jnp.dot(yb, wb_ref[:, cs], preferred_element_type=F32)
        g0 = _sigmoid(gate_ref[:, cs].astype(F32))
        g1 = _sigmoid(gate_ref[:, d + c:d + c + MIX_COLS].astype(F32))
        mixed_ref[:, cs] = (g0 * a + g1 * bmix).astype(BF16)
    h = x_ref[...] + jnp.dot(mixed_ref[...], wo_ref[...], preferred_element_type=F32)
    h_ref[...] = h
    t = h * lax.rsqrt(jnp.mean(h * h, axis=-1, keepdims=True) + RMS_EPS) * fg_ref[...]
    _store_row_tiles(t_ref, 0, t)

    t_hi = t.astype(BF16)
    t_lo = (t - t_hi.astype(F32)).astype(BF16)
    r1 = jnp.dot(t_hi, wr_ref[...], preferred_element_type=F32)
    r2 = jnp.dot(t_lo, wr_ref[:, :ROUTE_COLS], preferred_element_type=F32)
    logits = r1[:, :ROUTE_COLS] + (r1[:, ROUTE_COLS:] + r2) + br_ref[...]
    lane = lax.broadcasted_iota(I32, logits.shape, 1)
    ninf = -jnp.inf
    big = jnp.int32(10 ** 6)

    def first_argmax(v, vmax):
        return jnp.min(jnp.where(v == vmax, lane, big), axis=-1, keepdims=True)

    gl = jnp.where(lane < N_GROUPS, logits, ninf)
    gmax = jnp.max(gl, axis=-1, keepdims=True)
    gsel = first_argmax(gl, gmax)
    p_g = 1.0 / jnp.sum(jnp.exp(gl - gmax), axis=-1, keepdims=True)
    lo = N_GROUPS + gsel * EXPERTS_PER_GROUP
    el = jnp.where((lane >= lo) & (lane < lo + EXPERTS_PER_GROUP), logits, ninf)
    v1 = jnp.max(el, axis=-1, keepdims=True)
    i1 = first_argmax(el, v1)
    el2 = jnp.where(lane == i1, ninf, el)
    v2 = jnp.max(el2, axis=-1, keepdims=True)
    i2 = first_argmax(el2, v2)
    e2 = jnp.exp(v2 - v1)
    w1 = p_g / (1.0 + e2)
    w2 = p_g * e2 / (1.0 + e2)
    rw_ref[...] = jnp.where(lane == 0, w1, jnp.where(lane == 1, w2, 0.0))
    ri_ref[...] = jnp.where(lane == 0, i1 - N_GROUPS, jnp.where(lane == 1, i2 - N_GROUPS, 0))


def _mix(x2, ya, os_, lses, gates, wa, wb, wo, ffn_norm, wr, br):
    n, d = x2.shape
    tm = MIX_TM
    row = lambda c: pl.BlockSpec((tm, c), lambda i: (i, 0))
    res = lambda a: pl.BlockSpec((tm * a.shape[0] // n, a.shape[1]), lambda i: (i, 0))
    const = lambda s: pl.BlockSpec(s, lambda i: (0, 0))
    return pl.pallas_call(
        _mix_kernel,
        out_shape=[jax.ShapeDtypeStruct((n, d), F32), jax.ShapeDtypeStruct((n * (d // LANES), LANES), F32),
                   jax.ShapeDtypeStruct((n, ROUTE_COLS), F32), jax.ShapeDtypeStruct((n, ROUTE_COLS), I32)],
        grid=(n // tm,),
        in_specs=[row(d), row(512)] + [res(a) for a in os_] + [res(a) for a in lses] + [row(2 * d),
                  const(wa.shape), const(wb.shape), const(wo.shape), const((1, d)),
                  const(wr.shape), const((1, ROUTE_COLS))],
        out_specs=[row(d), pl.BlockSpec((tm * (d // LANES), LANES), lambda i: (i, 0)),
                   row(ROUTE_COLS), row(ROUTE_COLS)],
        scratch_shapes=[pltpu.VMEM((tm, d), BF16)] + [pltpu.VMEM((DIL_OUT // LANES, tm, LANES), F32)] * 6,
        compiler_params=_cparams(("parallel",)),
        name="mix",
    )(x2, ya, *os_, *lses, gates, wa, wb, wo, ffn_norm.reshape(1, d), wr, br)


GATHER_UNROLL = 16


def _start_row_gather(src_hbm, idx_ref, nrows, dt, buf, sem, slot):
    def body(g, c):
        for u in range(GATHER_UNROLL):
            r = g * GATHER_UNROLL + u
            src = pl.multiple_of(idx_ref[0, 0, r] * dt, dt)
            dst = pl.multiple_of((slot * nrows + r) * dt, dt)
            pltpu.make_async_copy(src_hbm.at[pl.ds(src, dt)], buf.at[pl.ds(dst, dt)],
                                  sem.at[slot]).start(priority=u % 2)
        return c
    lax.fori_loop(0, nrows // GATHER_UNROLL, body, 0)


def _wait_row_gather(src_hbm, nrows, dt, buf, sem, slot):
    dst = pl.multiple_of(slot * nrows * dt, dt)
    pltpu.make_async_copy(src_hbm.at[pl.ds(0, nrows * dt)], buf.at[pl.ds(dst, nrows * dt)], sem.at[slot]).wait()


def _gather_pipeline(i, nsteps, src_hbm, cur_ref, nxt_ref, nrows, dt, buf, sem):
    slot = i % 2

    @pl.when(i == 0)
    def _():
        _start_row_gather(src_hbm, cur_ref, nrows, dt, buf, sem, 0)

    @pl.when(i + 1 < nsteps)
    def _():
        _start_row_gather(src_hbm, nxt_ref, nrows, dt, buf, sem, 1 - slot)

    _wait_row_gather(src_hbm, nrows, dt, buf, sem, slot)
    return slot


def _expert_kernel(te_ref, cur_ref, nxt_ref, t_hbm, wg_ref, wu_ref, wd_ref, y_ref, buf, sem):
    i = pl.program_id(0)
    dt = wg_ref.shape[1] // LANES
    slot = _gather_pipeline(i, pl.num_programs(0), t_hbm, cur_ref, nxt_ref, MOE_TM, dt, buf, sem)
    xt = _load_row_tiles(buf, slot * MOE_TM, MOE_TM, dt).astype(BF16)
    hg = jnp.dot(xt, wg_ref[0], preferred_element_type=F32)
    hu = jnp.dot(xt, wu_ref[0], preferred_element_type=F32)
    hid = (hg * jax.nn.sigmoid(hg)) * hu
    _store_row_tiles(y_ref, 0, jnp.dot(hid.astype(BF16), wd_ref[0], preferred_element_type=F32))


def _experts(tile_expert, row_token, t, wg, wu, wd, d):
    dt = d // LANES
    ntiles = tile_expert.shape[0]
    ff = wg.shape[2]
    tok3 = row_token.reshape(ntiles, 1, MOE_TM)
    smem_cur = pl.BlockSpec((1, 1, MOE_TM), lambda i, te: (i, 0, 0), memory_space=pltpu.SMEM)
    smem_nxt = pl.BlockSpec((1, 1, MOE_TM), lambda i, te: (jnp.minimum(i + 1, ntiles - 1), 0, 0),
                            memory_space=pltpu.SMEM)
    wspec = lambda s: pl.BlockSpec((1,) + s, lambda i, te: (te[i], 0, 0))
    return pl.pallas_call(
        _expert_kernel,
        out_shape=jax.ShapeDtypeStruct((ntiles * MOE_TM * dt, LANES), F32),
        grid_spec=pltpu.PrefetchScalarGridSpec(
            num_scalar_prefetch=1,
            grid=(ntiles,),
            in_specs=[smem_cur, smem_nxt, pl.BlockSpec(memory_space=pl.ANY),
                      wspec((d, ff)), wspec((d, ff)), wspec((ff, d))],
            out_specs=pl.BlockSpec((MOE_TM * dt, LANES), lambda i, te: (i, 0)),
            scratch_shapes=[pltpu.VMEM((2 * MOE_TM * dt, LANES), F32), pltpu.SemaphoreType.DMA((2,))],
        ),
        compiler_params=_cparams(("arbitrary",)),
        name="experts",
    )(tile_expert, tok3, tok3, t, wg, wu, wd)


def _final_kernel(cur_ref, nxt_ref, y_hbm, h_ref, rw_ref, fn_ref, o_ref, buf, sem):
    i = pl.program_id(0)
    dt = h_ref.shape[1] // LANES
    slot = _gather_pipeline(i, pl.num_programs(0), y_hbm, cur_ref, nxt_ref, 2 * FIN_TM, dt, buf, sem)
    rw = rw_ref[...]
    y0 = _load_row_tiles(buf, slot * 2 * FIN_TM, FIN_TM, dt)
    y1 = _load_row_tiles(buf, slot * 2 * FIN_TM + FIN_TM, FIN_TM, dt)
    h = h_ref[...] + rw[:, 0:1] * y0 + rw[:, 1:2] * y1
    o_ref[...] = h * lax.rsqrt(jnp.mean(h * h, axis=-1, keepdims=True) + RMS_EPS) * fn_ref[...]


def _final(pos_tiles, y_sorted, h, rw, final_norm):
    n, d = h.shape
    tm = FIN_TM
    nt = n // tm
    smem_cur = pl.BlockSpec((1, 1, 2 * tm), lambda i: (i, 0, 0), memory_space=pltpu.SMEM)
    smem_nxt = pl.BlockSpec((1, 1, 2 * tm), lambda i: (jnp.minimum(i + 1, nt - 1), 0, 0),
                            memory_space=pltpu.SMEM)
    return pl.pallas_call(
        _final_kernel,
        out_shape=jax.ShapeDtypeStruct((n, d), F32),
        grid=(nt,),
        in_specs=[smem_cur, smem_nxt, pl.BlockSpec(memory_space=pl.ANY),
                  pl.BlockSpec((tm, d), lambda i: (i, 0)), pl.BlockSpec((tm, ROUTE_COLS), lambda i: (i, 0)),
                  pl.BlockSpec((1, d), lambda i: (0, 0))],
        out_specs=pl.BlockSpec((tm, d), lambda i: (i, 0)),
        scratch_shapes=[pltpu.VMEM((2 * 2 * tm * (d // LANES), LANES), F32), pltpu.SemaphoreType.DMA((2,))],
        compiler_params=_cparams(("arbitrary",)),
        name="final",
    )(pos_tiles, pos_tiles, y_sorted, h, rw, final_norm.reshape(1, d))


def _route_plan(gid):
    n = gid.shape[0]
    e = gid.reshape(-1)
    onehot = (e[:, None] == jnp.arange(N_EXPERTS, dtype=I32)[None, :]).astype(F32)
    chunk = 256
    oh3 = onehot.reshape(-1, chunk, N_EXPERTS)
    within = jnp.einsum("ij,tjk->tik", jnp.tril(jnp.ones((chunk, chunk), F32)), oh3)
    totals = within[:, -1, :]
    before = jnp.cumsum(totals, axis=0) - totals
    csum = (within + before[:, None, :]).reshape(-1, N_EXPERTS)
    rank = jnp.sum(csum * onehot, axis=1).astype(I32) - 1
    counts = (before[-1] + totals[-1]).astype(I32)
    padded = ((counts + MOE_TM - 1) // MOE_TM) * MOE_TM
    seg_end = jnp.cumsum(padded)
    pos = (seg_end - padded)[e] + rank
    nrows = 2 * n + N_EXPERTS * MOE_TM
    tile_start = jnp.arange(nrows // MOE_TM, dtype=I32) * MOE_TM
    tile_expert = jnp.minimum(jnp.sum(tile_start[:, None] >= seg_end[None, :], axis=1), N_EXPERTS - 1).astype(I32)
    order = jnp.argsort(e, stable=True).astype(I32)
    first = jnp.cumsum(counts) - counts
    row_expert = jnp.repeat(tile_expert, MOE_TM)
    src = jnp.arange(nrows, dtype=I32) - (seg_end - padded)[row_expert] + first[row_expert]
    row_token = order[jnp.clip(src, 0, 2 * n - 1)] // 2
    return row_token, tile_expert, pos.reshape(n, 2)


def _pack_w_in(w):
    d = w.shape[0]
    o_kv, o_qi, o_ki, o_wi = 512, 768, 1024, 1088
    o_dil = o_wi + IDX_HEADS
    o_gate = o_dil + 9 * 256
    pad = jnp.zeros((d, LANES - IDX_DIM - IDX_HEADS), w.dtype)
    packed = jnp.concatenate([w[:, :o_ki], w[:, o_ki:o_wi], w[:, o_wi:o_dil], pad, w[:, o_dil:o_gate],
                              w[:, o_gate:]], axis=1)
    assert packed.shape[1] == C_END
    return packed.astype(BF16)


def kernel(x, attn_norm, w_in, kv_norm, w_uk, w_uv, rel_bias, w_branch_a, w_branch_b, w_out, ffn_norm,
           w_router_group, b_router_group, w_router_expert, b_router_expert, w_gate, w_up, w_down,
           final_norm):
    b, seq, d = x.shape
    n = b * seq
    nkc = seq // BLK
    assert w_in.shape[0] == 1, "one layer"
    x2 = x.reshape(n, d)

    outs = _proj(x2, attn_norm[0], _pack_w_in(w_in[0]), kv_norm[0])
    qa, ckv, qi, kw = outs[:4]
    dil = outs[4:13]
    gates = outs[13]

    dsa_bias, dil_bias = _bias_tiles(rel_bias, nkc)

    t3 = lambda a: jnp.swapaxes(a.reshape(b, seq, a.shape[-1]), 1, 2)
    kw3 = kw.reshape(b, seq, LANES)
    wiT = jnp.swapaxes(kw3[:, :, IDX_DIM:IDX_DIM + 8], 1, 2)
    kidx = kw3[:, :, :IDX_DIM].astype(BF16)
    ckv3 = ckv.reshape(b, seq, KV_LATENT)
    ckvT = jnp.swapaxes(ckv3, 1, 2)
    ya = _dsa(t3(qi), wiT, t3(qa), kidx, ckv3, ckvT, w_uk[0].astype(BF16), w_uv[0].astype(BF16), dsa_bias)

    os_, lses = [], []
    for g, (_, dilation) in enumerate(DIL_GROUPS):
        o, lse = _dilated_group(dil[g], dil[3 + g], dil[6 + g], dil_bias, g, dilation, b)
        os_.append(o)
        lses.append(lse)

    wr = jnp.concatenate([w_router_group[0],
                          jnp.swapaxes(w_router_expert[0], 0, 1).reshape(d, N_EXPERTS),
                          jnp.zeros((d, ROUTE_COLS - N_GROUPS - N_EXPERTS), F32)], axis=1)
    wr_hi = wr.astype(BF16)
    wr = jnp.concatenate([wr_hi, (wr - wr_hi.astype(F32)).astype(BF16)], axis=1)
    br = jnp.concatenate([b_router_group[0], b_router_expert[0].reshape(-1),
                          jnp.zeros((ROUTE_COLS - N_GROUPS - N_EXPERTS,), F32)]).reshape(1, ROUTE_COLS)
    h, t, rw, ri = _mix(x2, ya.reshape(n, -1), os_, lses, gates,
                        w_branch_a[0].astype(BF16), w_branch_b[0].astype(BF16), w_out[0].astype(BF16),
                        ffn_norm[0], wr, br)

    row_token, tile_expert, pos = _route_plan(ri[:, :2])
    y_sorted = _experts(tile_expert, row_token, t, w_gate[0].astype(BF16), w_up[0].astype(BF16),
                        w_down[0].astype(BF16), d)

    pos_tiles = jnp.swapaxes(pos.reshape(n // FIN_TM, FIN_TM, 2), 1, 2).reshape(n // FIN_TM, 1, 2 * FIN_TM)
    out = _final(pos_tiles, y_sorted, h, rw, final_norm)
    return out.reshape(b, seq, d)
```

```python
import functools
import math

import numpy as np
import jax
import jax.numpy as jnp
from jax import lax
from jax.experimental import pallas as pl
from jax.experimental.pallas import tpu as pltpu

F32 = jnp.float32
BF16 = jnp.bfloat16
I32 = jnp.int32

LANES = 128
VMEM_LIMIT_BYTES = 56 * 1024 * 1024

HEAD_DIM = 64
DSA_HEADS = 8
KV_LATENT = 256
IDX_HEADS = 4
IDX_DIM = 64
TOPK_MAX = 256
DIL_GROUPS = ((128, 1), (512, 4), (2048, 16))
DIL_HPG = 4
DIL_OUT = DIL_HPG * HEAD_DIM
NUM_BUCKETS = 32
MAX_DISTANCE = 2048
N_GROUPS = 4
EXPERTS_PER_GROUP = 8
N_EXPERTS = N_GROUPS * EXPERTS_PER_GROUP
RMS_EPS = 1e-6
NEG = -1e30
LOG2E = math.log2(math.e)
INT_MIN = -2 ** 31

BLK = 128
PROJ_TM = 512
MIX_TM = 512
MOE_TM = 256
FIN_TM = 512
ROUTE_COLS = 128


def _cparams(sem):
    return pltpu.CompilerParams(dimension_semantics=sem, vmem_limit_bytes=VMEM_LIMIT_BYTES)


C_QA = 0
C_KV = 512
C_QI = 768
C_KW = 1024
C_DIL = 1152
C_GATE = C_DIL + 9 * 256
C_END = C_GATE + 2048


def _proj_kernel(x_ref, g_ref, w_ref, kvg_ref, qa_ref, ckv_ref, qi_ref, kw_ref, *rest):
    dil_refs = rest[:9]
    gate_ref = rest[9]
    stage_ref = rest[10]
    x = x_ref[...]
    u = x * lax.rsqrt(jnp.mean(x * x, axis=-1, keepdims=True) + RMS_EPS) * g_ref[...]
    u = u.astype(BF16)

    def mm(a, b):
        return jnp.dot(u, w_ref[:, a:b], preferred_element_type=F32)

    qa_ref[...] = mm(C_QA, C_KV).astype(BF16)
    c = mm(C_KV, C_QI)
    c = c * lax.rsqrt(jnp.mean(c * c, axis=-1, keepdims=True) + RMS_EPS) * kvg_ref[...]
    ckv_ref[...] = c.astype(BF16)
    qi_ref[...] = mm(C_QI, C_KW).astype(BF16)
    kw_ref[...] = mm(C_KW, C_DIL)
    for j in range(9):
        val = mm(C_DIL + 256 * j, C_DIL + 256 * (j + 1))
        r = DIL_GROUPS[j % 3][1]
        if r == 1:
            dil_refs[j][...] = val.astype(BF16)
        else:
            for hf in range(2):
                stage_ref[hf] = val[:, hf * LANES:(hf + 1) * LANES]
            for rho in range(r):
                for hf in range(2):
                    dil_refs[j][:, rho * 256 + hf * LANES:rho * 256 + (hf + 1) * LANES] = (
                        stage_ref[hf, pl.ds(rho, val.shape[0] // r, stride=r), :].astype(BF16))
    for j in range(4):
        gate_ref[:, 512 * j:512 * (j + 1)] = mm(C_GATE + 512 * j, C_GATE + 512 * (j + 1)).astype(BF16)


def _proj(x2, attn_norm, w_packed, kv_norm):
    n, d = x2.shape
    tm = PROJ_TM
    row = lambda i: (i, 0)
    const = lambda i: (0, 0)
    outs = [jax.ShapeDtypeStruct((n, 512), BF16), jax.ShapeDtypeStruct((n, 256), BF16),
            jax.ShapeDtypeStruct((n, 256), BF16), jax.ShapeDtypeStruct((n, 128), F32)]
    for j in range(9):
        r = DIL_GROUPS[j % 3][1]
        outs.append(jax.ShapeDtypeStruct((n // r, r * 256), BF16))
    outs += [jax.ShapeDtypeStruct((n, 2048), BF16)]
    out_specs = [pl.BlockSpec((tm * s.shape[0] // n, s.shape[1]), row) for s in outs]
    return pl.pallas_call(
        _proj_kernel,
        out_shape=outs,
        grid=(n // tm,),
        in_specs=[pl.BlockSpec((tm, d), row), pl.BlockSpec((1, d), const),
                  pl.BlockSpec((d, C_END), const), pl.BlockSpec((1, KV_LATENT), const)],
        out_specs=out_specs,
        scratch_shapes=[pltpu.VMEM((2, tm, LANES), F32)],
        compiler_params=_cparams(("parallel",)),
        name="proj",
    )(x2, attn_norm.reshape(1, d), w_packed, kv_norm.reshape(1, KV_LATENT))


def _bucket_thresholds():
    max_exact = NUM_BUCKETS // 2
    d = np.arange(0, MAX_DISTANCE + 1)
    nf = np.maximum(d, 1).astype(np.float32)
    large = max_exact + (np.log(nf / np.float32(max_exact)) / np.float32(math.log(MAX_DISTANCE / max_exact))
                         * np.float32(NUM_BUCKETS - max_exact)).astype(np.int32)
    large = np.minimum(large, NUM_BUCKETS - 1)
    bucket = np.where(d < max_exact, d, large)
    assert np.all(np.diff(bucket) >= 0)
    return [int(np.argmax(bucket >= b)) for b in range(1, NUM_BUCKETS)]


_BUCKET_THR = _bucket_thresholds()


def _bias_from_distance(dist, tab_ref, heads):
    masks = [dist >= t for t in _BUCKET_THR]
    out = []
    for h in heads:
        v = jnp.full(dist.shape, tab_ref[0, h], F32)
        for b in range(1, NUM_BUCKETS):
            v = jnp.where(masks[b - 1], tab_ref[b, h], v)
        out.append(v)
    return out


def _dsa_bias_kernel(tab_ref, o_ref):
    delta = pl.program_id(0)
    j = lax.broadcasted_iota(I32, (BLK, BLK), 0)
    i = lax.broadcasted_iota(I32, (BLK, BLK), 1)
    dist = jnp.maximum(delta * BLK + i - j, 0)
    tiles = _bias_from_distance(dist, tab_ref, range(DSA_HEADS))
    for h in range(DSA_HEADS):
        o_ref[0, h] = tiles[h] * LOG2E


def _dil_bias_kernel(tab_ref, o_ref, *, dilations):
    g = pl.program_id(0)
    j = lax.broadcasted_iota(I32, (2 * BLK, BLK), 0)
    i = lax.broadcasted_iota(I32, (2 * BLK, BLK), 1)
    step = i + BLK - j
    valid = (step >= 0) & (step <= BLK)
    for gi, r in enumerate(dilations):
        @pl.when(g == gi)
        def _():
            dist = jnp.maximum(step, 0) * r
            heads = [DSA_HEADS + gi * DIL_HPG + hh for hh in range(DIL_HPG)]
            tiles = _bias_from_distance(dist, tab_ref, heads)
            for hh in range(DIL_HPG):
                o_ref[0, :, hh * BLK:(hh + 1) * BLK] = jnp.where(valid, tiles[hh], NEG)


def _bias_tiles(rel_bias, nkc):
    smem = pl.BlockSpec(memory_space=pltpu.SMEM)
    dsa = pl.pallas_call(
        _dsa_bias_kernel,
        out_shape=jax.ShapeDtypeStruct((nkc, DSA_HEADS, BLK, BLK), F32),
        grid=(nkc,),
        in_specs=[smem],
        out_specs=pl.BlockSpec((1, DSA_HEADS, BLK, BLK), lambda d: (d, 0, 0, 0)),
        compiler_params=_cparams(("parallel",)),
        name="dsa_bias",
    )(rel_bias)
    dil = pl.pallas_call(
        functools.partial(_dil_bias_kernel, dilations=tuple(r for _, r in DIL_GROUPS)),
        out_shape=jax.ShapeDtypeStruct((len(DIL_GROUPS), 2 * BLK, DIL_HPG * BLK), F32),
        grid=(len(DIL_GROUPS),),
        in_specs=[smem],
        out_specs=pl.BlockSpec((1, 2 * BLK, DIL_HPG * BLK), lambda g: (g, 0, 0)),
        compiler_params=_cparams(("parallel",)),
        name="dil_bias",
    )(rel_bias)
    return dsa, dil


SUP = 2
SROWS = SUP * BLK


PLANE_KEYS = 32 * 8


def _bit_planes(words):
    x = list(words)
    j, m = 16, 0x0000FFFF
    while j:
        k = 0
        while k < 32:
            t = (x[k] ^ lax.shift_right_logical(x[k + j], jnp.int32(j))) & m
            x[k] = x[k] ^ t
            x[k + j] = x[k + j] ^ jnp.left_shift(t, jnp.int32(j))
            k = (k + j + 1) & ~j
        j >>= 1
        m = (m ^ (m << j)) & 0xFFFFFFFF
        m = m - (1 << 32) if m >= (1 << 31) else m
    return x


def _dsa_kernel(qiT_ref, wiT_ref, qaT_ref, kidx_ref, ckv_ref, ckvT_ref, wuk_ref, wuv_ref, bias_ref, tri_ref,
                y_ref, sc_ref, planes_ref, qlT_ref, x_ref, pT_ref, *, topk, nsc, qb0):
    qb = qb0 + pl.program_id(1)
    row = lax.broadcasted_iota(I32, (SROWS, BLK), 0)
    col = lax.broadcasted_iota(I32, (SROWS, BLK), 1)
    trips = [slice(sc * SROWS, (sc + 1) * SROWS) for sc in range(nsc)]

    def causal(sc):
        return row <= col + (qb * BLK - sc * SROWS)

    for h in range(DSA_HEADS):
        ql = jnp.dot(wuk_ref[h], qaT_ref[0, h * HEAD_DIM:(h + 1) * HEAD_DIM, :],
                     preferred_element_type=F32) * (HEAD_DIM ** -0.5 * LOG2E)
        qlT_ref[:, h * BLK:(h + 1) * BLK] = ql.astype(BF16)

    wq = wiT_ref[0] * (IDX_HEADS ** -0.5)
    for sc, ts in enumerate(trips):
        kx = kidx_ref[0, ts, :]
        acc = jnp.zeros((SROWS, BLK), F32)
        for h in range(IDX_HEADS):
            s = jnp.dot(kx, qiT_ref[0, h * IDX_DIM:(h + 1) * IDX_DIM, :],
                        preferred_element_type=F32) * (IDX_DIM ** -0.5)
            acc = acc + wq[h:h + 1, :] * jnp.maximum(s, 0.0)
        acc = jnp.where(causal(sc), acc, NEG)
        sc_ref[ts, :] = acc
        bits = pltpu.bitcast(acc, I32)
        bits = jnp.where(bits == INT_MIN, 0, bits)
        ukey = bits ^ ((bits >> 31) & 0x7FFFFFFF) ^ INT_MIN
        for grp in range(SROWS // PLANE_KEYS):
            tiles = [ukey[grp * PLANE_KEYS + j * 8:grp * PLANE_KEYS + (j + 1) * 8] for j in range(32)]
            for b, plane in enumerate(_bit_planes(tiles)):
                planes_ref[b, sc * (SROWS // PLANE_KEYS) + grp] = plane

    def count(pred):
        cnt = jnp.zeros((8, BLK), I32)
        for ts in trips:
            cnt = cnt + jnp.sum(jnp.where(pred(sc_ref[ts, :]), 1, 0).reshape(SROWS // 8, 8, BLK), axis=0)
        return jnp.sum(cnt, axis=0, keepdims=True)

    def as_float(key):
        return pltpu.bitcast(key ^ ((key >> 31) & 0x7FFFFFFF), F32)

    def bit_body(it, carry):
        alive, above, code = carry
        ones = alive & planes_ref[it]
        c = jnp.sum(jnp.sum(lax.population_count(ones), axis=0), axis=0, keepdims=True)
        take = above + c >= topk
        alive = jnp.where(take, ones, alive ^ ones)
        above = jnp.where(take, above, above + c)
        code = code | jnp.where(take, jnp.left_shift(jnp.int32(1), 31 - it), 0)
        return alive, above, code

    nw = nsc * (SROWS // PLANE_KEYS)
    _, _, code = lax.fori_loop(
        0, 32, bit_body,
        (jnp.full((nw, 8, BLK), -1, I32), jnp.zeros((1, BLK), I32), jnp.zeros((1, BLK), I32)))
    guess = as_float(code ^ INT_MIN)
    n_gt_guess = count(lambda v: v > guess)
    proven = (n_gt_guess < topk) & (count(lambda v: v >= guess) >= topk)

    def bisect():
        def thr_body(it, lo):
            cand = lo + jnp.left_shift(jnp.int32(1), 31 - it)
            cand_f = as_float(cand)
            return jnp.where(count(lambda v: v >= cand_f) >= topk, cand, lo)
        t = as_float(lax.fori_loop(0, 32, thr_body, jnp.full((1, BLK), INT_MIN, I32)))
        return t, count(lambda v: v > t)

    thr, n_gt = lax.cond(jnp.min(jnp.where(proven, 1, 0)) > 0, lambda: (guess, n_gt_guess), bisect)
    ties_wanted = (topk - n_gt).astype(F32)

    m = [jnp.full((8, BLK), NEG, F32) for _ in range(DSA_HEADS)]
    ties_before = jnp.zeros((1, BLK), F32)
    for sc, ts in enumerate(trips):
        k = sc_ref[ts, :]
        tie = k == thr
        tie_rank = jnp.dot(tri_ref[...], jnp.where(tie, 1.0, 0.0).astype(BF16),
                           preferred_element_type=F32) + ties_before
        ties_before = tie_rank[SROWS - 1:SROWS, :]
        sel = ((k > thr) | (tie & (tie_rank <= ties_wanted))) & causal(sc)
        am = jnp.where(sel, 0.0, NEG)
        ck = ckv_ref[0, ts, :]
        for hp in range(DSA_HEADS // 2):
            lg2 = jnp.dot(ck, qlT_ref[:, 2 * hp * BLK:(2 * hp + 2) * BLK], preferred_element_type=F32)
            for h in (2 * hp, 2 * hp + 1):
                lg = lg2[:, (h % 2) * BLK:(h % 2 + 1) * BLK]
                for j in range(SUP):
                    rs = slice(j * BLK, (j + 1) * BLK)
                    delta = jnp.maximum(qb - (sc * SUP + j), 0)
                    x = lg[rs] + bias_ref[delta, h] + am[rs]
                    x_ref[sc * SROWS + j * BLK:sc * SROWS + (j + 1) * BLK, h * BLK:(h + 1) * BLK] = x
                    m[h] = jnp.maximum(m[h], jnp.max(x.reshape(BLK // 8, 8, BLK), axis=0))
    m = [jnp.max(v, axis=0, keepdims=True) for v in m]

    l = [jnp.zeros((8, BLK), F32) for _ in range(DSA_HEADS)]
    for sc, ts in enumerate(trips):
        for h in range(DSA_HEADS):
            hs = slice(h * BLK, (h + 1) * BLK)
            p = jnp.exp2(x_ref[ts, hs] - m[h])
            pT_ref[ts, hs] = p.astype(BF16)
            l[h] = l[h] + jnp.sum(p.reshape(SROWS // 8, 8, BLK), axis=0)

    for hp in range(DSA_HEADS // 2):
        o2 = jnp.dot(ckvT_ref[0], pT_ref[:, 2 * hp * BLK:(2 * hp + 2) * BLK],
                     preferred_element_type=F32)
        for h in (2 * hp, 2 * hp + 1):
            inv = 1.0 / jnp.sum(l[h], axis=0, keepdims=True)
            oh = (o2[:, (h % 2) * BLK:(h % 2 + 1) * BLK] * inv).T.astype(BF16)
            yh = jnp.dot(oh, wuv_ref[h], preferred_element_type=F32)
            y_ref[0, :, h * HEAD_DIM:(h + 1) * HEAD_DIM] = yh.astype(BF16)


def _dsa_group(g, qiT, wiT, qaT, kidx, ckv, ckvT, wuk, wuv, bias_tiles, tri):
    b, seq, _ = ckv.shape
    nkc = seq // BLK
    nsc = g + 1
    nk = nsc * SROWS
    topk = min(TOPK_MAX, seq // 4)
    qblk = lambda rows: pl.BlockSpec((1, rows, BLK), lambda bi, qi: (bi, 0, g * SUP + qi))
    head3 = lambda cols: pl.BlockSpec((1, nk, cols), lambda bi, qi: (bi, 0, 0))
    const = lambda s: pl.BlockSpec(s, lambda bi, qi: (0,) * len(s))
    hl = DSA_HEADS * BLK
    return pl.pallas_call(
        functools.partial(_dsa_kernel, topk=topk, nsc=nsc, qb0=g * SUP),
        out_shape=jax.ShapeDtypeStruct((b, SROWS, DSA_HEADS * HEAD_DIM), BF16),
        grid=(b, SUP),
        in_specs=[qblk(IDX_HEADS * IDX_DIM), qblk(8), qblk(DSA_HEADS * HEAD_DIM),
                  head3(IDX_DIM), head3(KV_LATENT),
                  pl.BlockSpec((1, KV_LATENT, nk), lambda bi, qi: (bi, 0, 0)),
                  const((DSA_HEADS, KV_LATENT, HEAD_DIM)), const((DSA_HEADS, KV_LATENT, HEAD_DIM)),
                  const((nkc, DSA_HEADS, BLK, BLK)), const((SROWS, SROWS))],
        out_specs=pl.BlockSpec((1, BLK, DSA_HEADS * HEAD_DIM), lambda bi, qi: (bi, qi, 0)),
        scratch_shapes=[pltpu.VMEM((nk, BLK), F32),
                        pltpu.VMEM((32, nk // PLANE_KEYS, 8, BLK), I32),
                        pltpu.VMEM((KV_LATENT, hl), BF16),
                        pltpu.VMEM((nk, hl), F32),
                        pltpu.VMEM((nk, hl), BF16)],
        compiler_params=_cparams(("parallel", "arbitrary")),
        name=f"dsa_g{g}",
    )(qiT, wiT, qaT, kidx, ckv, ckvT, wuk, wuv, bias_tiles, tri)


def _dsa(qiT, wiT, qaT, kidx, ckv, ckvT, wuk, wuv, bias_tiles):
    seq = ckv.shape[1]
    assert seq % SROWS == 0 and seq >= 4 * TOPK_MAX
    tri = jnp.tril(jnp.ones((SROWS, SROWS), BF16))
    groups = [_dsa_group(g, qiT, wiT, qaT, kidx, ckv, ckvT, wuk, wuv, bias_tiles, tri)
              for g in range(seq // SROWS)]
    return jnp.concatenate(groups, axis=1)


DIL_UNROLL = 5
DIL_STEP_ROWS = 2048


def _dil_kernel(q_ref, k_ref, v_ref, bm_ref, o_ref, lse_ref, vT_ref, *, nblk, nres):
    hq = DIL_HPG * BLK
    rowh = lax.broadcasted_iota(I32, (hq, DIL_OUT), 0) // BLK
    colh = lax.broadcasted_iota(I32, (hq, DIL_OUT), 1) // HEAD_DIM
    same_head = rowh == colh

    for res in range(nres):
        cs = slice(res * DIL_OUT, (res + 1) * DIL_OUT)

        for n in range(nblk):
            vT_ref[n] = v_ref[0, n * BLK:(n + 1) * BLK, cs].astype(F32).T.astype(BF16)

        def block(qo, kw, vT, bm, cs=cs):
            q = q_ref[0, pl.ds(qo, BLK), cs]
            qd = jnp.where(same_head, jnp.concatenate([q] * DIL_HPG, axis=0), jnp.zeros((), BF16))
            s = lax.dot_general(kw, qd, (((1,), (1,)), ((), ())), preferred_element_type=F32)
            s = s * (HEAD_DIM ** -0.5) + bm
            m = jnp.max(s, axis=0, keepdims=True)
            e = jnp.exp(s - m)
            l = jnp.sum(e, axis=0, keepdims=True)
            oT = jnp.dot(vT, e.astype(BF16), preferred_element_type=F32)
            inv = 1.0 / l
            lse = m + jnp.log(l)
            outs, lses = [], []
            for hh in range(DIL_HPG):
                qs = slice(hh * BLK, (hh + 1) * BLK)
                outs.append(oT[hh * HEAD_DIM:(hh + 1) * HEAD_DIM, qs] * inv[:, qs])
                lses.append(jnp.broadcast_to(lse[:, qs], (HEAD_DIM, BLK)))
            o_ref[0, pl.ds(qo, BLK), cs] = jnp.concatenate(outs, axis=0).T.astype(BF16)
            lse_ref[0, pl.ds(qo, BLK), cs] = jnp.concatenate(lses, axis=0).T

        block(0, k_ref[0, 0:BLK, cs], vT_ref[0], bm_ref[0, BLK:, :])

        def body(n, carry, cs=cs, block=block):
            ko = pl.multiple_of((n - 1) * BLK, BLK)
            vT = jnp.concatenate([vT_ref[n - 1], vT_ref[n]], axis=1)
            block(pl.multiple_of(n * BLK, BLK), k_ref[0, pl.ds(ko, 2 * BLK), cs], vT, bm_ref[0])
            return carry

        lax.fori_loop(1, nblk, body, 0, unroll=DIL_UNROLL)


def _dilated_group(q, k, v, bm, g, dilation, b):
    c = DIL_OUT
    ls = q.shape[0] // b
    nblk = ls // BLK
    nres = max(1, min(dilation, DIL_STEP_ROWS // ls))
    view = lambda a: a.reshape(b, ls, dilation * c)
    blk = pl.BlockSpec((1, ls, nres * c), lambda bi, ri: (bi, 0, ri))
    o, lse = pl.pallas_call(
        functools.partial(_dil_kernel, nblk=nblk, nres=nres),
        out_shape=[jax.ShapeDtypeStruct((b, ls, dilation * c), BF16),
                   jax.ShapeDtypeStruct((b, ls, dilation * c), F32)],
        grid=(b, dilation // nres),
        in_specs=[blk, blk, blk, pl.BlockSpec((1, 2 * BLK, DIL_HPG * BLK), lambda bi, ri: (g, 0, 0))],
        out_specs=[blk, blk],
        scratch_shapes=[pltpu.VMEM((nblk, c, BLK), BF16)],
        compiler_params=_cparams(("parallel", "parallel")),
        name=f"dilated_g{g}",
    )(view(q), view(k), view(v), bm)
    return o.reshape(b * ls, dilation * c), lse.reshape(b * ls, dilation * c)


def _store_row_tiles(ref, base, val):
    rows, d = val.shape
    dt = d // LANES
    for s in range(dt):
        ref[pl.ds(base * dt + s, rows, stride=dt), :] = val[:, s * LANES:(s + 1) * LANES]


def _load_row_tiles(ref, base, rows, dt):
    return jnp.concatenate([ref[pl.ds(base * dt + s, rows, stride=dt), :] for s in range(dt)], axis=1)


MIX_COLS = 256


def _token_major(ref, stage_ref, r):
    if r == 1:
        return ref[...].astype(F32)
    rows = ref.shape[0]
    for rho in range(r):
        for hf in range(DIL_OUT // LANES):
            c0 = rho * DIL_OUT + hf * LANES
            stage_ref[hf, pl.ds(rho, rows, stride=r), :] = ref[:, c0:c0 + LANES].astype(F32)
    return jnp.concatenate([stage_ref[hf] for hf in range(DIL_OUT // LANES)], axis=1)


def _sigmoid(v):
    return 0.5 * jnp.tanh(0.5 * v) + 0.5


def _mix_kernel(x_ref, ya_ref, o1_ref, o2_ref, o3_ref, l1_ref, l2_ref, l3_ref, gate_ref,
                wa_ref, wb_ref, wo_ref, fg_ref, wr_ref, br_ref,
                h_ref, t_ref, rw_ref, ri_ref, mixed_ref, *stage_refs):
    dils = [r for _, r in DIL_GROUPS]
    o = [_token_major(ref, st, r) for ref, st, r in zip((o1_ref, o2_ref, o3_ref), stage_refs[:3], dils)]
    l1, l2, l3 = [_token_major(ref, st, r) for ref, st, r in zip((l1_ref, l2_ref, l3_ref), stage_refs[3:], dils)]
    mx = jnp.maximum(jnp.maximum(l1, l2), l3)
    e1, e2, e3 = jnp.exp(l1 - mx), jnp.exp(l2 - mx), jnp.exp(l3 - mx)
    inv = 1.0 / (e1 + e2 + e3)
    yb = ((e1 * inv) * o[0] + (e2 * inv) * o[1] + (e3 * inv) * o[2]).astype(BF16)
    ya = ya_ref[...]
    d = x_ref.shape[1]
    for c in range(0, d, MIX_COLS):
        cs = slice(c, c + MIX_COLS)
        a = jnp.dot(ya, wa_ref[:, cs], preferred_element_type=F32)
        bmix = jnp.dot(yb, wb_ref[:, cs], preferred_element_type=F32)
        g0 = _sigmoid(gate_ref[:, cs].astype(F32))
        g1 = _sigmoid(gate_ref[:, d + c:d + c + MIX_COLS].astype(F32))
        mixed_ref[:, cs] = (g0 * a + g1 * bmix).astype(BF16)
    h = x_ref[...] + jnp.dot(mixed_ref[...], wo_ref[...], preferred_element_type=F32)
    h_ref[...] = h
    t = h * lax.rsqrt(jnp.mean(h * h, axis=-1, keepdims=True) + RMS_EPS) * fg_ref[...]
    _store_row_tiles(t_ref, 0, t)

    t_hi = t.astype(BF16)
    t_lo = (t - t_hi.astype(F32)).astype(BF16)
    r1 = jnp.dot(t_hi, wr_ref[...], preferred_element_type=F32)
    r2 = jnp.dot(t_lo, wr_ref[:, :ROUTE_COLS], preferred_element_type=F32)
    logits = r1[:, :ROUTE_COLS] + (r1[:, ROUTE_COLS:] + r2) + br_ref[...]
    lane = lax.broadcasted_iota(I32, logits.shape, 1)
    ninf = -jnp.inf
    big = jnp.int32(10 ** 6)

    def first_argmax(v, vmax):
        return jnp.min(jnp.where(v == vmax, lane, big), axis=-1, keepdims=True)

    gl = jnp.where(lane < N_GROUPS, logits, ninf)
    gmax = jnp.max(gl, axis=-1, keepdims=True)
    gsel = first_argmax(gl, gmax)
    p_g = 1.0 / jnp.sum(jnp.exp(gl - gmax), axis=-1, keepdims=True)
    lo = N_GROUPS + gsel * EXPERTS_PER_GROUP
    el = jnp.where((lane >= lo) & (lane < lo + EXPERTS_PER_GROUP), logits, ninf)
    v1 = jnp.max(el, axis=-1, keepdims=True)
    i1 = first_argmax(el, v1)
    el2 = jnp.where(lane == i1, ninf, el)
    v2 = jnp.max(el2, axis=-1, keepdims=True)
    i2 = first_argmax(el2, v2)
    e2 = jnp.exp(v2 - v1)
    w1 = p_g / (1.0 + e2)
    w2 = p_g * e2 / (1.0 + e2)
    rw_ref[...] = jnp.where(lane == 0, w1, jnp.where(lane == 1, w2, 0.0))
    ri_ref[...] = jnp.where(lane == 0, i1 - N_GROUPS, jnp.where(lane == 1, i2 - N_GROUPS, 0))


def _mix(x2, ya, os_, lses, gates, wa, wb, wo, ffn_norm, wr, br):
    n, d = x2.shape
    tm = MIX_TM
    row = lambda c: pl.BlockSpec((tm, c), lambda i: (i, 0))
    res = lambda a: pl.BlockSpec((tm * a.shape[0] // n, a.shape[1]), lambda i: (i, 0))
    const = lambda s: pl.BlockSpec(s, lambda i: (0, 0))
    return pl.pallas_call(
        _mix_kernel,
        out_shape=[jax.ShapeDtypeStruct((n, d), F32), jax.ShapeDtypeStruct((n * (d // LANES), LANES), F32),
                   jax.ShapeDtypeStruct((n, ROUTE_COLS), F32), jax.ShapeDtypeStruct((n, ROUTE_COLS), I32)],
        grid=(n // tm,),
        in_specs=[row(d), row(512)] + [res(a) for a in os_] + [res(a) for a in lses] + [row(2 * d),
                  const(wa.shape), const(wb.shape), const(wo.shape), const((1, d)),
                  const(wr.shape), const((1, ROUTE_COLS))],
        out_specs=[row(d), pl.BlockSpec((tm * (d // LANES), LANES), lambda i: (i, 0)),
                   row(ROUTE_COLS), row(ROUTE_COLS)],
        scratch_shapes=[pltpu.VMEM((tm, d), BF16)] + [pltpu.VMEM((DIL_OUT // LANES, tm, LANES), F32)] * 6,
        compiler_params=_cparams(("parallel",)),
        name="mix",
    )(x2, ya, *os_, *lses, gates, wa, wb, wo, ffn_norm.reshape(1, d), wr, br)


GATHER_UNROLL = 32


def _start_row_gather(src_hbm, idx_ref, nrows, dt, buf, sem, slot):
    def body(g, c):
        for u in range(GATHER_UNROLL):
            r = g * GATHER_UNROLL + u
            src = pl.multiple_of(idx_ref[0, 0, r], dt)
            dst = pl.multiple_of((slot * nrows + r) * dt, dt)
            pltpu.make_async_copy(src_hbm.at[pl.ds(src, dt)], buf.at[pl.ds(dst, dt)],
                                  sem.at[slot]).start(priority=u % 2)
        return c
    lax.fori_loop(0, nrows // GATHER_UNROLL, body, 0)


def _wait_row_gather(src_hbm, nrows, dt, buf, sem, slot):
    dst = pl.multiple_of(slot * nrows * dt, dt)
    pltpu.make_async_copy(src_hbm.at[pl.ds(0, nrows * dt)], buf.at[pl.ds(dst, nrows * dt)], sem.at[slot]).wait()


def _gather_pipeline(i, nsteps, src_hbm, cur_ref, nxt_ref, nrows, dt, buf, sem):
    slot = i % 2

    @pl.when(i == 0)
    def _():
        _start_row_gather(src_hbm, cur_ref, nrows, dt, buf, sem, 0)

    @pl.when(i + 1 < nsteps)
    def _():
        _start_row_gather(src_hbm, nxt_ref, nrows, dt, buf, sem, 1 - slot)

    _wait_row_gather(src_hbm, nrows, dt, buf, sem, slot)
    return slot


def _expert_kernel(te_ref, cur_ref, nxt_ref, t_hbm, wg_ref, wu_ref, wd_ref, y_ref, buf, sem):
    i = pl.program_id(0)
    dt = wg_ref.shape[1] // LANES
    slot = _gather_pipeline(i, pl.num_programs(0), t_hbm, cur_ref, nxt_ref, MOE_TM, dt, buf, sem)
    xt = _load_row_tiles(buf, slot * MOE_TM, MOE_TM, dt).astype(BF16)
    hg = jnp.dot(xt, wg_ref[0], preferred_element_type=F32)
    hu = jnp.dot(xt, wu_ref[0], preferred_element_type=F32)
    hid = (hg * jax.nn.sigmoid(hg)) * hu
    _store_row_tiles(y_ref, 0, jnp.dot(hid.astype(BF16), wd_ref[0], preferred_element_type=F32))


def _experts(tile_expert, row_token, t, wg, wu, wd, d):
    dt = d // LANES
    ntiles = tile_expert.shape[0]
    ff = wg.shape[2]
    tok3 = (row_token * dt).reshape(ntiles, 1, MOE_TM)
    smem_cur = pl.BlockSpec((1, 1, MOE_TM), lambda i, te: (i, 0, 0), memory_space=pltpu.SMEM)
    smem_nxt = pl.BlockSpec((1, 1, MOE_TM), lambda i, te: (jnp.minimum(i + 1, ntiles - 1), 0, 0),
                            memory_space=pltpu.SMEM)
    wspec = lambda s: pl.BlockSpec((1,) + s, lambda i, te: (te[i], 0, 0))
    return pl.pallas_call(
        _expert_kernel,
        out_shape=jax.ShapeDtypeStruct((ntiles * MOE_TM * dt, LANES), F32),
        grid_spec=pltpu.PrefetchScalarGridSpec(
            num_scalar_prefetch=1,
            grid=(ntiles,),
            in_specs=[smem_cur, smem_nxt, pl.BlockSpec(memory_space=pl.ANY),
                      wspec((d, ff)), wspec((d, ff)), wspec((ff, d))],
            out_specs=pl.BlockSpec((MOE_TM * dt, LANES), lambda i, te: (i, 0)),
            scratch_shapes=[pltpu.VMEM((2 * MOE_TM * dt, LANES), F32), pltpu.SemaphoreType.DMA((2,))],
        ),
        compiler_params=_cparams(("arbitrary",)),
        name="experts",
    )(tile_expert, tok3, tok3, t, wg, wu, wd)


def _final_kernel(cur_ref, nxt_ref, y_hbm, h_ref, rw_ref, fn_ref, o_ref, buf, sem):
    i = pl.program_id(0)
    dt = h_ref.shape[1] // LANES
    slot = _gather_pipeline(i, pl.num_programs(0), y_hbm, cur_ref, nxt_ref, 2 * FIN_TM, dt, buf, sem)
    rw = rw_ref[...]
    y0 = _load_row_tiles(buf, slot * 2 * FIN_TM, FIN_TM, dt)
    y1 = _load_row_tiles(buf, slot * 2 * FIN_TM + FIN_TM, FIN_TM, dt)
    h = h_ref[...] + rw[:, 0:1] * y0 + rw[:, 1:2] * y1
    o_ref[...] = h * lax.rsqrt(jnp.mean(h * h, axis=-1, keepdims=True) + RMS_EPS) * fn_ref[...]


def _final(pos_tiles, y_sorted, h, rw, final_norm):
    n, d = h.shape
    tm = FIN_TM
    nt = n // tm
    smem_cur = pl.BlockSpec((1, 1, 2 * tm), lambda i: (i, 0, 0), memory_space=pltpu.SMEM)
    smem_nxt = pl.BlockSpec((1, 1, 2 * tm), lambda i: (jnp.minimum(i + 1, nt - 1), 0, 0),
                            memory_space=pltpu.SMEM)
    return pl.pallas_call(
        _final_kernel,
        out_shape=jax.ShapeDtypeStruct((n, d), F32),
        grid=(nt,),
        in_specs=[smem_cur, smem_nxt, pl.BlockSpec(memory_space=pl.ANY),
                  pl.BlockSpec((tm, d), lambda i: (i, 0)), pl.BlockSpec((tm, ROUTE_COLS), lambda i: (i, 0)),
                  pl.BlockSpec((1, d), lambda i: (0, 0))],
        out_specs=pl.BlockSpec((tm, d), lambda i: (i, 0)),
        scratch_shapes=[pltpu.VMEM((2 * 2 * tm * (d // LANES), LANES), F32), pltpu.SemaphoreType.DMA((2,))],
        compiler_params=_cparams(("arbitrary",)),
        name="final",
    )(pos_tiles * (d // LANES), pos_tiles * (d // LANES), y_sorted, h, rw, final_norm.reshape(1, d))


def _route_plan(gid):
    n = gid.shape[0]
    e = gid.reshape(-1)
    onehot = (e[:, None] == jnp.arange(N_EXPERTS, dtype=I32)[None, :]).astype(F32)
    chunk = 256
    oh3 = onehot.reshape(-1, chunk, N_EXPERTS)
    within = jnp.einsum("ij,tjk->tik", jnp.tril(jnp.ones((chunk, chunk), F32)), oh3)
    totals = within[:, -1, :]
    before = jnp.cumsum(totals, axis=0) - totals
    csum = (within + before[:, None, :]).reshape(-1, N_EXPERTS)
    rank = jnp.sum(csum * onehot, axis=1).astype(I32) - 1
    counts = (before[-1] + totals[-1]).astype(I32)
    padded = ((counts + MOE_TM - 1) // MOE_TM) * MOE_TM
    seg_end = jnp.cumsum(padded)
    pos = (seg_end - padded)[e] + rank
    nrows = 2 * n + N_EXPERTS * MOE_TM
    tile_start = jnp.arange(nrows // MOE_TM, dtype=I32) * MOE_TM
    tile_expert = jnp.minimum(jnp.sum(tile_start[:, None] >= seg_end[None, :], axis=1), N_EXPERTS - 1).astype(I32)
    order = jnp.argsort(e, stable=True).astype(I32)
    first = jnp.cumsum(counts) - counts
    row_expert = jnp.repeat(tile_expert, MOE_TM)
    src = jnp.arange(nrows, dtype=I32) - (seg_end - padded)[row_expert] + first[row_expert]
    row_token = order[jnp.clip(src, 0, 2 * n - 1)] // 2
    return row_token, tile_expert, pos.reshape(n, 2)


def _pack_w_in(w):
    d = w.shape[0]
    o_kv, o_qi, o_ki, o_wi = 512, 768, 1024, 1088
    o_dil = o_wi + IDX_HEADS
    o_gate = o_dil + 9 * 256
    pad = jnp.zeros((d, LANES - IDX_DIM - IDX_HEADS), w.dtype)
    packed = jnp.concatenate([w[:, :o_ki], w[:, o_ki:o_wi], w[:, o_wi:o_dil], pad, w[:, o_dil:o_gate],
                              w[:, o_gate:]], axis=1)
    assert packed.shape[1] == C_END
    return packed.astype(BF16)


def kernel(x, attn_norm, w_in, kv_norm, w_uk, w_uv, rel_bias, w_branch_a, w_branch_b, w_out, ffn_norm,
           w_router_group, b_router_group, w_router_expert, b_router_expert, w_gate, w_up, w_down,
           final_norm):
    b, seq, d = x.shape
    n = b * seq
    nkc = seq // BLK
    assert w_in.shape[0] == 1, "one layer"
    x2 = x.reshape(n, d)

    outs = _proj(x2, attn_norm[0], _pack_w_in(w_in[0]), kv_norm[0])
    qa, ckv, qi, kw = outs[:4]
    dil = outs[4:13]
    gates = outs[13]

    dsa_bias, dil_bias = _bias_tiles(rel_bias, nkc)

    t3 = lambda a: jnp.swapaxes(a.reshape(b, seq, a.shape[-1]), 1, 2)
    kw3 = kw.reshape(b, seq, LANES)
    wiT = jnp.swapaxes(kw3[:, :, IDX_DIM:IDX_DIM + 8], 1, 2)
    kidx = kw3[:, :, :IDX_DIM].astype(BF16)
    ckv3 = ckv.reshape(b, seq, KV_LATENT)
    ckvT = jnp.swapaxes(ckv3, 1, 2)
    ya = _dsa(t3(qi), wiT, t3(qa), kidx, ckv3, ckvT, w_uk[0].astype(BF16), w_uv[0].astype(BF16), dsa_bias)

    os_, lses = [], []
    for g, (_, dilation) in enumerate(DIL_GROUPS):
        o, lse = _dilated_group(dil[g], dil[3 + g], dil[6 + g], dil_bias, g, dilation, b)
        os_.append(o)
        lses.append(lse)

    wr = jnp.concatenate([w_router_group[0],
                          jnp.swapaxes(w_router_expert[0], 0, 1).reshape(d, N_EXPERTS),
                          jnp.zeros((d, ROUTE_COLS - N_GROUPS - N_EXPERTS), F32)], axis=1)
    wr_hi = wr.astype(BF16)
    wr = jnp.concatenate([wr_hi, (wr - wr_hi.astype(F32)).astype(BF16)], axis=1)
    br = jnp.concatenate([b_router_group[0], b_router_expert[0].reshape(-1),
                          jnp.zeros((ROUTE_COLS - N_GROUPS - N_EXPERTS,), F32)]).reshape(1, ROUTE_COLS)
    h, t, rw, ri = _mix(x2, ya.reshape(n, -1), os_, lses, gates,
                        w_branch_a[0].astype(BF16), w_branch_b[0].astype(BF16), w_out[0].astype(BF16),
                        ffn_norm[0], wr, br)

    row_token, tile_expert, pos = _route_plan(ri[:, :2])
    y_sorted = _experts(tile_expert, row_token, t, w_gate[0].astype(BF16), w_up[0].astype(BF16),
                        w_down[0].astype(BF16), d)

    pos_tiles = jnp.swapaxes(pos.reshape(n // FIN_TM, FIN_TM, 2), 1, 2).reshape(n // FIN_TM, 1, 2 * FIN_TM)
    out = _final(pos_tiles, y_sorted, h, rw, final_norm)
    return out.reshape(b, seq, d)
```

```python
import functools
import math

import numpy as np
import jax
import jax.numpy as jnp
from jax import lax
from jax.experimental import pallas as pl
from jax.experimental.pallas import tpu as pltpu

F32 = jnp.float32
BF16 = jnp.bfloat16
I32 = jnp.int32

LANES = 128
VMEM_LIMIT_BYTES = 56 * 1024 * 1024

HEAD_DIM = 64
DSA_HEADS = 8
KV_LATENT = 256
IDX_HEADS = 4
IDX_DIM = 64
TOPK_MAX = 256
DIL_GROUPS = ((128, 1), (512, 4), (2048, 16))
DIL_HPG = 4
DIL_OUT = DIL_HPG * HEAD_DIM
NUM_BUCKETS = 32
MAX_DISTANCE = 2048
N_GROUPS = 4
EXPERTS_PER_GROUP = 8
N_EXPERTS = N_GROUPS * EXPERTS_PER_GROUP
RMS_EPS = 1e-6
NEG = -1e30
LOG2E = math.log2(math.e)
INT_MIN = -2 ** 31

BLK = 128
PROJ_TM = 512
MIX_TM = 512
MOE_TM = 256
FIN_TM = 512
ROUTE_COLS = 128


def _cparams(sem):
    return pltpu.CompilerParams(dimension_semantics=sem, vmem_limit_bytes=VMEM_LIMIT_BYTES)


C_QA = 0
C_KV = 512
C_QI = 768
C_KW = 1024
C_DIL = 1152
C_GATE = C_DIL + 9 * 256
C_END = C_GATE + 2048


def _proj_kernel(x_ref, g_ref, w_ref, kvg_ref, qaT_ref, ckv_ref, ckvT_ref, qiT_ref, kw_ref, wiT_ref, *rest):
    dil_refs = rest[:9]
    gate_ref = rest[9]
    stage_ref = rest[10]
    x = x_ref[...]
    u = x * lax.rsqrt(jnp.mean(x * x, axis=-1, keepdims=True) + RMS_EPS) * g_ref[...]
    u = u.astype(BF16)

    def mm(a, b):
        return jnp.dot(u, w_ref[:, a:b], preferred_element_type=F32)

    qaT_ref[...] = mm(C_QA, C_KV).T.astype(BF16)
    c = mm(C_KV, C_QI)
    c = c * lax.rsqrt(jnp.mean(c * c, axis=-1, keepdims=True) + RMS_EPS) * kvg_ref[...]
    ckv_ref[...] = c.astype(BF16)
    ckvT_ref[...] = c.T.astype(BF16)
    qiT_ref[...] = mm(C_QI, C_KW).T.astype(BF16)
    kw = mm(C_KW, C_DIL)
    kw_ref[...] = kw
    wiT_ref[...] = kw.T[IDX_DIM:IDX_DIM + 8, :]
    for j in range(9):
        val = mm(C_DIL + 256 * j, C_DIL + 256 * (j + 1))
        r = DIL_GROUPS[j % 3][1]
        if r == 1:
            dil_refs[j][...] = val.astype(BF16)
        else:
            for hf in range(2):
                stage_ref[hf] = val[:, hf * LANES:(hf + 1) * LANES]
            for rho in range(r):
                for hf in range(2):
                    dil_refs[j][:, rho * 256 + hf * LANES:rho * 256 + (hf + 1) * LANES] = (
                        stage_ref[hf, pl.ds(rho, val.shape[0] // r, stride=r), :].astype(BF16))
    for j in range(4):
        gate_ref[:, 512 * j:512 * (j + 1)] = mm(C_GATE + 512 * j, C_GATE + 512 * (j + 1)).astype(BF16)


def _proj(x2, attn_norm, w_packed, kv_norm):
    n, d = x2.shape
    tm = PROJ_TM
    row = lambda i: (i, 0)
    const = lambda i: (0, 0)
    col = lambda i: (0, i)
    tok_lanes = lambda rows, dt: (jax.ShapeDtypeStruct((rows, n), dt), pl.BlockSpec((rows, tm), col))
    tok_rows = lambda r, cols, dt: (jax.ShapeDtypeStruct((n // r, cols), dt), pl.BlockSpec((tm // r, cols), row))
    outs = [tok_lanes(512, BF16), tok_rows(1, 256, BF16), tok_lanes(256, BF16), tok_lanes(256, BF16),
            tok_rows(1, 128, F32), tok_lanes(8, F32)]
    outs += [tok_rows(DIL_GROUPS[j % 3][1], DIL_GROUPS[j % 3][1] * 256, BF16) for j in range(9)]
    outs += [tok_rows(1, 2048, BF16)]
    outs, out_specs = [o for o, _ in outs], [sp for _, sp in outs]
    return pl.pallas_call(
        _proj_kernel,
        out_shape=outs,
        grid=(n // tm,),
        in_specs=[pl.BlockSpec((tm, d), row), pl.BlockSpec((1, d), const),
                  pl.BlockSpec((d, C_END), const), pl.BlockSpec((1, KV_LATENT), const)],
        out_specs=out_specs,
        scratch_shapes=[pltpu.VMEM((2, tm, LANES), F32)],
        compiler_params=_cparams(("parallel",)),
        name="proj",
    )(x2, attn_norm.reshape(1, d), w_packed, kv_norm.reshape(1, KV_LATENT))


def _bucket_thresholds():
    max_exact = NUM_BUCKETS // 2
    d = np.arange(0, MAX_DISTANCE + 1)
    nf = np.maximum(d, 1).astype(np.float32)
    large = max_exact + (np.log(nf / np.float32(max_exact)) / np.float32(math.log(MAX_DISTANCE / max_exact))
                         * np.float32(NUM_BUCKETS - max_exact)).astype(np.int32)
    large = np.minimum(large, NUM_BUCKETS - 1)
    bucket = np.where(d < max_exact, d, large)
    assert np.all(np.diff(bucket) >= 0)
    return [int(np.argmax(bucket >= b)) for b in range(1, NUM_BUCKETS)]


_BUCKET_THR = _bucket_thresholds()


def _bias_from_distance(dist, tab_ref, heads):
    masks = [dist >= t for t in _BUCKET_THR]
    out = []
    for h in heads:
        v = jnp.full(dist.shape, tab_ref[0, h], F32)
        for b in range(1, NUM_BUCKETS):
            v = jnp.where(masks[b - 1], tab_ref[b, h], v)
        out.append(v)
    return out


def _dsa_bias_kernel(tab_ref, o_ref):
    delta = pl.program_id(0)
    j = lax.broadcasted_iota(I32, (BLK, BLK), 0)
    i = lax.broadcasted_iota(I32, (BLK, BLK), 1)
    dist = jnp.maximum(delta * BLK + i - j, 0)
    tiles = _bias_from_distance(dist, tab_ref, range(DSA_HEADS))
    for h in range(DSA_HEADS):
        o_ref[0, h] = tiles[h] * LOG2E


def _dil_bias_kernel(tab_ref, o_ref, *, dilations):
    g = pl.program_id(0)
    j = lax.broadcasted_iota(I32, (2 * BLK, BLK), 0)
    i = lax.broadcasted_iota(I32, (2 * BLK, BLK), 1)
    step = i + BLK - j
    valid = (step >= 0) & (step <= BLK)
    for gi, r in enumerate(dilations):
        @pl.when(g == gi)
        def _():
            dist = jnp.maximum(step, 0) * r
            heads = [DSA_HEADS + gi * DIL_HPG + hh for hh in range(DIL_HPG)]
            tiles = _bias_from_distance(dist, tab_ref, heads)
            for hh in range(DIL_HPG):
                o_ref[0, :, hh * BLK:(hh + 1) * BLK] = jnp.where(valid, tiles[hh], NEG)


def _bias_tiles(rel_bias, nkc):
    smem = pl.BlockSpec(memory_space=pltpu.SMEM)
    dsa = pl.pallas_call(
        _dsa_bias_kernel,
        out_shape=jax.ShapeDtypeStruct((nkc, DSA_HEADS, BLK, BLK), F32),
        grid=(nkc,),
        in_specs=[smem],
        out_specs=pl.BlockSpec((1, DSA_HEADS, BLK, BLK), lambda d: (d, 0, 0, 0)),
        compiler_params=_cparams(("parallel",)),
        name="dsa_bias",
    )(rel_bias)
    dil = pl.pallas_call(
        functools.partial(_dil_bias_kernel, dilations=tuple(r for _, r in DIL_GROUPS)),
        out_shape=jax.ShapeDtypeStruct((len(DIL_GROUPS), 2 * BLK, DIL_HPG * BLK), F32),
        grid=(len(DIL_GROUPS),),
        in_specs=[smem],
        out_specs=pl.BlockSpec((1, 2 * BLK, DIL_HPG * BLK), lambda g: (g, 0, 0)),
        compiler_params=_cparams(("parallel",)),
        name="dil_bias",
    )(rel_bias)
    return dsa, dil


SUP = 2
SROWS = SUP * BLK


PLANE_KEYS = 32 * 8


def _bit_planes(words):
    x = list(words)
    j, m = 16, 0x0000FFFF
    while j:
        k = 0
        while k < 32:
            t = (x[k] ^ lax.shift_right_logical(x[k + j], jnp.int32(j))) & m
            x[k] = x[k] ^ t
            x[k + j] = x[k + j] ^ jnp.left_shift(t, jnp.int32(j))
            k = (k + j + 1) & ~j
        j >>= 1
        m = (m ^ (m << j)) & 0xFFFFFFFF
        m = m - (1 << 32) if m >= (1 << 31) else m
    return x


def _dsa_kernel(qiT_ref, wiT_ref, qaT_ref, kidx_ref, ckv_ref, ckvT_ref, wuk_ref, wuv_ref, bias_ref, tri_ref,
                y_ref, sc_ref, planes_ref, qlT_ref, x_ref, pT_ref, *, topk, nsc, qb0):
    qb = qb0 + pl.program_id(1)
    row = lax.broadcasted_iota(I32, (SROWS, BLK), 0)
    col = lax.broadcasted_iota(I32, (SROWS, BLK), 1)
    trips = [slice(sc * SROWS, (sc + 1) * SROWS) for sc in range(nsc)]

    def causal(sc):
        return row <= col + (qb * BLK - sc * SROWS)

    for h in range(DSA_HEADS):
        ql = jnp.dot(wuk_ref[h], qaT_ref[h * HEAD_DIM:(h + 1) * HEAD_DIM, :],
                     preferred_element_type=F32) * (HEAD_DIM ** -0.5 * LOG2E)
        qlT_ref[:, h * BLK:(h + 1) * BLK] = ql.astype(BF16)

    wq = wiT_ref[...] * (IDX_HEADS ** -0.5)
    for sc, ts in enumerate(trips):
        kx = kidx_ref[0, ts, :]
        acc = jnp.zeros((SROWS, BLK), F32)
        for h in range(IDX_HEADS):
            s = jnp.dot(kx, qiT_ref[h * IDX_DIM:(h + 1) * IDX_DIM, :],
                        preferred_element_type=F32) * (IDX_DIM ** -0.5)
            acc = acc + wq[h:h + 1, :] * jnp.maximum(s, 0.0)
        acc = jnp.where(causal(sc), acc, NEG)
        sc_ref[ts, :] = acc
        bits = pltpu.bitcast(acc, I32)
        bits = jnp.where(bits == INT_MIN, 0, bits)
        ukey = bits ^ ((bits >> 31) & 0x7FFFFFFF) ^ INT_MIN
        for grp in range(SROWS // PLANE_KEYS):
            tiles = [ukey[grp * PLANE_KEYS + j * 8:grp * PLANE_KEYS + (j + 1) * 8] for j in range(32)]
            for b, plane in enumerate(_bit_planes(tiles)):
                planes_ref[b, sc * (SROWS // PLANE_KEYS) + grp] = plane

    def count(pred):
        cnt = jnp.zeros((8, BLK), I32)
        for ts in trips:
            cnt = cnt + jnp.sum(jnp.where(pred(sc_ref[ts, :]), 1, 0).reshape(SROWS // 8, 8, BLK), axis=0)
        return jnp.sum(cnt, axis=0, keepdims=True)

    def as_float(key):
        return pltpu.bitcast(key ^ ((key >> 31) & 0x7FFFFFFF), F32)

    def bit_body(it, carry):
        alive, above, code = carry
        ones = alive & planes_ref[it]
        c = jnp.sum(jnp.sum(lax.population_count(ones), axis=0), axis=0, keepdims=True)
        take = above + c >= topk
        alive = jnp.where(take, ones, alive ^ ones)
        above = jnp.where(take, above, above + c)
        code = code | jnp.where(take, jnp.left_shift(jnp.int32(1), 31 - it), 0)
        return alive, above, code

    nw = nsc * (SROWS // PLANE_KEYS)
    _, _, code = lax.fori_loop(
        0, 32, bit_body,
        (jnp.full((nw, 8, BLK), -1, I32), jnp.zeros((1, BLK), I32), jnp.zeros((1, BLK), I32)))
    guess = as_float(code ^ INT_MIN)
    n_gt_guess = count(lambda v: v > guess)
    proven = (n_gt_guess < topk) & (count(lambda v: v >= guess) >= topk)

    def bisect():
        def thr_body(it, lo):
            cand = lo + jnp.left_shift(jnp.int32(1), 31 - it)
            cand_f = as_float(cand)
            return jnp.where(count(lambda v: v >= cand_f) >= topk, cand, lo)
        t = as_float(lax.fori_loop(0, 32, thr_body, jnp.full((1, BLK), INT_MIN, I32)))
        return t, count(lambda v: v > t)

    thr, n_gt = lax.cond(jnp.min(jnp.where(proven, 1, 0)) > 0, lambda: (guess, n_gt_guess), bisect)
    ties_wanted = (topk - n_gt).astype(F32)

    m = [jnp.full((8, BLK), NEG, F32) for _ in range(DSA_HEADS)]
    ties_before = jnp.zeros((1, BLK), F32)
    for sc, ts in enumerate(trips):
        k = sc_ref[ts, :]
        tie = k == thr
        tie_rank = jnp.dot(tri_ref[...], jnp.where(tie, 1.0, 0.0).astype(BF16),
                           preferred_element_type=F32) + ties_before
        ties_before = tie_rank[SROWS - 1:SROWS, :]
        sel = ((k > thr) | (tie & (tie_rank <= ties_wanted))) & causal(sc)
        am = jnp.where(sel, 0.0, NEG)
        ck = ckv_ref[0, ts, :]
        for hp in range(DSA_HEADS // 2):
            lg2 = jnp.dot(ck, qlT_ref[:, 2 * hp * BLK:(2 * hp + 2) * BLK], preferred_element_type=F32)
            for h in (2 * hp, 2 * hp + 1):
                lg = lg2[:, (h % 2) * BLK:(h % 2 + 1) * BLK]
                for j in range(SUP):
                    rs = slice(j * BLK, (j + 1) * BLK)
                    delta = jnp.maximum(qb - (sc * SUP + j), 0)
                    x = lg[rs] + bias_ref[delta, h] + am[rs]
                    x_ref[sc * SROWS + j * BLK:sc * SROWS + (j + 1) * BLK, h * BLK:(h + 1) * BLK] = x
                    m[h] = jnp.maximum(m[h], jnp.max(x.reshape(BLK // 8, 8, BLK), axis=0))
    m = [jnp.max(v, axis=0, keepdims=True) for v in m]

    l = [jnp.zeros((8, BLK), F32) for _ in range(DSA_HEADS)]
    for sc, ts in enumerate(trips):
        for h in range(DSA_HEADS):
            hs = slice(h * BLK, (h + 1) * BLK)
            p = jnp.exp2(x_ref[ts, hs] - m[h])
            pT_ref[ts, hs] = p.astype(BF16)
            l[h] = l[h] + jnp.sum(p.reshape(SROWS // 8, 8, BLK), axis=0)

    for hp in range(DSA_HEADS // 2):
        o2 = jnp.dot(ckvT_ref[:, :nsc * SROWS], pT_ref[:, 2 * hp * BLK:(2 * hp + 2) * BLK],
                     preferred_element_type=F32)
        for h in (2 * hp, 2 * hp + 1):
            inv = 1.0 / jnp.sum(l[h], axis=0, keepdims=True)
            oh = (o2[:, (h % 2) * BLK:(h % 2 + 1) * BLK] * inv).T.astype(BF16)
            yh = jnp.dot(oh, wuv_ref[h], preferred_element_type=F32)
            y_ref[0, :, h * HEAD_DIM:(h + 1) * HEAD_DIM] = yh.astype(BF16)


def _dsa_group(g, qiT, wiT, qaT, kidx, ckv, ckvT, wuk, wuv, bias_tiles, tri):
    b, seq, _ = ckv.shape
    nkc = seq // BLK
    nsc = g + 1
    nk = nsc * SROWS
    topk = min(TOPK_MAX, seq // 4)
    qblk = lambda rows: pl.BlockSpec((rows, BLK), lambda bi, qi: (0, bi * nkc + g * SUP + qi))
    head3 = lambda cols: pl.BlockSpec((1, nk, cols), lambda bi, qi: (bi, 0, 0))
    const = lambda s: pl.BlockSpec(s, lambda bi, qi: (0,) * len(s))
    hl = DSA_HEADS * BLK
    return pl.pallas_call(
        functools.partial(_dsa_kernel, topk=topk, nsc=nsc, qb0=g * SUP),
        out_shape=jax.ShapeDtypeStruct((b, SROWS, DSA_HEADS * HEAD_DIM), BF16),
        grid=(b, SUP),
        in_specs=[qblk(IDX_HEADS * IDX_DIM), qblk(8), qblk(DSA_HEADS * HEAD_DIM),
                  head3(IDX_DIM), head3(KV_LATENT),
                  pl.BlockSpec((KV_LATENT, seq), lambda bi, qi: (0, bi)),
                  const((DSA_HEADS, KV_LATENT, HEAD_DIM)), const((DSA_HEADS, KV_LATENT, HEAD_DIM)),
                  const((nkc, DSA_HEADS, BLK, BLK)), const((SROWS, SROWS))],
        out_specs=pl.BlockSpec((1, BLK, DSA_HEADS * HEAD_DIM), lambda bi, qi: (bi, qi, 0)),
        scratch_shapes=[pltpu.VMEM((nk, BLK), F32),
                        pltpu.VMEM((32, nk // PLANE_KEYS, 8, BLK), I32),
                        pltpu.VMEM((KV_LATENT, hl), BF16),
                        pltpu.VMEM((nk, hl), F32),
                        pltpu.VMEM((nk, hl), BF16)],
        compiler_params=_cparams(("parallel", "arbitrary")),
        name=f"dsa_g{g}",
    )(qiT, wiT, qaT, kidx, ckv, ckvT, wuk, wuv, bias_tiles, tri)


def _dsa(qiT, wiT, qaT, kidx, ckv, ckvT, wuk, wuv, bias_tiles):
    seq = ckv.shape[1]
    assert seq % SROWS == 0 and seq >= 4 * TOPK_MAX
    tri = jnp.tril(jnp.ones((SROWS, SROWS), BF16))
    groups = [_dsa_group(g, qiT, wiT, qaT, kidx, ckv, ckvT, wuk, wuv, bias_tiles, tri)
              for g in range(seq // SROWS)]
    return jnp.concatenate(groups, axis=1)


DIL_UNROLL = 5
DIL_STEP_ROWS = 2048


def _dil_kernel(q_ref, k_ref, v_ref, bm_ref, o_ref, lse_ref, vT_ref, *, nblk, nres):
    hq = DIL_HPG * BLK
    rowh = lax.broadcasted_iota(I32, (hq, DIL_OUT), 0) // BLK
    colh = lax.broadcasted_iota(I32, (hq, DIL_OUT), 1) // HEAD_DIM
    same_head = rowh == colh

    for res in range(nres):
        cs = slice(res * DIL_OUT, (res + 1) * DIL_OUT)

        for n in range(nblk):
            vT_ref[n] = v_ref[0, n * BLK:(n + 1) * BLK, cs].astype(F32).T.astype(BF16)

        def block(qo, kw, vT, bm, cs=cs):
            q = q_ref[0, pl.ds(qo, BLK), cs]
            qd = jnp.where(same_head, jnp.concatenate([q] * DIL_HPG, axis=0), jnp.zeros((), BF16))
            s = lax.dot_general(kw, qd, (((1,), (1,)), ((), ())), preferred_element_type=F32)
            s = s * (HEAD_DIM ** -0.5) + bm
            m = jnp.max(s, axis=0, keepdims=True)
            e = jnp.exp(s - m)
            l = jnp.sum(e, axis=0, keepdims=True)
            oT = jnp.dot(vT, e.astype(BF16), preferred_element_type=F32)
            inv = 1.0 / l
            lse = m + jnp.log(l)
            outs, lses = [], []
            for hh in range(DIL_HPG):
                qs = slice(hh * BLK, (hh + 1) * BLK)
                outs.append(oT[hh * HEAD_DIM:(hh + 1) * HEAD_DIM, qs] * inv[:, qs])
                lses.append(jnp.broadcast_to(lse[:, qs], (HEAD_DIM, BLK)))
            o_ref[0, pl.ds(qo, BLK), cs] = jnp.concatenate(outs, axis=0).T.astype(BF16)
            lse_ref[0, pl.ds(qo, BLK), cs] = jnp.concatenate(lses, axis=0).T

        block(0, k_ref[0, 0:BLK, cs], vT_ref[0], bm_ref[0, BLK:, :])

        def body(n, carry, cs=cs, block=block):
            ko = pl.multiple_of((n - 1) * BLK, BLK)
            vT = jnp.concatenate([vT_ref[n - 1], vT_ref[n]], axis=1)
            block(pl.multiple_of(n * BLK, BLK), k_ref[0, pl.ds(ko, 2 * BLK), cs], vT, bm_ref[0])
            return carry

        lax.fori_loop(1, nblk, body, 0, unroll=DIL_UNROLL)


def _dilated_group(q, k, v, bm, g, dilation, b):
    c = DIL_OUT
    ls = q.shape[0] // b
    nblk = ls // BLK
    nres = max(1, min(dilation, DIL_STEP_ROWS // ls))
    view = lambda a: a.reshape(b, ls, dilation * c)
    blk = pl.BlockSpec((1, ls, nres * c), lambda bi, ri: (bi, 0, ri))
    o, lse = pl.pallas_call(
        functools.partial(_dil_kernel, nblk=nblk, nres=nres),
        out_shape=[jax.ShapeDtypeStruct((b, ls, dilation * c), BF16),
                   jax.ShapeDtypeStruct((b, ls, dilation * c), F32)],
        grid=(b, dilation // nres),
        in_specs=[blk, blk, blk, pl.BlockSpec((1, 2 * BLK, DIL_HPG * BLK), lambda bi, ri: (g, 0, 0))],
        out_specs=[blk, blk],
        scratch_shapes=[pltpu.VMEM((nblk, c, BLK), BF16)],
        compiler_params=_cparams(("parallel", "parallel")),
        name=f"dilated_g{g}",
    )(view(q), view(k), view(v), bm)
    return o.reshape(b * ls, dilation * c), lse.reshape(b * ls, dilation * c)


def _store_row_tiles(ref, base, val):
    rows, d = val.shape
    dt = d // LANES
    for s in range(dt):
        ref[pl.ds(base * dt + s, rows, stride=dt), :] = val[:, s * LANES:(s + 1) * LANES]


def _load_row_tiles(ref, base, rows, dt):
    return jnp.concatenate([ref[pl.ds(base * dt + s, rows, stride=dt), :] for s in range(dt)], axis=1)


MIX_COLS = 256


def _token_major(ref, stage_ref, r):
    if r == 1:
        return ref[...].astype(F32)
    rows = ref.shape[0]
    for rho in range(r):
        for hf in range(DIL_OUT // LANES):
            c0 = rho * DIL_OUT + hf * LANES
            stage_ref[hf, pl.ds(rho, rows, stride=r), :] = ref[:, c0:c0 + LANES].astype(F32)
    return jnp.concatenate([stage_ref[hf] for hf in range(DIL_OUT // LANES)], axis=1)


def _sigmoid(v):
    return 0.5 * jnp.tanh(0.5 * v) + 0.5


def _mix_kernel(x_ref, ya_ref, o1_ref, o2_ref, o3_ref, l1_ref, l2_ref, l3_ref, gate_ref,
                wa_ref, wb_ref, wo_ref, fg_ref, wr_ref, br_ref,
                h_ref, t_ref, rw_ref, ri_ref, mixed_ref, *stage_refs):
    dils = [r for _, r in DIL_GROUPS]
    o = [_token_major(ref, st, r) for ref, st, r in zip((o1_ref, o2_ref, o3_ref), stage_refs[:3], dils)]
    l1, l2, l3 = [_token_major(ref, st, r) for ref, st, r in zip((l1_ref, l2_ref, l3_ref), stage_refs[3:], dils)]
    mx = jnp.maximum(jnp.maximum(l1, l2), l3)
    e1, e2, e3 = jnp.exp(l1 - mx), jnp.exp(l2 - mx), jnp.exp(l3 - mx)
    inv = 1.0 / (e1 + e2 + e3)
    yb = ((e1 * inv) * o[0] + (e2 * inv) * o[1] + (e3 * inv) * o[2]).astype(BF16)
    ya = ya_ref[...]
    d = x_ref.shape[1]
    for c in range(0, d, MIX_COLS):
        cs = slice(c, c + MIX_COLS)
        a = jnp.dot(ya, wa_ref[:, cs], preferred_element_type=F32)
        bmix = jnp.dot(yb, wb_ref[:, cs], preferred_element_type=F32)
        g0 = _sigmoid(gate_ref[:, cs].astype(F32))
        g1 = _sigmoid(gate_ref[:, d + c:d + c + MIX_COLS].astype(F32))
        mixed_ref[:, cs] = (g0 * a + g1 * bmix).astype(BF16)
    h = x_ref[...] + jnp.dot(mixed_ref[...], wo_ref[...], preferred_element_type=F32)
    h_ref[...] = h
    t = h * lax.rsqrt(jnp.mean(h * h, axis=-1, keepdims=True) + RMS_EPS) * fg_ref[...]
    _store_row_tiles(t_ref, 0, t)

    t_hi = t.astype(BF16)
    t_lo = (t - t_hi.astype(F32)).astype(BF16)
    r1 = jnp.dot(t_hi, wr_ref[...], preferred_element_type=F32)
    r2 = jnp.dot(t_lo, wr_ref[:, :ROUTE_COLS], preferred_element_type=F32)
    logits = r1[:, :ROUTE_COLS] + (r1[:, ROUTE_COLS:] + r2) + br_ref[...]
    lane = lax.broadcasted_iota(I32, logits.shape, 1)
    ninf = -jnp.inf
    big = jnp.int32(10 ** 6)

    def first_argmax(v, vmax):
        return jnp.min(jnp.where(v == vmax, lane, big), axis=-1, keepdims=True)

    gl = jnp.where(lane < N_GROUPS, logits, ninf)
    gmax = jnp.max(gl, axis=-1, keepdims=True)
    gsel = first_argmax(gl, gmax)
    p_g = 1.0 / jnp.sum(jnp.exp(gl - gmax), axis=-1, keepdims=True)
    lo = N_GROUPS + gsel * EXPERTS_PER_GROUP
    el = jnp.where((lane >= lo) & (lane < lo + EXPERTS_PER_GROUP), logits, ninf)
    v1 = jnp.max(el, axis=-1, keepdims=True)
    i1 = first_argmax(el, v1)
    el2 = jnp.where(lane == i1, ninf, el)
    v2 = jnp.max(el2, axis=-1, keepdims=True)
    i2 = first_argmax(el2, v2)
    e2 = jnp.exp(v2 - v1)
    w1 = p_g / (1.0 + e2)
    w2 = p_g * e2 / (1.0 + e2)
    rw_ref[...] = jnp.where(lane == 0, w1, jnp.where(lane == 1, w2, 0.0))
    ri_ref[...] = jnp.where(lane == 0, i1 - N_GROUPS, jnp.where(lane == 1, i2 - N_GROUPS, 0))


def _mix(x2, ya, os_, lses, gates, wa, wb, wo, ffn_norm, wr, br):
    n, d = x2.shape
    tm = MIX_TM
    row = lambda c: pl.BlockSpec((tm, c), lambda i: (i, 0))
    res = lambda a: pl.BlockSpec((tm * a.shape[0] // n, a.shape[1]), lambda i: (i, 0))
    const = lambda s: pl.BlockSpec(s, lambda i: (0, 0))
    return pl.pallas_call(
        _mix_kernel,
        out_shape=[jax.ShapeDtypeStruct((n, d), F32), jax.ShapeDtypeStruct((n * (d // LANES), LANES), F32),
                   jax.ShapeDtypeStruct((n, ROUTE_COLS), F32), jax.ShapeDtypeStruct((n, ROUTE_COLS), I32)],
        grid=(n // tm,),
        in_specs=[row(d), row(512)] + [res(a) for a in os_] + [res(a) for a in lses] + [row(2 * d),
                  const(wa.shape), const(wb.shape), const(wo.shape), const((1, d)),
                  const(wr.shape), const((1, ROUTE_COLS))],
        out_specs=[row(d), pl.BlockSpec((tm * (d // LANES), LANES), lambda i: (i, 0)),
                   row(ROUTE_COLS), row(ROUTE_COLS)],
        scratch_shapes=[pltpu.VMEM((tm, d), BF16)] + [pltpu.VMEM((DIL_OUT // LANES, tm, LANES), F32)] * 6,
        compiler_params=_cparams(("parallel",)),
        name="mix",
    )(x2, ya, *os_, *lses, gates, wa, wb, wo, ffn_norm.reshape(1, d), wr, br)


GATHER_UNROLL = 32


def _start_row_gather(src_hbm, idx_ref, nrows, dt, buf, sem, slot):
    def body(g, c):
        for u in range(GATHER_UNROLL):
            r = g * GATHER_UNROLL + u
            src = pl.multiple_of(idx_ref[0, 0, r], dt)
            dst = pl.multiple_of((slot * nrows + r) * dt, dt)
            pltpu.make_async_copy(src_hbm.at[pl.ds(src, dt)], buf.at[pl.ds(dst, dt)],
                                  sem.at[slot]).start(priority=u % 2)
        return c
    lax.fori_loop(0, nrows // GATHER_UNROLL, body, 0)


def _wait_row_gather(src_hbm, nrows, dt, buf, sem, slot):
    dst = pl.multiple_of(slot * nrows * dt, dt)
    pltpu.make_async_copy(src_hbm.at[pl.ds(0, nrows * dt)], buf.at[pl.ds(dst, nrows * dt)], sem.at[slot]).wait()


def _gather_pipeline(i, nsteps, src_hbm, cur_ref, nxt_ref, nrows, dt, buf, sem):
    slot = i % 2

    @pl.when(i == 0)
    def _():
        _start_row_gather(src_hbm, cur_ref, nrows, dt, buf, sem, 0)

    @pl.when(i + 1 < nsteps)
    def _():
        _start_row_gather(src_hbm, nxt_ref, nrows, dt, buf, sem, 1 - slot)

    _wait_row_gather(src_hbm, nrows, dt, buf, sem, slot)
    return slot


def _expert_kernel(te_ref, cur_ref, nxt_ref, t_hbm, wg_ref, wu_ref, wd_ref, y_ref, buf, sem):
    i = pl.program_id(0)
    dt = wg_ref.shape[1] // LANES
    slot = _gather_pipeline(i, pl.num_programs(0), t_hbm, cur_ref, nxt_ref, MOE_TM, dt, buf, sem)
    xt = _load_row_tiles(buf, slot * MOE_TM, MOE_TM, dt).astype(BF16)
    hg = jnp.dot(xt, wg_ref[0], preferred_element_type=F32)
    hu = jnp.dot(xt, wu_ref[0], preferred_element_type=F32)
    hid = (hg * jax.nn.sigmoid(hg)) * hu
    _store_row_tiles(y_ref, 0, jnp.dot(hid.astype(BF16), wd_ref[0], preferred_element_type=F32))


def _experts(tile_expert, row_token, t, wg, wu, wd, d):
    dt = d // LANES
    ntiles = tile_expert.shape[0]
    ff = wg.shape[2]
    tok3 = (row_token * dt).reshape(ntiles, 1, MOE_TM)
    smem_cur = pl.BlockSpec((1, 1, MOE_TM), lambda i, te: (i, 0, 0), memory_space=pltpu.SMEM)
    smem_nxt = pl.BlockSpec((1, 1, MOE_TM), lambda i, te: (jnp.minimum(i + 1, ntiles - 1), 0, 0),
                            memory_space=pltpu.SMEM)
    wspec = lambda s: pl.BlockSpec((1,) + s, lambda i, te: (te[i], 0, 0))
    return pl.pallas_call(
        _expert_kernel,
        out_shape=jax.ShapeDtypeStruct((ntiles * MOE_TM * dt, LANES), F32),
        grid_spec=pltpu.PrefetchScalarGridSpec(
            num_scalar_prefetch=1,
            grid=(ntiles,),
            in_specs=[smem_cur, smem_nxt, pl.BlockSpec(memory_space=pl.ANY),
                      wspec((d, ff)), wspec((d, ff)), wspec((ff, d))],
            out_specs=pl.BlockSpec((MOE_TM * dt, LANES), lambda i, te: (i, 0)),
            scratch_shapes=[pltpu.VMEM((2 * MOE_TM * dt, LANES), F32), pltpu.SemaphoreType.DMA((2,))],
        ),
        compiler_params=_cparams(("arbitrary",)),
        name="experts",
    )(tile_expert, tok3, tok3, t, wg, wu, wd)


def _final_kernel(cur_ref, nxt_ref, y_hbm, h_ref, rw_ref, fn_ref, o_ref, buf, sem):
    i = pl.program_id(0)
    dt = h_ref.shape[1] // LANES
    slot = _gather_pipeline(i, pl.num_programs(0), y_hbm, cur_ref, nxt_ref, 2 * FIN_TM, dt, buf, sem)
    rw = rw_ref[...]
    y0 = _load_row_tiles(buf, slot * 2 * FIN_TM, FIN_TM, dt)
    y1 = _load_row_tiles(buf, slot * 2 * FIN_TM + FIN_TM, FIN_TM, dt)
    h = h_ref[...] + rw[:, 0:1] * y0 + rw[:, 1:2] * y1
    o_ref[...] = h * lax.rsqrt(jnp.mean(h * h, axis=-1, keepdims=True) + RMS_EPS) * fn_ref[...]


def _final(pos_tiles, y_sorted, h, rw, final_norm):
    n, d = h.shape
    tm = FIN_TM
    nt = n // tm
    smem_cur = pl.BlockSpec((1, 1, 2 * tm), lambda i: (i, 0, 0), memory_space=pltpu.SMEM)
    smem_nxt = pl.BlockSpec((1, 1, 2 * tm), lambda i: (jnp.minimum(i + 1, nt - 1), 0, 0),
                            memory_space=pltpu.SMEM)
    return pl.pallas_call(
        _final_kernel,
        out_shape=jax.ShapeDtypeStruct((n, d), F32),
        grid=(nt,),
        in_specs=[smem_cur, smem_nxt, pl.BlockSpec(memory_space=pl.ANY),
                  pl.BlockSpec((tm, d), lambda i: (i, 0)), pl.BlockSpec((tm, ROUTE_COLS), lambda i: (i, 0)),
                  pl.BlockSpec((1, d), lambda i: (0, 0))],
        out_specs=pl.BlockSpec((tm, d), lambda i: (i, 0)),
        scratch_shapes=[pltpu.VMEM((2 * 2 * tm * (d // LANES), LANES), F32), pltpu.SemaphoreType.DMA((2,))],
        compiler_params=_cparams(("arbitrary",)),
        name="final",
    )(pos_tiles * (d // LANES), pos_tiles * (d // LANES), y_sorted, h, rw, final_norm.reshape(1, d))


def _route_plan(gid):
    n = gid.shape[0]
    e = gid.reshape(-1)
    onehot = (e[:, None] == jnp.arange(N_EXPERTS, dtype=I32)[None, :]).astype(F32)
    chunk = 256
    oh3 = onehot.reshape(-1, chunk, N_EXPERTS)
    within = jnp.einsum("ij,tjk->tik", jnp.tril(jnp.ones((chunk, chunk), F32)), oh3)
    totals = within[:, -1, :]
    before = jnp.cumsum(totals, axis=0) - totals
    csum = (within + before[:, None, :]).reshape(-1, N_EXPERTS)
    rank = jnp.sum(csum * onehot, axis=1).astype(I32) - 1
    counts = (before[-1] + totals[-1]).astype(I32)
    padded = ((counts + MOE_TM - 1) // MOE_TM) * MOE_TM
    seg_end = jnp.cumsum(padded)
    pos = (seg_end - padded)[e] + rank
    nrows = 2 * n + N_EXPERTS * MOE_TM
    tile_start = jnp.arange(nrows // MOE_TM, dtype=I32) * MOE_TM
    tile_expert = jnp.minimum(jnp.sum(tile_start[:, None] >= seg_end[None, :], axis=1), N_EXPERTS - 1).astype(I32)
    order = jnp.argsort(e, stable=True).astype(I32)
    first = jnp.cumsum(counts) - counts
    row_expert = jnp.repeat(tile_expert, MOE_TM)
    src = jnp.arange(nrows, dtype=I32) - (seg_end - padded)[row_expert] + first[row_expert]
    row_token = order[jnp.clip(src, 0, 2 * n - 1)] // 2
    return row_token, tile_expert, pos.reshape(n, 2)


def _pack_w_in(w):
    d = w.shape[0]
    o_kv, o_qi, o_ki, o_wi = 512, 768, 1024, 1088
    o_dil = o_wi + IDX_HEADS
    o_gate = o_dil + 9 * 256
    pad = jnp.zeros((d, LANES - IDX_DIM - IDX_HEADS), w.dtype)
    packed = jnp.concatenate([w[:, :o_ki], w[:, o_ki:o_wi], w[:, o_wi:o_dil], pad, w[:, o_dil:o_gate],
                              w[:, o_gate:]], axis=1)
    assert packed.shape[1] == C_END
    return packed.astype(BF16)


def kernel(x, attn_norm, w_in, kv_norm, w_uk, w_uv, rel_bias, w_branch_a, w_branch_b, w_out, ffn_norm,
           w_router_group, b_router_group, w_router_expert, b_router_expert, w_gate, w_up, w_down,
           final_norm):
    b, seq, d = x.shape
    n = b * seq
    nkc = seq // BLK
    assert w_in.shape[0] == 1, "one layer"
    x2 = x.reshape(n, d)

    outs = _proj(x2, attn_norm[0], _pack_w_in(w_in[0]), kv_norm[0])
    qaT, ckv, ckvT, qiT, kw, wiT = outs[:6]
    dil = outs[6:15]
    gates = outs[15]

    dsa_bias, dil_bias = _bias_tiles(rel_bias, nkc)

    kidx = kw.reshape(b, seq, LANES)[:, :, :IDX_DIM].astype(BF16)
    ckv3 = ckv.reshape(b, seq, KV_LATENT)
    ya = _dsa(qiT, wiT, qaT, kidx, ckv3, ckvT, w_uk[0].astype(BF16), w_uv[0].astype(BF16), dsa_bias)

    os_, lses = [], []
    for g, (_, dilation) in enumerate(DIL_GROUPS):
        o, lse = _dilated_group(dil[g], dil[3 + g], dil[6 + g], dil_bias, g, dilation, b)
        os_.append(o)
        lses.append(lse)

    wr = jnp.concatenate([w_router_group[0],
                          jnp.swapaxes(w_router_expert[0], 0, 1).reshape(d, N_EXPERTS),
                          jnp.zeros((d, ROUTE_COLS - N_GROUPS - N_EXPERTS), F32)], axis=1)
    wr_hi = wr.astype(BF16)
    wr = jnp.concatenate([wr_hi, (wr - wr_hi.astype(F32)).astype(BF16)], axis=1)
    br = jnp.concatenate([b_router_group[0], b_router_expert[0].reshape(-1),
                          jnp.zeros((ROUTE_COLS - N_GROUPS - N_EXPERTS,), F32)]).reshape(1, ROUTE_COLS)
    h, t, rw, ri = _mix(x2, ya.reshape(n, -1), os_, lses, gates,
                        w_branch_a[0].astype(BF16), w_branch_b[0].astype(BF16), w_out[0].astype(BF16),
                        ffn_norm[0], wr, br)

    row_token, tile_expert, pos = _route_plan(ri[:, :2])
    y_sorted = _experts(tile_expert, row_token, t, w_gate[0].astype(BF16), w_up[0].astype(BF16),
                        w_down[0].astype(BF16), d)

    pos_tiles = jnp.swapaxes(pos.reshape(n // FIN_TM, FIN_TM, 2), 1, 2).reshape(n // FIN_TM, 1, 2 * FIN_TM)
    out = _final(pos_tiles, y_sorted, h, rw, final_norm)
    return out.reshape(b, seq, d)
```
